```python
import functools
import jax, jax.numpy as jnp
from jax import lax
import numpy as np

D_MODEL = 1024
BATCH = 8
SEQ = 4096
DEPTH = 1
DEC_BATCH = 32
DEC_SEQ = 1
PAST_LEN = 16384
PAGE_SIZE = 128

D_RNN = D_MODEL
RNN_BLOCKS = 4
RNN_BLOCK_W = D_RNN // RNN_BLOCKS
CONV_W = 4
LRU_C = 8.0
N_HEADS = 8
HEAD_DIM = 128
N_KV_HEADS = 2
D_ATT = N_HEADS * HEAD_DIM
N_IDX_HEADS = 8
IDX_DIM = 64
IDX_W_SCALE = (N_IDX_HEADS * IDX_DIM) ** -0.5
TOPK_MAX = 256
ROPE_THETA = 10000.0
Q_BLOCK = 128
EPS = 1e-6
SPLITS = (D_RNN, D_RNN, D_ATT, N_KV_HEADS * HEAD_DIM, N_KV_HEADS * HEAD_DIM, D_ATT,
          N_IDX_HEADS * IDX_DIM, IDX_DIM, N_IDX_HEADS, D_MODEL, D_MODEL)
D_IN = sum(SPLITS)

kernel_name = 'hybrid_rglru_dsa_decoder_step'


def rms_norm(x, g):
    xf = x.astype(jnp.float32)
    y = xf * lax.rsqrt(jnp.mean(xf * xf, axis=-1, keepdims=True) + EPS)
    return (y * g.astype(jnp.float32)).astype(x.dtype)


def layer_norm(x, g, b):
    xf = x.astype(jnp.float32)
    mu = jnp.mean(xf, axis=-1, keepdims=True)
    var = jnp.mean(jnp.square(xf - mu), axis=-1, keepdims=True)
    y = (xf - mu) * lax.rsqrt(var + EPS) * g.astype(jnp.float32) + b.astype(jnp.float32)
    return y.astype(x.dtype)


def rotary(x, pos):
    half = x.shape[-1] // 2
    inv_freq = ROPE_THETA ** (-jnp.arange(half, dtype=jnp.float32) / half)
    ang = pos.astype(jnp.float32)[:, None] * inv_freq[None, :]
    cos = jnp.cos(ang)[None, :, None, :]
    sin = jnp.sin(ang)[None, :, None, :]
    xf = x.astype(jnp.float32)
    x1, x2 = xf[..., :half], xf[..., half:]
    return jnp.concatenate([x1 * cos - x2 * sin, x2 * cos + x1 * sin], axis=-1).astype(x.dtype)


def adaln_modulation(c, w, b):
    m = jax.nn.silu(c) @ w + b
    shift, scale, gate = jnp.split(m, 3, axis=-1)
    return shift[:, None, :], scale[:, None, :], gate[:, None, :]


def split_cols(z):
    offsets = np.cumsum(SPLITS)[:-1].tolist()
    return jnp.split(z, offsets, axis=-1)


def causal_depthwise_conv(x, buf, w, b):
    T = x.shape[1]
    xp = jnp.concatenate([buf.astype(x.dtype), x], axis=1)
    y = b
    for j in range(CONV_W):
        y = y + xp[:, j:j + T] * w[j]
    return y, xp[:, T:]


def block_diag(x, w):
    B, T, _ = x.shape
    xb = x.reshape(B, T, RNN_BLOCKS, RNN_BLOCK_W)
    return jnp.einsum('btnc,ncd->btnd', xb, w).reshape(B, T, D_RNN)


def rg_lru(x, h0, w_ra, b_ra, w_rx, b_rx, lam):
    r = jax.nn.sigmoid(block_diag(x, w_ra) + b_ra).astype(jnp.float32)
    i = jax.nn.sigmoid(block_diag(x, w_rx) + b_rx)
    log_a = -LRU_C * r * jax.nn.softplus(-lam.astype(jnp.float32))
    a = jnp.exp(log_a)
    u = jnp.sqrt(-jnp.expm1(2.0 * log_a)) * (i * x).astype(jnp.float32)

    def step(h, au):
        a_t, u_t = au
        h = a_t * h + u_t
        return h, h

    h_last, hs = lax.scan(step, h0.astype(jnp.float32), (jnp.swapaxes(a, 0, 1), jnp.swapaxes(u, 0, 1)))
    return jnp.swapaxes(hs, 0, 1).astype(x.dtype), h_last.astype(h0.dtype)


def take_rows(rows, idx):
    return jax.vmap(lambda r, i: r[i])(rows, idx)


def indexer_topk(qi, wi, ki, q_pos, k_sel):
    s = jnp.einsum('bthd,bsd->bths', qi, ki).astype(jnp.float32)
    score = jnp.einsum('bths,bth->bts', jax.nn.relu(s), wi.astype(jnp.float32))
    key_pos = jnp.arange(ki.shape[1], dtype=jnp.int32)
    admissible = key_pos[None, None, :] <= q_pos[None, :, None]
    score = jnp.where(admissible, score, -jnp.inf)
    _, sel = lax.top_k(score, k_sel)
    valid = sel <= q_pos[None, :, None]
    return sel, valid


def gathered_attention(q, k_sel, v_sel, valid):
    B, T, H, Dh = q.shape
    qg = q.reshape(B, T, N_KV_HEADS, H // N_KV_HEADS, Dh)
    s = jnp.einsum('btgrd,btkgd->btgrk', qg, k_sel).astype(jnp.float32) * (HEAD_DIM ** -0.5)
    s = jnp.where(valid[:, :, None, None, :], s, -jnp.inf)
    p = jax.nn.softmax(s, axis=-1).astype(v_sel.dtype)
    o = jnp.einsum('btgrk,btkgd->btgrd', p, v_sel)
    return o.reshape(B, T, H * Dh)


def prompt_sparse_attention(q, k, v, qi, ki, wi, pos, k_sel):
    B, S = q.shape[0], q.shape[1]

    def one_block(t0):
        sl = lambda arr: lax.dynamic_slice_in_dim(arr, t0, Q_BLOCK, axis=1)
        pos_b = lax.dynamic_slice_in_dim(pos, t0, Q_BLOCK, axis=0)
        sel, valid = indexer_topk(sl(qi), sl(wi), ki, pos_b, k_sel)
        return gathered_attention(sl(q), take_rows(k, sel), take_rows(v, sel), valid)

    o = lax.map(one_block, jnp.arange(0, S, Q_BLOCK))
    return jnp.swapaxes(o, 0, 1).reshape(B, S, -1)


def sample_sparse_attention(q, k, v, qi, ki, wi, pos, k_sel, cache_k, cache_v, cache_idx_k, page_table, layer):
    B, T = q.shape[0], q.shape[1]
    past_ki = cache_idx_k[layer, page_table].reshape(B, -1, IDX_DIM)
    past_len = past_ki.shape[1]
    all_ki = jnp.concatenate([past_ki.astype(ki.dtype), ki], axis=1)
    sel, valid = indexer_topk(qi, wi, all_ki, pos, k_sel)
    in_past = (sel < past_len)[..., None, None]
    sp = jnp.minimum(sel, past_len - 1)
    phys = jnp.take_along_axis(page_table, (sp // PAGE_SIZE).reshape(B, -1), axis=1).reshape(sel.shape)
    off = sp % PAGE_SIZE
    sn = jnp.clip(sel - past_len, 0, T - 1)
    k_rows = jnp.where(in_past, cache_k[layer, phys, off].astype(k.dtype), take_rows(k, sn))
    v_rows = jnp.where(in_past, cache_v[layer, phys, off].astype(v.dtype), take_rows(v, sn))
    return gathered_attention(q, k_rows, v_rows, valid)


def trunk_layer(x, c, pos, conv_buf, h0, attend, p):
    B, T, _ = x.shape
    shift, scale, gate = adaln_modulation(c, p['w_ada'], p['b_ada'])
    xn = rms_norm(x, p['g_norm']) * (1.0 + scale) + shift
    z = xn @ p['w_in']
    xa, ga, q, k, v, gb, qi, ki, wi, ma, mb = split_cols(z)
    xa_conv, new_buf = causal_depthwise_conv(xa, conv_buf, p['w_conv'], p['b_conv'])
    ha, h_last = rg_lru(xa_conv, h0, p['w_ra'], p['b_ra'], p['w_rx'], p['b_rx'], p['lru_lambda'])
    ya = (ha * jax.nn.silu(ga)) @ p['w_pa']
    q = rotary(q.reshape(B, T, N_HEADS, HEAD_DIM), pos)
    k = rotary(k.reshape(B, T, N_KV_HEADS, HEAD_DIM), pos)
    v = v.reshape(B, T, N_KV_HEADS, HEAD_DIM)
    qi = rotary(qi.reshape(B, T, N_IDX_HEADS, IDX_DIM), pos)
    ki = rotary(layer_norm(ki, p['idx_k_norm_g'], p['idx_k_norm_b'])[:, :, None, :], pos)[:, :, 0, :]
    wi = wi * IDX_W_SCALE
    o = attend(q, k, v, qi, ki, wi, pos)
    yb = (o * jax.nn.silu(gb)) @ p['w_pb']
    m = jax.nn.sigmoid(ma) * ya + jax.nn.sigmoid(mb) * yb
    x = x + gate * (m @ p['w_o'])
    return x, (k, v, ki, new_buf, h_last)


def setup_inputs(seed: int = 0) -> dict:
    key = jax.random.key(seed)
    ks = jax.random.split(key, 32)
    f32 = jnp.float32
    n_pages = PAST_LEN // PAGE_SIZE
    n_used = DEC_BATCH * n_pages
    n_pool = n_used + max(1, n_used // 4)
    nrm = lambda k, shape, s: s * jax.random.normal(k, shape, f32)
    page_table = jax.random.permutation(ks[0], n_pool)[:n_used].reshape(DEC_BATCH, n_pages).astype(jnp.int32)
    a_base = jax.random.uniform(ks[1], (DEPTH, D_RNN), f32, 0.9, 0.999)
    s_base = a_base ** (1.0 / LRU_C)
    lru_lambda = jnp.log(s_base) - jnp.log1p(-s_base)
    return {
        'x_prompt': nrm(ks[2], (BATCH, SEQ, D_MODEL), 1.0),
        'x_sample': nrm(ks[3], (DEC_BATCH, DEC_SEQ, D_MODEL), 1.0),
        'cache_k': nrm(ks[4], (DEPTH, n_pool, PAGE_SIZE, N_KV_HEADS, HEAD_DIM), 1.0),
        'cache_v': nrm(ks[5], (DEPTH, n_pool, PAGE_SIZE, N_KV_HEADS, HEAD_DIM), 1.0),
        'cache_idx_k': nrm(ks[6], (DEPTH, n_pool, PAGE_SIZE, IDX_DIM), 1.0),
        'state_conv': nrm(ks[7], (DEPTH, DEC_BATCH, CONV_W - 1, D_RNN), 1.0),
        'state_rglru': nrm(ks[8], (DEPTH, DEC_BATCH, D_RNN), 0.5),
        'page_table': page_table,
        'c_prompt': nrm(ks[9], (BATCH, D_MODEL), 1.0),
        'c_sample': nrm(ks[10], (DEC_BATCH, D_MODEL), 1.0),
        'w_ada': nrm(ks[11], (DEPTH, D_MODEL, 3 * D_MODEL), 0.5 * D_MODEL ** -0.5),
        'b_ada': nrm(ks[12], (DEPTH, 3 * D_MODEL), 0.02),
        'g_norm': 1.0 + nrm(ks[13], (DEPTH, D_MODEL), 0.02),
        'w_in': nrm(ks[14], (DEPTH, D_MODEL, D_IN), D_MODEL ** -0.5),
        'w_conv': nrm(ks[15], (DEPTH, CONV_W, D_RNN), CONV_W ** -0.5),
        'b_conv': nrm(ks[16], (DEPTH, D_RNN), 0.02),
        'w_ra': nrm(ks[17], (DEPTH, RNN_BLOCKS, RNN_BLOCK_W, RNN_BLOCK_W), RNN_BLOCK_W ** -0.5),
        'b_ra': nrm(ks[18], (DEPTH, D_RNN), 0.02),
        'w_rx': nrm(ks[19], (DEPTH, RNN_BLOCKS, RNN_BLOCK_W, RNN_BLOCK_W), RNN_BLOCK_W ** -0.5),
        'b_rx': nrm(ks[20], (DEPTH, D_RNN), 0.02),
        'lru_lambda': lru_lambda,
        'idx_k_norm_g': 1.0 + nrm(ks[21], (DEPTH, IDX_DIM), 0.02),
        'idx_k_norm_b': nrm(ks[22], (DEPTH, IDX_DIM), 0.02),
        'w_pa': nrm(ks[23], (DEPTH, D_RNN, D_MODEL), D_RNN ** -0.5),
        'w_pb': nrm(ks[24], (DEPTH, D_ATT, D_MODEL), D_ATT ** -0.5),
        'w_o': nrm(ks[25], (DEPTH, D_MODEL, D_MODEL), D_MODEL ** -0.5),
        'g_final': 1.0 + nrm(ks[26], (D_MODEL,), 0.02),
    }


def reference(x_prompt, x_sample, cache_k, cache_v, cache_idx_k, state_conv, state_rglru, page_table,
              c_prompt, c_sample, w_ada, b_ada, g_norm, w_in, w_conv, b_conv, w_ra, b_ra, w_rx, b_rx,
              lru_lambda, idx_k_norm_g, idx_k_norm_b, w_pa, w_pb, w_o, g_final):
    pos_prompt = jnp.arange(SEQ, dtype=jnp.int32)
    pos_sample = PAST_LEN + jnp.arange(DEC_SEQ, dtype=jnp.int32)
    k_prompt = min(TOPK_MAX, SEQ // 4)
    k_sample = min(TOPK_MAX, (PAST_LEN + DEC_SEQ) // 4)
    hp, hs = x_prompt, x_sample
    kp, vp, kip, cp, lp = [], [], [], [], []
    ksm, vsm, kism, csm, lsm = [], [], [], [], []
    for l in range(DEPTH):
        p = {
            'w_ada': w_ada[l], 'b_ada': b_ada[l], 'g_norm': g_norm[l], 'w_in': w_in[l],
            'w_conv': w_conv[l], 'b_conv': b_conv[l], 'w_ra': w_ra[l], 'b_ra': b_ra[l],
            'w_rx': w_rx[l], 'b_rx': b_rx[l], 'lru_lambda': lru_lambda[l],
            'idx_k_norm_g': idx_k_norm_g[l], 'idx_k_norm_b': idx_k_norm_b[l],
            'w_pa': w_pa[l], 'w_pb': w_pb[l], 'w_o': w_o[l],
        }
        prompt_attend = functools.partial(prompt_sparse_attention, k_sel=k_prompt)
        sample_attend = functools.partial(sample_sparse_attention, k_sel=k_sample, cache_k=cache_k,
                                          cache_v=cache_v, cache_idx_k=cache_idx_k,
                                          page_table=page_table, layer=l)
        conv0 = jnp.zeros((BATCH, CONV_W - 1, D_RNN), x_prompt.dtype)
        h0 = jnp.zeros((BATCH, D_RNN), x_prompt.dtype)
        hp, st_p = trunk_layer(hp, c_prompt, pos_prompt, conv0, h0, prompt_attend, p)
        hs, st_s = trunk_layer(hs, c_sample, pos_sample, state_conv[l], state_rglru[l], sample_attend, p)
        kp.append(st_p[0]); vp.append(st_p[1]); kip.append(st_p[2]); cp.append(st_p[3]); lp.append(st_p[4])
        ksm.append(st_s[0]); vsm.append(st_s[1]); kism.append(st_s[2]); csm.append(st_s[3]); lsm.append(st_s[4])
    y_prompt = rms_norm(hp, g_final)
    y_sample = rms_norm(hs, g_final)
    new_k_prompt = jnp.stack(kp)
    new_v_prompt = jnp.stack(vp)
    new_idx_k_prompt = jnp.stack(kip)
    new_conv_prompt = jnp.stack(cp)
    new_lru_prompt = jnp.stack(lp)
    new_k_sample = jnp.stack(ksm)
    new_v_sample = jnp.stack(vsm)
    new_idx_k_sample = jnp.stack(kism)
    new_conv_sample = jnp.stack(csm)
    new_lru_sample = jnp.stack(lsm)
    return (y_prompt, y_sample, new_k_prompt, new_v_prompt, new_idx_k_prompt, new_conv_prompt, new_lru_prompt,
            new_k_sample, new_v_sample, new_idx_k_sample, new_conv_sample, new_lru_sample)
```

```python
import functools

import jax
import jax.numpy as jnp
import numpy as np
from jax import lax
from jax.experimental import pallas as pl
from jax.experimental.pallas import tpu as pltpu

D_MODEL = 1024
BATCH = 8
SEQ = 4096
DEC_BATCH = 32
PAST_LEN = 16384
PAGE_SIZE = 128
N_PAGES = PAST_LEN // PAGE_SIZE
D_RNN = D_MODEL
RNN_BLOCKS = 4
RNN_BLOCK_W = D_RNN // RNN_BLOCKS
CONV_W = 4
LRU_C = 8.0
N_HEADS = 8
HEAD_DIM = 128
N_KV_HEADS = 2
HEADS_PER_KV = N_HEADS // N_KV_HEADS
D_ATT = N_HEADS * HEAD_DIM
D_KV = N_KV_HEADS * HEAD_DIM
N_IDX_HEADS = 8
IDX_DIM = 64
D_IDX = N_IDX_HEADS * IDX_DIM
IDX_W_SCALE = (N_IDX_HEADS * IDX_DIM) ** -0.5
TOPK = 256
ROPE_THETA = 10000.0
EPS = 1e-6
SPLITS = (D_RNN, D_RNN, D_ATT, D_KV, D_KV, D_ATT, D_IDX, IDX_DIM, N_IDX_HEADS, D_MODEL, D_MODEL)

LANES = 128
SUBLANES = 8

C_XA, C_GA, C_Q, C_GB, C_MA, C_MB = 0, 1024, 2048, 3072, 4096, 5120
C_K, C_V, C_QI, C_KW = 6144, 6400, 6656, 7168
D_IN_PACKED = 7296

PROJ_ROWS = 256
Q_BLOCK = 128
KEY_CHUNK = 256
assert KEY_CHUNK >= TOPK and PAGE_SIZE == HEAD_DIM == LANES
OUT_ROWS = 512
SOFTMAX_SCALE_LOG2E = (HEAD_DIM ** -0.5) * float(np.log2(np.e))
NEG_INF = float("-inf")
INT_MIN = -2 ** 31
VMEM_LIMIT = 56 * 1024 * 1024


def _sigmoid(x):
    return 1.0 / (1.0 + jnp.exp(-x))


def _silu(x):
    return x * _sigmoid(x)


def _dot(a, b):
    return jnp.dot(a, b, preferred_element_type=jnp.float32)


def _dot_nt(a, b):
    return lax.dot_general(a, b, (((1,), (1,)), ((), ())), preferred_element_type=jnp.float32)


def _bf16(x):
    return x.astype(jnp.bfloat16)


def _const_spec(shape, single=True):
    nd = len(shape)
    kwargs = {"pipeline_mode": pl.Buffered(1)} if single else {}
    return pl.BlockSpec(shape, lambda *_: (0,) * nd, **kwargs)


def _rope_kernel(invf_ref, cos_h_ref, sin_h_ref, cos_i_ref, sin_ia_ref, sin_ib_ref, *, pos0, pos_step, rows):
    r0 = pl.program_id(0) * rows
    row = lax.broadcasted_iota(jnp.int32, (rows, LANES), 0) + r0
    lane = lax.broadcasted_iota(jnp.int32, (rows, LANES), 1)
    pos = (pos0 + pos_step * row).astype(jnp.float32)
    ang_h = pos * invf_ref[0:1, :]
    ang_i = pos * invf_ref[1:2, :]
    cos_h_ref[...] = jnp.cos(ang_h)
    sh = jnp.sin(ang_h)
    sin_h_ref[...] = jnp.where(lane < HEAD_DIM // 2, -sh, sh)
    cos_i_ref[...] = jnp.cos(ang_i)
    si = jnp.sin(ang_i)
    first_half = (lane % IDX_DIM) < IDX_DIM // 2
    sin_ia_ref[...] = jnp.where(first_half, -si, 0.0)
    sin_ib_ref[...] = jnp.where(first_half, 0.0, si)


def _rope_tables(invf, n, pos0, pos_step):
    rows = min(n, 512)
    out = jax.ShapeDtypeStruct((n, LANES), jnp.float32)
    spec = pl.BlockSpec((rows, LANES), lambda i: (i, 0))
    return pl.pallas_call(
        functools.partial(_rope_kernel, pos0=pos0, pos_step=pos_step, rows=rows),
        grid=(n // rows,),
        in_specs=[pl.BlockSpec((SUBLANES, LANES), lambda i: (0, 0))],
        out_specs=[spec] * 5,
        out_shape=[out] * 5,
        name="rope_tables",
    )(invf)


def _rot_head(z, cos, sin_signed):
    return z * cos + pltpu.roll(z, HEAD_DIM // 2, 1) * sin_signed


def _rot_idx(z, cos, sin_a, sin_b):
    return z * cos + pltpu.roll(z, LANES - IDX_DIM // 2, 1) * sin_a + pltpu.roll(z, IDX_DIM // 2, 1) * sin_b


def _ada_kernel(c_ref, w_ref, b_ref, o_ref):
    o_ref[...] = _dot(_bf16(_silu(c_ref[...])), w_ref[...]) + b_ref[...]


def _ada_modulation(c_all, w_ada, b_ada):
    n = c_all.shape[0]
    return pl.pallas_call(
        _ada_kernel,
        grid=(1,),
        in_specs=[_const_spec((n, D_MODEL)), _const_spec((D_MODEL, 3 * D_MODEL)), _const_spec((1, 3 * D_MODEL))],
        out_specs=_const_spec((n, 3 * D_MODEL), single=False),
        out_shape=jax.ShapeDtypeStruct((n, 3 * D_MODEL), jnp.float32),
        compiler_params=pltpu.CompilerParams(vmem_limit_bytes=VMEM_LIMIT),
        name="ada_modulation",
    )(c_all, w_ada, b_ada)


def _modulated_norm(x, g, scale, shift):
    y = x * lax.rsqrt(jnp.mean(x * x, axis=-1, keepdims=True) + EPS) * g
    return _bf16(y * (1.0 + scale) + shift)


def _lru_gates(xc, w_ra_ref, b_ra, w_rx_ref, b_rx, lam):
    xcb = _bf16(xc)
    r_parts, i_parts = [], []
    for n in range(RNN_BLOCKS):
        sl = slice(n * RNN_BLOCK_W, (n + 1) * RNN_BLOCK_W)
        r_parts.append(_dot(xcb[:, sl], w_ra_ref[n]))
        i_parts.append(_dot(xcb[:, sl], w_rx_ref[n]))
    r = _sigmoid(jnp.concatenate(r_parts, axis=1) + b_ra)
    i = _sigmoid(jnp.concatenate(i_parts, axis=1) + b_rx)
    neg_lam = -lam
    softplus = jnp.maximum(neg_lam, 0.0) + jnp.log1p(jnp.exp(-jnp.abs(neg_lam)))
    log_a = (-LRU_C) * r * softplus
    a = jnp.exp(log_a)
    u = jnp.sqrt(-jnp.tanh(log_a) * (a * a + 1.0)) * (i * xc)
    return a, u


def _idx_key_slab(z_kw, g, b, cos_i, sin_ia, sin_ib):
    lane = lax.broadcasted_iota(jnp.int32, z_kw.shape, 1)
    is_key = lane < IDX_DIM
    mu = jnp.sum(jnp.where(is_key, z_kw, 0.0), axis=-1, keepdims=True) * (1.0 / IDX_DIM)
    d = jnp.where(is_key, z_kw - mu, 0.0)
    var = jnp.sum(d * d, axis=-1, keepdims=True) * (1.0 / IDX_DIM)
    y = d * lax.rsqrt(var + EPS) * g + b
    key = _rot_idx(y, cos_i, sin_ia, sin_ib)
    is_w = jnp.logical_and(lane >= IDX_DIM, lane < IDX_DIM + N_IDX_HEADS)
    return key + jnp.where(is_w, z_kw * IDX_W_SCALE, 0.0)


def _proj_kernel(x_ref, shift_ref, scale_ref, gn_ref, w_in_ref, wconv_ref, bconv_ref, w_ra_ref, b_ra_ref,
                 w_rx_ref, b_rx_ref, lam_ref, ig_ref, ib_ref, w_pa_ref,
                 cos_h_ref, sin_h_ref, cos_i_ref, sin_ia_ref, sin_ib_ref,
                 k_ref, v_ref, ki_ref, kb_ref, vt_ref, kib_ref, q_ref, qi_ref, kiwi_ref,
                 mpa_ref, sgb_ref, smb_ref, conv_ref, lru_ref,
                 xa_ext, a_s, u_s, h_carry):
    ts = PROJ_ROWS
    j = pl.program_id(1)
    last = pl.num_programs(1) - 1

    @pl.when(j == 0)
    def _():
        xa_ext[0:SUBLANES, :] = jnp.zeros((SUBLANES, D_RNN), jnp.float32)
        h_carry[...] = jnp.zeros((1, D_RNN), jnp.float32)

    xn = _modulated_norm(x_ref[...], gn_ref[...], scale_ref[...], shift_ref[...])

    xa_ext[SUBLANES:SUBLANES + ts, :] = _dot(xn, w_in_ref[:, C_XA:C_XA + D_RNN])
    xc = bconv_ref[...]
    for t in range(CONV_W):
        off = SUBLANES - (CONV_W - 1) + t
        xc = xc + xa_ext[off:off + ts, :] * wconv_ref[t:t + 1, :]

    @pl.when(j == last)
    def _():
        conv_ref[...] = xa_ext[ts + SUBLANES - (CONV_W - 1):ts + SUBLANES, :]

    xa_ext[0:SUBLANES, :] = xa_ext[ts:ts + SUBLANES, :]

    a, u = _lru_gates(xc, w_ra_ref, b_ra_ref[...], w_rx_ref, b_rx_ref[...], lam_ref[...])
    a_s[...] = a
    u_s[...] = u
    row = lax.broadcasted_iota(jnp.int32, (SUBLANES, D_RNN), 0)

    def scan_group(g, hc):
        r0 = pl.multiple_of(g * SUBLANES, SUBLANES)
        a8 = a_s[pl.ds(r0, SUBLANES), :]
        u8 = u_s[pl.ds(r0, SUBLANES), :]
        for d in (1, 2, 4):
            keep = row >= d
            u8 = jnp.where(keep, a8 * pltpu.roll(u8, d, 0) + u8, u8)
            a8 = jnp.where(keep, a8 * pltpu.roll(a8, d, 0), a8)
        h8 = a8 * hc + u8
        u_s[pl.ds(r0, SUBLANES), :] = h8
        return h8[SUBLANES - 1:SUBLANES, :]

    hc = lax.fori_loop(0, ts // SUBLANES, scan_group, h_carry[...])
    h_carry[...] = hc

    @pl.when(j == last)
    def _():
        lru_ref[...] = hc

    ga = _dot(xn, w_in_ref[:, C_GA:C_GA + D_RNN])
    ya = _dot(_bf16(u_s[...] * _silu(ga)), w_pa_ref[...])
    ma = _dot(xn, w_in_ref[:, C_MA:C_MA + D_MODEL])
    mpa_ref[...] = _sigmoid(ma) * ya
    sgb_ref[...] = _silu(_dot(xn, w_in_ref[:, C_GB:C_GB + D_ATT]))
    smb_ref[...] = _sigmoid(_dot(xn, w_in_ref[:, C_MB:C_MB + D_MODEL]))

    cos_h, sin_h = cos_h_ref[...], sin_h_ref[...]
    cos_i, sin_ia, sin_ib = cos_i_ref[...], sin_ia_ref[...], sin_ib_ref[...]
    zq = _dot(xn, w_in_ref[:, C_Q:C_Q + D_ATT])
    for h in range(N_HEADS):
        sl = slice(h * HEAD_DIM, (h + 1) * HEAD_DIM)
        q_ref[:, sl] = _bf16(_rot_head(zq[:, sl], cos_h, sin_h))
    zk = _dot(xn, w_in_ref[:, C_K:C_K + D_KV])
    for g in range(N_KV_HEADS):
        sl = slice(g * HEAD_DIM, (g + 1) * HEAD_DIM)
        kr = _rot_head(zk[:, sl], cos_h, sin_h)
        k_ref[:, sl] = kr
        kb_ref[:, sl] = _bf16(kr)
    zv = _dot(xn, w_in_ref[:, C_V:C_V + D_KV])
    v_ref[...] = zv
    vt = _bf16(zv.T)
    for c in range(ts // KEY_CHUNK):
        vt_ref[c] = vt[:, c * KEY_CHUNK:(c + 1) * KEY_CHUNK]
    zqi = _dot(xn, w_in_ref[:, C_QI:C_QI + D_IDX])
    for p in range(D_IDX // LANES):
        sl = slice(p * LANES, (p + 1) * LANES)
        qi_ref[:, sl] = _bf16(_rot_idx(zqi[:, sl], cos_i, sin_ia, sin_ib))
    slab = _idx_key_slab(_dot(xn, w_in_ref[:, C_KW:C_KW + LANES]), ig_ref[...], ib_ref[...], cos_i, sin_ia, sin_ib)
    kiwi_ref[...] = slab
    ki_ref[...] = slab[:, :IDX_DIM]
    key_even = jnp.where(lax.broadcasted_iota(jnp.int32, slab.shape, 1) < IDX_DIM, slab, 0.0)
    kib_ref[:, 0:LANES] = _bf16(key_even)
    kib_ref[:, LANES:2 * LANES] = _bf16(pltpu.roll(key_even, IDX_DIM, 1))


def _prompt_projection(x, shift, scale, wts, tabs):
    ts = PROJ_ROWS
    nt = SEQ // ts
    f32, bf16 = jnp.float32, jnp.bfloat16
    row_spec = lambda w: pl.BlockSpec((None, ts, w), lambda b, j: (b, j, 0))
    bvec_spec = pl.BlockSpec((None, 1, D_MODEL), lambda b, j: (b, 0, 0))
    tab_spec = pl.BlockSpec((ts, LANES), lambda b, j: (j, 0))
    in_specs = [
        row_spec(D_MODEL), bvec_spec, bvec_spec, _const_spec((1, D_MODEL)),
        _const_spec((D_MODEL, D_IN_PACKED)), _const_spec((CONV_W, D_RNN)), _const_spec((1, D_RNN)),
        _const_spec((RNN_BLOCKS, RNN_BLOCK_W, RNN_BLOCK_W)), _const_spec((1, D_RNN)),
        _const_spec((RNN_BLOCKS, RNN_BLOCK_W, RNN_BLOCK_W)), _const_spec((1, D_RNN)), _const_spec((1, D_RNN)),
        _const_spec((1, LANES)), _const_spec((1, LANES)), _const_spec((D_RNN, D_MODEL)),
    ] + [tab_spec] * 5
    out_shape = [
        jax.ShapeDtypeStruct((BATCH, SEQ, D_KV), f32),
        jax.ShapeDtypeStruct((BATCH, SEQ, D_KV), f32),
        jax.ShapeDtypeStruct((BATCH, SEQ, IDX_DIM), f32),
        jax.ShapeDtypeStruct((BATCH, SEQ, D_KV), bf16),
        jax.ShapeDtypeStruct((BATCH, SEQ // KEY_CHUNK, D_KV, KEY_CHUNK), bf16),
        jax.ShapeDtypeStruct((BATCH, SEQ, 2 * LANES), bf16),
        jax.ShapeDtypeStruct((BATCH, SEQ, D_ATT), bf16),
        jax.ShapeDtypeStruct((BATCH, SEQ, D_IDX), bf16),
        jax.ShapeDtypeStruct((BATCH, SEQ, LANES), f32),
        jax.ShapeDtypeStruct((BATCH, SEQ, D_MODEL), f32),
        jax.ShapeDtypeStruct((BATCH, SEQ, D_ATT), f32),
        jax.ShapeDtypeStruct((BATCH, SEQ, D_MODEL), f32),
        jax.ShapeDtypeStruct((BATCH, CONV_W - 1, D_RNN), f32),
        jax.ShapeDtypeStruct((BATCH, 1, D_RNN), f32),
    ]
    out_specs = [
        row_spec(D_KV), row_spec(D_KV), row_spec(IDX_DIM), row_spec(D_KV),
        pl.BlockSpec((None, ts // KEY_CHUNK, D_KV, KEY_CHUNK), lambda b, j: (b, j, 0, 0)),
        row_spec(2 * LANES), row_spec(D_ATT), row_spec(D_IDX), row_spec(LANES),
        row_spec(D_MODEL), row_spec(D_ATT), row_spec(D_MODEL),
        pl.BlockSpec((None, CONV_W - 1, D_RNN), lambda b, j: (b, 0, 0)),
        pl.BlockSpec((None, 1, D_RNN), lambda b, j: (b, 0, 0)),
    ]
    scratch = [
        pltpu.VMEM((ts + SUBLANES, D_RNN), f32), pltpu.VMEM((ts, D_RNN), f32),
        pltpu.VMEM((ts, D_RNN), f32), pltpu.VMEM((1, D_RNN), f32),
    ]
    return pl.pallas_call(
        _proj_kernel,
        grid=(BATCH, nt),
        in_specs=in_specs, out_specs=out_specs, out_shape=out_shape, scratch_shapes=scratch,
        compiler_params=pltpu.CompilerParams(
            dimension_semantics=("arbitrary", "arbitrary"), vmem_limit_bytes=VMEM_LIMIT),
        name="prompt_projection",
    )(x, shift, scale, wts["g_norm"], wts["w_in"], wts["w_conv"], wts["b_conv"], wts["w_ra"], wts["b_ra"],
      wts["w_rx"], wts["b_rx"], wts["lam"], wts["idx_g"], wts["idx_b"], wts["w_pa"], *tabs)


def _sortable_key(score):
    bits = pltpu.bitcast(jnp.where(score == 0.0, 0.0, score), jnp.int32)
    return bits ^ jnp.bitwise_and(jnp.right_shift(bits, 31), jnp.int32(0x7FFFFFFF))


def _radix_threshold(count_ge, shape):
    def bit_step(b, t_u):
        cand = jnp.bitwise_or(t_u, jnp.left_shift(jnp.int32(1), 31 - b))
        ok = count_ge(jnp.bitwise_xor(cand, jnp.int32(INT_MIN))) >= TOPK
        return jnp.where(ok, cand, t_u)
    t_u = lax.fori_loop(0, 32, bit_step, jnp.zeros(shape, jnp.int32))
    return jnp.bitwise_xor(t_u, jnp.int32(INT_MIN))


def _tie_cutoff(count_tie_below, need, n_bits, shape):
    def bit_step(b, x):
        cand = jnp.bitwise_or(x, jnp.left_shift(jnp.int32(1), n_bits - 1 - b))
        return jnp.where(count_tie_below(cand) < need, cand, x)
    return lax.fori_loop(0, n_bits, bit_step, jnp.zeros(shape, jnp.int32))


def _attn_kernel(q_ref, qi_ref, kiwi_ref, kb_ref, vt_ref, kib_ref, o_ref, keys_s, bias_s, acc_s):
    i = pl.program_id(1)
    t0 = i * Q_BLOCK
    n_chunks = (t0 + Q_BLOCK + KEY_CHUNK - 1) // KEY_CHUNK
    kc_rows = lax.broadcasted_iota(jnp.int32, (KEY_CHUNK, Q_BLOCK), 0)
    q_pos = t0 + lax.broadcasted_iota(jnp.int32, (KEY_CHUNK, Q_BLOCK), 1)
    lane_shape = (1, Q_BLOCK)

    def chunk_rows(c):
        return pl.ds(pl.multiple_of(c * KEY_CHUNK, KEY_CHUNK), KEY_CHUNK)

    w_t = kiwi_ref[...].T[IDX_DIM:IDX_DIM + N_IDX_HEADS, :]
    qi = qi_ref[...]
    n_pairs = D_IDX // LANES
    qi_rows = jnp.concatenate([qi[:, p * LANES:(p + 1) * LANES] for p in range(n_pairs)], axis=0)

    def score_chunk(c, carry):
        s_par = [_dot_nt(kib_ref[chunk_rows(c), par * LANES:(par + 1) * LANES], qi_rows) for par in range(2)]
        score = jnp.zeros((KEY_CHUNK, Q_BLOCK), jnp.float32)
        for h in range(N_IDX_HEADS):
            s_h = s_par[h % 2][:, (h // 2) * Q_BLOCK:(h // 2 + 1) * Q_BLOCK]
            score = score + jnp.maximum(s_h, 0.0) * w_t[h:h + 1, :]
        admissible = (c * KEY_CHUNK + kc_rows) <= q_pos
        keys_s[chunk_rows(c), :] = _sortable_key(jnp.where(admissible, score, NEG_INF))
        return carry

    lax.fori_loop(0, n_chunks, score_chunk, 0)

    def count_where(pred):
        def body(c, acc):
            blk = keys_s[chunk_rows(c), :]
            return acc + jnp.sum(jnp.where(pred(blk, c), 1.0, 0.0), axis=0, keepdims=True)
        return lax.fori_loop(0, n_chunks, body, jnp.zeros(lane_shape, jnp.float32))

    thr = _radix_threshold(lambda t: count_where(lambda blk, c: blk >= t), lane_shape)
    need = TOPK - count_where(lambda blk, c: blk > thr)
    cutoff = _tie_cutoff(
        lambda x: count_where(lambda blk, c: jnp.logical_and(blk == thr, (c * KEY_CHUNK + kc_rows) < x)),
        need, int(np.log2(SEQ)), lane_shape)

    def bias_chunk(c, carry):
        blk = keys_s[chunk_rows(c), :]
        idx = c * KEY_CHUNK + kc_rows
        sel = jnp.logical_or(blk > thr, jnp.logical_and(blk == thr, idx <= cutoff))
        sel = jnp.logical_and(sel, idx <= q_pos)
        bias_s[chunk_rows(c), :] = jnp.where(sel, 0.0, NEG_INF)
        return carry

    lax.fori_loop(0, n_chunks, bias_chunk, 0)

    q = q_ref[...]
    n_lanes = HEADS_PER_KV * Q_BLOCK
    for g in range(N_KV_HEADS):
        q_rows = jnp.concatenate(
            [q[:, (g * HEADS_PER_KV + h) * HEAD_DIM:(g * HEADS_PER_KV + h + 1) * HEAD_DIM]
             for h in range(HEADS_PER_KV)], axis=0)
        acc_s[...] = jnp.zeros((HEAD_DIM, n_lanes), jnp.float32)

        def attend_chunk(c, carry, g=g, q_rows=q_rows):
            m_old, l_old = carry
            kc = kb_ref[chunk_rows(c), g * HEAD_DIM:(g + 1) * HEAD_DIM]
            b = bias_s[chunk_rows(c), :]
            s = _dot_nt(kc, q_rows) + jnp.concatenate([b] * HEADS_PER_KV, axis=1)
            m_new = jnp.maximum(m_old, jnp.max(s, axis=0, keepdims=True))
            m_safe = jnp.where(m_new == NEG_INF, 0.0, m_new)
            p = jnp.exp2((s - m_safe) * SOFTMAX_SCALE_LOG2E)
            alpha = jnp.exp2((m_old - m_safe) * SOFTMAX_SCALE_LOG2E)
            l_new = alpha * l_old + jnp.sum(p, axis=0, keepdims=True)
            vt = vt_ref[c, g * HEAD_DIM:(g + 1) * HEAD_DIM, :]
            acc_s[...] = acc_s[...] * alpha + _dot(vt, _bf16(p))
            return m_new, l_new

        init = (jnp.full((1, n_lanes), NEG_INF, jnp.float32), jnp.zeros((1, n_lanes), jnp.float32))
        _, l_fin = lax.fori_loop(0, n_chunks, attend_chunk, init)
        o_t = acc_s[...] / l_fin
        for h in range(HEADS_PER_KV):
            col = (g * HEADS_PER_KV + h) * HEAD_DIM
            o_ref[:, col:col + HEAD_DIM] = o_t[:, h * Q_BLOCK:(h + 1) * Q_BLOCK].T


def _prompt_attention(q, qi, kiwi, kb, vt, kib):
    nq = SEQ // Q_BLOCK
    blk = lambda w: pl.BlockSpec((None, Q_BLOCK, w), lambda b, i: (b, i, 0))
    return pl.pallas_call(
        _attn_kernel,
        grid=(BATCH, nq),
        in_specs=[
            blk(D_ATT), blk(D_IDX), blk(LANES),
            pl.BlockSpec((None, SEQ, D_KV), lambda b, i: (b, 0, 0)),
            pl.BlockSpec((None, SEQ // KEY_CHUNK, D_KV, KEY_CHUNK), lambda b, i: (b, 0, 0, 0)),
            pl.BlockSpec((None, SEQ, 2 * LANES), lambda b, i: (b, 0, 0)),
        ],
        out_specs=blk(D_ATT),
        out_shape=jax.ShapeDtypeStruct((BATCH, SEQ, D_ATT), jnp.float32),
        scratch_shapes=[
            pltpu.VMEM((SEQ, Q_BLOCK), jnp.int32), pltpu.VMEM((SEQ, Q_BLOCK), jnp.float32),
            pltpu.VMEM((HEAD_DIM, HEADS_PER_KV * Q_BLOCK), jnp.float32),
        ],
        compiler_params=pltpu.CompilerParams(
            dimension_semantics=("arbitrary", "arbitrary"), vmem_limit_bytes=VMEM_LIMIT),
        name="prompt_attention",
    )(q, qi, kiwi, kb, vt, kib)


def _out_kernel(o_ref, sgb_ref, mpa_ref, smb_ref, x_ref, gate_ref, w_pb_ref, w_o_ref, gf_ref, y_ref):
    yb = _dot(_bf16(o_ref[...] * sgb_ref[...]), w_pb_ref[...])
    m = mpa_ref[...] + smb_ref[...] * yb
    r = x_ref[...] + gate_ref[...] * _dot(_bf16(m), w_o_ref[...])
    y_ref[...] = r * lax.rsqrt(jnp.mean(r * r, axis=-1, keepdims=True) + EPS) * gf_ref[...]


def _output_projection(o, sgb, mpa, smb, x, gate, wts, rows):
    ng, nr, _ = x.shape
    row_spec = pl.BlockSpec((None, rows, D_MODEL), lambda b, j: (b, j, 0))
    if gate.shape[1] == 1:
        gate_spec = pl.BlockSpec((None, 1, D_MODEL), lambda b, j: (b, 0, 0))
    else:
        gate_spec = row_spec
    return pl.pallas_call(
        _out_kernel,
        grid=(ng, nr // rows),
        in_specs=[row_spec] * 5 + [gate_spec, _const_spec((D_ATT, D_MODEL)), _const_spec((D_MODEL, D_MODEL)),
                                   _const_spec((1, D_MODEL))],
        out_specs=row_spec,
        out_shape=jax.ShapeDtypeStruct(x.shape, jnp.float32),
        compiler_params=pltpu.CompilerParams(
            dimension_semantics=("arbitrary", "arbitrary"), vmem_limit_bytes=VMEM_LIMIT),
        name="output_projection",
    )(o, sgb, mpa, smb, x, gate, wts["w_pb"], wts["w_o"], wts["g_final"])


def _sample_proj_kernel(x_ref, shift_ref, scale_ref, gn_ref, w_in_ref, wconv_ref, bconv_ref, w_ra_ref, b_ra_ref,
                        w_rx_ref, b_rx_ref, lam_ref, ig_ref, ib_ref, w_pa_ref,
                        cos_h_ref, sin_h_ref, cos_i_ref, sin_ia_ref, sin_ib_ref, buf_ref, h0_ref,
                        k_ref, v_ref, q_ref, qi_ref, kiwi_ref, mpa_ref, sgb_ref, smb_ref, conv_ref, lru_ref):
    xn = _modulated_norm(x_ref[...], gn_ref[...], scale_ref[...], shift_ref[...])
    xa = _dot(xn, w_in_ref[:, C_XA:C_XA + D_RNN])
    xc = bconv_ref[...]
    for t in range(CONV_W - 1):
        xc = xc + buf_ref[t] * wconv_ref[t:t + 1, :]
        if t > 0:
            conv_ref[t - 1] = buf_ref[t]
    xc = xc + xa * wconv_ref[CONV_W - 1:CONV_W, :]
    conv_ref[CONV_W - 2] = xa
    a, u = _lru_gates(xc, w_ra_ref, b_ra_ref[...], w_rx_ref, b_rx_ref[...], lam_ref[...])
    h = a * h0_ref[...] + u
    lru_ref[...] = h
    ga = _dot(xn, w_in_ref[:, C_GA:C_GA + D_RNN])
    ya = _dot(_bf16(h * _silu(ga)), w_pa_ref[...])
    mpa_ref[...] = _sigmoid(_dot(xn, w_in_ref[:, C_MA:C_MA + D_MODEL])) * ya
    sgb_ref[...] = _silu(_dot(xn, w_in_ref[:, C_GB:C_GB + D_ATT]))
    smb_ref[...] = _sigmoid(_dot(xn, w_in_ref[:, C_MB:C_MB + D_MODEL]))

    cos_h, sin_h = cos_h_ref[0:1, :], sin_h_ref[0:1, :]
    cos_i, sin_ia, sin_ib = cos_i_ref[0:1, :], sin_ia_ref[0:1, :], sin_ib_ref[0:1, :]
    zq = _dot(xn, w_in_ref[:, C_Q:C_Q + D_ATT])
    for hd in range(N_HEADS):
        sl = slice(hd * HEAD_DIM, (hd + 1) * HEAD_DIM)
        q_ref[:, sl] = _bf16(_rot_head(zq[:, sl], cos_h, sin_h))
    zk = _dot(xn, w_in_ref[:, C_K:C_K + D_KV])
    for g in range(N_KV_HEADS):
        sl = slice(g * HEAD_DIM, (g + 1) * HEAD_DIM)
        k_ref[:, sl] = _rot_head(zk[:, sl], cos_h, sin_h)
    v_ref[...] = _dot(xn, w_in_ref[:, C_V:C_V + D_KV])
    zqi = _dot(xn, w_in_ref[:, C_QI:C_QI + D_IDX])
    for p in range(D_IDX // LANES):
        sl = slice(p * LANES, (p + 1) * LANES)
        qi_ref[:, sl] = _bf16(_rot_idx(zqi[:, sl], cos_i, sin_ia, sin_ib))
    kiwi_ref[...] = _idx_key_slab(_dot(xn, w_in_ref[:, C_KW:C_KW + LANES]), ig_ref[...], ib_ref[...],
                                  cos_i, sin_ia, sin_ib)


def _sample_projection(x, shift, scale, wts, tabs, buf_t, h0):
    n = DEC_BATCH
    f32, bf16 = jnp.float32, jnp.bfloat16
    in_specs = [
        _const_spec((n, D_MODEL)), _const_spec((n, D_MODEL)), _const_spec((n, D_MODEL)), _const_spec((1, D_MODEL)),
        _const_spec((D_MODEL, D_IN_PACKED)), _const_spec((CONV_W, D_RNN)), _const_spec((1, D_RNN)),
        _const_spec((RNN_BLOCKS, RNN_BLOCK_W, RNN_BLOCK_W)), _const_spec((1, D_RNN)),
        _const_spec((RNN_BLOCKS, RNN_BLOCK_W, RNN_BLOCK_W)), _const_spec((1, D_RNN)), _const_spec((1, D_RNN)),
        _const_spec((1, LANES)), _const_spec((1, LANES)), _const_spec((D_RNN, D_MODEL)),
    ] + [_const_spec((SUBLANES, LANES))] * 5 + [_const_spec((CONV_W - 1, n, D_RNN)), _const_spec((n, D_RNN))]
    shapes = [
        ((n, D_KV), f32), ((n, D_KV), f32), ((n, D_ATT), bf16), ((n, D_IDX), bf16), ((n, LANES), f32),
        ((n, D_MODEL), f32), ((n, D_ATT), f32), ((n, D_MODEL), f32), ((CONV_W - 1, n, D_RNN), f32), ((n, D_RNN), f32),
    ]
    return pl.pallas_call(
        _sample_proj_kernel,
        grid=(1,),
        in_specs=in_specs,
        out_specs=[_const_spec(s, single=False) for s, _ in shapes],
        out_shape=[jax.ShapeDtypeStruct(s, d) for s, d in shapes],
        compiler_params=pltpu.CompilerParams(vmem_limit_bytes=VMEM_LIMIT),
        name="sample_projection",
    )(x, shift, scale, wts["g_norm"], wts["w_in"], wts["w_conv"], wts["b_conv"], wts["w_ra"], wts["b_ra"],
      wts["w_rx"], wts["b_rx"], wts["lam"], wts["idx_g"], wts["idx_b"], wts["w_pa"], *tabs, buf_t, h0)


def _self_page(row_vec, n_cols):
    r = lax.broadcasted_iota(jnp.int32, (PAGE_SIZE, n_cols), 0)
    return jnp.where(r == 0, jnp.broadcast_to(row_vec, (PAGE_SIZE, n_cols)), 0.0)


def _sample_score_kernel(pt_ref, page_ref, qi_ref, w_ref, kinew_ref, o_ref):
    p = pl.program_id(1)
    is_self = p == N_PAGES
    page = jnp.where(is_self, _self_page(kinew_ref[...][:, :IDX_DIM], IDX_DIM), page_ref[...])
    s = _dot_nt(qi_ref[...], _bf16(page))
    score = jnp.sum(jnp.maximum(s, 0.0) * w_ref[...], axis=0, keepdims=True)
    lane = lax.broadcasted_iota(jnp.int32, (1, PAGE_SIZE), 1)
    o_ref[...] = jnp.where(jnp.logical_and(is_self, lane >= 1), NEG_INF, score)


def _sample_scores(page_table_flat, cache_idx_k, qi3, w_col, kiwi3):
    page_map = lambda b, p, pt: (pt[b * N_PAGES + jnp.minimum(p, N_PAGES - 1)], 0, 0)
    return pl.pallas_call(
        _sample_score_kernel,
        grid_spec=pltpu.PrefetchScalarGridSpec(
            num_scalar_prefetch=1,
            grid=(DEC_BATCH, N_PAGES + 1),
            in_specs=[
                pl.BlockSpec((None, PAGE_SIZE, IDX_DIM), page_map),
                pl.BlockSpec((None, N_IDX_HEADS, IDX_DIM), lambda b, p, pt: (b, 0, 0)),
                pl.BlockSpec((None, N_IDX_HEADS, 1), lambda b, p, pt: (b, 0, 0)),
                pl.BlockSpec((None, 1, LANES), lambda b, p, pt: (b, 0, 0)),
            ],
            out_specs=pl.BlockSpec((None, None, 1, PAGE_SIZE), lambda b, p, pt: (b, p, 0, 0)),
        ),
        out_shape=jax.ShapeDtypeStruct((DEC_BATCH, N_PAGES + 1, 1, PAGE_SIZE), jnp.float32),
        compiler_params=pltpu.CompilerParams(dimension_semantics=("arbitrary", "arbitrary")),
        name="sample_scores",
    )(page_table_flat, cache_idx_k, qi3, w_col, kiwi3)


def _sample_select_kernel(score_ref, bias_ref):
    keys = _sortable_key(score_ref[...])
    idx = lax.broadcasted_iota(jnp.int32, keys.shape, 1)
    col_shape = (keys.shape[0], 1)
    count = lambda pred: jnp.sum(jnp.where(pred, 1.0, 0.0), axis=1, keepdims=True)
    thr = _radix_threshold(lambda t: count(keys >= t), col_shape)
    need = TOPK - count(keys > thr)
    tie = keys == thr
    n_bits = int(np.ceil(np.log2(keys.shape[1])))
    cutoff = _tie_cutoff(lambda x: count(jnp.logical_and(tie, idx < x)), need, n_bits, col_shape)
    sel = jnp.logical_or(keys > thr, jnp.logical_and(tie, idx <= cutoff))
    bias_ref[...] = jnp.where(sel, 0.0, NEG_INF)


def _sample_select(scores):
    return pl.pallas_call(
        _sample_select_kernel,
        grid=(1,),
        in_specs=[_const_spec(scores.shape)],
        out_specs=_const_spec(scores.shape, single=False),
        out_shape=jax.ShapeDtypeStruct(scores.shape, jnp.float32),
        compiler_params=pltpu.CompilerParams(vmem_limit_bytes=VMEM_LIMIT),
        name="sample_select",
    )(scores)


def _sample_attn_kernel(pt_ref, kpage_ref, vpage_ref, bias_ref, q_ref, knew_ref, vnew_ref, o_ref, m_s, l_s, acc_s):
    p = pl.program_id(1)
    is_self = p == N_PAGES

    @pl.when(p == 0)
    def _():
        m_s[...] = jnp.full(m_s.shape, NEG_INF, jnp.float32)
        l_s[...] = jnp.zeros(l_s.shape, jnp.float32)
        acc_s[...] = jnp.zeros(acc_s.shape, jnp.float32)

    kpage = _bf16(jnp.where(is_self, _self_page(knew_ref[...], D_KV), kpage_ref[...]))
    vpage = _bf16(jnp.where(is_self, _self_page(vnew_ref[...], D_KV), vpage_ref[...]))
    q = q_ref[...]
    head = lax.broadcasted_iota(jnp.int32, (N_HEADS, PAGE_SIZE), 0)

    def own_group(per_group):
        out = per_group[N_KV_HEADS - 1]
        for g in range(N_KV_HEADS - 2, -1, -1):
            out = jnp.where(head < (g + 1) * HEADS_PER_KV, per_group[g], out)
        return out

    s = own_group([_dot_nt(q, kpage[:, g * HEAD_DIM:(g + 1) * HEAD_DIM]) for g in range(N_KV_HEADS)])
    s = s + bias_ref[...]
    m_old = m_s[...]
    m_new = jnp.maximum(m_old, jnp.max(s, axis=1, keepdims=True))
    m_safe = jnp.where(m_new == NEG_INF, 0.0, m_new)
    pr = jnp.exp2((s - m_safe) * SOFTMAX_SCALE_LOG2E)
    alpha = jnp.exp2((m_old - m_safe) * SOFTMAX_SCALE_LOG2E)
    l_s[...] = alpha * l_s[...] + jnp.sum(pr, axis=1, keepdims=True)
    prb = _bf16(pr)
    pv = own_group([_dot(prb, vpage[:, g * HEAD_DIM:(g + 1) * HEAD_DIM]) for g in range(N_KV_HEADS)])
    acc_s[...] = acc_s[...] * alpha + pv
    m_s[...] = m_new

    @pl.when(is_self)
    def _():
        o_ref[...] = acc_s[...] / l_s[...]


def _sample_attention(page_table_flat, cache_k, cache_v, bias4, q3, knew3, vnew3):
    page_map = lambda b, p, pt: (pt[b * N_PAGES + jnp.minimum(p, N_PAGES - 1)], 0, 0)
    per_sample = lambda r, w: pl.BlockSpec((None, r, w), lambda b, p, pt: (b, 0, 0))
    return pl.pallas_call(
        _sample_attn_kernel,
        grid_spec=pltpu.PrefetchScalarGridSpec(
            num_scalar_prefetch=1,
            grid=(DEC_BATCH, N_PAGES + 1),
            in_specs=[
                pl.BlockSpec((None, PAGE_SIZE, D_KV), page_map),
                pl.BlockSpec((None, PAGE_SIZE, D_KV), page_map),
                pl.BlockSpec((None, None, 1, PAGE_SIZE), lambda b, p, pt: (b, p, 0, 0)),
                per_sample(N_HEADS, HEAD_DIM), per_sample(1, D_KV), per_sample(1, D_KV),
            ],
            out_specs=per_sample(N_HEADS, HEAD_DIM),
            scratch_shapes=[pltpu.VMEM((N_HEADS, 1), jnp.float32), pltpu.VMEM((N_HEADS, 1), jnp.float32),
                            pltpu.VMEM((N_HEADS, HEAD_DIM), jnp.float32)],
        ),
        out_shape=jax.ShapeDtypeStruct((DEC_BATCH, N_HEADS, HEAD_DIM), jnp.float32),
        compiler_params=pltpu.CompilerParams(dimension_semantics=("arbitrary", "arbitrary")),
        name="sample_attention",
    )(page_table_flat, cache_k, cache_v, bias4, q3, knew3, vnew3)


def _pack_w_in(w_in):
    xa, ga, q, k, v, gb, qi, ki, wi, ma, mb = jnp.split(w_in, np.cumsum(SPLITS)[:-1].tolist(), axis=-1)
    pad = jnp.zeros((D_MODEL, LANES - IDX_DIM - N_IDX_HEADS), w_in.dtype)
    return _bf16(jnp.concatenate([xa, ga, q, gb, ma, mb, k, v, qi, ki, wi, pad], axis=-1))


def _lane_pad(v):
    return jnp.pad(v.reshape(1, -1), ((0, 0), (0, LANES - v.shape[-1])))


def kernel(x_prompt, x_sample, cache_k, cache_v, cache_idx_k, state_conv, state_rglru, page_table, c_prompt, c_sample, w_ada, b_ada, g_norm, w_in, w_conv, b_conv, w_ra, b_ra, w_rx, b_rx, lru_lambda, idx_k_norm_g, idx_k_norm_b, w_pa, w_pb, w_o, g_final):
    assert w_in.shape[0] == 1, "one layer"
    wts = {
        "g_norm": g_norm[0].reshape(1, -1), "w_in": _pack_w_in(w_in[0]), "w_conv": w_conv[0],
        "b_conv": b_conv[0].reshape(1, -1), "w_ra": _bf16(w_ra[0]), "b_ra": b_ra[0].reshape(1, -1),
        "w_rx": _bf16(w_rx[0]), "b_rx": b_rx[0].reshape(1, -1), "lam": lru_lambda[0].reshape(1, -1),
        "idx_g": _lane_pad(idx_k_norm_g[0]), "idx_b": _lane_pad(idx_k_norm_b[0]),
        "w_pa": _bf16(w_pa[0]), "w_pb": _bf16(w_pb[0]), "w_o": _bf16(w_o[0]), "g_final": g_final.reshape(1, -1),
    }
    half_h, half_i = HEAD_DIM // 2, IDX_DIM // 2
    invf_h = ROPE_THETA ** (-jnp.arange(half_h, dtype=jnp.float32) / half_h)
    invf_i = ROPE_THETA ** (-jnp.arange(half_i, dtype=jnp.float32) / half_i)
    invf = jnp.zeros((SUBLANES, LANES), jnp.float32)
    invf = invf.at[0].set(jnp.tile(invf_h, LANES // half_h)).at[1].set(jnp.tile(invf_i, LANES // half_i))
    tabs_prompt = _rope_tables(invf, SEQ, 0, 1)
    tabs_sample = _rope_tables(invf, SUBLANES, PAST_LEN, 0)

    mod = _ada_modulation(jnp.concatenate([c_prompt, c_sample], axis=0), _bf16(w_ada[0]), b_ada[0].reshape(1, -1))
    shift, scale, gate = mod[:, :D_MODEL], mod[:, D_MODEL:2 * D_MODEL], mod[:, 2 * D_MODEL:]

    (k_p, v_p, ki_p, kb, vt, kib, q, qi, kiwi, mpa, sgb, smb, conv_p, lru_p) = _prompt_projection(
        x_prompt, shift[:BATCH, None, :], scale[:BATCH, None, :], wts, tabs_prompt)
    o = _prompt_attention(q, qi, kiwi, kb, vt, kib)
    y_prompt = _output_projection(o, sgb, mpa, smb, x_prompt, gate[:BATCH, None, :], wts, OUT_ROWS)

    xs = x_sample[:, 0, :]
    (k_s, v_s, q_s, qi_s, kiwi_s, mpa_s, sgb_s, smb_s, conv_s, lru_s) = _sample_projection(
        xs, shift[BATCH:], scale[BATCH:], wts, tabs_sample, jnp.swapaxes(state_conv[0], 0, 1), state_rglru[0])
    pt_flat = page_table.reshape(-1)
    w_col = kiwi_s[:, IDX_DIM:IDX_DIM + N_IDX_HEADS, None]
    scores = _sample_scores(pt_flat, cache_idx_k[0], qi_s.reshape(DEC_BATCH, N_IDX_HEADS, IDX_DIM), w_col,
                            kiwi_s[:, None, :])
    bias = _sample_select(scores.reshape(DEC_BATCH, (N_PAGES + 1) * PAGE_SIZE))
    o_s = _sample_attention(
        pt_flat, cache_k[0].reshape(-1, PAGE_SIZE, D_KV), cache_v[0].reshape(-1, PAGE_SIZE, D_KV),
        bias.reshape(DEC_BATCH, N_PAGES + 1, 1, PAGE_SIZE), q_s.reshape(DEC_BATCH, N_HEADS, HEAD_DIM),
        k_s[:, None, :], v_s[:, None, :])
    y_sample = _output_projection(
        o_s.reshape(1, DEC_BATCH, D_ATT), sgb_s[None], mpa_s[None], smb_s[None], xs[None], gate[None, BATCH:],
        wts, DEC_BATCH)

    kv_p = lambda t: t.reshape(1, BATCH, SEQ, N_KV_HEADS, HEAD_DIM)
    kv_s = lambda t: t.reshape(1, DEC_BATCH, 1, N_KV_HEADS, HEAD_DIM)
    return (
        y_prompt, y_sample.reshape(DEC_BATCH, 1, D_MODEL),
        kv_p(k_p), kv_p(v_p), ki_p[None], conv_p[None], lru_p.reshape(1, BATCH, D_RNN),
        kv_s(k_s), kv_s(v_s), kiwi_s[:, :IDX_DIM].reshape(1, DEC_BATCH, 1, IDX_DIM),
        jnp.swapaxes(conv_s, 0, 1)[None], lru_s[None],
    )
```

```python
import functools

import jax
import jax.numpy as jnp
import numpy as np
from jax import lax
from jax.experimental import pallas as pl
from jax.experimental.pallas import tpu as pltpu

D_MODEL = 1024
BATCH = 8
SEQ = 4096
DEC_BATCH = 32
PAST_LEN = 16384
PAGE_SIZE = 128
N_PAGES = PAST_LEN // PAGE_SIZE
D_RNN = D_MODEL
RNN_BLOCKS = 4
RNN_BLOCK_W = D_RNN // RNN_BLOCKS
CONV_W = 4
LRU_C = 8.0
N_HEADS = 8
HEAD_DIM = 128
N_KV_HEADS = 2
HEADS_PER_KV = N_HEADS // N_KV_HEADS
D_ATT = N_HEADS * HEAD_DIM
D_KV = N_KV_HEADS * HEAD_DIM
N_IDX_HEADS = 8
IDX_DIM = 64
D_IDX = N_IDX_HEADS * IDX_DIM
IDX_W_SCALE = (N_IDX_HEADS * IDX_DIM) ** -0.5
TOPK = 256
ROPE_THETA = 10000.0
EPS = 1e-6
SPLITS = (D_RNN, D_RNN, D_ATT, D_KV, D_KV, D_ATT, D_IDX, IDX_DIM, N_IDX_HEADS, D_MODEL, D_MODEL)

LANES = 128
SUBLANES = 8

C_XA, C_GA, C_Q, C_GB, C_MA, C_MB = 0, 1024, 2048, 3072, 4096, 5120
C_K, C_V, C_QI, C_KW = 6144, 6400, 6656, 7168
D_IN_PACKED = 7296

PROJ_ROWS = 256
Q_BLOCK = 128
KEY_CHUNK = 256
ATT_CHUNK = 512
PACKED_ROWS = 16
HALF_RANGE = 1 << 15
assert KEY_CHUNK >= TOPK and ATT_CHUNK % KEY_CHUNK == 0 and SEQ % ATT_CHUNK == 0
assert PAGE_SIZE == HEAD_DIM == LANES
OUT_ROWS = 512
SCORE_PAGES = 16
ATTN_PAGES = 16
assert N_PAGES % SCORE_PAGES == 0 and N_PAGES % ATTN_PAGES == 0
SOFTMAX_SCALE_LOG2E = (HEAD_DIM ** -0.5) * float(np.log2(np.e))
NEG_INF = float("-inf")
INT_MIN = -2 ** 31
VMEM_LIMIT = 56 * 1024 * 1024


def _sigmoid(x):
    return 1.0 / (1.0 + jnp.exp(-x))


def _silu(x):
    return x * _sigmoid(x)


def _dot(a, b):
    return jnp.dot(a, b, preferred_element_type=jnp.float32)


def _dot_nt(a, b):
    return lax.dot_general(a, b, (((1,), (1,)), ((), ())), preferred_element_type=jnp.float32)


def _bf16(x):
    return x.astype(jnp.bfloat16)


def _const_spec(shape, single=True):
    nd = len(shape)
    kwargs = {"pipeline_mode": pl.Buffered(1)} if single else {}
    return pl.BlockSpec(shape, lambda *_: (0,) * nd, **kwargs)


def _rope_kernel(invf_ref, cos_h_ref, sin_h_ref, cos_i_ref, sin_ia_ref, sin_ib_ref, *, pos0, pos_step, rows):
    r0 = pl.program_id(0) * rows
    row = lax.broadcasted_iota(jnp.int32, (rows, LANES), 0) + r0
    lane = lax.broadcasted_iota(jnp.int32, (rows, LANES), 1)
    pos = (pos0 + pos_step * row).astype(jnp.float32)
    ang_h = pos * invf_ref[0:1, :]
    ang_i = pos * invf_ref[1:2, :]
    cos_h_ref[...] = jnp.cos(ang_h)
    sh = jnp.sin(ang_h)
    sin_h_ref[...] = jnp.where(lane < HEAD_DIM // 2, -sh, sh)
    cos_i_ref[...] = jnp.cos(ang_i)
    si = jnp.sin(ang_i)
    first_half = (lane % IDX_DIM) < IDX_DIM // 2
    sin_ia_ref[...] = jnp.where(first_half, -si, 0.0)
    sin_ib_ref[...] = jnp.where(first_half, 0.0, si)


def _rope_tables(invf, n, pos0, pos_step):
    rows = min(n, 512)
    out = jax.ShapeDtypeStruct((n, LANES), jnp.float32)
    spec = pl.BlockSpec((rows, LANES), lambda i: (i, 0))
    return pl.pallas_call(
        functools.partial(_rope_kernel, pos0=pos0, pos_step=pos_step, rows=rows),
        grid=(n // rows,),
        in_specs=[pl.BlockSpec((SUBLANES, LANES), lambda i: (0, 0))],
        out_specs=[spec] * 5,
        out_shape=[out] * 5,
        name="rope_tables",
    )(invf)


def _rot_head(z, cos, sin_signed):
    return z * cos + pltpu.roll(z, HEAD_DIM // 2, 1) * sin_signed


def _rot_idx(z, cos, sin_a, sin_b):
    return z * cos + pltpu.roll(z, LANES - IDX_DIM // 2, 1) * sin_a + pltpu.roll(z, IDX_DIM // 2, 1) * sin_b


def _ada_kernel(c_ref, w_ref, b_ref, o_ref):
    o_ref[...] = _dot(_bf16(_silu(c_ref[...])), w_ref[...]) + b_ref[...]


def _ada_modulation(c_all, w_ada, b_ada):
    n = c_all.shape[0]
    return pl.pallas_call(
        _ada_kernel,
        grid=(1,),
        in_specs=[_const_spec((n, D_MODEL)), _const_spec((D_MODEL, 3 * D_MODEL)), _const_spec((1, 3 * D_MODEL))],
        out_specs=_const_spec((n, 3 * D_MODEL), single=False),
        out_shape=jax.ShapeDtypeStruct((n, 3 * D_MODEL), jnp.float32),
        compiler_params=pltpu.CompilerParams(vmem_limit_bytes=VMEM_LIMIT),
        name="ada_modulation",
    )(c_all, w_ada, b_ada)


def _modulated_norm(x, g, scale, shift):
    y = x * lax.rsqrt(jnp.mean(x * x, axis=-1, keepdims=True) + EPS) * g
    return _bf16(y * (1.0 + scale) + shift)


def _lru_gates(xc, w_ra_ref, b_ra, w_rx_ref, b_rx, lam):
    xcb = _bf16(xc)
    r_parts, i_parts = [], []
    for n in range(RNN_BLOCKS):
        sl = slice(n * RNN_BLOCK_W, (n + 1) * RNN_BLOCK_W)
        r_parts.append(_dot(xcb[:, sl], w_ra_ref[n]))
        i_parts.append(_dot(xcb[:, sl], w_rx_ref[n]))
    r = _sigmoid(jnp.concatenate(r_parts, axis=1) + b_ra)
    i = _sigmoid(jnp.concatenate(i_parts, axis=1) + b_rx)
    neg_lam = -lam
    softplus = jnp.maximum(neg_lam, 0.0) + jnp.log1p(jnp.exp(-jnp.abs(neg_lam)))
    log_a = (-LRU_C) * r * softplus
    a = jnp.exp(log_a)
    u = jnp.sqrt(-jnp.tanh(log_a) * (a * a + 1.0)) * (i * xc)
    return a, u


def _idx_key_slab(z_kw, g, b, cos_i, sin_ia, sin_ib):
    lane = lax.broadcasted_iota(jnp.int32, z_kw.shape, 1)
    is_key = lane < IDX_DIM
    mu = jnp.sum(jnp.where(is_key, z_kw, 0.0), axis=-1, keepdims=True) * (1.0 / IDX_DIM)
    d = jnp.where(is_key, z_kw - mu, 0.0)
    var = jnp.sum(d * d, axis=-1, keepdims=True) * (1.0 / IDX_DIM)
    y = d * lax.rsqrt(var + EPS) * g + b
    key = _rot_idx(y, cos_i, sin_ia, sin_ib)
    is_w = jnp.logical_and(lane >= IDX_DIM, lane < IDX_DIM + N_IDX_HEADS)
    return key + jnp.where(is_w, z_kw * IDX_W_SCALE, 0.0)


def _proj_kernel(x_ref, shift_ref, scale_ref, gn_ref, w_in_ref, wconv_ref, bconv_ref, w_ra_ref, b_ra_ref,
                 w_rx_ref, b_rx_ref, lam_ref, ig_ref, ib_ref, w_pa_ref,
                 cos_h_ref, sin_h_ref, cos_i_ref, sin_ia_ref, sin_ib_ref,
                 k_ref, v_ref, ki_ref, kb_ref, vt_ref, kib_ref, q_ref, qi_ref, kiwi_ref,
                 mpa_ref, sgb_ref, smb_ref, conv_ref, lru_ref,
                 xa_ext, a_s, u_s, h_carry):
    ts = PROJ_ROWS
    j = pl.program_id(1)
    last = pl.num_programs(1) - 1

    @pl.when(j == 0)
    def _():
        xa_ext[0:SUBLANES, :] = jnp.zeros((SUBLANES, D_RNN), jnp.float32)
        h_carry[...] = jnp.zeros((1, D_RNN), jnp.float32)

    xn = _modulated_norm(x_ref[...], gn_ref[...], scale_ref[...], shift_ref[...])

    xa_ext[SUBLANES:SUBLANES + ts, :] = _dot(xn, w_in_ref[:, C_XA:C_XA + D_RNN])
    xc = bconv_ref[...]
    for t in range(CONV_W):
        off = SUBLANES - (CONV_W - 1) + t
        xc = xc + xa_ext[off:off + ts, :] * wconv_ref[t:t + 1, :]

    @pl.when(j == last)
    def _():
        conv_ref[...] = xa_ext[ts + SUBLANES - (CONV_W - 1):ts + SUBLANES, :]

    xa_ext[0:SUBLANES, :] = xa_ext[ts:ts + SUBLANES, :]

    a, u = _lru_gates(xc, w_ra_ref, b_ra_ref[...], w_rx_ref, b_rx_ref[...], lam_ref[...])
    a_s[...] = a
    u_s[...] = u
    row = lax.broadcasted_iota(jnp.int32, (SUBLANES, D_RNN), 0)

    def scan_group(g, hc):
        r0 = pl.multiple_of(g * SUBLANES, SUBLANES)
        a8 = a_s[pl.ds(r0, SUBLANES), :]
        u8 = u_s[pl.ds(r0, SUBLANES), :]
        for d in (1, 2, 4):
            keep = row >= d
            u8 = jnp.where(keep, a8 * pltpu.roll(u8, d, 0) + u8, u8)
            a8 = jnp.where(keep, a8 * pltpu.roll(a8, d, 0), a8)
        h8 = a8 * hc + u8
        u_s[pl.ds(r0, SUBLANES), :] = h8
        return h8[SUBLANES - 1:SUBLANES, :]

    hc = lax.fori_loop(0, ts // SUBLANES, scan_group, h_carry[...])
    h_carry[...] = hc

    @pl.when(j == last)
    def _():
        lru_ref[...] = hc

    ga = _dot(xn, w_in_ref[:, C_GA:C_GA + D_RNN])
    ya = _dot(_bf16(u_s[...] * _silu(ga)), w_pa_ref[...])
    ma = _dot(xn, w_in_ref[:, C_MA:C_MA + D_MODEL])
    mpa_ref[...] = _sigmoid(ma) * ya
    sgb_ref[...] = _silu(_dot(xn, w_in_ref[:, C_GB:C_GB + D_ATT]))
    smb_ref[...] = _sigmoid(_dot(xn, w_in_ref[:, C_MB:C_MB + D_MODEL]))

    cos_h, sin_h = cos_h_ref[...], sin_h_ref[...]
    cos_i, sin_ia, sin_ib = cos_i_ref[...], sin_ia_ref[...], sin_ib_ref[...]
    zq = _dot(xn, w_in_ref[:, C_Q:C_Q + D_ATT])
    for h in range(N_HEADS):
        sl = slice(h * HEAD_DIM, (h + 1) * HEAD_DIM)
        q_ref[:, sl] = _bf16(_rot_head(zq[:, sl], cos_h, sin_h))
    zk = _dot(xn, w_in_ref[:, C_K:C_K + D_KV])
    for g in range(N_KV_HEADS):
        sl = slice(g * HEAD_DIM, (g + 1) * HEAD_DIM)
        kr = _rot_head(zk[:, sl], cos_h, sin_h)
        k_ref[:, sl] = kr
        kb_ref[:, sl] = _bf16(kr)
    zv = _dot(xn, w_in_ref[:, C_V:C_V + D_KV])
    v_ref[...] = zv
    vt = _bf16(zv.T)
    for c in range(ts // KEY_CHUNK):
        vt_ref[c] = vt[:, c * KEY_CHUNK:(c + 1) * KEY_CHUNK]
    zqi = _dot(xn, w_in_ref[:, C_QI:C_QI + D_IDX])
    for p in range(D_IDX // LANES):
        sl = slice(p * LANES, (p + 1) * LANES)
        qi_ref[:, sl] = _bf16(_rot_idx(zqi[:, sl], cos_i, sin_ia, sin_ib))
    slab = _idx_key_slab(_dot(xn, w_in_ref[:, C_KW:C_KW + LANES]), ig_ref[...], ib_ref[...], cos_i, sin_ia, sin_ib)
    kiwi_ref[...] = slab
    ki_ref[...] = slab.T[:IDX_DIM, :]
    key_even = jnp.where(lax.broadcasted_iota(jnp.int32, slab.shape, 1) < IDX_DIM, slab, 0.0)
    kib_ref[:, 0:LANES] = _bf16(key_even)
    kib_ref[:, LANES:2 * LANES] = _bf16(pltpu.roll(key_even, IDX_DIM, 1))


def _prompt_projection(x, shift, scale, wts, tabs):
    ts = PROJ_ROWS
    nt = SEQ // ts
    f32, bf16 = jnp.float32, jnp.bfloat16
    row_spec = lambda w: pl.BlockSpec((None, ts, w), lambda b, j: (b, j, 0))
    bvec_spec = pl.BlockSpec((None, 1, D_MODEL), lambda b, j: (b, 0, 0))
    tab_spec = pl.BlockSpec((ts, LANES), lambda b, j: (j, 0))
    in_specs = [
        row_spec(D_MODEL), bvec_spec, bvec_spec, _const_spec((1, D_MODEL)),
        _const_spec((D_MODEL, D_IN_PACKED)), _const_spec((CONV_W, D_RNN)), _const_spec((1, D_RNN)),
        _const_spec((RNN_BLOCKS, RNN_BLOCK_W, RNN_BLOCK_W)), _const_spec((1, D_RNN)),
        _const_spec((RNN_BLOCKS, RNN_BLOCK_W, RNN_BLOCK_W)), _const_spec((1, D_RNN)), _const_spec((1, D_RNN)),
        _const_spec((1, LANES)), _const_spec((1, LANES)), _const_spec((D_RNN, D_MODEL)),
    ] + [tab_spec] * 5
    out_shape = [
        jax.ShapeDtypeStruct((BATCH, SEQ, D_KV), f32),
        jax.ShapeDtypeStruct((BATCH, SEQ, D_KV), f32),
        jax.ShapeDtypeStruct((BATCH, IDX_DIM, SEQ), f32),
        jax.ShapeDtypeStruct((BATCH, SEQ, D_KV), bf16),
        jax.ShapeDtypeStruct((BATCH, SEQ // KEY_CHUNK, D_KV, KEY_CHUNK), bf16),
        jax.ShapeDtypeStruct((BATCH, SEQ, 2 * LANES), bf16),
        jax.ShapeDtypeStruct((BATCH, SEQ, D_ATT), bf16),
        jax.ShapeDtypeStruct((BATCH, SEQ, D_IDX), bf16),
        jax.ShapeDtypeStruct((BATCH, SEQ, LANES), f32),
        jax.ShapeDtypeStruct((BATCH, SEQ, D_MODEL), f32),
        jax.ShapeDtypeStruct((BATCH, SEQ, D_ATT), f32),
        jax.ShapeDtypeStruct((BATCH, SEQ, D_MODEL), f32),
        jax.ShapeDtypeStruct((BATCH, CONV_W - 1, D_RNN), f32),
        jax.ShapeDtypeStruct((BATCH, 1, D_RNN), f32),
    ]
    out_specs = [
        row_spec(D_KV), row_spec(D_KV), pl.BlockSpec((None, IDX_DIM, ts), lambda b, j: (b, 0, j)), row_spec(D_KV),
        pl.BlockSpec((None, ts // KEY_CHUNK, D_KV, KEY_CHUNK), lambda b, j: (b, j, 0, 0)),
        row_spec(2 * LANES), row_spec(D_ATT), row_spec(D_IDX), row_spec(LANES),
        row_spec(D_MODEL), row_spec(D_ATT), row_spec(D_MODEL),
        pl.BlockSpec((None, CONV_W - 1, D_RNN), lambda b, j: (b, 0, 0)),
        pl.BlockSpec((None, 1, D_RNN), lambda b, j: (b, 0, 0)),
    ]
    scratch = [
        pltpu.VMEM((ts + SUBLANES, D_RNN), f32), pltpu.VMEM((ts, D_RNN), f32),
        pltpu.VMEM((ts, D_RNN), f32), pltpu.VMEM((1, D_RNN), f32),
    ]
    return pl.pallas_call(
        _proj_kernel,
        grid=(BATCH, nt),
        in_specs=in_specs, out_specs=out_specs, out_shape=out_shape, scratch_shapes=scratch,
        compiler_params=pltpu.CompilerParams(
            dimension_semantics=("arbitrary", "arbitrary"), vmem_limit_bytes=VMEM_LIMIT),
        name="prompt_projection",
    )(x, shift, scale, wts["g_norm"], wts["w_in"], wts["w_conv"], wts["b_conv"], wts["w_ra"], wts["b_ra"],
      wts["w_rx"], wts["b_rx"], wts["lam"], wts["idx_g"], wts["idx_b"], wts["w_pa"], *tabs)


def _sortable_key(score):
    bits = pltpu.bitcast(jnp.where(score == 0.0, 0.0, score), jnp.int32)
    return bits ^ jnp.bitwise_and(jnp.right_shift(bits, 31), jnp.int32(0x7FFFFFFF))


def _greedy_bits(count_ge, n_bits, shape, count_all):
    def bit_step(b, carry):
        t, cnt = carry
        cand = jnp.bitwise_or(t, jnp.left_shift(jnp.int32(1), n_bits - 1 - b))
        c = count_ge(cand)
        ok = c >= TOPK
        return jnp.where(ok, cand, t), jnp.where(ok, c, cnt)
    return lax.fori_loop(0, n_bits, bit_step, (jnp.zeros(shape, jnp.int32), count_all))


def _fold_rows(x, rows):
    parts = [x[r:r + rows] for r in range(0, x.shape[0], rows)]
    while len(parts) > 1:
        parts = [parts[k] + parts[k + 1] for k in range(0, len(parts) - 1, 2)] + (
            [parts[-1]] if len(parts) % 2 else [])
    return parts[0]


def _tie_cutoff(count_tie_below, need, n_bits, shape):
    def bit_step(b, x):
        cand = jnp.bitwise_or(x, jnp.left_shift(jnp.int32(1), n_bits - 1 - b))
        return jnp.where(count_tie_below(cand) < need, cand, x)
    return lax.fori_loop(0, n_bits, bit_step, jnp.zeros(shape, jnp.int32))


def _attn_kernel(q_ref, qi_ref, kiwi_ref, kb_ref, vt_ref, kib_ref, o_ref,
                 keys_s, hi_s, lo_s, bias_s, cut_s, acc_s):
    i = pl.program_id(1)
    t0 = i * Q_BLOCK
    n_steps = (t0 + Q_BLOCK + ATT_CHUNK - 1) // ATT_CHUNK
    step_iota = lax.broadcasted_iota(jnp.int32, (ATT_CHUNK, Q_BLOCK), 0)
    sub_iota = lax.broadcasted_iota(jnp.int32, (KEY_CHUNK, Q_BLOCK), 0)
    q_pos = t0 + lax.broadcasted_iota(jnp.int32, (1, Q_BLOCK), 1)
    lane_shape = (1, Q_BLOCK)

    def step_rows(c):
        return pl.ds(pl.multiple_of(c * ATT_CHUNK, ATT_CHUNK), ATT_CHUNK)

    w_t = kiwi_ref[...].T[IDX_DIM:IDX_DIM + N_IDX_HEADS, :]
    qi = qi_ref[...]
    n_pairs = D_IDX // LANES
    qi_rows = jnp.concatenate([qi[:, p * LANES:(p + 1) * LANES] for p in range(n_pairs)], axis=0)

    def score_step(c, carry):
        for sub in range(ATT_CHUNK // KEY_CHUNK):
            r0 = pl.multiple_of(c * ATT_CHUNK + sub * KEY_CHUNK, KEY_CHUNK)
            rows = pl.ds(r0, KEY_CHUNK)
            s_par = [_dot_nt(kib_ref[rows, par * LANES:(par + 1) * LANES], qi_rows) for par in range(2)]
            score = jnp.zeros((KEY_CHUNK, Q_BLOCK), jnp.float32)
            for h in range(N_IDX_HEADS):
                s_h = s_par[h % 2][:, (h // 2) * Q_BLOCK:(h // 2 + 1) * Q_BLOCK]
                score = score + jnp.maximum(s_h, 0.0) * w_t[h:h + 1, :]
            key = _sortable_key(jnp.where((r0 + sub_iota) <= q_pos, score, NEG_INF))
            keys_s[rows, :] = key
            hi_s[rows, :] = jnp.right_shift(key, 16).astype(jnp.int16)
        return carry

    lax.fori_loop(0, n_steps, score_step, 0)

    def count16(ref, t_unsigned):
        t16 = (t_unsigned - HALF_RANGE).astype(jnp.int16)
        def body(c, acc):
            one = jnp.where(ref[step_rows(c), :] >= t16, jnp.int16(1), jnp.int16(0))
            return acc + _fold_rows(one, PACKED_ROWS)
        acc = lax.fori_loop(0, n_steps, body, jnp.zeros((PACKED_ROWS, Q_BLOCK), jnp.int16))
        return jnp.sum(acc.astype(jnp.float32), axis=0, keepdims=True)

    count_all = jnp.full(lane_shape, 1.0, jnp.float32) * (n_steps * ATT_CHUNK).astype(jnp.float32)
    hi_u, cnt_hi = _greedy_bits(lambda t: count16(hi_s, t), 16, lane_shape, count_all)
    hi_t = hi_u - HALF_RANGE

    def low_half_step(c, carry):
        key = keys_s[step_rows(c), :]
        hi = jnp.right_shift(key, 16)
        lo = jnp.bitwise_and(key, 0xFFFF) - HALF_RANGE
        lo = jnp.where(hi == hi_t, lo, jnp.where(hi > hi_t, HALF_RANGE - 1, -HALF_RANGE))
        lo_s[step_rows(c), :] = lo.astype(jnp.int16)
        return carry

    lax.fori_loop(0, n_steps, low_half_step, 0)
    lo_u, cnt_ge = _greedy_bits(lambda t: count16(lo_s, t), 16, lane_shape, cnt_hi)
    thr = hi_t * (2 * HALF_RANGE) + lo_u

    cut_s[...] = jnp.full(lane_shape, SEQ, jnp.int32)

    @pl.when(jnp.max(jnp.where(cnt_ge > TOPK, 1.0, 0.0)) > 0.5)
    def _():
        def count32(pred):
            def body(c, acc):
                one = jnp.where(pred(keys_s[step_rows(c), :], c * ATT_CHUNK + step_iota), 1.0, 0.0)
                return acc + _fold_rows(one, SUBLANES)
            acc = lax.fori_loop(0, n_steps, body, jnp.zeros((SUBLANES, Q_BLOCK), jnp.float32))
            return jnp.sum(acc, axis=0, keepdims=True)

        need = TOPK - count32(lambda blk, idx: blk > thr)
        cut_s[...] = _tie_cutoff(
            lambda x: count32(lambda blk, idx: jnp.logical_and(blk == thr, idx < x)),
            need, int(np.log2(SEQ)), lane_shape)

    cutoff = cut_s[...]

    def bias_step(c, carry):
        blk = keys_s[step_rows(c), :]
        idx = c * ATT_CHUNK + step_iota
        sel = jnp.logical_or(blk > thr, jnp.logical_and(blk == thr, idx <= cutoff))
        sel = jnp.logical_and(sel, idx <= q_pos)
        bias_s[step_rows(c), :] = jnp.where(sel, 0.0, NEG_INF)
        return carry

    lax.fori_loop(0, n_steps, bias_step, 0)

    q = q_ref[...]
    n_lanes = HEADS_PER_KV * Q_BLOCK
    q_rows = [
        jnp.concatenate([q[:, (g * HEADS_PER_KV + h) * HEAD_DIM:(g * HEADS_PER_KV + h + 1) * HEAD_DIM]
                         for h in range(HEADS_PER_KV)], axis=0)
        for g in range(N_KV_HEADS)]
    acc_s[...] = jnp.zeros(acc_s.shape, jnp.float32)
    subs = ATT_CHUNK // KEY_CHUNK

    def attend_step(c, carry):
        b = bias_s[step_rows(c), :]
        b4 = jnp.concatenate([b] * HEADS_PER_KV, axis=1)
        out = []
        for g in range(N_KV_HEADS):
            m_old, l_old = carry[g]
            d_sl = slice(g * HEAD_DIM, (g + 1) * HEAD_DIM)
            s = _dot_nt(kb_ref[step_rows(c), d_sl], q_rows[g]) + b4
            m_new = jnp.maximum(m_old, jnp.max(s, axis=0, keepdims=True))
            m_safe = jnp.where(m_new == NEG_INF, 0.0, m_new)
            p = jnp.exp2((s - m_safe) * SOFTMAX_SCALE_LOG2E)
            alpha = jnp.exp2((m_old - m_safe) * SOFTMAX_SCALE_LOG2E)
            l_new = alpha * l_old + jnp.sum(p, axis=0, keepdims=True)
            pb = _bf16(p)
            pv = _dot(vt_ref[c * subs, d_sl, :], pb[0:KEY_CHUNK])
            for sub in range(1, subs):
                pv = pv + _dot(vt_ref[c * subs + sub, d_sl, :], pb[sub * KEY_CHUNK:(sub + 1) * KEY_CHUNK])
            acc_s[g] = acc_s[g] * alpha + pv
            out.append((m_new, l_new))
        return tuple(out)

    init = tuple((jnp.full((1, n_lanes), NEG_INF, jnp.float32), jnp.zeros((1, n_lanes), jnp.float32))
                 for _ in range(N_KV_HEADS))
    fin = lax.fori_loop(0, n_steps, attend_step, init)
    for g in range(N_KV_HEADS):
        o_t = acc_s[g] / fin[g][1]
        for h in range(HEADS_PER_KV):
            col = (g * HEADS_PER_KV + h) * HEAD_DIM
            o_ref[:, col:col + HEAD_DIM] = o_t[:, h * Q_BLOCK:(h + 1) * Q_BLOCK].T


def _prompt_attention(q, qi, kiwi, kb, vt, kib):
    nq = SEQ // Q_BLOCK
    blk = lambda w: pl.BlockSpec((None, Q_BLOCK, w), lambda b, i: (b, i, 0))
    return pl.pallas_call(
        _attn_kernel,
        grid=(BATCH, nq),
        in_specs=[
            blk(D_ATT), blk(D_IDX), blk(LANES),
            pl.BlockSpec((None, SEQ, D_KV), lambda b, i: (b, 0, 0)),
            pl.BlockSpec((None, SEQ // KEY_CHUNK, D_KV, KEY_CHUNK), lambda b, i: (b, 0, 0, 0)),
            pl.BlockSpec((None, SEQ, 2 * LANES), lambda b, i: (b, 0, 0)),
        ],
        out_specs=blk(D_ATT),
        out_shape=jax.ShapeDtypeStruct((BATCH, SEQ, D_ATT), jnp.float32),
        scratch_shapes=[
            pltpu.VMEM((SEQ, Q_BLOCK), jnp.int32),
            pltpu.VMEM((SEQ, Q_BLOCK), jnp.int16), pltpu.VMEM((SEQ, Q_BLOCK), jnp.int16),
            pltpu.VMEM((SEQ, Q_BLOCK), jnp.float32),
            pltpu.VMEM((1, Q_BLOCK), jnp.int32),
            pltpu.VMEM((N_KV_HEADS, HEAD_DIM, HEADS_PER_KV * Q_BLOCK), jnp.float32),
        ],
        compiler_params=pltpu.CompilerParams(
            dimension_semantics=("arbitrary", "arbitrary"), vmem_limit_bytes=VMEM_LIMIT),
        name="prompt_attention",
    )(q, qi, kiwi, kb, vt, kib)


def _out_kernel(o_ref, sgb_ref, mpa_ref, smb_ref, x_ref, gate_ref, w_pb_ref, w_o_ref, gf_ref, y_ref):
    yb = _dot(_bf16(o_ref[...] * sgb_ref[...]), w_pb_ref[...])
    m = mpa_ref[...] + smb_ref[...] * yb
    r = x_ref[...] + gate_ref[...] * _dot(_bf16(m), w_o_ref[...])
    y_ref[...] = r * lax.rsqrt(jnp.mean(r * r, axis=-1, keepdims=True) + EPS) * gf_ref[...]


def _output_projection(o, sgb, mpa, smb, x, gate, wts, rows):
    ng, nr, _ = x.shape
    row_spec = pl.BlockSpec((None, rows, D_MODEL), lambda b, j: (b, j, 0))
    if gate.shape[1] == 1:
        gate_spec = pl.BlockSpec((None, 1, D_MODEL), lambda b, j: (b, 0, 0))
    else:
        gate_spec = row_spec
    return pl.pallas_call(
        _out_kernel,
        grid=(ng, nr // rows),
        in_specs=[row_spec] * 5 + [gate_spec, _const_spec((D_ATT, D_MODEL)), _const_spec((D_MODEL, D_MODEL)),
                                   _const_spec((1, D_MODEL))],
        out_specs=row_spec,
        out_shape=jax.ShapeDtypeStruct(x.shape, jnp.float32),
        compiler_params=pltpu.CompilerParams(
            dimension_semantics=("arbitrary", "arbitrary"), vmem_limit_bytes=VMEM_LIMIT),
        name="output_projection",
    )(o, sgb, mpa, smb, x, gate, wts["w_pb"], wts["w_o"], wts["g_final"])


def _sample_proj_kernel(x_ref, shift_ref, scale_ref, gn_ref, w_in_ref, wconv_ref, bconv_ref, w_ra_ref, b_ra_ref,
                        w_rx_ref, b_rx_ref, lam_ref, ig_ref, ib_ref, w_pa_ref,
                        cos_h_ref, sin_h_ref, cos_i_ref, sin_ia_ref, sin_ib_ref, buf_ref, h0_ref,
                        k_ref, v_ref, q_ref, qi_ref, kiwi_ref, mpa_ref, sgb_ref, smb_ref, conv_ref, lru_ref):
    xn = _modulated_norm(x_ref[...], gn_ref[...], scale_ref[...], shift_ref[...])
    xa = _dot(xn, w_in_ref[:, C_XA:C_XA + D_RNN])
    xc = bconv_ref[...]
    for t in range(CONV_W - 1):
        xc = xc + buf_ref[t] * wconv_ref[t:t + 1, :]
        if t > 0:
            conv_ref[t - 1] = buf_ref[t]
    xc = xc + xa * wconv_ref[CONV_W - 1:CONV_W, :]
    conv_ref[CONV_W - 2] = xa
    a, u = _lru_gates(xc, w_ra_ref, b_ra_ref[...], w_rx_ref, b_rx_ref[...], lam_ref[...])
    h = a * h0_ref[...] + u
    lru_ref[...] = h
    ga = _dot(xn, w_in_ref[:, C_GA:C_GA + D_RNN])
    ya = _dot(_bf16(h * _silu(ga)), w_pa_ref[...])
    mpa_ref[...] = _sigmoid(_dot(xn, w_in_ref[:, C_MA:C_MA + D_MODEL])) * ya
    sgb_ref[...] = _silu(_dot(xn, w_in_ref[:, C_GB:C_GB + D_ATT]))
    smb_ref[...] = _sigmoid(_dot(xn, w_in_ref[:, C_MB:C_MB + D_MODEL]))

    cos_h, sin_h = cos_h_ref[0:1, :], sin_h_ref[0:1, :]
    cos_i, sin_ia, sin_ib = cos_i_ref[0:1, :], sin_ia_ref[0:1, :], sin_ib_ref[0:1, :]
    zq = _dot(xn, w_in_ref[:, C_Q:C_Q + D_ATT])
    for hd in range(N_HEADS):
        sl = slice(hd * HEAD_DIM, (hd + 1) * HEAD_DIM)
        q_ref[:, sl] = _bf16(_rot_head(zq[:, sl], cos_h, sin_h))
    zk = _dot(xn, w_in_ref[:, C_K:C_K + D_KV])
    for g in range(N_KV_HEADS):
        sl = slice(g * HEAD_DIM, (g + 1) * HEAD_DIM)
        k_ref[:, sl] = _rot_head(zk[:, sl], cos_h, sin_h)
    v_ref[...] = _dot(xn, w_in_ref[:, C_V:C_V + D_KV])
    zqi = _dot(xn, w_in_ref[:, C_QI:C_QI + D_IDX])
    for p in range(D_IDX // LANES):
        sl = slice(p * LANES, (p + 1) * LANES)
        qi_ref[:, sl] = _bf16(_rot_idx(zqi[:, sl], cos_i, sin_ia, sin_ib))
    kiwi_ref[...] = _idx_key_slab(_dot(xn, w_in_ref[:, C_KW:C_KW + LANES]), ig_ref[...], ib_ref[...],
                                  cos_i, sin_ia, sin_ib)


def _sample_projection(x, shift, scale, wts, tabs, buf_t, h0):
    n = DEC_BATCH
    f32, bf16 = jnp.float32, jnp.bfloat16
    in_specs = [
        _const_spec((n, D_MODEL)), _const_spec((n, D_MODEL)), _const_spec((n, D_MODEL)), _const_spec((1, D_MODEL)),
        _const_spec((D_MODEL, D_IN_PACKED)), _const_spec((CONV_W, D_RNN)), _const_spec((1, D_RNN)),
        _const_spec((RNN_BLOCKS, RNN_BLOCK_W, RNN_BLOCK_W)), _const_spec((1, D_RNN)),
        _const_spec((RNN_BLOCKS, RNN_BLOCK_W, RNN_BLOCK_W)), _const_spec((1, D_RNN)), _const_spec((1, D_RNN)),
        _const_spec((1, LANES)), _const_spec((1, LANES)), _const_spec((D_RNN, D_MODEL)),
    ] + [_const_spec((SUBLANES, LANES))] * 5 + [_const_spec((CONV_W - 1, n, D_RNN)), _const_spec((n, D_RNN))]
    shapes = [
        ((n, D_KV), f32), ((n, D_KV), f32), ((n, D_ATT), bf16), ((n, D_IDX), bf16), ((n, LANES), f32),
        ((n, D_MODEL), f32), ((n, D_ATT), f32), ((n, D_MODEL), f32), ((CONV_W - 1, n, D_RNN), f32), ((n, D_RNN), f32),
    ]
    return pl.pallas_call(
        _sample_proj_kernel,
        grid=(1,),
        in_specs=in_specs,
        out_specs=[_const_spec(s, single=False) for s, _ in shapes],
        out_shape=[jax.ShapeDtypeStruct(s, d) for s, d in shapes],
        compiler_params=pltpu.CompilerParams(vmem_limit_bytes=VMEM_LIMIT),
        name="sample_projection",
    )(x, shift, scale, wts["g_norm"], wts["w_in"], wts["w_conv"], wts["b_conv"], wts["w_ra"], wts["b_ra"],
      wts["w_rx"], wts["b_rx"], wts["lam"], wts["idx_g"], wts["idx_b"], wts["w_pa"], *tabs, buf_t, h0)


def _sample_score_kernel(pt_ref, qi_ref, w_ref, kinew_ref, idx_hbm, o_ref, buf, sem):
    b = pl.program_id(0)
    slot = b % 2

    def page_copy(sample, p, sl):
        return pltpu.make_async_copy(idx_hbm.at[pt_ref[sample * N_PAGES + p]], buf.at[sl, p], sem.at[sl])

    def start_sample(sample, sl):
        def body(p, carry):
            page_copy(sample, p, sl).start()
            return carry
        lax.fori_loop(0, N_PAGES, body, 0)

    @pl.when(b == 0)
    def _():
        start_sample(0, 0)

    @pl.when(b + 1 < pl.num_programs(0))
    def _():
        start_sample(b + 1, 1 - slot)

    def wait_page(p, carry):
        page_copy(b, p, slot).wait()
        return carry

    lax.fori_loop(0, N_PAGES, wait_page, 0)

    qi = qi_ref[...]
    w = w_ref[...]

    def score_pages(i, carry):
        p0 = i * SCORE_PAGES
        kt = _bf16(jnp.concatenate([buf[slot, p0 + t] for t in range(SCORE_PAGES)], axis=1))
        s = _dot(qi, kt)
        score = jnp.sum(jnp.maximum(s, 0.0) * w, axis=0, keepdims=True)
        for t in range(SCORE_PAGES):
            o_ref[pl.ds(p0 + t, 1), :] = score[:, t * PAGE_SIZE:(t + 1) * PAGE_SIZE]
        return carry

    lax.fori_loop(0, N_PAGES // SCORE_PAGES, score_pages, 0)

    k_self = _bf16(kinew_ref[...][:, :IDX_DIM]).astype(jnp.float32)
    s_self = jnp.sum(qi.astype(jnp.float32) * k_self, axis=1, keepdims=True)
    score_self = jnp.sum(jnp.maximum(s_self, 0.0) * w, axis=0, keepdims=True)
    lane = lax.broadcasted_iota(jnp.int32, (1, PAGE_SIZE), 1)
    o_ref[N_PAGES:N_PAGES + 1, :] = jnp.where(lane == 0, score_self, NEG_INF)


def _sample_scores(page_table_flat, idx_pages, qi3, w_col, kiwi3):
    per_sample = lambda r, w: pl.BlockSpec((None, r, w), lambda b, pt: (b, 0, 0))
    return pl.pallas_call(
        _sample_score_kernel,
        grid_spec=pltpu.PrefetchScalarGridSpec(
            num_scalar_prefetch=1,
            grid=(DEC_BATCH,),
            in_specs=[per_sample(N_IDX_HEADS, IDX_DIM), per_sample(N_IDX_HEADS, 1), per_sample(1, LANES),
                      pl.BlockSpec(memory_space=pl.ANY)],
            out_specs=per_sample(N_PAGES + 1, PAGE_SIZE),
            scratch_shapes=[pltpu.VMEM((2, N_PAGES, IDX_DIM, PAGE_SIZE), jnp.float32),
                            pltpu.SemaphoreType.DMA((2,))],
        ),
        out_shape=jax.ShapeDtypeStruct((DEC_BATCH, N_PAGES + 1, PAGE_SIZE), jnp.float32),
        compiler_params=pltpu.CompilerParams(dimension_semantics=("arbitrary",), vmem_limit_bytes=VMEM_LIMIT),
        name="sample_scores",
    )(page_table_flat, qi3, w_col, kiwi3, idx_pages)


def _sample_select_kernel(score_ref, bias_ref):
    keys = _sortable_key(score_ref[...])
    idx = lax.broadcasted_iota(jnp.int32, keys.shape, 1)
    col_shape = (keys.shape[0], 1)
    count = lambda pred: jnp.sum(jnp.where(pred, 1.0, 0.0), axis=1, keepdims=True)
    total = jnp.full(col_shape, float(keys.shape[1]), jnp.float32)
    thr_u, _ = _greedy_bits(lambda t: count(keys >= jnp.bitwise_xor(t, jnp.int32(INT_MIN))), 32, col_shape, total)
    thr = jnp.bitwise_xor(thr_u, jnp.int32(INT_MIN))
    need = TOPK - count(keys > thr)
    tie = keys == thr
    n_bits = int(np.ceil(np.log2(keys.shape[1])))
    cutoff = _tie_cutoff(lambda x: count(jnp.logical_and(tie, idx < x)), need, n_bits, col_shape)
    sel = jnp.logical_or(keys > thr, jnp.logical_and(tie, idx <= cutoff))
    bias_ref[...] = jnp.where(sel, 0.0, NEG_INF)


def _sample_select(scores):
    return pl.pallas_call(
        _sample_select_kernel,
        grid=(1,),
        in_specs=[_const_spec(scores.shape)],
        out_specs=_const_spec(scores.shape, single=False),
        out_shape=jax.ShapeDtypeStruct(scores.shape, jnp.float32),
        compiler_params=pltpu.CompilerParams(vmem_limit_bytes=VMEM_LIMIT),
        name="sample_select",
    )(scores)


def _sample_attn_kernel(pt_ref, bias_ref, bias_self_ref, q_ref, knew_ref, vnew_ref, k_hbm, v_hbm, o_ref,
                        kbuf, vbuf, sem, m_s, l_s, acc_s):
    b, j = pl.program_id(0), pl.program_id(1)
    n_j = pl.num_programs(1)
    step = b * n_j + j
    slot = step % 2

    def page_copies(st, sl):
        out = []
        for t in range(ATTN_PAGES):
            page = pt_ref[st * ATTN_PAGES + t]
            out.append(pltpu.make_async_copy(k_hbm.at[page], kbuf.at[sl, t], sem.at[0, sl]))
            out.append(pltpu.make_async_copy(v_hbm.at[page], vbuf.at[sl, t], sem.at[1, sl]))
        return out

    @pl.when(step == 0)
    def _():
        for c in page_copies(0, 0):
            c.start()

    @pl.when(step + 1 < pl.num_programs(0) * n_j)
    def _():
        for c in page_copies(step + 1, 1 - slot):
            c.start()

    for c in page_copies(step, slot):
        c.wait()

    @pl.when(j == 0)
    def _():
        m_s[...] = jnp.full(m_s.shape, NEG_INF, jnp.float32)
        l_s[...] = jnp.zeros(l_s.shape, jnp.float32)
        acc_s[...] = jnp.zeros(acc_s.shape, jnp.float32)

    def online_update(s, pv_of):
        m_old = m_s[...]
        m_new = jnp.maximum(m_old, jnp.max(s, axis=1, keepdims=True))
        m_safe = jnp.where(m_new == NEG_INF, 0.0, m_new)
        pr = jnp.exp2((s - m_safe) * SOFTMAX_SCALE_LOG2E)
        alpha = jnp.exp2((m_old - m_safe) * SOFTMAX_SCALE_LOG2E)
        l_s[...] = alpha * l_s[...] + jnp.sum(pr, axis=1, keepdims=True)
        acc_s[...] = acc_s[...] * alpha + pv_of(_bf16(pr))
        m_s[...] = m_new

    q = q_ref[...]
    rows_per_page = PAGE_SIZE * N_KV_HEADS
    n_cols = ATTN_PAGES * rows_per_page
    k_all = _bf16(kbuf[slot].reshape(n_cols, HEAD_DIM))
    v_all = _bf16(vbuf[slot].reshape(n_cols, HEAD_DIM))
    dup = jnp.where(lax.broadcasted_iota(jnp.int32, (PAGE_SIZE, rows_per_page), 1) // N_KV_HEADS
                    == lax.broadcasted_iota(jnp.int32, (PAGE_SIZE, rows_per_page), 0), 1.0, 0.0)
    sel_pages = _dot(_bf16(jnp.where(bias_ref[...] == 0.0, 1.0, 0.0)), _bf16(dup))
    sel_row = jnp.concatenate([sel_pages[t:t + 1, :] for t in range(ATTN_PAGES)], axis=1)
    head = lax.broadcasted_iota(jnp.int32, (N_HEADS, n_cols), 0)
    col = lax.broadcasted_iota(jnp.int32, (N_HEADS, n_cols), 1)
    own = (col % N_KV_HEADS) == (head // HEADS_PER_KV)
    s = jnp.where(jnp.logical_and(own, sel_row > 0.5), _dot_nt(q, k_all), NEG_INF)
    online_update(s, lambda pb: _dot(pb, v_all))

    @pl.when(j == n_j - 1)
    def _():
        head_d = lax.broadcasted_iota(jnp.int32, (N_HEADS, HEAD_DIM), 0) // HEADS_PER_KV

        def own_row(ref):
            rows = _bf16(ref[...]).astype(jnp.float32)
            out = jnp.broadcast_to(rows[N_KV_HEADS - 1:N_KV_HEADS, :], (N_HEADS, HEAD_DIM))
            for g in range(N_KV_HEADS - 2, -1, -1):
                out = jnp.where(head_d == g, rows[g:g + 1, :], out)
            return out

        s_self = jnp.sum(q.astype(jnp.float32) * own_row(knew_ref), axis=1, keepdims=True)
        v_own = own_row(vnew_ref)
        online_update(s_self + bias_self_ref[...][:, 0:1], lambda pb: pb.astype(jnp.float32) * v_own)
        o_ref[...] = acc_s[...] / l_s[...]


def _sample_attention(page_table_flat, k_pages, v_pages, bias3, bias_self, q3, knew3, vnew3):
    per_sample = lambda r, w: pl.BlockSpec((None, r, w), lambda b, j, pt: (b, 0, 0))
    rows_per_page = PAGE_SIZE * N_KV_HEADS
    return pl.pallas_call(
        _sample_attn_kernel,
        grid_spec=pltpu.PrefetchScalarGridSpec(
            num_scalar_prefetch=1,
            grid=(DEC_BATCH, N_PAGES // ATTN_PAGES),
            in_specs=[
                pl.BlockSpec((None, ATTN_PAGES, PAGE_SIZE), lambda b, j, pt: (b, j, 0)),
                per_sample(1, PAGE_SIZE), per_sample(N_HEADS, HEAD_DIM),
                per_sample(N_KV_HEADS, HEAD_DIM), per_sample(N_KV_HEADS, HEAD_DIM),
                pl.BlockSpec(memory_space=pl.ANY), pl.BlockSpec(memory_space=pl.ANY),
            ],
            out_specs=per_sample(N_HEADS, HEAD_DIM),
            scratch_shapes=[
                pltpu.VMEM((2, ATTN_PAGES, rows_per_page, HEAD_DIM), jnp.float32),
                pltpu.VMEM((2, ATTN_PAGES, rows_per_page, HEAD_DIM), jnp.float32),
                pltpu.SemaphoreType.DMA((2, 2)),
                pltpu.VMEM((N_HEADS, 1), jnp.float32), pltpu.VMEM((N_HEADS, 1), jnp.float32),
                pltpu.VMEM((N_HEADS, HEAD_DIM), jnp.float32),
            ],
        ),
        out_shape=jax.ShapeDtypeStruct((DEC_BATCH, N_HEADS, HEAD_DIM), jnp.float32),
        compiler_params=pltpu.CompilerParams(
            dimension_semantics=("arbitrary", "arbitrary"), vmem_limit_bytes=VMEM_LIMIT),
        name="sample_attention",
    )(page_table_flat, bias3, bias_self, q3, knew3, vnew3, k_pages, v_pages)


def _pack_w_in(w_in):
    xa, ga, q, k, v, gb, qi, ki, wi, ma, mb = jnp.split(w_in, np.cumsum(SPLITS)[:-1].tolist(), axis=-1)
    pad = jnp.zeros((D_MODEL, LANES - IDX_DIM - N_IDX_HEADS), w_in.dtype)
    return _bf16(jnp.concatenate([xa, ga, q, gb, ma, mb, k, v, qi, ki, wi, pad], axis=-1))


def _lane_pad(v):
    return jnp.pad(v.reshape(1, -1), ((0, 0), (0, LANES - v.shape[-1])))


def kernel(x_prompt, x_sample, cache_k, cache_v, cache_idx_k, state_conv, state_rglru, page_table, c_prompt, c_sample, w_ada, b_ada, g_norm, w_in, w_conv, b_conv, w_ra, b_ra, w_rx, b_rx, lru_lambda, idx_k_norm_g, idx_k_norm_b, w_pa, w_pb, w_o, g_final):
    assert w_in.shape[0] == 1, "one layer"
    wts = {
        "g_norm": g_norm[0].reshape(1, -1), "w_in": _pack_w_in(w_in[0]), "w_conv": w_conv[0],
        "b_conv": b_conv[0].reshape(1, -1), "w_ra": _bf16(w_ra[0]), "b_ra": b_ra[0].reshape(1, -1),
        "w_rx": _bf16(w_rx[0]), "b_rx": b_rx[0].reshape(1, -1), "lam": lru_lambda[0].reshape(1, -1),
        "idx_g": _lane_pad(idx_k_norm_g[0]), "idx_b": _lane_pad(idx_k_norm_b[0]),
        "w_pa": _bf16(w_pa[0]), "w_pb": _bf16(w_pb[0]), "w_o": _bf16(w_o[0]), "g_final": g_final.reshape(1, -1),
    }
    half_h, half_i = HEAD_DIM // 2, IDX_DIM // 2
    invf_h = ROPE_THETA ** (-jnp.arange(half_h, dtype=jnp.float32) / half_h)
    invf_i = ROPE_THETA ** (-jnp.arange(half_i, dtype=jnp.float32) / half_i)
    invf = jnp.zeros((SUBLANES, LANES), jnp.float32)
    invf = invf.at[0].set(jnp.tile(invf_h, LANES // half_h)).at[1].set(jnp.tile(invf_i, LANES // half_i))
    tabs_prompt = _rope_tables(invf, SEQ, 0, 1)
    tabs_sample = _rope_tables(invf, SUBLANES, PAST_LEN, 0)

    mod = _ada_modulation(jnp.concatenate([c_prompt, c_sample], axis=0), _bf16(w_ada[0]), b_ada[0].reshape(1, -1))
    shift, scale, gate = mod[:, :D_MODEL], mod[:, D_MODEL:2 * D_MODEL], mod[:, 2 * D_MODEL:]

    (k_p, v_p, ki_p, kb, vt, kib, q, qi, kiwi, mpa, sgb, smb, conv_p, lru_p) = _prompt_projection(
        x_prompt, shift[:BATCH, None, :], scale[:BATCH, None, :], wts, tabs_prompt)
    o = _prompt_attention(q, qi, kiwi, kb, vt, kib)
    y_prompt = _output_projection(o, sgb, mpa, smb, x_prompt, gate[:BATCH, None, :], wts, OUT_ROWS)

    xs = x_sample[:, 0, :]
    (k_s, v_s, q_s, qi_s, kiwi_s, mpa_s, sgb_s, smb_s, conv_s, lru_s) = _sample_projection(
        xs, shift[BATCH:], scale[BATCH:], wts, tabs_sample, jnp.swapaxes(state_conv[0], 0, 1), state_rglru[0])
    pt_flat = page_table.reshape(-1)
    w_col = kiwi_s[:, IDX_DIM:IDX_DIM + N_IDX_HEADS, None]
    idx_pages = jnp.swapaxes(cache_idx_k[0], 1, 2)
    kv_pages = lambda t: t[0].reshape(-1, PAGE_SIZE * N_KV_HEADS, HEAD_DIM)
    scores = _sample_scores(pt_flat, idx_pages, qi_s.reshape(DEC_BATCH, N_IDX_HEADS, IDX_DIM), w_col,
                            kiwi_s[:, None, :])
    bias = _sample_select(scores.reshape(DEC_BATCH, (N_PAGES + 1) * PAGE_SIZE))
    bias = bias.reshape(DEC_BATCH, N_PAGES + 1, PAGE_SIZE)
    o_s = _sample_attention(
        pt_flat, kv_pages(cache_k), kv_pages(cache_v), bias, bias[:, N_PAGES:, :],
        q_s.reshape(DEC_BATCH, N_HEADS, HEAD_DIM), k_s.reshape(DEC_BATCH, N_KV_HEADS, HEAD_DIM),
        v_s.reshape(DEC_BATCH, N_KV_HEADS, HEAD_DIM))
    y_sample = _output_projection(
        o_s.reshape(1, DEC_BATCH, D_ATT), sgb_s[None], mpa_s[None], smb_s[None], xs[None], gate[None, BATCH:],
        wts, DEC_BATCH)

    kv_p = lambda t: t.reshape(1, BATCH, SEQ, N_KV_HEADS, HEAD_DIM)
    kv_s = lambda t: t.reshape(1, DEC_BATCH, 1, N_KV_HEADS, HEAD_DIM)
    return (
        y_prompt, y_sample.reshape(DEC_BATCH, 1, D_MODEL),
        kv_p(k_p), kv_p(v_p), jnp.swapaxes(ki_p, 1, 2)[None], conv_p[None], lru_p.reshape(1, BATCH, D_RNN),
        kv_s(k_s), kv_s(v_s), kiwi_s[:, :IDX_DIM].reshape(1, DEC_BATCH, 1, IDX_DIM),
        jnp.swapaxes(conv_s, 0, 1)[None], lru_s[None],
    )
```

```python
import functools

import jax
import jax.numpy as jnp
import numpy as np
from jax import lax
from jax.experimental import pallas as pl
from jax.experimental.pallas import tpu as pltpu

D_MODEL = 1024
BATCH = 8
SEQ = 4096
DEC_BATCH = 32
PAST_LEN = 16384
PAGE_SIZE = 128
N_PAGES = PAST_LEN // PAGE_SIZE
D_RNN = D_MODEL
RNN_BLOCKS = 4
RNN_BLOCK_W = D_RNN // RNN_BLOCKS
CONV_W = 4
LRU_C = 8.0
N_HEADS = 8
HEAD_DIM = 128
N_KV_HEADS = 2
HEADS_PER_KV = N_HEADS // N_KV_HEADS
D_ATT = N_HEADS * HEAD_DIM
D_KV = N_KV_HEADS * HEAD_DIM
N_IDX_HEADS = 8
IDX_DIM = 64
D_IDX = N_IDX_HEADS * IDX_DIM
IDX_W_SCALE = (N_IDX_HEADS * IDX_DIM) ** -0.5
TOPK = 256
ROPE_THETA = 10000.0
EPS = 1e-6
SPLITS = (D_RNN, D_RNN, D_ATT, D_KV, D_KV, D_ATT, D_IDX, IDX_DIM, N_IDX_HEADS, D_MODEL, D_MODEL)

LANES = 128
SUBLANES = 8

C_XA, C_GA, C_Q, C_GB, C_MA, C_MB = 0, 1024, 2048, 3072, 4096, 5120
C_K, C_V, C_QI, C_KW = 6144, 6400, 6656, 7168
D_IN_PACKED = 7296

PROJ_ROWS = 256
Q_BLOCK = 256
KEY_CHUNK = 256
ATT_CHUNK = 512
PACKED_ROWS = 16
assert KEY_CHUNK >= TOPK and ATT_CHUNK % KEY_CHUNK == 0 and SEQ % ATT_CHUNK == 0 and ATT_CHUNK % Q_BLOCK == 0
assert PAGE_SIZE == HEAD_DIM == LANES
OUT_ROWS = 512
SCORE_PAGES = 16
ATTN_PAGES = 16
assert N_PAGES % SCORE_PAGES == 0 and N_PAGES % ATTN_PAGES == 0
SOFTMAX_SCALE_LOG2E = (HEAD_DIM ** -0.5) * float(np.log2(np.e))
NEG_INF = float("-inf")
INT_MIN = -2 ** 31
KEY_NEG_INF = INT_MIN + 0x7FFFFF
VMEM_LIMIT = 56 * 1024 * 1024


def _sigmoid(x):
    return 1.0 / (1.0 + jnp.exp(-x))


def _silu(x):
    return x * _sigmoid(x)


def _dot(a, b):
    return jnp.dot(a, b, preferred_element_type=jnp.float32)


def _dot_nt(a, b):
    return lax.dot_general(a, b, (((1,), (1,)), ((), ())), preferred_element_type=jnp.float32)


def _bf16(x):
    return x.astype(jnp.bfloat16)


def _const_spec(shape, single=True):
    nd = len(shape)
    kwargs = {"pipeline_mode": pl.Buffered(1)} if single else {}
    return pl.BlockSpec(shape, lambda *_: (0,) * nd, **kwargs)


def _rope_kernel(invf_ref, cos_h_ref, sin_h_ref, cos_i_ref, sin_ia_ref, sin_ib_ref, *, pos0, pos_step, rows):
    r0 = pl.program_id(0) * rows
    row = lax.broadcasted_iota(jnp.int32, (rows, LANES), 0) + r0
    lane = lax.broadcasted_iota(jnp.int32, (rows, LANES), 1)
    pos = (pos0 + pos_step * row).astype(jnp.float32)
    ang_h = pos * invf_ref[0:1, :]
    ang_i = pos * invf_ref[1:2, :]
    cos_h_ref[...] = jnp.cos(ang_h)
    sh = jnp.sin(ang_h)
    sin_h_ref[...] = jnp.where(lane < HEAD_DIM // 2, -sh, sh)
    cos_i_ref[...] = jnp.cos(ang_i)
    si = jnp.sin(ang_i)
    first_half = (lane % IDX_DIM) < IDX_DIM // 2
    sin_ia_ref[...] = jnp.where(first_half, -si, 0.0)
    sin_ib_ref[...] = jnp.where(first_half, 0.0, si)


def _rope_tables(invf, n, pos0, pos_step):
    rows = min(n, 512)
    out = jax.ShapeDtypeStruct((n, LANES), jnp.float32)
    spec = pl.BlockSpec((rows, LANES), lambda i: (i, 0))
    return pl.pallas_call(
        functools.partial(_rope_kernel, pos0=pos0, pos_step=pos_step, rows=rows),
        grid=(n // rows,),
        in_specs=[pl.BlockSpec((SUBLANES, LANES), lambda i: (0, 0))],
        out_specs=[spec] * 5,
        out_shape=[out] * 5,
        name="rope_tables",
    )(invf)


def _rot_head(z, cos, sin_signed):
    return z * cos + pltpu.roll(z, HEAD_DIM // 2, 1) * sin_signed


def _rot_idx(z, cos, sin_a, sin_b):
    return z * cos + pltpu.roll(z, LANES - IDX_DIM // 2, 1) * sin_a + pltpu.roll(z, IDX_DIM // 2, 1) * sin_b


def _ada_kernel(c_ref, w_ref, b_ref, o_ref):
    o_ref[...] = _dot(_bf16(_silu(c_ref[...])), w_ref[...]) + b_ref[...]


def _ada_modulation(c_all, w_ada, b_ada):
    n = c_all.shape[0]
    return pl.pallas_call(
        _ada_kernel,
        grid=(1,),
        in_specs=[_const_spec((n, D_MODEL)), _const_spec((D_MODEL, 3 * D_MODEL)), _const_spec((1, 3 * D_MODEL))],
        out_specs=_const_spec((n, 3 * D_MODEL), single=False),
        out_shape=jax.ShapeDtypeStruct((n, 3 * D_MODEL), jnp.float32),
        compiler_params=pltpu.CompilerParams(vmem_limit_bytes=VMEM_LIMIT),
        name="ada_modulation",
    )(c_all, w_ada, b_ada)


def _modulated_norm(x, g, scale, shift):
    y = x * lax.rsqrt(jnp.mean(x * x, axis=-1, keepdims=True) + EPS) * g
    return _bf16(y * (1.0 + scale) + shift)


def _lru_gates(xc, w_ra_ref, b_ra, w_rx_ref, b_rx, lam):
    xcb = _bf16(xc)
    r_parts, i_parts = [], []
    for n in range(RNN_BLOCKS):
        sl = slice(n * RNN_BLOCK_W, (n + 1) * RNN_BLOCK_W)
        r_parts.append(_dot(xcb[:, sl], w_ra_ref[n]))
        i_parts.append(_dot(xcb[:, sl], w_rx_ref[n]))
    r = _sigmoid(jnp.concatenate(r_parts, axis=1) + b_ra)
    i = _sigmoid(jnp.concatenate(i_parts, axis=1) + b_rx)
    neg_lam = -lam
    softplus = jnp.maximum(neg_lam, 0.0) + jnp.log1p(jnp.exp(-jnp.abs(neg_lam)))
    log_a = (-LRU_C) * r * softplus
    a = jnp.exp(log_a)
    u = jnp.sqrt(-jnp.tanh(log_a) * (a * a + 1.0)) * (i * xc)
    return a, u


def _idx_key_slab(z_kw, g, b, cos_i, sin_ia, sin_ib):
    lane = lax.broadcasted_iota(jnp.int32, z_kw.shape, 1)
    is_key = lane < IDX_DIM
    mu = jnp.sum(jnp.where(is_key, z_kw, 0.0), axis=-1, keepdims=True) * (1.0 / IDX_DIM)
    d = jnp.where(is_key, z_kw - mu, 0.0)
    var = jnp.sum(d * d, axis=-1, keepdims=True) * (1.0 / IDX_DIM)
    y = d * lax.rsqrt(var + EPS) * g + b
    key = _rot_idx(y, cos_i, sin_ia, sin_ib)
    is_w = jnp.logical_and(lane >= IDX_DIM, lane < IDX_DIM + N_IDX_HEADS)
    return key + jnp.where(is_w, z_kw * IDX_W_SCALE, 0.0)


def _proj_kernel(x_ref, shift_ref, scale_ref, gn_ref, w_in_ref, wconv_ref, bconv_ref, w_ra_ref, b_ra_ref,
                 w_rx_ref, b_rx_ref, lam_ref, ig_ref, ib_ref, w_pa_ref,
                 cos_h_ref, sin_h_ref, cos_i_ref, sin_ia_ref, sin_ib_ref,
                 k_ref, v_ref, ki_ref, kb_ref, vt_ref, kib_ref, q_ref, qi_ref, kiwi_ref,
                 mpa_ref, sgb_ref, smb_ref, conv_ref, lru_ref,
                 xa_ext, a_s, u_s, h_carry, tail_s):
    ts = PROJ_ROWS

    @pl.when(pl.program_id(1) == 0)
    def _():
        tail_s[...] = jnp.zeros(tail_s.shape, jnp.float32)
        h_carry[...] = jnp.zeros(h_carry.shape, jnp.float32)

    xn = _modulated_norm(x_ref[...], gn_ref[...], scale_ref[...], shift_ref[...])

    xa_ext[0:SUBLANES, :] = tail_s[...]
    xa_ext[SUBLANES:SUBLANES + ts, :] = _dot(xn, w_in_ref[:, C_XA:C_XA + D_RNN])
    xc = bconv_ref[...]
    for t in range(CONV_W):
        off = SUBLANES - (CONV_W - 1) + t
        xc = xc + xa_ext[off:off + ts, :] * wconv_ref[t:t + 1, :]
    conv_ref[...] = xa_ext[ts + SUBLANES - (CONV_W - 1):ts + SUBLANES, :]
    tail_s[...] = xa_ext[ts:ts + SUBLANES, :]

    a, u = _lru_gates(xc, w_ra_ref, b_ra_ref[...], w_rx_ref, b_rx_ref[...], lam_ref[...])
    a_s[...] = a
    u_s[...] = u
    row = lax.broadcasted_iota(jnp.int32, (SUBLANES, D_RNN), 0)

    hc = h_carry[...]
    for g in range(ts // SUBLANES):
        rows = slice(g * SUBLANES, (g + 1) * SUBLANES)
        a8 = a_s[rows, :]
        u8 = u_s[rows, :]
        for d in (1, 2, 4):
            keep = row >= d
            u8 = jnp.where(keep, a8 * pltpu.roll(u8, d, 0) + u8, u8)
            a8 = jnp.where(keep, a8 * pltpu.roll(a8, d, 0), a8)
        h8 = a8 * hc + u8
        u_s[rows, :] = h8
        hc = h8[SUBLANES - 1:SUBLANES, :]
    h_carry[...] = hc
    lru_ref[...] = hc

    ga = _dot(xn, w_in_ref[:, C_GA:C_GA + D_RNN])
    ya = _dot(_bf16(u_s[...] * _silu(ga)), w_pa_ref[...])
    ma = _dot(xn, w_in_ref[:, C_MA:C_MA + D_MODEL])
    mpa_ref[...] = _bf16(_sigmoid(ma) * ya)
    sgb_ref[...] = _bf16(_silu(_dot(xn, w_in_ref[:, C_GB:C_GB + D_ATT])))
    smb_ref[...] = _bf16(_sigmoid(_dot(xn, w_in_ref[:, C_MB:C_MB + D_MODEL])))

    cos_h, sin_h = cos_h_ref[...], sin_h_ref[...]
    cos_i, sin_ia, sin_ib = cos_i_ref[...], sin_ia_ref[...], sin_ib_ref[...]
    zq = _dot(xn, w_in_ref[:, C_Q:C_Q + D_ATT])
    for h in range(N_HEADS):
        sl = slice(h * HEAD_DIM, (h + 1) * HEAD_DIM)
        q_ref[:, sl] = _bf16(_rot_head(zq[:, sl], cos_h, sin_h) * SOFTMAX_SCALE_LOG2E)
    zk = _dot(xn, w_in_ref[:, C_K:C_K + D_KV])
    for g in range(N_KV_HEADS):
        sl = slice(g * HEAD_DIM, (g + 1) * HEAD_DIM)
        kr = _rot_head(zk[:, sl], cos_h, sin_h)
        k_ref[:, sl] = kr
        kb_ref[:, sl] = _bf16(kr)
    zv = _dot(xn, w_in_ref[:, C_V:C_V + D_KV])
    v_ref[...] = zv
    vt = _bf16(zv.T)
    for c in range(ts // KEY_CHUNK):
        vt_ref[c] = vt[:, c * KEY_CHUNK:(c + 1) * KEY_CHUNK]
    zqi = _dot(xn, w_in_ref[:, C_QI:C_QI + D_IDX])
    for p in range(D_IDX // LANES):
        sl = slice(p * LANES, (p + 1) * LANES)
        qi_ref[:, sl] = _bf16(_rot_idx(zqi[:, sl], cos_i, sin_ia, sin_ib))
    slab = _idx_key_slab(_dot(xn, w_in_ref[:, C_KW:C_KW + LANES]), ig_ref[...], ib_ref[...], cos_i, sin_ia, sin_ib)
    kiwi_ref[...] = slab
    ki_ref[...] = slab.T[:IDX_DIM, :]
    key_even = jnp.where(lax.broadcasted_iota(jnp.int32, slab.shape, 1) < IDX_DIM, slab, 0.0)
    kib_ref[:, 0:LANES] = _bf16(key_even)
    kib_ref[:, LANES:2 * LANES] = _bf16(pltpu.roll(key_even, IDX_DIM, 1))


def _prompt_projection(x, shift, scale, wts, tabs):
    ts = PROJ_ROWS
    nt = SEQ // ts
    f32, bf16 = jnp.float32, jnp.bfloat16
    row_spec = lambda w: pl.BlockSpec((None, ts, w), lambda b, j: (b, j, 0))
    bvec_spec = pl.BlockSpec((None, 1, D_MODEL), lambda b, j: (b, 0, 0))
    tab_spec = pl.BlockSpec((ts, LANES), lambda b, j: (j, 0))
    in_specs = [
        row_spec(D_MODEL), bvec_spec, bvec_spec, _const_spec((1, D_MODEL)),
        _const_spec((D_MODEL, D_IN_PACKED)), _const_spec((CONV_W, D_RNN)), _const_spec((1, D_RNN)),
        _const_spec((RNN_BLOCKS, RNN_BLOCK_W, RNN_BLOCK_W)), _const_spec((1, D_RNN)),
        _const_spec((RNN_BLOCKS, RNN_BLOCK_W, RNN_BLOCK_W)), _const_spec((1, D_RNN)), _const_spec((1, D_RNN)),
        _const_spec((1, LANES)), _const_spec((1, LANES)), _const_spec((D_RNN, D_MODEL)),
    ] + [tab_spec] * 5
    out_shape = [
        jax.ShapeDtypeStruct((BATCH, SEQ, D_KV), f32),
        jax.ShapeDtypeStruct((BATCH, SEQ, D_KV), f32),
        jax.ShapeDtypeStruct((BATCH, IDX_DIM, SEQ), f32),
        jax.ShapeDtypeStruct((BATCH, SEQ, D_KV), bf16),
        jax.ShapeDtypeStruct((BATCH, SEQ // KEY_CHUNK, D_KV, KEY_CHUNK), bf16),
        jax.ShapeDtypeStruct((BATCH, SEQ, 2 * LANES), bf16),
        jax.ShapeDtypeStruct((BATCH, SEQ, D_ATT), bf16),
        jax.ShapeDtypeStruct((BATCH, SEQ, D_IDX), bf16),
        jax.ShapeDtypeStruct((BATCH, SEQ, LANES), f32),
        jax.ShapeDtypeStruct((BATCH, SEQ, D_MODEL), bf16),
        jax.ShapeDtypeStruct((BATCH, SEQ, D_ATT), bf16),
        jax.ShapeDtypeStruct((BATCH, SEQ, D_MODEL), bf16),
        jax.ShapeDtypeStruct((BATCH, CONV_W - 1, D_RNN), f32),
        jax.ShapeDtypeStruct((BATCH, 1, D_RNN), f32),
    ]
    out_specs = [
        row_spec(D_KV), row_spec(D_KV), pl.BlockSpec((None, IDX_DIM, ts), lambda b, j: (b, 0, j)), row_spec(D_KV),
        pl.BlockSpec((None, ts // KEY_CHUNK, D_KV, KEY_CHUNK), lambda b, j: (b, j, 0, 0)),
        row_spec(2 * LANES), row_spec(D_ATT), row_spec(D_IDX), row_spec(LANES),
        row_spec(D_MODEL), row_spec(D_ATT), row_spec(D_MODEL),
        pl.BlockSpec((None, CONV_W - 1, D_RNN), lambda b, j: (b, 0, 0)),
        pl.BlockSpec((None, 1, D_RNN), lambda b, j: (b, 0, 0)),
    ]
    scratch = [
        pltpu.VMEM((ts + SUBLANES, D_RNN), f32), pltpu.VMEM((ts, D_RNN), f32),
        pltpu.VMEM((ts, D_RNN), f32), pltpu.VMEM((1, D_RNN), f32), pltpu.VMEM((SUBLANES, D_RNN), f32),
    ]
    return pl.pallas_call(
        _proj_kernel,
        grid=(BATCH, nt),
        in_specs=in_specs, out_specs=out_specs, out_shape=out_shape, scratch_shapes=scratch,
        compiler_params=pltpu.CompilerParams(
            dimension_semantics=("arbitrary", "arbitrary"), vmem_limit_bytes=VMEM_LIMIT),
        name="prompt_projection",
    )(x, shift, scale, wts["g_norm"], wts["w_in"], wts["w_conv"], wts["b_conv"], wts["w_ra"], wts["b_ra"],
      wts["w_rx"], wts["b_rx"], wts["lam"], wts["idx_g"], wts["idx_b"], wts["w_pa"], *tabs)


def _threshold_value(t_unsigned):
    key = jnp.maximum(jnp.bitwise_xor(t_unsigned, jnp.int32(INT_MIN)), jnp.int32(KEY_NEG_INF))
    bits = key ^ jnp.bitwise_and(jnp.right_shift(key, 31), jnp.int32(0x7FFFFFFF))
    return pltpu.bitcast(bits, jnp.float32)


def _greedy_bits(count_ge, n_bits, shape, count_all):
    def bit_step(b, carry):
        t, cnt = carry
        cand = jnp.bitwise_or(t, jnp.left_shift(jnp.int32(1), n_bits - 1 - b))
        c = count_ge(cand)
        ok = c >= TOPK
        return jnp.where(ok, cand, t), jnp.where(ok, c, cnt)
    return lax.fori_loop(0, n_bits, bit_step, (jnp.zeros(shape, jnp.int32), count_all))


def _fold_rows(x, rows, op=jnp.add, chains=4):
    parts = [x[r:r + rows] for r in range(0, x.shape[0], rows)]
    acc = parts[:chains]
    for k, part in enumerate(parts[chains:]):
        acc[k % len(acc)] = op(acc[k % len(acc)], part)
    while len(acc) > 1:
        acc = [op(acc[k], acc[k + 1]) for k in range(0, len(acc) - 1, 2)] + ([acc[-1]] if len(acc) % 2 else [])
    return acc[0]


def _tie_cutoff(count_tie_below, need, n_bits, shape):
    def bit_step(b, x):
        cand = jnp.bitwise_or(x, jnp.left_shift(jnp.int32(1), n_bits - 1 - b))
        return jnp.where(count_tie_below(cand) < need, cand, x)
    return lax.fori_loop(0, n_bits, bit_step, jnp.zeros(shape, jnp.int32))


def _attn_kernel(q_ref, qi_ref, kiwi_ref, kb_ref, vt_ref, kib_ref, o_ref,
                 score_s, bias_s, s_scr, acc_s):
    i = pl.program_id(1)
    t0 = i * Q_BLOCK
    n_steps = (t0 + Q_BLOCK + ATT_CHUNK - 1) // ATT_CHUNK
    step_iota = lax.broadcasted_iota(jnp.int32, (ATT_CHUNK, Q_BLOCK), 0)
    sub_iota = lax.broadcasted_iota(jnp.int32, (KEY_CHUNK, Q_BLOCK), 0)
    q_pos = t0 + lax.broadcasted_iota(jnp.int32, (1, Q_BLOCK), 1)
    lane_shape = (1, Q_BLOCK)

    def step_rows(c):
        return pl.ds(pl.multiple_of(c * ATT_CHUNK, ATT_CHUNK), ATT_CHUNK)

    w_t = kiwi_ref[...].T[IDX_DIM:IDX_DIM + N_IDX_HEADS, :]
    qi = qi_ref[...]
    n_pairs = D_IDX // LANES
    qi_rows = jnp.concatenate([qi[:, p * LANES:(p + 1) * LANES] for p in range(n_pairs)], axis=0)

    def score_step(c, causal):
        for sub in range(ATT_CHUNK // KEY_CHUNK):
            r0 = pl.multiple_of(c * ATT_CHUNK + sub * KEY_CHUNK, KEY_CHUNK)
            rows = pl.ds(r0, KEY_CHUNK)
            s_par = [_dot_nt(kib_ref[rows, par * LANES:(par + 1) * LANES], qi_rows) for par in range(2)]
            score = jnp.zeros((KEY_CHUNK, Q_BLOCK), jnp.float32)
            for h in range(N_IDX_HEADS):
                s_h = s_par[h % 2][:, (h // 2) * Q_BLOCK:(h // 2 + 1) * Q_BLOCK]
                score = score + jnp.maximum(s_h, 0.0) * w_t[h:h + 1, :]
            if causal:
                score = jnp.where((r0 + sub_iota) <= q_pos, score, NEG_INF)
            score_s[rows, :] = score

    def early_score_step(c, carry):
        score_step(c, False)
        return carry

    lax.fori_loop(0, n_steps - 1, early_score_step, 0)
    score_step(n_steps - 1, True)

    def count_where(pred):
        def body(c, acc):
            one = jnp.where(pred(score_s[step_rows(c), :], c * ATT_CHUNK + step_iota), 1.0, 0.0)
            return acc + _fold_rows(one, SUBLANES)
        acc = lax.fori_loop(0, n_steps, body, jnp.zeros((SUBLANES, Q_BLOCK), jnp.float32))
        return jnp.sum(acc, axis=0, keepdims=True)

    def count_ge(t_unsigned):
        cand = _threshold_value(t_unsigned)
        return count_where(lambda blk, idx: blk >= cand)

    count_all = jnp.full(lane_shape, 1.0, jnp.float32) * (n_steps * ATT_CHUNK).astype(jnp.float32)
    thr_u, cnt_ge = _greedy_bits(count_ge, 32, lane_shape, count_all)
    thr = _threshold_value(thr_u)

    def write_bias(select):
        def step(c, causal):
            blk = score_s[step_rows(c), :]
            idx = c * ATT_CHUNK + step_iota
            sel = select(blk, idx)
            if causal:
                sel = jnp.logical_and(sel, idx <= q_pos)
            bias_s[step_rows(c), :] = jnp.where(sel, 0.0, NEG_INF)

        def early_step(c, carry):
            step(c, False)
            return carry

        lax.fori_loop(0, n_steps - 1, early_step, 0)
        step(n_steps - 1, True)

    has_tie = jnp.max(jnp.where(cnt_ge > TOPK, 1.0, 0.0)) > 0.5

    @pl.when(jnp.logical_not(has_tie))
    def _():
        write_bias(lambda blk, idx: blk >= thr)

    @pl.when(has_tie)
    def _():
        need = TOPK - count_where(lambda blk, idx: blk > thr)
        cutoff = _tie_cutoff(
            lambda x: count_where(lambda blk, idx: jnp.logical_and(blk == thr, idx < x)),
            need, int(np.log2(SEQ)), lane_shape)
        write_bias(lambda blk, idx: jnp.logical_or(blk > thr, jnp.logical_and(blk == thr, idx <= cutoff)))

    q = q_ref[...]
    n_lanes = HEADS_PER_KV * Q_BLOCK
    q_rows = [
        jnp.concatenate([q[:, (g * HEADS_PER_KV + h) * HEAD_DIM:(g * HEADS_PER_KV + h + 1) * HEAD_DIM]
                         for h in range(HEADS_PER_KV)], axis=0)
        for g in range(N_KV_HEADS)]
    acc_s[...] = jnp.zeros(acc_s.shape, jnp.float32)
    subs = ATT_CHUNK // KEY_CHUNK

    ones_rows = jnp.ones((PACKED_ROWS, KEY_CHUNK), jnp.bfloat16)

    d_sl = [slice(g * HEAD_DIM, (g + 1) * HEAD_DIM) for g in range(N_KV_HEADS)]

    def logits(c, slot):
        maxima = []
        for g in range(N_KV_HEADS):
            mx = None
            for r in range(0, ATT_CHUNK, KEY_CHUNK):
                rows = pl.ds(pl.multiple_of(c * ATT_CHUNK + r, KEY_CHUNK), KEY_CHUNK)
                b = bias_s[rows, :]
                x = _dot_nt(kb_ref[rows, d_sl[g]], q_rows[g]) + jnp.concatenate([b] * HEADS_PER_KV, axis=1)
                s_scr[slot, g, r:r + KEY_CHUNK, :] = x
                f = _fold_rows(x, SUBLANES, jnp.maximum)
                mx = f if mx is None else jnp.maximum(mx, f)
            maxima.append(jnp.max(mx, axis=0, keepdims=True))
        return tuple(maxima)

    def accumulate(c, slot, m_old, m_step):
        m_out = []
        for g in range(N_KV_HEADS):
            m_new = jnp.maximum(m_old[g], m_step[g])
            m_safe = jnp.where(m_new == NEG_INF, 0.0, m_new)
            alpha = jnp.exp2(m_old[g] - m_safe)
            pv = None
            for sub in range(subs):
                r = sub * KEY_CHUNK
                pb = _bf16(jnp.exp2(s_scr[slot, g, r:r + KEY_CHUNK, :] - m_safe))
                lhs = jnp.concatenate([vt_ref[c * subs + sub, d_sl[g], :], ones_rows], axis=0)
                part = _dot(lhs, pb)
                pv = part if pv is None else pv + part
            acc_s[g] = acc_s[g] * alpha + pv
            m_out.append(m_new)
        return tuple(m_out)

    def attend_pair(p, carry):
        m_run, m_even = carry
        c = 2 * p
        m_odd = logits(c + 1, 1)
        m_run = accumulate(c, 0, m_run, m_even)
        m_even = logits(jnp.minimum(c + 2, n_steps - 1), 0)
        m_run = accumulate(c + 1, 1, m_run, m_odd)
        return m_run, m_even

    m_init = tuple(jnp.full((1, n_lanes), NEG_INF, jnp.float32) for _ in range(N_KV_HEADS))
    m_run, m_even = lax.fori_loop(0, n_steps // 2, attend_pair, (m_init, logits(0, 0)))

    @pl.when(n_steps % 2 == 1)
    def _():
        accumulate(n_steps - 1, 0, m_run, m_even)

    for g in range(N_KV_HEADS):
        o_t = acc_s[g, 0:HEAD_DIM, :] / acc_s[g, HEAD_DIM:HEAD_DIM + 1, :]
        for h in range(HEADS_PER_KV):
            col = (g * HEADS_PER_KV + h) * HEAD_DIM
            o_ref[:, col:col + HEAD_DIM] = _bf16(o_t[:, h * Q_BLOCK:(h + 1) * Q_BLOCK].T)


def _prompt_attention(q, qi, kiwi, kb, vt, kib):
    nq = SEQ // Q_BLOCK
    blk = lambda w: pl.BlockSpec((None, Q_BLOCK, w), lambda b, i: (b, i, 0))
    return pl.pallas_call(
        _attn_kernel,
        grid=(BATCH, nq),
        in_specs=[
            blk(D_ATT), blk(D_IDX), blk(LANES),
            pl.BlockSpec((None, SEQ, D_KV), lambda b, i: (b, 0, 0)),
            pl.BlockSpec((None, SEQ // KEY_CHUNK, D_KV, KEY_CHUNK), lambda b, i: (b, 0, 0, 0)),
            pl.BlockSpec((None, SEQ, 2 * LANES), lambda b, i: (b, 0, 0)),
        ],
        out_specs=blk(D_ATT),
        out_shape=jax.ShapeDtypeStruct((BATCH, SEQ, D_ATT), jnp.bfloat16),
        scratch_shapes=[
            pltpu.VMEM((SEQ, Q_BLOCK), jnp.float32),
            pltpu.VMEM((SEQ, Q_BLOCK), jnp.float32),
            pltpu.VMEM((2, N_KV_HEADS, ATT_CHUNK, HEADS_PER_KV * Q_BLOCK), jnp.float32),
            pltpu.VMEM((N_KV_HEADS, HEAD_DIM + PACKED_ROWS, HEADS_PER_KV * Q_BLOCK), jnp.float32),
        ],
        compiler_params=pltpu.CompilerParams(
            dimension_semantics=("arbitrary", "arbitrary"), vmem_limit_bytes=VMEM_LIMIT),
        name="prompt_attention",
    )(q, qi, kiwi, kb, vt, kib)


def _out_kernel(o_ref, sgb_ref, mpa_ref, smb_ref, x_ref, gate_ref, w_pb_ref, w_o_ref, gf_ref, y_ref):
    f32 = jnp.float32
    yb = _dot(_bf16(o_ref[...].astype(f32) * sgb_ref[...].astype(f32)), w_pb_ref[...])
    m = mpa_ref[...].astype(f32) + smb_ref[...].astype(f32) * yb
    r = x_ref[...] + gate_ref[...] * _dot(_bf16(m), w_o_ref[...])
    y_ref[...] = r * lax.rsqrt(jnp.mean(r * r, axis=-1, keepdims=True) + EPS) * gf_ref[...]


def _output_projection(o, sgb, mpa, smb, x, gate, wts, rows):
    ng, nr, _ = x.shape
    row_spec = pl.BlockSpec((None, rows, D_MODEL), lambda b, j: (b, j, 0))
    if gate.shape[1] == 1:
        gate_spec = pl.BlockSpec((None, 1, D_MODEL), lambda b, j: (b, 0, 0))
    else:
        gate_spec = row_spec
    return pl.pallas_call(
        _out_kernel,
        grid=(ng, nr // rows),
        in_specs=[row_spec] * 5 + [gate_spec, _const_spec((D_ATT, D_MODEL)), _const_spec((D_MODEL, D_MODEL)),
                                   _const_spec((1, D_MODEL))],
        out_specs=row_spec,
        out_shape=jax.ShapeDtypeStruct(x.shape, jnp.float32),
        compiler_params=pltpu.CompilerParams(
            dimension_semantics=("arbitrary", "arbitrary"), vmem_limit_bytes=VMEM_LIMIT),
        name="output_projection",
    )(o, sgb, mpa, smb, x, gate, wts["w_pb"], wts["w_o"], wts["g_final"])


def _sample_proj_kernel(x_ref, shift_ref, scale_ref, gn_ref, w_in_ref, wconv_ref, bconv_ref, w_ra_ref, b_ra_ref,
                        w_rx_ref, b_rx_ref, lam_ref, ig_ref, ib_ref, w_pa_ref,
                        cos_h_ref, sin_h_ref, cos_i_ref, sin_ia_ref, sin_ib_ref, buf_ref, h0_ref,
                        k_ref, v_ref, q_ref, qi_ref, kiwi_ref, mpa_ref, sgb_ref, smb_ref, conv_ref, lru_ref):
    xn = _modulated_norm(x_ref[...], gn_ref[...], scale_ref[...], shift_ref[...])
    xa = _dot(xn, w_in_ref[:, C_XA:C_XA + D_RNN])
    xc = bconv_ref[...]
    for t in range(CONV_W - 1):
        xc = xc + buf_ref[t] * wconv_ref[t:t + 1, :]
        if t > 0:
            conv_ref[t - 1] = buf_ref[t]
    xc = xc + xa * wconv_ref[CONV_W - 1:CONV_W, :]
    conv_ref[CONV_W - 2] = xa
    a, u = _lru_gates(xc, w_ra_ref, b_ra_ref[...], w_rx_ref, b_rx_ref[...], lam_ref[...])
    h = a * h0_ref[...] + u
    lru_ref[...] = h
    ga = _dot(xn, w_in_ref[:, C_GA:C_GA + D_RNN])
    ya = _dot(_bf16(h * _silu(ga)), w_pa_ref[...])
    mpa_ref[...] = _sigmoid(_dot(xn, w_in_ref[:, C_MA:C_MA + D_MODEL])) * ya
    sgb_ref[...] = _silu(_dot(xn, w_in_ref[:, C_GB:C_GB + D_ATT]))
    smb_ref[...] = _sigmoid(_dot(xn, w_in_ref[:, C_MB:C_MB + D_MODEL]))

    cos_h, sin_h = cos_h_ref[0:1, :], sin_h_ref[0:1, :]
    cos_i, sin_ia, sin_ib = cos_i_ref[0:1, :], sin_ia_ref[0:1, :], sin_ib_ref[0:1, :]
    zq = _dot(xn, w_in_ref[:, C_Q:C_Q + D_ATT])
    for hd in range(N_HEADS):
        sl = slice(hd * HEAD_DIM, (hd + 1) * HEAD_DIM)
        q_ref[:, sl] = _bf16(_rot_head(zq[:, sl], cos_h, sin_h))
    zk = _dot(xn, w_in_ref[:, C_K:C_K + D_KV])
    for g in range(N_KV_HEADS):
        sl = slice(g * HEAD_DIM, (g + 1) * HEAD_DIM)
        k_ref[:, sl] = _rot_head(zk[:, sl], cos_h, sin_h)
    v_ref[...] = _dot(xn, w_in_ref[:, C_V:C_V + D_KV])
    zqi = _dot(xn, w_in_ref[:, C_QI:C_QI + D_IDX])
    for p in range(D_IDX // LANES):
        sl = slice(p * LANES, (p + 1) * LANES)
        qi_ref[:, sl] = _bf16(_rot_idx(zqi[:, sl], cos_i, sin_ia, sin_ib))
    kiwi_ref[...] = _idx_key_slab(_dot(xn, w_in_ref[:, C_KW:C_KW + LANES]), ig_ref[...], ib_ref[...],
                                  cos_i, sin_ia, sin_ib)


def _sample_projection(x, shift, scale, wts, tabs, buf_t, h0):
    n = DEC_BATCH
    f32, bf16 = jnp.float32, jnp.bfloat16
    in_specs = [
        _const_spec((n, D_MODEL)), _const_spec((n, D_MODEL)), _const_spec((n, D_MODEL)), _const_spec((1, D_MODEL)),
        _const_spec((D_MODEL, D_IN_PACKED)), _const_spec((CONV_W, D_RNN)), _const_spec((1, D_RNN)),
        _const_spec((RNN_BLOCKS, RNN_BLOCK_W, RNN_BLOCK_W)), _const_spec((1, D_RNN)),
        _const_spec((RNN_BLOCKS, RNN_BLOCK_W, RNN_BLOCK_W)), _const_spec((1, D_RNN)), _const_spec((1, D_RNN)),
        _const_spec((1, LANES)), _const_spec((1, LANES)), _const_spec((D_RNN, D_MODEL)),
    ] + [_const_spec((SUBLANES, LANES))] * 5 + [_const_spec((CONV_W - 1, n, D_RNN)), _const_spec((n, D_RNN))]
    shapes = [
        ((n, D_KV), f32), ((n, D_KV), f32), ((n, D_ATT), bf16), ((n, D_IDX), bf16), ((n, LANES), f32),
        ((n, D_MODEL), f32), ((n, D_ATT), f32), ((n, D_MODEL), f32), ((CONV_W - 1, n, D_RNN), f32), ((n, D_RNN), f32),
    ]
    return pl.pallas_call(
        _sample_proj_kernel,
        grid=(1,),
        in_specs=in_specs,
        out_specs=[_const_spec(s, single=False) for s, _ in shapes],
        out_shape=[jax.ShapeDtypeStruct(s, d) for s, d in shapes],
        compiler_params=pltpu.CompilerParams(vmem_limit_bytes=VMEM_LIMIT),
        name="sample_projection",
    )(x, shift, scale, wts["g_norm"], wts["w_in"], wts["w_conv"], wts["b_conv"], wts["w_ra"], wts["b_ra"],
      wts["w_rx"], wts["b_rx"], wts["lam"], wts["idx_g"], wts["idx_b"], wts["w_pa"], *tabs, buf_t, h0)


def _sample_score_kernel(pt_ref, qi_ref, w_ref, kinew_ref, idx_hbm, o_ref, buf, sem):
    b = pl.program_id(0)
    slot = b % 2

    def page_copy(sample, p, sl):
        return pltpu.make_async_copy(idx_hbm.at[pt_ref[sample * N_PAGES + p]], buf.at[sl, p], sem.at[sl])

    def start_sample(sample, sl):
        def body(p, carry):
            page_copy(sample, p, sl).start()
            return carry
        lax.fori_loop(0, N_PAGES, body, 0)

    @pl.when(b == 0)
    def _():
        start_sample(0, 0)

    @pl.when(b + 1 < pl.num_programs(0))
    def _():
        start_sample(b + 1, 1 - slot)

    def wait_page(p, carry):
        page_copy(b, p, slot).wait()
        return carry

    lax.fori_loop(0, N_PAGES, wait_page, 0)

    qi = qi_ref[...]
    w = w_ref[...]

    def score_pages(i, carry):
        p0 = i * SCORE_PAGES
        kt = _bf16(jnp.concatenate([buf[slot, p0 + t] for t in range(SCORE_PAGES)], axis=1))
        s = _dot(qi, kt)
        score = jnp.sum(jnp.maximum(s, 0.0) * w, axis=0, keepdims=True)
        for t in range(SCORE_PAGES):
            o_ref[pl.ds(p0 + t, 1), :] = score[:, t * PAGE_SIZE:(t + 1) * PAGE_SIZE]
        return carry

    lax.fori_loop(0, N_PAGES // SCORE_PAGES, score_pages, 0)

    k_self = _bf16(kinew_ref[...][:, :IDX_DIM]).astype(jnp.float32)
    s_self = jnp.sum(qi.astype(jnp.float32) * k_self, axis=1, keepdims=True)
    score_self = jnp.sum(jnp.maximum(s_self, 0.0) * w, axis=0, keepdims=True)
    lane = lax.broadcasted_iota(jnp.int32, (1, PAGE_SIZE), 1)
    o_ref[N_PAGES:N_PAGES + 1, :] = jnp.where(lane == 0, score_self, NEG_INF)


def _sample_scores(page_table_flat, idx_pages, qi3, w_col, kiwi3):
    per_sample = lambda r, w: pl.BlockSpec((None, r, w), lambda b, pt: (b, 0, 0))
    return pl.pallas_call(
        _sample_score_kernel,
        grid_spec=pltpu.PrefetchScalarGridSpec(
            num_scalar_prefetch=1,
            grid=(DEC_BATCH,),
            in_specs=[per_sample(N_IDX_HEADS, IDX_DIM), per_sample(N_IDX_HEADS, 1), per_sample(1, LANES),
                      pl.BlockSpec(memory_space=pl.ANY)],
            out_specs=per_sample(N_PAGES + 1, PAGE_SIZE),
            scratch_shapes=[pltpu.VMEM((2, N_PAGES, IDX_DIM, PAGE_SIZE), jnp.float32),
                            pltpu.SemaphoreType.DMA((2,))],
        ),
        out_shape=jax.ShapeDtypeStruct((DEC_BATCH, N_PAGES + 1, PAGE_SIZE), jnp.float32),
        compiler_params=pltpu.CompilerParams(dimension_semantics=("arbitrary",), vmem_limit_bytes=VMEM_LIMIT),
        name="sample_scores",
    )(page_table_flat, qi3, w_col, kiwi3, idx_pages)


def _sample_select_kernel(score_ref, bias_ref):
    keys = score_ref[...]
    idx = lax.broadcasted_iota(jnp.int32, keys.shape, 1)
    col_shape = (keys.shape[0], 1)
    count = lambda pred: jnp.sum(jnp.where(pred, 1.0, 0.0), axis=1, keepdims=True)
    total = jnp.full(col_shape, float(keys.shape[1]), jnp.float32)
    thr_u, _ = _greedy_bits(lambda t: count(keys >= _threshold_value(t)), 32, col_shape, total)
    thr = _threshold_value(thr_u)
    need = TOPK - count(keys > thr)
    tie = keys == thr
    n_bits = int(np.ceil(np.log2(keys.shape[1])))
    cutoff = _tie_cutoff(lambda x: count(jnp.logical_and(tie, idx < x)), need, n_bits, col_shape)
    sel = jnp.logical_or(keys > thr, jnp.logical_and(tie, idx <= cutoff))
    bias_ref[...] = jnp.where(sel, 0.0, NEG_INF)


def _sample_select(scores):
    return pl.pallas_call(
        _sample_select_kernel,
        grid=(1,),
        in_specs=[_const_spec(scores.shape)],
        out_specs=_const_spec(scores.shape, single=False),
        out_shape=jax.ShapeDtypeStruct(scores.shape, jnp.float32),
        compiler_params=pltpu.CompilerParams(vmem_limit_bytes=VMEM_LIMIT),
        name="sample_select",
    )(scores)


def _sample_attn_kernel(pt_ref, bias_ref, bias_self_ref, q_ref, knew_ref, vnew_ref, k_hbm, v_hbm, o_ref,
                        kbuf, vbuf, sem, m_s, l_s, acc_s):
    b, j = pl.program_id(0), pl.program_id(1)
    n_j = pl.num_programs(1)
    step = b * n_j + j
    slot = step % 2

    def page_copies(st, sl):
        out = []
        for t in range(ATTN_PAGES):
            page = pt_ref[st * ATTN_PAGES + t]
            out.append(pltpu.make_async_copy(k_hbm.at[page], kbuf.at[sl, t], sem.at[0, sl]))
            out.append(pltpu.make_async_copy(v_hbm.at[page], vbuf.at[sl, t], sem.at[1, sl]))
        return out

    @pl.when(step == 0)
    def _():
        for c in page_copies(0, 0):
            c.start()

    @pl.when(step + 1 < pl.num_programs(0) * n_j)
    def _():
        for c in page_copies(step + 1, 1 - slot):
            c.start()

    for c in page_copies(step, slot):
        c.wait()

    @pl.when(j == 0)
    def _():
        m_s[...] = jnp.full(m_s.shape, NEG_INF, jnp.float32)
        l_s[...] = jnp.zeros(l_s.shape, jnp.float32)
        acc_s[...] = jnp.zeros(acc_s.shape, jnp.float32)

    def online_update(s, pv_of):
        m_old = m_s[...]
        m_new = jnp.maximum(m_old, jnp.max(s, axis=1, keepdims=True))
        m_safe = jnp.where(m_new == NEG_INF, 0.0, m_new)
        pr = jnp.exp2((s - m_safe) * SOFTMAX_SCALE_LOG2E)
        alpha = jnp.exp2((m_old - m_safe) * SOFTMAX_SCALE_LOG2E)
        l_s[...] = alpha * l_s[...] + jnp.sum(pr, axis=1, keepdims=True)
        acc_s[...] = acc_s[...] * alpha + pv_of(_bf16(pr))
        m_s[...] = m_new

    q = q_ref[...]
    rows_per_page = PAGE_SIZE * N_KV_HEADS
    n_cols = ATTN_PAGES * rows_per_page
    k_all = _bf16(kbuf[slot].reshape(n_cols, HEAD_DIM))
    v_all = _bf16(vbuf[slot].reshape(n_cols, HEAD_DIM))
    dup = jnp.where(lax.broadcasted_iota(jnp.int32, (PAGE_SIZE, rows_per_page), 1) // N_KV_HEADS
                    == lax.broadcasted_iota(jnp.int32, (PAGE_SIZE, rows_per_page), 0), 1.0, 0.0)
    sel_pages = _dot(_bf16(jnp.where(bias_ref[...] == 0.0, 1.0, 0.0)), _bf16(dup))
    sel_row = jnp.concatenate([sel_pages[t:t + 1, :] for t in range(ATTN_PAGES)], axis=1)
    head = lax.broadcasted_iota(jnp.int32, (N_HEADS, n_cols), 0)
    col = lax.broadcasted_iota(jnp.int32, (N_HEADS, n_cols), 1)
    own = (col % N_KV_HEADS) == (head // HEADS_PER_KV)
    s = jnp.where(jnp.logical_and(own, sel_row > 0.5), _dot_nt(q, k_all), NEG_INF)
    online_update(s, lambda pb: _dot(pb, v_all))

    @pl.when(j == n_j - 1)
    def _():
        head_d = lax.broadcasted_iota(jnp.int32, (N_HEADS, HEAD_DIM), 0) // HEADS_PER_KV

        def own_row(ref):
            rows = _bf16(ref[...]).astype(jnp.float32)
            out = jnp.broadcast_to(rows[N_KV_HEADS - 1:N_KV_HEADS, :], (N_HEADS, HEAD_DIM))
            for g in range(N_KV_HEADS - 2, -1, -1):
                out = jnp.where(head_d == g, rows[g:g + 1, :], out)
            return out

        s_self = jnp.sum(q.astype(jnp.float32) * own_row(knew_ref), axis=1, keepdims=True)
        v_own = own_row(vnew_ref)
        online_update(s_self + bias_self_ref[...][:, 0:1], lambda pb: pb.astype(jnp.float32) * v_own)
        o_ref[...] = acc_s[...] / l_s[...]


def _sample_attention(page_table_flat, k_pages, v_pages, bias3, bias_self, q3, knew3, vnew3):
    per_sample = lambda r, w: pl.BlockSpec((None, r, w), lambda b, j, pt: (b, 0, 0))
    rows_per_page = PAGE_SIZE * N_KV_HEADS
    return pl.pallas_call(
        _sample_attn_kernel,
        grid_spec=pltpu.PrefetchScalarGridSpec(
            num_scalar_prefetch=1,
            grid=(DEC_BATCH, N_PAGES // ATTN_PAGES),
            in_specs=[
                pl.BlockSpec((None, ATTN_PAGES, PAGE_SIZE), lambda b, j, pt: (b, j, 0)),
                per_sample(1, PAGE_SIZE), per_sample(N_HEADS, HEAD_DIM),
                per_sample(N_KV_HEADS, HEAD_DIM), per_sample(N_KV_HEADS, HEAD_DIM),
                pl.BlockSpec(memory_space=pl.ANY), pl.BlockSpec(memory_space=pl.ANY),
            ],
            out_specs=per_sample(N_HEADS, HEAD_DIM),
            scratch_shapes=[
                pltpu.VMEM((2, ATTN_PAGES, rows_per_page, HEAD_DIM), jnp.float32),
                pltpu.VMEM((2, ATTN_PAGES, rows_per_page, HEAD_DIM), jnp.float32),
                pltpu.SemaphoreType.DMA((2, 2)),
                pltpu.VMEM((N_HEADS, 1), jnp.float32), pltpu.VMEM((N_HEADS, 1), jnp.float32),
                pltpu.VMEM((N_HEADS, HEAD_DIM), jnp.float32),
            ],
        ),
        out_shape=jax.ShapeDtypeStruct((DEC_BATCH, N_HEADS, HEAD_DIM), jnp.float32),
        compiler_params=pltpu.CompilerParams(
            dimension_semantics=("arbitrary", "arbitrary"), vmem_limit_bytes=VMEM_LIMIT),
        name="sample_attention",
    )(page_table_flat, bias3, bias_self, q3, knew3, vnew3, k_pages, v_pages)


def _pack_w_in(w_in):
    xa, ga, q, k, v, gb, qi, ki, wi, ma, mb = jnp.split(w_in, np.cumsum(SPLITS)[:-1].tolist(), axis=-1)
    pad = jnp.zeros((D_MODEL, LANES - IDX_DIM - N_IDX_HEADS), w_in.dtype)
    return _bf16(jnp.concatenate([xa, ga, q, gb, ma, mb, k, v, qi, ki, wi, pad], axis=-1))


def _lane_pad(v):
    return jnp.pad(v.reshape(1, -1), ((0, 0), (0, LANES - v.shape[-1])))


def kernel(x_prompt, x_sample, cache_k, cache_v, cache_idx_k, state_conv, state_rglru, page_table, c_prompt, c_sample, w_ada, b_ada, g_norm, w_in, w_conv, b_conv, w_ra, b_ra, w_rx, b_rx, lru_lambda, idx_k_norm_g, idx_k_norm_b, w_pa, w_pb, w_o, g_final):
    assert w_in.shape[0] == 1, "one layer"
    wts = {
        "g_norm": g_norm[0].reshape(1, -1), "w_in": _pack_w_in(w_in[0]), "w_conv": w_conv[0],
        "b_conv": b_conv[0].reshape(1, -1), "w_ra": _bf16(w_ra[0]), "b_ra": b_ra[0].reshape(1, -1),
        "w_rx": _bf16(w_rx[0]), "b_rx": b_rx[0].reshape(1, -1), "lam": lru_lambda[0].reshape(1, -1),
        "idx_g": _lane_pad(idx_k_norm_g[0]), "idx_b": _lane_pad(idx_k_norm_b[0]),
        "w_pa": _bf16(w_pa[0]), "w_pb": _bf16(w_pb[0]), "w_o": _bf16(w_o[0]), "g_final": g_final.reshape(1, -1),
    }
    half_h, half_i = HEAD_DIM // 2, IDX_DIM // 2
    invf_h = ROPE_THETA ** (-jnp.arange(half_h, dtype=jnp.float32) / half_h)
    invf_i = ROPE_THETA ** (-jnp.arange(half_i, dtype=jnp.float32) / half_i)
    invf = jnp.zeros((SUBLANES, LANES), jnp.float32)
    invf = invf.at[0].set(jnp.tile(invf_h, LANES // half_h)).at[1].set(jnp.tile(invf_i, LANES // half_i))
    tabs_prompt = _rope_tables(invf, SEQ, 0, 1)
    tabs_sample = _rope_tables(invf, SUBLANES, PAST_LEN, 0)

    mod = _ada_modulation(jnp.concatenate([c_prompt, c_sample], axis=0), _bf16(w_ada[0]), b_ada[0].reshape(1, -1))
    shift, scale, gate = mod[:, :D_MODEL], mod[:, D_MODEL:2 * D_MODEL], mod[:, 2 * D_MODEL:]

    (k_p, v_p, ki_p, kb, vt, kib, q, qi, kiwi, mpa, sgb, smb, conv_p, lru_p) = _prompt_projection(
        x_prompt, shift[:BATCH, None, :], scale[:BATCH, None, :], wts, tabs_prompt)
    o = _prompt_attention(q, qi, kiwi, kb, vt, kib)
    y_prompt = _output_projection(o, sgb, mpa, smb, x_prompt, gate[:BATCH, None, :], wts, OUT_ROWS)

    xs = x_sample[:, 0, :]
    (k_s, v_s, q_s, qi_s, kiwi_s, mpa_s, sgb_s, smb_s, conv_s, lru_s) = _sample_projection(
        xs, shift[BATCH:], scale[BATCH:], wts, tabs_sample, jnp.swapaxes(state_conv[0], 0, 1), state_rglru[0])
    pt_flat = page_table.reshape(-1)
    w_col = kiwi_s[:, IDX_DIM:IDX_DIM + N_IDX_HEADS, None]
    idx_pages = jnp.swapaxes(cache_idx_k[0], 1, 2)
    kv_pages = lambda t: t[0].reshape(-1, PAGE_SIZE * N_KV_HEADS, HEAD_DIM)
    scores = _sample_scores(pt_flat, idx_pages, qi_s.reshape(DEC_BATCH, N_IDX_HEADS, IDX_DIM), w_col,
                            kiwi_s[:, None, :])
    bias = _sample_select(scores.reshape(DEC_BATCH, (N_PAGES + 1) * PAGE_SIZE))
    bias = bias.reshape(DEC_BATCH, N_PAGES + 1, PAGE_SIZE)
    o_s = _sample_attention(
        pt_flat, kv_pages(cache_k), kv_pages(cache_v), bias, bias[:, N_PAGES:, :],
        q_s.reshape(DEC_BATCH, N_HEADS, HEAD_DIM), k_s.reshape(DEC_BATCH, N_KV_HEADS, HEAD_DIM),
        v_s.reshape(DEC_BATCH, N_KV_HEADS, HEAD_DIM))
    y_sample = _output_projection(
        o_s.reshape(1, DEC_BATCH, D_ATT), sgb_s[None], mpa_s[None], smb_s[None], xs[None], gate[None, BATCH:],
        wts, DEC_BATCH)

    kv_p = lambda t: t.reshape(1, BATCH, SEQ, N_KV_HEADS, HEAD_DIM)
    kv_s = lambda t: t.reshape(1, DEC_BATCH, 1, N_KV_HEADS, HEAD_DIM)
    return (
        y_prompt, y_sample.reshape(DEC_BATCH, 1, D_MODEL),
        kv_p(k_p), kv_p(v_p), jnp.swapaxes(ki_p, 1, 2)[None], conv_p[None], lru_p.reshape(1, BATCH, D_RNN),
        kv_s(k_s), kv_s(v_s), kiwi_s[:, :IDX_DIM].reshape(1, DEC_BATCH, 1, IDX_DIM),
        jnp.swapaxes(conv_s, 0, 1)[None], lru_s[None],
    )
```

```python
import functools

import jax
import jax.numpy as jnp
import numpy as np
from jax import lax
from jax.experimental import pallas as pl
from jax.experimental.pallas import tpu as pltpu

D_MODEL = 1024
BATCH = 8
SEQ = 4096
DEC_BATCH = 32
PAST_LEN = 16384
PAGE_SIZE = 128
N_PAGES = PAST_LEN // PAGE_SIZE
D_RNN = D_MODEL
RNN_BLOCKS = 4
RNN_BLOCK_W = D_RNN // RNN_BLOCKS
CONV_W = 4
LRU_C = 8.0
N_HEADS = 8
HEAD_DIM = 128
N_KV_HEADS = 2
HEADS_PER_KV = N_HEADS // N_KV_HEADS
D_ATT = N_HEADS * HEAD_DIM
D_KV = N_KV_HEADS * HEAD_DIM
N_IDX_HEADS = 8
IDX_DIM = 64
D_IDX = N_IDX_HEADS * IDX_DIM
IDX_W_SCALE = (N_IDX_HEADS * IDX_DIM) ** -0.5
TOPK = 256
ROPE_THETA = 10000.0
EPS = 1e-6
SPLITS = (D_RNN, D_RNN, D_ATT, D_KV, D_KV, D_ATT, D_IDX, IDX_DIM, N_IDX_HEADS, D_MODEL, D_MODEL)

LANES = 128
SUBLANES = 8

C_XA, C_GA, C_Q, C_GB, C_MA, C_MB = 0, 1024, 2048, 3072, 4096, 5120
C_K, C_V, C_QI, C_KW = 6144, 6400, 6656, 7168
D_IN_PACKED = 7296

PROJ_ROWS = 256
Q_BLOCK = 256
KEY_CHUNK = 256
ATT_CHUNK = 512
PACKED_ROWS = 16
assert KEY_CHUNK >= TOPK and ATT_CHUNK % KEY_CHUNK == 0 and SEQ % ATT_CHUNK == 0 and ATT_CHUNK % Q_BLOCK == 0
assert PAGE_SIZE == HEAD_DIM == LANES
OUT_ROWS = 512
SCORE_PAGES = 16
ATTN_PAGES = 16
assert N_PAGES % SCORE_PAGES == 0 and N_PAGES % ATTN_PAGES == 0
SOFTMAX_SCALE_LOG2E = (HEAD_DIM ** -0.5) * float(np.log2(np.e))
NEG_INF = float("-inf")
INT_MIN = -2 ** 31
KEY_NEG_INF = INT_MIN + 0x7FFFFF
VMEM_LIMIT = 56 * 1024 * 1024


def _sigmoid(x):
    return 1.0 / (1.0 + jnp.exp(-x))


def _silu(x):
    return x * _sigmoid(x)


def _dot(a, b):
    return jnp.dot(a, b, preferred_element_type=jnp.float32)


def _dot_nt(a, b):
    return lax.dot_general(a, b, (((1,), (1,)), ((), ())), preferred_element_type=jnp.float32)


def _bf16(x):
    return x.astype(jnp.bfloat16)


def _const_spec(shape, single=True):
    nd = len(shape)
    kwargs = {"pipeline_mode": pl.Buffered(1)} if single else {}
    return pl.BlockSpec(shape, lambda *_: (0,) * nd, **kwargs)


def _rope_kernel(invf_ref, cos_h_ref, sin_h_ref, cos_i_ref, sin_ia_ref, sin_ib_ref, *, pos0, pos_step, rows):
    r0 = pl.program_id(0) * rows
    row = lax.broadcasted_iota(jnp.int32, (rows, LANES), 0) + r0
    lane = lax.broadcasted_iota(jnp.int32, (rows, LANES), 1)
    pos = (pos0 + pos_step * row).astype(jnp.float32)
    ang_h = pos * invf_ref[0:1, :]
    ang_i = pos * invf_ref[1:2, :]
    cos_h_ref[...] = jnp.cos(ang_h)
    sh = jnp.sin(ang_h)
    sin_h_ref[...] = jnp.where(lane < HEAD_DIM // 2, -sh, sh)
    cos_i_ref[...] = jnp.cos(ang_i)
    si = jnp.sin(ang_i)
    first_half = (lane % IDX_DIM) < IDX_DIM // 2
    sin_ia_ref[...] = jnp.where(first_half, -si, 0.0)
    sin_ib_ref[...] = jnp.where(first_half, 0.0, si)


def _rope_tables(invf, n, pos0, pos_step):
    rows = min(n, 512)
    out = jax.ShapeDtypeStruct((n, LANES), jnp.float32)
    spec = pl.BlockSpec((rows, LANES), lambda i: (i, 0))
    return pl.pallas_call(
        functools.partial(_rope_kernel, pos0=pos0, pos_step=pos_step, rows=rows),
        grid=(n // rows,),
        in_specs=[pl.BlockSpec((SUBLANES, LANES), lambda i: (0, 0))],
        out_specs=[spec] * 5,
        out_shape=[out] * 5,
        name="rope_tables",
    )(invf)


def _rot_head(z, cos, sin_signed):
    return z * cos + pltpu.roll(z, HEAD_DIM // 2, 1) * sin_signed


def _rot_idx(z, cos, sin_a, sin_b):
    return z * cos + pltpu.roll(z, LANES - IDX_DIM // 2, 1) * sin_a + pltpu.roll(z, IDX_DIM // 2, 1) * sin_b


def _ada_kernel(c_ref, w_ref, b_ref, o_ref):
    o_ref[...] = _dot(_bf16(_silu(c_ref[...])), w_ref[...]) + b_ref[...]


def _ada_modulation(c_all, w_ada, b_ada):
    n = c_all.shape[0]
    return pl.pallas_call(
        _ada_kernel,
        grid=(1,),
        in_specs=[_const_spec((n, D_MODEL)), _const_spec((D_MODEL, 3 * D_MODEL)), _const_spec((1, 3 * D_MODEL))],
        out_specs=_const_spec((n, 3 * D_MODEL), single=False),
        out_shape=jax.ShapeDtypeStruct((n, 3 * D_MODEL), jnp.float32),
        compiler_params=pltpu.CompilerParams(vmem_limit_bytes=VMEM_LIMIT),
        name="ada_modulation",
    )(c_all, w_ada, b_ada)


def _modulated_norm(x, g, scale, shift):
    y = x * lax.rsqrt(jnp.mean(x * x, axis=-1, keepdims=True) + EPS) * g
    return _bf16(y * (1.0 + scale) + shift)


def _lru_gates(xc, w_ra_ref, b_ra, w_rx_ref, b_rx, lam):
    xcb = _bf16(xc)
    r_parts, i_parts = [], []
    for n in range(RNN_BLOCKS):
        sl = slice(n * RNN_BLOCK_W, (n + 1) * RNN_BLOCK_W)
        r_parts.append(_dot(xcb[:, sl], w_ra_ref[n]))
        i_parts.append(_dot(xcb[:, sl], w_rx_ref[n]))
    r = _sigmoid(jnp.concatenate(r_parts, axis=1) + b_ra)
    i = _sigmoid(jnp.concatenate(i_parts, axis=1) + b_rx)
    neg_lam = -lam
    softplus = jnp.maximum(neg_lam, 0.0) + jnp.log1p(jnp.exp(-jnp.abs(neg_lam)))
    log_a = (-LRU_C) * r * softplus
    a = jnp.exp(log_a)
    u = jnp.sqrt(-jnp.tanh(log_a) * (a * a + 1.0)) * (i * xc)
    return a, u


def _idx_key_slab(z_kw, g, b, cos_i, sin_ia, sin_ib):
    lane = lax.broadcasted_iota(jnp.int32, z_kw.shape, 1)
    is_key = lane < IDX_DIM
    mu = jnp.sum(jnp.where(is_key, z_kw, 0.0), axis=-1, keepdims=True) * (1.0 / IDX_DIM)
    d = jnp.where(is_key, z_kw - mu, 0.0)
    var = jnp.sum(d * d, axis=-1, keepdims=True) * (1.0 / IDX_DIM)
    y = d * lax.rsqrt(var + EPS) * g + b
    key = _rot_idx(y, cos_i, sin_ia, sin_ib)
    is_w = jnp.logical_and(lane >= IDX_DIM, lane < IDX_DIM + N_IDX_HEADS)
    return key + jnp.where(is_w, z_kw * IDX_W_SCALE, 0.0)


def _proj_kernel(x_ref, shift_ref, scale_ref, gn_ref, w_in_ref, wconv_ref, bconv_ref, w_ra_ref, b_ra_ref,
                 w_rx_ref, b_rx_ref, lam_ref, ig_ref, ib_ref, w_pa_ref,
                 cos_h_ref, sin_h_ref, cos_i_ref, sin_ia_ref, sin_ib_ref,
                 k_ref, v_ref, ki_ref, kb_ref, vt_ref, kib_ref, q_ref, qi_ref, kiwi_ref,
                 mpa_ref, sgb_ref, smb_ref, conv_ref, lru_ref,
                 xa_ext, a_s, u_s, h_carry, tail_s):
    ts = PROJ_ROWS

    @pl.when(pl.program_id(1) == 0)
    def _():
        tail_s[...] = jnp.zeros(tail_s.shape, jnp.float32)
        h_carry[...] = jnp.zeros(h_carry.shape, jnp.float32)

    xn = _modulated_norm(x_ref[...], gn_ref[...], scale_ref[...], shift_ref[...])

    xa_ext[0:SUBLANES, :] = tail_s[...]
    xa_ext[SUBLANES:SUBLANES + ts, :] = _dot(xn, w_in_ref[:, C_XA:C_XA + D_RNN])
    xc = bconv_ref[...]
    for t in range(CONV_W):
        off = SUBLANES - (CONV_W - 1) + t
        xc = xc + xa_ext[off:off + ts, :] * wconv_ref[t:t + 1, :]
    conv_ref[...] = xa_ext[ts + SUBLANES - (CONV_W - 1):ts + SUBLANES, :]
    tail_s[...] = xa_ext[ts:ts + SUBLANES, :]

    a, u = _lru_gates(xc, w_ra_ref, b_ra_ref[...], w_rx_ref, b_rx_ref[...], lam_ref[...])
    a_s[...] = a
    u_s[...] = u
    row = lax.broadcasted_iota(jnp.int32, (SUBLANES, D_RNN), 0)

    hc = h_carry[...]
    for g in range(ts // SUBLANES):
        rows = slice(g * SUBLANES, (g + 1) * SUBLANES)
        a8 = a_s[rows, :]
        u8 = u_s[rows, :]
        for d in (1, 2, 4):
            keep = row >= d
            u8 = jnp.where(keep, a8 * pltpu.roll(u8, d, 0) + u8, u8)
            a8 = jnp.where(keep, a8 * pltpu.roll(a8, d, 0), a8)
        h8 = a8 * hc + u8
        u_s[rows, :] = h8
        hc = h8[SUBLANES - 1:SUBLANES, :]
    h_carry[...] = hc
    lru_ref[...] = hc

    ga = _dot(xn, w_in_ref[:, C_GA:C_GA + D_RNN])
    ya = _dot(_bf16(u_s[...] * _silu(ga)), w_pa_ref[...])
    ma = _dot(xn, w_in_ref[:, C_MA:C_MA + D_MODEL])
    mpa_ref[...] = _bf16(_sigmoid(ma) * ya)
    sgb_ref[...] = _bf16(_silu(_dot(xn, w_in_ref[:, C_GB:C_GB + D_ATT])))
    smb_ref[...] = _bf16(_sigmoid(_dot(xn, w_in_ref[:, C_MB:C_MB + D_MODEL])))

    cos_h, sin_h = cos_h_ref[...], sin_h_ref[...]
    cos_i, sin_ia, sin_ib = cos_i_ref[...], sin_ia_ref[...], sin_ib_ref[...]
    zq = _dot(xn, w_in_ref[:, C_Q:C_Q + D_ATT])
    for h in range(N_HEADS):
        sl = slice(h * HEAD_DIM, (h + 1) * HEAD_DIM)
        q_ref[:, sl] = _bf16(_rot_head(zq[:, sl], cos_h, sin_h) * SOFTMAX_SCALE_LOG2E)
    zk = _dot(xn, w_in_ref[:, C_K:C_K + D_KV])
    for g in range(N_KV_HEADS):
        sl = slice(g * HEAD_DIM, (g + 1) * HEAD_DIM)
        kr = _rot_head(zk[:, sl], cos_h, sin_h)
        k_ref[:, sl] = kr
        kb_ref[:, sl] = _bf16(kr)
    zv = _dot(xn, w_in_ref[:, C_V:C_V + D_KV])
    v_ref[...] = zv
    vt = _bf16(zv.T)
    for c in range(ts // KEY_CHUNK):
        vt_ref[c] = vt[:, c * KEY_CHUNK:(c + 1) * KEY_CHUNK]
    zqi = _dot(xn, w_in_ref[:, C_QI:C_QI + D_IDX])
    for p in range(D_IDX // LANES):
        sl = slice(p * LANES, (p + 1) * LANES)
        qi_ref[:, sl] = _bf16(_rot_idx(zqi[:, sl], cos_i, sin_ia, sin_ib))
    slab = _idx_key_slab(_dot(xn, w_in_ref[:, C_KW:C_KW + LANES]), ig_ref[...], ib_ref[...], cos_i, sin_ia, sin_ib)
    kiwi_ref[...] = slab
    ki_ref[...] = slab.T[:IDX_DIM, :]
    key_even = jnp.where(lax.broadcasted_iota(jnp.int32, slab.shape, 1) < IDX_DIM, slab, 0.0)
    kib_ref[:, 0:LANES] = _bf16(key_even)
    kib_ref[:, LANES:2 * LANES] = _bf16(pltpu.roll(key_even, IDX_DIM, 1))


def _prompt_projection(x, shift, scale, wts, tabs):
    ts = PROJ_ROWS
    nt = SEQ // ts
    f32, bf16 = jnp.float32, jnp.bfloat16
    row_spec = lambda w: pl.BlockSpec((None, ts, w), lambda b, j: (b, j, 0))
    bvec_spec = pl.BlockSpec((None, 1, D_MODEL), lambda b, j: (b, 0, 0))
    tab_spec = pl.BlockSpec((ts, LANES), lambda b, j: (j, 0))
    in_specs = [
        row_spec(D_MODEL), bvec_spec, bvec_spec, _const_spec((1, D_MODEL)),
        _const_spec((D_MODEL, D_IN_PACKED)), _const_spec((CONV_W, D_RNN)), _const_spec((1, D_RNN)),
        _const_spec((RNN_BLOCKS, RNN_BLOCK_W, RNN_BLOCK_W)), _const_spec((1, D_RNN)),
        _const_spec((RNN_BLOCKS, RNN_BLOCK_W, RNN_BLOCK_W)), _const_spec((1, D_RNN)), _const_spec((1, D_RNN)),
        _const_spec((1, LANES)), _const_spec((1, LANES)), _const_spec((D_RNN, D_MODEL)),
    ] + [tab_spec] * 5
    out_shape = [
        jax.ShapeDtypeStruct((BATCH, SEQ, D_KV), f32),
        jax.ShapeDtypeStruct((BATCH, SEQ, D_KV), f32),
        jax.ShapeDtypeStruct((BATCH, IDX_DIM, SEQ), f32),
        jax.ShapeDtypeStruct((BATCH, SEQ, D_KV), bf16),
        jax.ShapeDtypeStruct((BATCH, SEQ // KEY_CHUNK, D_KV, KEY_CHUNK), bf16),
        jax.ShapeDtypeStruct((BATCH, SEQ, 2 * LANES), bf16),
        jax.ShapeDtypeStruct((BATCH, SEQ, D_ATT), bf16),
        jax.ShapeDtypeStruct((BATCH, SEQ, D_IDX), bf16),
        jax.ShapeDtypeStruct((BATCH, SEQ, LANES), f32),
        jax.ShapeDtypeStruct((BATCH, SEQ, D_MODEL), bf16),
        jax.ShapeDtypeStruct((BATCH, SEQ, D_ATT), bf16),
        jax.ShapeDtypeStruct((BATCH, SEQ, D_MODEL), bf16),
        jax.ShapeDtypeStruct((BATCH, CONV_W - 1, D_RNN), f32),
        jax.ShapeDtypeStruct((BATCH, 1, D_RNN), f32),
    ]
    out_specs = [
        row_spec(D_KV), row_spec(D_KV), pl.BlockSpec((None, IDX_DIM, ts), lambda b, j: (b, 0, j)), row_spec(D_KV),
        pl.BlockSpec((None, ts // KEY_CHUNK, D_KV, KEY_CHUNK), lambda b, j: (b, j, 0, 0)),
        row_spec(2 * LANES), row_spec(D_ATT), row_spec(D_IDX), row_spec(LANES),
        row_spec(D_MODEL), row_spec(D_ATT), row_spec(D_MODEL),
        pl.BlockSpec((None, CONV_W - 1, D_RNN), lambda b, j: (b, 0, 0)),
        pl.BlockSpec((None, 1, D_RNN), lambda b, j: (b, 0, 0)),
    ]
    scratch = [
        pltpu.VMEM((ts + SUBLANES, D_RNN), f32), pltpu.VMEM((ts, D_RNN), f32),
        pltpu.VMEM((ts, D_RNN), f32), pltpu.VMEM((1, D_RNN), f32), pltpu.VMEM((SUBLANES, D_RNN), f32),
    ]
    return pl.pallas_call(
        _proj_kernel,
        grid=(BATCH, nt),
        in_specs=in_specs, out_specs=out_specs, out_shape=out_shape, scratch_shapes=scratch,
        compiler_params=pltpu.CompilerParams(
            dimension_semantics=("arbitrary", "arbitrary"), vmem_limit_bytes=VMEM_LIMIT),
        name="prompt_projection",
    )(x, shift, scale, wts["g_norm"], wts["w_in"], wts["w_conv"], wts["b_conv"], wts["w_ra"], wts["b_ra"],
      wts["w_rx"], wts["b_rx"], wts["lam"], wts["idx_g"], wts["idx_b"], wts["w_pa"], *tabs)


def _threshold_value(t_unsigned):
    key = jnp.maximum(jnp.bitwise_xor(t_unsigned, jnp.int32(INT_MIN)), jnp.int32(KEY_NEG_INF))
    bits = key ^ jnp.bitwise_and(jnp.right_shift(key, 31), jnp.int32(0x7FFFFFFF))
    return pltpu.bitcast(bits, jnp.float32)


def _greedy_bits(count_ge, n_bits, shape, count_all):
    def bit_step(b, carry):
        t, cnt = carry
        cand = jnp.bitwise_or(t, jnp.left_shift(jnp.int32(1), n_bits - 1 - b))
        c = count_ge(cand)
        ok = c >= TOPK
        return jnp.where(ok, cand, t), jnp.where(ok, c, cnt)
    return lax.fori_loop(0, n_bits, bit_step, (jnp.zeros(shape, jnp.int32), count_all))


def _fold_rows(x, rows, op=jnp.add, chains=4):
    parts = [x[r:r + rows] for r in range(0, x.shape[0], rows)]
    acc = parts[:chains]
    for k, part in enumerate(parts[chains:]):
        acc[k % len(acc)] = op(acc[k % len(acc)], part)
    while len(acc) > 1:
        acc = [op(acc[k], acc[k + 1]) for k in range(0, len(acc) - 1, 2)] + ([acc[-1]] if len(acc) % 2 else [])
    return acc[0]


def _tie_cutoff(count_tie_below, need, n_bits, shape):
    def bit_step(b, x):
        cand = jnp.bitwise_or(x, jnp.left_shift(jnp.int32(1), n_bits - 1 - b))
        return jnp.where(count_tie_below(cand) < need, cand, x)
    return lax.fori_loop(0, n_bits, bit_step, jnp.zeros(shape, jnp.int32))


def _attn_kernel(q_ref, qi_ref, kiwi_ref, kb_ref, vt_ref, kib_ref, o_ref,
                 score_s, hi_s, bias_s, s_scr, acc_s):
    i = pl.program_id(1)
    t0 = i * Q_BLOCK
    n_steps = (t0 + Q_BLOCK + ATT_CHUNK - 1) // ATT_CHUNK
    step_iota = lax.broadcasted_iota(jnp.int32, (ATT_CHUNK, Q_BLOCK), 0)
    sub_iota = lax.broadcasted_iota(jnp.int32, (KEY_CHUNK, Q_BLOCK), 0)
    q_pos = t0 + lax.broadcasted_iota(jnp.int32, (1, Q_BLOCK), 1)
    lane_shape = (1, Q_BLOCK)

    def step_rows(c):
        return pl.ds(pl.multiple_of(c * ATT_CHUNK, ATT_CHUNK), ATT_CHUNK)

    w_t = kiwi_ref[...].T[IDX_DIM:IDX_DIM + N_IDX_HEADS, :]
    qi = qi_ref[...]
    n_pairs = D_IDX // LANES
    qi_rows = jnp.concatenate([qi[:, p * LANES:(p + 1) * LANES] for p in range(n_pairs)], axis=0)

    def score_step(c, causal):
        for sub in range(ATT_CHUNK // KEY_CHUNK):
            r0 = pl.multiple_of(c * ATT_CHUNK + sub * KEY_CHUNK, KEY_CHUNK)
            rows = pl.ds(r0, KEY_CHUNK)
            s_par = [_dot_nt(kib_ref[rows, par * LANES:(par + 1) * LANES], qi_rows) for par in range(2)]
            score = jnp.zeros((KEY_CHUNK, Q_BLOCK), jnp.float32)
            for h in range(N_IDX_HEADS):
                s_h = s_par[h % 2][:, (h // 2) * Q_BLOCK:(h // 2 + 1) * Q_BLOCK]
                score = score + jnp.maximum(s_h, 0.0) * w_t[h:h + 1, :]
            if causal:
                score = jnp.where((r0 + sub_iota) <= q_pos, score, NEG_INF)
            score_s[rows, :] = score
            hi_s[rows, :] = _bf16(score)

    def early_score_step(c, carry):
        score_step(c, False)
        return carry

    lax.fori_loop(0, n_steps - 1, early_score_step, 0)
    score_step(n_steps - 1, True)

    def count_where(pred):
        def body(c, acc):
            sel = pred(score_s[step_rows(c), :], c * ATT_CHUNK + step_iota)
            parts = [_fold_rows(jnp.where(sel[:, t * LANES:(t + 1) * LANES], 1.0, 0.0), SUBLANES, chains=2)
                     for t in range(Q_BLOCK // LANES)]
            return acc + jnp.concatenate(parts, axis=1)
        acc = lax.fori_loop(0, n_steps, body, jnp.zeros((SUBLANES, Q_BLOCK), jnp.float32))
        return jnp.sum(acc, axis=0, keepdims=True)

    def count_rounded_ge(t16):
        cand = _threshold_value(jnp.left_shift(t16, 16)).astype(jnp.bfloat16)
        def body(c, acc):
            one = jnp.where(hi_s[step_rows(c), :] >= cand, jnp.bfloat16(1), jnp.bfloat16(0))
            return acc + _fold_rows(one, PACKED_ROWS, chains=2)
        acc = lax.fori_loop(0, n_steps, body, jnp.zeros((PACKED_ROWS, Q_BLOCK), jnp.bfloat16))
        return jnp.sum(acc.astype(jnp.float32), axis=0, keepdims=True)

    count_all = jnp.full(lane_shape, 1.0, jnp.float32) * (n_steps * ATT_CHUNK).astype(jnp.float32)
    t1, _ = _greedy_bits(count_rounded_ge, 16, lane_shape, count_all)

    base = jnp.left_shift(jnp.maximum(t1 - 1, 0), 16)

    def count_ge(offset):
        cand = _threshold_value(base + offset)
        return count_where(lambda blk, idx: blk >= cand)

    off, cnt_ge = _greedy_bits(count_ge, 18, lane_shape, count_ge(jnp.zeros(lane_shape, jnp.int32)))
    thr = _threshold_value(base + off)

    def write_bias(select):
        def step(c, causal):
            blk = score_s[step_rows(c), :]
            idx = c * ATT_CHUNK + step_iota
            sel = select(blk, idx)
            if causal:
                sel = jnp.logical_and(sel, idx <= q_pos)
            bias_s[step_rows(c), :] = jnp.where(sel, 0.0, NEG_INF)

        def early_step(c, carry):
            step(c, False)
            return carry

        lax.fori_loop(0, n_steps - 1, early_step, 0)
        step(n_steps - 1, True)

    has_tie = jnp.max(jnp.where(cnt_ge > TOPK, 1.0, 0.0)) > 0.5

    @pl.when(jnp.logical_not(has_tie))
    def _():
        write_bias(lambda blk, idx: blk >= thr)

    @pl.when(has_tie)
    def _():
        need = TOPK - count_where(lambda blk, idx: blk > thr)
        cutoff = _tie_cutoff(
            lambda x: count_where(lambda blk, idx: jnp.logical_and(blk == thr, idx < x)),
            need, int(np.log2(SEQ)), lane_shape)
        write_bias(lambda blk, idx: jnp.logical_or(blk > thr, jnp.logical_and(blk == thr, idx <= cutoff)))

    q = q_ref[...]
    n_lanes = HEADS_PER_KV * Q_BLOCK
    q_rows = [
        jnp.concatenate([q[:, (g * HEADS_PER_KV + h) * HEAD_DIM:(g * HEADS_PER_KV + h + 1) * HEAD_DIM]
                         for h in range(HEADS_PER_KV)], axis=0)
        for g in range(N_KV_HEADS)]
    acc_s[...] = jnp.zeros(acc_s.shape, jnp.float32)
    subs = ATT_CHUNK // KEY_CHUNK

    ones_rows = jnp.ones((PACKED_ROWS, KEY_CHUNK), jnp.bfloat16)

    d_sl = [slice(g * HEAD_DIM, (g + 1) * HEAD_DIM) for g in range(N_KV_HEADS)]

    def logits(c, slot):
        maxima = []
        for g in range(N_KV_HEADS):
            mx = None
            for r in range(0, ATT_CHUNK, KEY_CHUNK):
                rows = pl.ds(pl.multiple_of(c * ATT_CHUNK + r, KEY_CHUNK), KEY_CHUNK)
                b = bias_s[rows, :]
                x = _dot_nt(kb_ref[rows, d_sl[g]], q_rows[g]) + jnp.concatenate([b] * HEADS_PER_KV, axis=1)
                s_scr[slot, g, r:r + KEY_CHUNK, :] = x
                f = _fold_rows(x, SUBLANES, jnp.maximum)
                mx = f if mx is None else jnp.maximum(mx, f)
            maxima.append(jnp.max(mx, axis=0, keepdims=True))
        return tuple(maxima)

    def accumulate(c, slot, m_old, m_step):
        m_out = []
        for g in range(N_KV_HEADS):
            m_new = jnp.maximum(m_old[g], m_step[g])
            m_safe = jnp.where(m_new == NEG_INF, 0.0, m_new)
            alpha = jnp.exp2(m_old[g] - m_safe)
            pv = None
            for sub in range(subs):
                r = sub * KEY_CHUNK
                pb = _bf16(jnp.exp2(s_scr[slot, g, r:r + KEY_CHUNK, :] - m_safe))
                lhs = jnp.concatenate([vt_ref[c * subs + sub, d_sl[g], :], ones_rows], axis=0)
                part = _dot(lhs, pb)
                pv = part if pv is None else pv + part
            acc_s[g] = acc_s[g] * alpha + pv
            m_out.append(m_new)
        return tuple(m_out)

    def attend_pair(p, carry):
        m_run, m_even = carry
        c = 2 * p
        m_odd = logits(c + 1, 1)
        m_run = accumulate(c, 0, m_run, m_even)
        m_even = logits(jnp.minimum(c + 2, n_steps - 1), 0)
        m_run = accumulate(c + 1, 1, m_run, m_odd)
        return m_run, m_even

    m_init = tuple(jnp.full((1, n_lanes), NEG_INF, jnp.float32) for _ in range(N_KV_HEADS))
    m_run, m_even = lax.fori_loop(0, n_steps // 2, attend_pair, (m_init, logits(0, 0)))

    @pl.when(n_steps % 2 == 1)
    def _():
        accumulate(n_steps - 1, 0, m_run, m_even)

    for g in range(N_KV_HEADS):
        o_t = acc_s[g, 0:HEAD_DIM, :] / acc_s[g, HEAD_DIM:HEAD_DIM + 1, :]
        for h in range(HEADS_PER_KV):
            col = (g * HEADS_PER_KV + h) * HEAD_DIM
            o_ref[:, col:col + HEAD_DIM] = _bf16(o_t[:, h * Q_BLOCK:(h + 1) * Q_BLOCK].T)


def _prompt_attention(q, qi, kiwi, kb, vt, kib):
    nq = SEQ // Q_BLOCK
    blk = lambda w: pl.BlockSpec((None, Q_BLOCK, w), lambda b, i: (b, i, 0))
    return pl.pallas_call(
        _attn_kernel,
        grid=(BATCH, nq),
        in_specs=[
            blk(D_ATT), blk(D_IDX), blk(LANES),
            pl.BlockSpec((None, SEQ, D_KV), lambda b, i: (b, 0, 0)),
            pl.BlockSpec((None, SEQ // KEY_CHUNK, D_KV, KEY_CHUNK), lambda b, i: (b, 0, 0, 0)),
            pl.BlockSpec((None, SEQ, 2 * LANES), lambda b, i: (b, 0, 0)),
        ],
        out_specs=blk(D_ATT),
        out_shape=jax.ShapeDtypeStruct((BATCH, SEQ, D_ATT), jnp.bfloat16),
        scratch_shapes=[
            pltpu.VMEM((SEQ, Q_BLOCK), jnp.float32),
            pltpu.VMEM((SEQ, Q_BLOCK), jnp.bfloat16),
            pltpu.VMEM((SEQ, Q_BLOCK), jnp.float32),
            pltpu.VMEM((2, N_KV_HEADS, ATT_CHUNK, HEADS_PER_KV * Q_BLOCK), jnp.float32),
            pltpu.VMEM((N_KV_HEADS, HEAD_DIM + PACKED_ROWS, HEADS_PER_KV * Q_BLOCK), jnp.float32),
        ],
        compiler_params=pltpu.CompilerParams(
            dimension_semantics=("arbitrary", "arbitrary"), vmem_limit_bytes=VMEM_LIMIT),
        name="prompt_attention",
    )(q, qi, kiwi, kb, vt, kib)


def _out_kernel(o_ref, sgb_ref, mpa_ref, smb_ref, x_ref, gate_ref, w_pb_ref, w_o_ref, gf_ref, y_ref):
    f32 = jnp.float32
    yb = _dot(_bf16(o_ref[...].astype(f32) * sgb_ref[...].astype(f32)), w_pb_ref[...])
    m = mpa_ref[...].astype(f32) + smb_ref[...].astype(f32) * yb
    r = x_ref[...] + gate_ref[...] * _dot(_bf16(m), w_o_ref[...])
    y_ref[...] = r * lax.rsqrt(jnp.mean(r * r, axis=-1, keepdims=True) + EPS) * gf_ref[...]


def _output_projection(o, sgb, mpa, smb, x, gate, wts, rows):
    ng, nr, _ = x.shape
    row_spec = pl.BlockSpec((None, rows, D_MODEL), lambda b, j: (b, j, 0))
    if gate.shape[1] == 1:
        gate_spec = pl.BlockSpec((None, 1, D_MODEL), lambda b, j: (b, 0, 0))
    else:
        gate_spec = row_spec
    return pl.pallas_call(
        _out_kernel,
        grid=(ng, nr // rows),
        in_specs=[row_spec] * 5 + [gate_spec, _const_spec((D_ATT, D_MODEL)), _const_spec((D_MODEL, D_MODEL)),
                                   _const_spec((1, D_MODEL))],
        out_specs=row_spec,
        out_shape=jax.ShapeDtypeStruct(x.shape, jnp.float32),
        compiler_params=pltpu.CompilerParams(
            dimension_semantics=("arbitrary", "arbitrary"), vmem_limit_bytes=VMEM_LIMIT),
        name="output_projection",
    )(o, sgb, mpa, smb, x, gate, wts["w_pb"], wts["w_o"], wts["g_final"])


def _sample_proj_kernel(x_ref, shift_ref, scale_ref, gn_ref, w_in_ref, wconv_ref, bconv_ref, w_ra_ref, b_ra_ref,
                        w_rx_ref, b_rx_ref, lam_ref, ig_ref, ib_ref, w_pa_ref,
                        cos_h_ref, sin_h_ref, cos_i_ref, sin_ia_ref, sin_ib_ref, buf_ref, h0_ref,
                        k_ref, v_ref, q_ref, qi_ref, kiwi_ref, mpa_ref, sgb_ref, smb_ref, conv_ref, lru_ref):
    xn = _modulated_norm(x_ref[...], gn_ref[...], scale_ref[...], shift_ref[...])
    xa = _dot(xn, w_in_ref[:, C_XA:C_XA + D_RNN])
    xc = bconv_ref[...]
    for t in range(CONV_W - 1):
        xc = xc + buf_ref[t] * wconv_ref[t:t + 1, :]
        if t > 0:
            conv_ref[t - 1] = buf_ref[t]
    xc = xc + xa * wconv_ref[CONV_W - 1:CONV_W, :]
    conv_ref[CONV_W - 2] = xa
    a, u = _lru_gates(xc, w_ra_ref, b_ra_ref[...], w_rx_ref, b_rx_ref[...], lam_ref[...])
    h = a * h0_ref[...] + u
    lru_ref[...] = h
    ga = _dot(xn, w_in_ref[:, C_GA:C_GA + D_RNN])
    ya = _dot(_bf16(h * _silu(ga)), w_pa_ref[...])
    mpa_ref[...] = _sigmoid(_dot(xn, w_in_ref[:, C_MA:C_MA + D_MODEL])) * ya
    sgb_ref[...] = _silu(_dot(xn, w_in_ref[:, C_GB:C_GB + D_ATT]))
    smb_ref[...] = _sigmoid(_dot(xn, w_in_ref[:, C_MB:C_MB + D_MODEL]))

    cos_h, sin_h = cos_h_ref[0:1, :], sin_h_ref[0:1, :]
    cos_i, sin_ia, sin_ib = cos_i_ref[0:1, :], sin_ia_ref[0:1, :], sin_ib_ref[0:1, :]
    zq = _dot(xn, w_in_ref[:, C_Q:C_Q + D_ATT])
    for hd in range(N_HEADS):
        sl = slice(hd * HEAD_DIM, (hd + 1) * HEAD_DIM)
        q_ref[:, sl] = _bf16(_rot_head(zq[:, sl], cos_h, sin_h))
    zk = _dot(xn, w_in_ref[:, C_K:C_K + D_KV])
    for g in range(N_KV_HEADS):
        sl = slice(g * HEAD_DIM, (g + 1) * HEAD_DIM)
        k_ref[:, sl] = _rot_head(zk[:, sl], cos_h, sin_h)
    v_ref[...] = _dot(xn, w_in_ref[:, C_V:C_V + D_KV])
    zqi = _dot(xn, w_in_ref[:, C_QI:C_QI + D_IDX])
    for p in range(D_IDX // LANES):
        sl = slice(p * LANES, (p + 1) * LANES)
        qi_ref[:, sl] = _bf16(_rot_idx(zqi[:, sl], cos_i, sin_ia, sin_ib))
    kiwi_ref[...] = _idx_key_slab(_dot(xn, w_in_ref[:, C_KW:C_KW + LANES]), ig_ref[...], ib_ref[...],
                                  cos_i, sin_ia, sin_ib)


def _sample_projection(x, shift, scale, wts, tabs, buf_t, h0):
    n = DEC_BATCH
    f32, bf16 = jnp.float32, jnp.bfloat16
    in_specs = [
        _const_spec((n, D_MODEL)), _const_spec((n, D_MODEL)), _const_spec((n, D_MODEL)), _const_spec((1, D_MODEL)),
        _const_spec((D_MODEL, D_IN_PACKED)), _const_spec((CONV_W, D_RNN)), _const_spec((1, D_RNN)),
        _const_spec((RNN_BLOCKS, RNN_BLOCK_W, RNN_BLOCK_W)), _const_spec((1, D_RNN)),
        _const_spec((RNN_BLOCKS, RNN_BLOCK_W, RNN_BLOCK_W)), _const_spec((1, D_RNN)), _const_spec((1, D_RNN)),
        _const_spec((1, LANES)), _const_spec((1, LANES)), _const_spec((D_RNN, D_MODEL)),
    ] + [_const_spec((SUBLANES, LANES))] * 5 + [_const_spec((CONV_W - 1, n, D_RNN)), _const_spec((n, D_RNN))]
    shapes = [
        ((n, D_KV), f32), ((n, D_KV), f32), ((n, D_ATT), bf16), ((n, D_IDX), bf16), ((n, LANES), f32),
        ((n, D_MODEL), f32), ((n, D_ATT), f32), ((n, D_MODEL), f32), ((CONV_W - 1, n, D_RNN), f32), ((n, D_RNN), f32),
    ]
    return pl.pallas_call(
        _sample_proj_kernel,
        grid=(1,),
        in_specs=in_specs,
        out_specs=[_const_spec(s, single=False) for s, _ in shapes],
        out_shape=[jax.ShapeDtypeStruct(s, d) for s, d in shapes],
        compiler_params=pltpu.CompilerParams(vmem_limit_bytes=VMEM_LIMIT),
        name="sample_projection",
    )(x, shift, scale, wts["g_norm"], wts["w_in"], wts["w_conv"], wts["b_conv"], wts["w_ra"], wts["b_ra"],
      wts["w_rx"], wts["b_rx"], wts["lam"], wts["idx_g"], wts["idx_b"], wts["w_pa"], *tabs, buf_t, h0)


def _sample_score_kernel(pt_ref, qi_ref, w_ref, kinew_ref, idx_hbm, o_ref, buf, sem):
    b = pl.program_id(0)
    slot = b % 2

    def page_copy(sample, p, sl):
        return pltpu.make_async_copy(idx_hbm.at[pt_ref[sample * N_PAGES + p]], buf.at[sl, p], sem.at[sl])

    def start_sample(sample, sl):
        def body(p, carry):
            page_copy(sample, p, sl).start()
            return carry
        lax.fori_loop(0, N_PAGES, body, 0)

    @pl.when(b == 0)
    def _():
        start_sample(0, 0)

    @pl.when(b + 1 < pl.num_programs(0))
    def _():
        start_sample(b + 1, 1 - slot)

    def wait_page(p, carry):
        page_copy(b, p, slot).wait()
        return carry

    lax.fori_loop(0, N_PAGES, wait_page, 0)

    qi = qi_ref[...]
    w = w_ref[...]

    def score_pages(i, carry):
        p0 = i * SCORE_PAGES
        kt = _bf16(jnp.concatenate([buf[slot, p0 + t] for t in range(SCORE_PAGES)], axis=1))
        s = _dot(qi, kt)
        score = jnp.sum(jnp.maximum(s, 0.0) * w, axis=0, keepdims=True)
        for t in range(SCORE_PAGES):
            o_ref[pl.ds(p0 + t, 1), :] = score[:, t * PAGE_SIZE:(t + 1) * PAGE_SIZE]
        return carry

    lax.fori_loop(0, N_PAGES // SCORE_PAGES, score_pages, 0)

    k_self = _bf16(kinew_ref[...][:, :IDX_DIM]).astype(jnp.float32)
    s_self = jnp.sum(qi.astype(jnp.float32) * k_self, axis=1, keepdims=True)
    score_self = jnp.sum(jnp.maximum(s_self, 0.0) * w, axis=0, keepdims=True)
    lane = lax.broadcasted_iota(jnp.int32, (1, PAGE_SIZE), 1)
    o_ref[N_PAGES:N_PAGES + 1, :] = jnp.where(lane == 0, score_self, NEG_INF)


def _sample_scores(page_table_flat, idx_pages, qi3, w_col, kiwi3):
    per_sample = lambda r, w: pl.BlockSpec((None, r, w), lambda b, pt: (b, 0, 0))
    return pl.pallas_call(
        _sample_score_kernel,
        grid_spec=pltpu.PrefetchScalarGridSpec(
            num_scalar_prefetch=1,
            grid=(DEC_BATCH,),
            in_specs=[per_sample(N_IDX_HEADS, IDX_DIM), per_sample(N_IDX_HEADS, 1), per_sample(1, LANES),
                      pl.BlockSpec(memory_space=pl.ANY)],
            out_specs=per_sample(N_PAGES + 1, PAGE_SIZE),
            scratch_shapes=[pltpu.VMEM((2, N_PAGES, IDX_DIM, PAGE_SIZE), jnp.float32),
                            pltpu.SemaphoreType.DMA((2,))],
        ),
        out_shape=jax.ShapeDtypeStruct((DEC_BATCH, N_PAGES + 1, PAGE_SIZE), jnp.float32),
        compiler_params=pltpu.CompilerParams(dimension_semantics=("arbitrary",), vmem_limit_bytes=VMEM_LIMIT),
        name="sample_scores",
    )(page_table_flat, qi3, w_col, kiwi3, idx_pages)


def _sample_select_kernel(score_ref, bias_ref):
    keys = score_ref[...]
    idx = lax.broadcasted_iota(jnp.int32, keys.shape, 1)
    col_shape = (keys.shape[0], 1)
    count = lambda pred: jnp.sum(jnp.where(pred, 1.0, 0.0), axis=1, keepdims=True)
    total = jnp.full(col_shape, float(keys.shape[1]), jnp.float32)
    thr_u, _ = _greedy_bits(lambda t: count(keys >= _threshold_value(t)), 32, col_shape, total)
    thr = _threshold_value(thr_u)
    need = TOPK - count(keys > thr)
    tie = keys == thr
    n_bits = int(np.ceil(np.log2(keys.shape[1])))
    cutoff = _tie_cutoff(lambda x: count(jnp.logical_and(tie, idx < x)), need, n_bits, col_shape)
    sel = jnp.logical_or(keys > thr, jnp.logical_and(tie, idx <= cutoff))
    bias_ref[...] = jnp.where(sel, 0.0, NEG_INF)


def _sample_select(scores):
    return pl.pallas_call(
        _sample_select_kernel,
        grid=(1,),
        in_specs=[_const_spec(scores.shape)],
        out_specs=_const_spec(scores.shape, single=False),
        out_shape=jax.ShapeDtypeStruct(scores.shape, jnp.float32),
        compiler_params=pltpu.CompilerParams(vmem_limit_bytes=VMEM_LIMIT),
        name="sample_select",
    )(scores)


def _sample_attn_kernel(pt_ref, bias_ref, bias_self_ref, q_ref, knew_ref, vnew_ref, k_hbm, v_hbm, o_ref,
                        kbuf, vbuf, sem, m_s, l_s, acc_s):
    b, j = pl.program_id(0), pl.program_id(1)
    n_j = pl.num_programs(1)
    step = b * n_j + j
    slot = step % 2

    def page_copies(st, sl):
        out = []
        for t in range(ATTN_PAGES):
            page = pt_ref[st * ATTN_PAGES + t]
            out.append(pltpu.make_async_copy(k_hbm.at[page], kbuf.at[sl, t], sem.at[0, sl]))
            out.append(pltpu.make_async_copy(v_hbm.at[page], vbuf.at[sl, t], sem.at[1, sl]))
        return out

    @pl.when(step == 0)
    def _():
        for c in page_copies(0, 0):
            c.start()

    @pl.when(step + 1 < pl.num_programs(0) * n_j)
    def _():
        for c in page_copies(step + 1, 1 - slot):
            c.start()

    for c in page_copies(step, slot):
        c.wait()

    @pl.when(j == 0)
    def _():
        m_s[...] = jnp.full(m_s.shape, NEG_INF, jnp.float32)
        l_s[...] = jnp.zeros(l_s.shape, jnp.float32)
        acc_s[...] = jnp.zeros(acc_s.shape, jnp.float32)

    def online_update(s, pv_of):
        m_old = m_s[...]
        m_new = jnp.maximum(m_old, jnp.max(s, axis=1, keepdims=True))
        m_safe = jnp.where(m_new == NEG_INF, 0.0, m_new)
        pr = jnp.exp2((s - m_safe) * SOFTMAX_SCALE_LOG2E)
        alpha = jnp.exp2((m_old - m_safe) * SOFTMAX_SCALE_LOG2E)
        l_s[...] = alpha * l_s[...] + jnp.sum(pr, axis=1, keepdims=True)
        acc_s[...] = acc_s[...] * alpha + pv_of(_bf16(pr))
        m_s[...] = m_new

    q = q_ref[...]
    rows_per_page = PAGE_SIZE * N_KV_HEADS
    n_cols = ATTN_PAGES * rows_per_page
    k_all = _bf16(kbuf[slot].reshape(n_cols, HEAD_DIM))
    v_all = _bf16(vbuf[slot].reshape(n_cols, HEAD_DIM))
    dup = jnp.where(lax.broadcasted_iota(jnp.int32, (PAGE_SIZE, rows_per_page), 1) // N_KV_HEADS
                    == lax.broadcasted_iota(jnp.int32, (PAGE_SIZE, rows_per_page), 0), 1.0, 0.0)
    sel_pages = _dot(_bf16(jnp.where(bias_ref[...] == 0.0, 1.0, 0.0)), _bf16(dup))
    sel_row = jnp.concatenate([sel_pages[t:t + 1, :] for t in range(ATTN_PAGES)], axis=1)
    head = lax.broadcasted_iota(jnp.int32, (N_HEADS, n_cols), 0)
    col = lax.broadcasted_iota(jnp.int32, (N_HEADS, n_cols), 1)
    own = (col % N_KV_HEADS) == (head // HEADS_PER_KV)
    s = jnp.where(jnp.logical_and(own, sel_row > 0.5), _dot_nt(q, k_all), NEG_INF)
    online_update(s, lambda pb: _dot(pb, v_all))

    @pl.when(j == n_j - 1)
    def _():
        head_d = lax.broadcasted_iota(jnp.int32, (N_HEADS, HEAD_DIM), 0) // HEADS_PER_KV

        def own_row(ref):
            rows = _bf16(ref[...]).astype(jnp.float32)
            out = jnp.broadcast_to(rows[N_KV_HEADS - 1:N_KV_HEADS, :], (N_HEADS, HEAD_DIM))
            for g in range(N_KV_HEADS - 2, -1, -1):
                out = jnp.where(head_d == g, rows[g:g + 1, :], out)
            return out

        s_self = jnp.sum(q.astype(jnp.float32) * own_row(knew_ref), axis=1, keepdims=True)
        v_own = own_row(vnew_ref)
        online_update(s_self + bias_self_ref[...][:, 0:1], lambda pb: pb.astype(jnp.float32) * v_own)
        o_ref[...] = acc_s[...] / l_s[...]


def _sample_attention(page_table_flat, k_pages, v_pages, bias3, bias_self, q3, knew3, vnew3):
    per_sample = lambda r, w: pl.BlockSpec((None, r, w), lambda b, j, pt: (b, 0, 0))
    rows_per_page = PAGE_SIZE * N_KV_HEADS
    return pl.pallas_call(
        _sample_attn_kernel,
        grid_spec=pltpu.PrefetchScalarGridSpec(
            num_scalar_prefetch=1,
            grid=(DEC_BATCH, N_PAGES // ATTN_PAGES),
            in_specs=[
                pl.BlockSpec((None, ATTN_PAGES, PAGE_SIZE), lambda b, j, pt: (b, j, 0)),
                per_sample(1, PAGE_SIZE), per_sample(N_HEADS, HEAD_DIM),
                per_sample(N_KV_HEADS, HEAD_DIM), per_sample(N_KV_HEADS, HEAD_DIM),
                pl.BlockSpec(memory_space=pl.ANY), pl.BlockSpec(memory_space=pl.ANY),
            ],
            out_specs=per_sample(N_HEADS, HEAD_DIM),
            scratch_shapes=[
                pltpu.VMEM((2, ATTN_PAGES, rows_per_page, HEAD_DIM), jnp.float32),
                pltpu.VMEM((2, ATTN_PAGES, rows_per_page, HEAD_DIM), jnp.float32),
                pltpu.SemaphoreType.DMA((2, 2)),
                pltpu.VMEM((N_HEADS, 1), jnp.float32), pltpu.VMEM((N_HEADS, 1), jnp.float32),
                pltpu.VMEM((N_HEADS, HEAD_DIM), jnp.float32),
            ],
        ),
        out_shape=jax.ShapeDtypeStruct((DEC_BATCH, N_HEADS, HEAD_DIM), jnp.float32),
        compiler_params=pltpu.CompilerParams(
            dimension_semantics=("arbitrary", "arbitrary"), vmem_limit_bytes=VMEM_LIMIT),
        name="sample_attention",
    )(page_table_flat, bias3, bias_self, q3, knew3, vnew3, k_pages, v_pages)


def _pack_w_in(w_in):
    xa, ga, q, k, v, gb, qi, ki, wi, ma, mb = jnp.split(w_in, np.cumsum(SPLITS)[:-1].tolist(), axis=-1)
    pad = jnp.zeros((D_MODEL, LANES - IDX_DIM - N_IDX_HEADS), w_in.dtype)
    return _bf16(jnp.concatenate([xa, ga, q, gb, ma, mb, k, v, qi, ki, wi, pad], axis=-1))


def _lane_pad(v):
    return jnp.pad(v.reshape(1, -1), ((0, 0), (0, LANES - v.shape[-1])))


def kernel(x_prompt, x_sample, cache_k, cache_v, cache_idx_k, state_conv, state_rglru, page_table, c_prompt, c_sample, w_ada, b_ada, g_norm, w_in, w_conv, b_conv, w_ra, b_ra, w_rx, b_rx, lru_lambda, idx_k_norm_g, idx_k_norm_b, w_pa, w_pb, w_o, g_final):
    assert w_in.shape[0] == 1, "one layer"
    wts = {
        "g_norm": g_norm[0].reshape(1, -1), "w_in": _pack_w_in(w_in[0]), "w_conv": w_conv[0],
        "b_conv": b_conv[0].reshape(1, -1), "w_ra": _bf16(w_ra[0]), "b_ra": b_ra[0].reshape(1, -1),
        "w_rx": _bf16(w_rx[0]), "b_rx": b_rx[0].reshape(1, -1), "lam": lru_lambda[0].reshape(1, -1),
        "idx_g": _lane_pad(idx_k_norm_g[0]), "idx_b": _lane_pad(idx_k_norm_b[0]),
        "w_pa": _bf16(w_pa[0]), "w_pb": _bf16(w_pb[0]), "w_o": _bf16(w_o[0]), "g_final": g_final.reshape(1, -1),
    }
    half_h, half_i = HEAD_DIM // 2, IDX_DIM // 2
    invf_h = ROPE_THETA ** (-jnp.arange(half_h, dtype=jnp.float32) / half_h)
    invf_i = ROPE_THETA ** (-jnp.arange(half_i, dtype=jnp.float32) / half_i)
    invf = jnp.zeros((SUBLANES, LANES), jnp.float32)
    invf = invf.at[0].set(jnp.tile(invf_h, LANES // half_h)).at[1].set(jnp.tile(invf_i, LANES // half_i))
    tabs_prompt = _rope_tables(invf, SEQ, 0, 1)
    tabs_sample = _rope_tables(invf, SUBLANES, PAST_LEN, 0)

    mod = _ada_modulation(jnp.concatenate([c_prompt, c_sample], axis=0), _bf16(w_ada[0]), b_ada[0].reshape(1, -1))
    shift, scale, gate = mod[:, :D_MODEL], mod[:, D_MODEL:2 * D_MODEL], mod[:, 2 * D_MODEL:]

    (k_p, v_p, ki_p, kb, vt, kib, q, qi, kiwi, mpa, sgb, smb, conv_p, lru_p) = _prompt_projection(
        x_prompt, shift[:BATCH, None, :], scale[:BATCH, None, :], wts, tabs_prompt)
    o = _prompt_attention(q, qi, kiwi, kb, vt, kib)
    y_prompt = _output_projection(o, sgb, mpa, smb, x_prompt, gate[:BATCH, None, :], wts, OUT_ROWS)

    xs = x_sample[:, 0, :]
    (k_s, v_s, q_s, qi_s, kiwi_s, mpa_s, sgb_s, smb_s, conv_s, lru_s) = _sample_projection(
        xs, shift[BATCH:], scale[BATCH:], wts, tabs_sample, jnp.swapaxes(state_conv[0], 0, 1), state_rglru[0])
    pt_flat = page_table.reshape(-1)
    w_col = kiwi_s[:, IDX_DIM:IDX_DIM + N_IDX_HEADS, None]
    idx_pages = jnp.swapaxes(cache_idx_k[0], 1, 2)
    kv_pages = lambda t: t[0].reshape(-1, PAGE_SIZE * N_KV_HEADS, HEAD_DIM)
    scores = _sample_scores(pt_flat, idx_pages, qi_s.reshape(DEC_BATCH, N_IDX_HEADS, IDX_DIM), w_col,
                            kiwi_s[:, None, :])
    bias = _sample_select(scores.reshape(DEC_BATCH, (N_PAGES + 1) * PAGE_SIZE))
    bias = bias.reshape(DEC_BATCH, N_PAGES + 1, PAGE_SIZE)
    o_s = _sample_attention(
        pt_flat, kv_pages(cache_k), kv_pages(cache_v), bias, bias[:, N_PAGES:, :],
        q_s.reshape(DEC_BATCH, N_HEADS, HEAD_DIM), k_s.reshape(DEC_BATCH, N_KV_HEADS, HEAD_DIM),
        v_s.reshape(DEC_BATCH, N_KV_HEADS, HEAD_DIM))
    y_sample = _output_projection(
        o_s.reshape(1, DEC_BATCH, D_ATT), sgb_s[None], mpa_s[None], smb_s[None], xs[None], gate[None, BATCH:],
        wts, DEC_BATCH)

    kv_p = lambda t: t.reshape(1, BATCH, SEQ, N_KV_HEADS, HEAD_DIM)
    kv_s = lambda t: t.reshape(1, DEC_BATCH, 1, N_KV_HEADS, HEAD_DIM)
    return (
        y_prompt, y_sample.reshape(DEC_BATCH, 1, D_MODEL),
        kv_p(k_p), kv_p(v_p), jnp.swapaxes(ki_p, 1, 2)[None], conv_p[None], lru_p.reshape(1, BATCH, D_RNN),
        kv_s(k_s), kv_s(v_s), kiwi_s[:, :IDX_DIM].reshape(1, DEC_BATCH, 1, IDX_DIM),
        jnp.swapaxes(conv_s, 0, 1)[None], lru_s[None],
    )
```

```python
import functools

import jax
import jax.numpy as jnp
import numpy as np
from jax import lax
from jax.experimental import pallas as pl
from jax.experimental.pallas import tpu as pltpu

D_MODEL = 1024
BATCH = 8
SEQ = 4096
DEC_BATCH = 32
PAST_LEN = 16384
PAGE_SIZE = 128
N_PAGES = PAST_LEN // PAGE_SIZE
D_RNN = D_MODEL
RNN_BLOCKS = 4
RNN_BLOCK_W = D_RNN // RNN_BLOCKS
CONV_W = 4
LRU_C = 8.0
N_HEADS = 8
HEAD_DIM = 128
N_KV_HEADS = 2
HEADS_PER_KV = N_HEADS // N_KV_HEADS
D_ATT = N_HEADS * HEAD_DIM
D_KV = N_KV_HEADS * HEAD_DIM
N_IDX_HEADS = 8
IDX_DIM = 64
D_IDX = N_IDX_HEADS * IDX_DIM
IDX_W_SCALE = (N_IDX_HEADS * IDX_DIM) ** -0.5
TOPK = 256
ROPE_THETA = 10000.0
EPS = 1e-6
SPLITS = (D_RNN, D_RNN, D_ATT, D_KV, D_KV, D_ATT, D_IDX, IDX_DIM, N_IDX_HEADS, D_MODEL, D_MODEL)

LANES = 128
SUBLANES = 8

C_XA, C_GA, C_Q, C_GB, C_MA, C_MB = 0, 1024, 2048, 3072, 4096, 5120
C_K, C_V, C_QI, C_KW = 6144, 6400, 6656, 7168
D_IN_PACKED = 7296

PROJ_ROWS = 256
Q_BLOCK = 256
KEY_CHUNK = 256
ATT_CHUNK = 512
PACKED_ROWS = 16
assert KEY_CHUNK >= TOPK and ATT_CHUNK % KEY_CHUNK == 0 and SEQ % ATT_CHUNK == 0 and ATT_CHUNK % Q_BLOCK == 0
assert PAGE_SIZE == HEAD_DIM == LANES
OUT_ROWS = 512
SCORE_PAGES = 16
ATTN_PAGES = 16
assert N_PAGES % SCORE_PAGES == 0 and N_PAGES % ATTN_PAGES == 0
SOFTMAX_SCALE_LOG2E = (HEAD_DIM ** -0.5) * float(np.log2(np.e))
NEG_INF = float("-inf")
INT_MIN = -2 ** 31
KEY_NEG_INF = INT_MIN + 0x7FFFFF
VMEM_LIMIT = 56 * 1024 * 1024


def _sigmoid(x):
    return 1.0 / (1.0 + jnp.exp(-x))


def _silu(x):
    return x * _sigmoid(x)


def _dot(a, b):
    return jnp.dot(a, b, preferred_element_type=jnp.float32)


def _dot_nt(a, b):
    return lax.dot_general(a, b, (((1,), (1,)), ((), ())), preferred_element_type=jnp.float32)


def _bf16(x):
    return x.astype(jnp.bfloat16)


def _const_spec(shape, single=True):
    nd = len(shape)
    kwargs = {"pipeline_mode": pl.Buffered(1)} if single else {}
    return pl.BlockSpec(shape, lambda *_: (0,) * nd, **kwargs)


def _rope_kernel(invf_ref, cos_h_ref, sin_h_ref, cos_i_ref, sin_ia_ref, sin_ib_ref, *, pos0, pos_step, rows):
    r0 = pl.program_id(0) * rows
    row = lax.broadcasted_iota(jnp.int32, (rows, LANES), 0) + r0
    lane = lax.broadcasted_iota(jnp.int32, (rows, LANES), 1)
    pos = (pos0 + pos_step * row).astype(jnp.float32)
    ang_h = pos * invf_ref[0:1, :]
    ang_i = pos * invf_ref[1:2, :]
    cos_h_ref[...] = jnp.cos(ang_h)
    sh = jnp.sin(ang_h)
    sin_h_ref[...] = jnp.where(lane < HEAD_DIM // 2, -sh, sh)
    cos_i_ref[...] = jnp.cos(ang_i)
    si = jnp.sin(ang_i)
    first_half = (lane % IDX_DIM) < IDX_DIM // 2
    sin_ia_ref[...] = jnp.where(first_half, -si, 0.0)
    sin_ib_ref[...] = jnp.where(first_half, 0.0, si)


def _rope_tables(invf, n, pos0, pos_step):
    rows = min(n, 512)
    out = jax.ShapeDtypeStruct((n, LANES), jnp.float32)
    spec = pl.BlockSpec((rows, LANES), lambda i: (i, 0))
    return pl.pallas_call(
        functools.partial(_rope_kernel, pos0=pos0, pos_step=pos_step, rows=rows),
        grid=(n // rows,),
        in_specs=[pl.BlockSpec((SUBLANES, LANES), lambda i: (0, 0))],
        out_specs=[spec] * 5,
        out_shape=[out] * 5,
        name="rope_tables",
    )(invf)


def _rot_head(z, cos, sin_signed):
    return z * cos + pltpu.roll(z, HEAD_DIM // 2, 1) * sin_signed


def _rot_idx(z, cos, sin_a, sin_b):
    return z * cos + pltpu.roll(z, LANES - IDX_DIM // 2, 1) * sin_a + pltpu.roll(z, IDX_DIM // 2, 1) * sin_b


def _ada_kernel(c_ref, w_ref, b_ref, o_ref):
    o_ref[...] = _dot(_bf16(_silu(c_ref[...])), w_ref[...]) + b_ref[...]


def _ada_modulation(c_all, w_ada, b_ada):
    n = c_all.shape[0]
    return pl.pallas_call(
        _ada_kernel,
        grid=(1,),
        in_specs=[_const_spec((n, D_MODEL)), _const_spec((D_MODEL, 3 * D_MODEL)), _const_spec((1, 3 * D_MODEL))],
        out_specs=_const_spec((n, 3 * D_MODEL), single=False),
        out_shape=jax.ShapeDtypeStruct((n, 3 * D_MODEL), jnp.float32),
        compiler_params=pltpu.CompilerParams(vmem_limit_bytes=VMEM_LIMIT),
        name="ada_modulation",
    )(c_all, w_ada, b_ada)


def _modulated_norm(x, g, scale, shift):
    y = x * lax.rsqrt(jnp.mean(x * x, axis=-1, keepdims=True) + EPS) * g
    return _bf16(y * (1.0 + scale) + shift)


def _lru_gates(xc, w_ra_ref, b_ra, w_rx_ref, b_rx, lam):
    xcb = _bf16(xc)
    r_parts, i_parts = [], []
    for n in range(RNN_BLOCKS):
        sl = slice(n * RNN_BLOCK_W, (n + 1) * RNN_BLOCK_W)
        r_parts.append(_dot(xcb[:, sl], w_ra_ref[n]))
        i_parts.append(_dot(xcb[:, sl], w_rx_ref[n]))
    r = _sigmoid(jnp.concatenate(r_parts, axis=1) + b_ra)
    i = _sigmoid(jnp.concatenate(i_parts, axis=1) + b_rx)
    neg_lam = -lam
    softplus = jnp.maximum(neg_lam, 0.0) + jnp.log1p(jnp.exp(-jnp.abs(neg_lam)))
    log_a = (-LRU_C) * r * softplus
    a = jnp.exp(log_a)
    u = jnp.sqrt(-jnp.tanh(log_a) * (a * a + 1.0)) * (i * xc)
    return a, u


def _idx_key_slab(z_kw, g, b, cos_i, sin_ia, sin_ib):
    lane = lax.broadcasted_iota(jnp.int32, z_kw.shape, 1)
    is_key = lane < IDX_DIM
    mu = jnp.sum(jnp.where(is_key, z_kw, 0.0), axis=-1, keepdims=True) * (1.0 / IDX_DIM)
    d = jnp.where(is_key, z_kw - mu, 0.0)
    var = jnp.sum(d * d, axis=-1, keepdims=True) * (1.0 / IDX_DIM)
    y = d * lax.rsqrt(var + EPS) * g + b
    key = _rot_idx(y, cos_i, sin_ia, sin_ib)
    is_w = jnp.logical_and(lane >= IDX_DIM, lane < IDX_DIM + N_IDX_HEADS)
    return key + jnp.where(is_w, z_kw * IDX_W_SCALE, 0.0)


def _proj_kernel(x_ref, shift_ref, scale_ref, gn_ref, w_in_ref, wconv_ref, bconv_ref, w_ra_ref, b_ra_ref,
                 w_rx_ref, b_rx_ref, lam_ref, ig_ref, ib_ref, w_pa_ref,
                 cos_h_ref, sin_h_ref, cos_i_ref, sin_ia_ref, sin_ib_ref,
                 k_ref, v_ref, ki_ref, kb_ref, vt_ref, kib_ref, q_ref, qi_ref, kiwi_ref,
                 mpa_ref, sgb_ref, smb_ref, conv_ref, lru_ref,
                 xa_ext, a_s, u_s, h_carry, tail_s, ga_s, ma_s):
    ts = PROJ_ROWS

    @pl.when(pl.program_id(1) == 0)
    def _():
        tail_s[...] = jnp.zeros(tail_s.shape, jnp.float32)
        h_carry[...] = jnp.zeros(h_carry.shape, jnp.float32)

    xn = _modulated_norm(x_ref[...], gn_ref[...], scale_ref[...], shift_ref[...])

    xa_ext[0:SUBLANES, :] = tail_s[...]
    xa_ext[SUBLANES:SUBLANES + ts, :] = _dot(xn, w_in_ref[:, C_XA:C_XA + D_RNN])
    sgb_ref[...] = _bf16(_silu(_dot(xn, w_in_ref[:, C_GB:C_GB + D_ATT])))
    smb_ref[...] = _bf16(_sigmoid(_dot(xn, w_in_ref[:, C_MB:C_MB + D_MODEL])))
    xc = bconv_ref[...]
    for t in range(CONV_W):
        off = SUBLANES - (CONV_W - 1) + t
        xc = xc + xa_ext[off:off + ts, :] * wconv_ref[t:t + 1, :]
    conv_ref[...] = xa_ext[ts + SUBLANES - (CONV_W - 1):ts + SUBLANES, :]
    tail_s[...] = xa_ext[ts:ts + SUBLANES, :]

    a, u = _lru_gates(xc, w_ra_ref, b_ra_ref[...], w_rx_ref, b_rx_ref[...], lam_ref[...])
    a_s[...] = a
    u_s[...] = u
    row = lax.broadcasted_iota(jnp.int32, (SUBLANES, D_RNN), 0)
    cos_h, sin_h = cos_h_ref[...], sin_h_ref[...]
    cos_i, sin_ia, sin_ib = cos_i_ref[...], sin_ia_ref[...], sin_ib_ref[...]

    def proj_gate_a():
        ga_s[...] = _silu(_dot(xn, w_in_ref[:, C_GA:C_GA + D_RNN]))

    def proj_merge_a():
        ma_s[...] = _sigmoid(_dot(xn, w_in_ref[:, C_MA:C_MA + D_MODEL]))

    def proj_q(lo, hi):
        def run():
            zq = _dot(xn, w_in_ref[:, C_Q + lo * HEAD_DIM:C_Q + hi * HEAD_DIM])
            for h in range(hi - lo):
                rot = _rot_head(zq[:, h * HEAD_DIM:(h + 1) * HEAD_DIM], cos_h, sin_h)
                q_ref[:, (lo + h) * HEAD_DIM:(lo + h + 1) * HEAD_DIM] = _bf16(rot * SOFTMAX_SCALE_LOG2E)
        return run

    def proj_kv():
        zk = _dot(xn, w_in_ref[:, C_K:C_K + D_KV])
        for g in range(N_KV_HEADS):
            sl = slice(g * HEAD_DIM, (g + 1) * HEAD_DIM)
            kr = _rot_head(zk[:, sl], cos_h, sin_h)
            k_ref[:, sl] = kr
            kb_ref[:, sl] = _bf16(kr)
        zv = _dot(xn, w_in_ref[:, C_V:C_V + D_KV])
        v_ref[...] = zv
        vt = _bf16(zv.T)
        for c in range(ts // KEY_CHUNK):
            vt_ref[c] = vt[:, c * KEY_CHUNK:(c + 1) * KEY_CHUNK]

    def proj_idx():
        zqi = _dot(xn, w_in_ref[:, C_QI:C_QI + D_IDX])
        for p in range(D_IDX // LANES):
            sl = slice(p * LANES, (p + 1) * LANES)
            qi_ref[:, sl] = _bf16(_rot_idx(zqi[:, sl], cos_i, sin_ia, sin_ib))
        slab = _idx_key_slab(_dot(xn, w_in_ref[:, C_KW:C_KW + LANES]), ig_ref[...], ib_ref[...],
                             cos_i, sin_ia, sin_ib)
        kiwi_ref[...] = slab
        ki_ref[...] = slab.T[:IDX_DIM, :]
        key_even = jnp.where(lax.broadcasted_iota(jnp.int32, slab.shape, 1) < IDX_DIM, slab, 0.0)
        kib_ref[:, 0:LANES] = _bf16(key_even)
        kib_ref[:, LANES:2 * LANES] = _bf16(pltpu.roll(key_even, IDX_DIM, 1))

    scan_work = [proj_gate_a, proj_merge_a, proj_q(0, N_HEADS // 2), proj_q(N_HEADS // 2, N_HEADS), proj_kv, proj_idx]

    hc = h_carry[...]
    n_groups = ts // SUBLANES
    per_chunk = n_groups // (len(scan_work) + 2)
    for g in range(n_groups):
        rows = slice(g * SUBLANES, (g + 1) * SUBLANES)
        a8 = a_s[rows, :]
        u8 = u_s[rows, :]
        for d in (1, 2, 4):
            keep = row >= d
            u8 = jnp.where(keep, a8 * pltpu.roll(u8, d, 0) + u8, u8)
            a8 = jnp.where(keep, a8 * pltpu.roll(a8, d, 0), a8)
        h8 = a8 * hc + u8
        u_s[rows, :] = h8
        hc = h8[SUBLANES - 1:SUBLANES, :]
        if g % per_chunk == per_chunk - 1 and g // per_chunk < len(scan_work):
            scan_work[g // per_chunk]()
    h_carry[...] = hc
    lru_ref[...] = hc

    ya = _dot(_bf16(u_s[...] * ga_s[...]), w_pa_ref[...])
    mpa_ref[...] = _bf16(ma_s[...] * ya)


def _prompt_projection(x, shift, scale, wts, tabs):
    ts = PROJ_ROWS
    nt = SEQ // ts
    f32, bf16 = jnp.float32, jnp.bfloat16
    row_spec = lambda w: pl.BlockSpec((None, ts, w), lambda b, j: (b, j, 0))
    bvec_spec = pl.BlockSpec((None, 1, D_MODEL), lambda b, j: (b, 0, 0))
    tab_spec = pl.BlockSpec((ts, LANES), lambda b, j: (j, 0))
    in_specs = [
        row_spec(D_MODEL), bvec_spec, bvec_spec, _const_spec((1, D_MODEL)),
        _const_spec((D_MODEL, D_IN_PACKED)), _const_spec((CONV_W, D_RNN)), _const_spec((1, D_RNN)),
        _const_spec((RNN_BLOCKS, RNN_BLOCK_W, RNN_BLOCK_W)), _const_spec((1, D_RNN)),
        _const_spec((RNN_BLOCKS, RNN_BLOCK_W, RNN_BLOCK_W)), _const_spec((1, D_RNN)), _const_spec((1, D_RNN)),
        _const_spec((1, LANES)), _const_spec((1, LANES)), _const_spec((D_RNN, D_MODEL)),
    ] + [tab_spec] * 5
    out_shape = [
        jax.ShapeDtypeStruct((BATCH, SEQ, D_KV), f32),
        jax.ShapeDtypeStruct((BATCH, SEQ, D_KV), f32),
        jax.ShapeDtypeStruct((BATCH, IDX_DIM, SEQ), f32),
        jax.ShapeDtypeStruct((BATCH, SEQ, D_KV), bf16),
        jax.ShapeDtypeStruct((BATCH, SEQ // KEY_CHUNK, D_KV, KEY_CHUNK), bf16),
        jax.ShapeDtypeStruct((BATCH, SEQ, 2 * LANES), bf16),
        jax.ShapeDtypeStruct((BATCH, SEQ, D_ATT), bf16),
        jax.ShapeDtypeStruct((BATCH, SEQ, D_IDX), bf16),
        jax.ShapeDtypeStruct((BATCH, SEQ, LANES), f32),
        jax.ShapeDtypeStruct((BATCH, SEQ, D_MODEL), bf16),
        jax.ShapeDtypeStruct((BATCH, SEQ, D_ATT), bf16),
        jax.ShapeDtypeStruct((BATCH, SEQ, D_MODEL), bf16),
        jax.ShapeDtypeStruct((BATCH, CONV_W - 1, D_RNN), f32),
        jax.ShapeDtypeStruct((BATCH, 1, D_RNN), f32),
    ]
    out_specs = [
        row_spec(D_KV), row_spec(D_KV), pl.BlockSpec((None, IDX_DIM, ts), lambda b, j: (b, 0, j)), row_spec(D_KV),
        pl.BlockSpec((None, ts // KEY_CHUNK, D_KV, KEY_CHUNK), lambda b, j: (b, j, 0, 0)),
        row_spec(2 * LANES), row_spec(D_ATT), row_spec(D_IDX), row_spec(LANES),
        row_spec(D_MODEL), row_spec(D_ATT), row_spec(D_MODEL),
        pl.BlockSpec((None, CONV_W - 1, D_RNN), lambda b, j: (b, 0, 0)),
        pl.BlockSpec((None, 1, D_RNN), lambda b, j: (b, 0, 0)),
    ]
    scratch = [
        pltpu.VMEM((ts + SUBLANES, D_RNN), f32), pltpu.VMEM((ts, D_RNN), f32),
        pltpu.VMEM((ts, D_RNN), f32), pltpu.VMEM((1, D_RNN), f32), pltpu.VMEM((SUBLANES, D_RNN), f32),
        pltpu.VMEM((ts, D_RNN), f32), pltpu.VMEM((ts, D_MODEL), f32),
    ]
    return pl.pallas_call(
        _proj_kernel,
        grid=(BATCH, nt),
        in_specs=in_specs, out_specs=out_specs, out_shape=out_shape, scratch_shapes=scratch,
        compiler_params=pltpu.CompilerParams(
            dimension_semantics=("arbitrary", "arbitrary"), vmem_limit_bytes=VMEM_LIMIT),
        name="prompt_projection",
    )(x, shift, scale, wts["g_norm"], wts["w_in"], wts["w_conv"], wts["b_conv"], wts["w_ra"], wts["b_ra"],
      wts["w_rx"], wts["b_rx"], wts["lam"], wts["idx_g"], wts["idx_b"], wts["w_pa"], *tabs)


def _threshold_value(t_unsigned):
    key = jnp.maximum(jnp.bitwise_xor(t_unsigned, jnp.int32(INT_MIN)), jnp.int32(KEY_NEG_INF))
    bits = key ^ jnp.bitwise_and(jnp.right_shift(key, 31), jnp.int32(0x7FFFFFFF))
    return pltpu.bitcast(bits, jnp.float32)


def _greedy_bits(count_ge, n_bits, shape, count_all):
    def bit_step(b, carry):
        t, cnt = carry
        cand = jnp.bitwise_or(t, jnp.left_shift(jnp.int32(1), n_bits - 1 - b))
        c = count_ge(cand)
        ok = c >= TOPK
        return jnp.where(ok, cand, t), jnp.where(ok, c, cnt)
    return lax.fori_loop(0, n_bits, bit_step, (jnp.zeros(shape, jnp.int32), count_all))


def _fold_rows(x, rows, op=jnp.add, chains=4):
    parts = [x[r:r + rows] for r in range(0, x.shape[0], rows)]
    acc = parts[:chains]
    for k, part in enumerate(parts[chains:]):
        acc[k % len(acc)] = op(acc[k % len(acc)], part)
    while len(acc) > 1:
        acc = [op(acc[k], acc[k + 1]) for k in range(0, len(acc) - 1, 2)] + ([acc[-1]] if len(acc) % 2 else [])
    return acc[0]


def _tie_cutoff(count_tie_below, need, n_bits, shape):
    def bit_step(b, x):
        cand = jnp.bitwise_or(x, jnp.left_shift(jnp.int32(1), n_bits - 1 - b))
        return jnp.where(count_tie_below(cand) < need, cand, x)
    return lax.fori_loop(0, n_bits, bit_step, jnp.zeros(shape, jnp.int32))


def _attn_kernel(q_ref, qi_ref, kiwi_ref, kb_ref, vt_ref, kib_ref, o_ref,
                 score_s, hi_s, bias_s, s_scr, acc_s):
    i = pl.program_id(1)
    t0 = i * Q_BLOCK
    n_steps = (t0 + Q_BLOCK + ATT_CHUNK - 1) // ATT_CHUNK
    step_iota = lax.broadcasted_iota(jnp.int32, (ATT_CHUNK, Q_BLOCK), 0)
    sub_iota = lax.broadcasted_iota(jnp.int32, (KEY_CHUNK, Q_BLOCK), 0)
    q_pos = t0 + lax.broadcasted_iota(jnp.int32, (1, Q_BLOCK), 1)
    lane_shape = (1, Q_BLOCK)

    def step_rows(c):
        return pl.ds(pl.multiple_of(c * ATT_CHUNK, ATT_CHUNK), ATT_CHUNK)

    w_t = kiwi_ref[...].T[IDX_DIM:IDX_DIM + N_IDX_HEADS, :]
    qi = qi_ref[...]
    n_pairs = D_IDX // LANES
    qi_rows = jnp.concatenate([qi[:, p * LANES:(p + 1) * LANES] for p in range(n_pairs)], axis=0)

    def score_step(c, causal):
        for sub in range(ATT_CHUNK // KEY_CHUNK):
            r0 = pl.multiple_of(c * ATT_CHUNK + sub * KEY_CHUNK, KEY_CHUNK)
            rows = pl.ds(r0, KEY_CHUNK)
            s_par = [_dot_nt(kib_ref[rows, par * LANES:(par + 1) * LANES], qi_rows) for par in range(2)]
            score = jnp.zeros((KEY_CHUNK, Q_BLOCK), jnp.float32)
            for h in range(N_IDX_HEADS):
                s_h = s_par[h % 2][:, (h // 2) * Q_BLOCK:(h // 2 + 1) * Q_BLOCK]
                score = score + jnp.maximum(s_h, 0.0) * w_t[h:h + 1, :]
            if causal:
                score = jnp.where((r0 + sub_iota) <= q_pos, score, NEG_INF)
            score_s[rows, :] = score
            hi_s[rows, :] = _bf16(score)

    def early_score_step(c, carry):
        score_step(c, False)
        return carry

    lax.fori_loop(0, n_steps - 1, early_score_step, 0)
    score_step(n_steps - 1, True)

    def count_where(pred):
        def body(c, acc):
            sel = pred(score_s[step_rows(c), :], c * ATT_CHUNK + step_iota)
            parts = [_fold_rows(jnp.where(sel[:, t * LANES:(t + 1) * LANES], 1.0, 0.0), SUBLANES, chains=2)
                     for t in range(Q_BLOCK // LANES)]
            return acc + jnp.concatenate(parts, axis=1)
        acc = lax.fori_loop(0, n_steps, body, jnp.zeros((SUBLANES, Q_BLOCK), jnp.float32))
        return jnp.sum(acc, axis=0, keepdims=True)

    def count_rounded_ge(t16):
        cand = _threshold_value(jnp.left_shift(t16, 16)).astype(jnp.bfloat16)
        def body(c, acc):
            one = jnp.where(hi_s[step_rows(c), :] >= cand, jnp.bfloat16(1), jnp.bfloat16(0))
            return acc + _fold_rows(one, PACKED_ROWS, chains=2)
        acc = lax.fori_loop(0, n_steps, body, jnp.zeros((PACKED_ROWS, Q_BLOCK), jnp.bfloat16))
        return jnp.sum(acc.astype(jnp.float32), axis=0, keepdims=True)

    count_all = jnp.full(lane_shape, 1.0, jnp.float32) * (n_steps * ATT_CHUNK).astype(jnp.float32)
    t1, _ = _greedy_bits(count_rounded_ge, 16, lane_shape, count_all)

    base = jnp.left_shift(jnp.maximum(t1 - 1, 0), 16)

    def count_ge(offset):
        cand = _threshold_value(base + offset)
        return count_where(lambda blk, idx: blk >= cand)

    off, cnt_ge = _greedy_bits(count_ge, 17, lane_shape, count_ge(jnp.zeros(lane_shape, jnp.int32)))
    thr = _threshold_value(base + off)

    def write_bias(select):
        def step(c, causal):
            blk = score_s[step_rows(c), :]
            idx = c * ATT_CHUNK + step_iota
            sel = select(blk, idx)
            if causal:
                sel = jnp.logical_and(sel, idx <= q_pos)
            bias_s[step_rows(c), :] = jnp.where(sel, 0.0, NEG_INF)

        def early_step(c, carry):
            step(c, False)
            return carry

        lax.fori_loop(0, n_steps - 1, early_step, 0)
        step(n_steps - 1, True)

    has_tie = jnp.max(jnp.where(cnt_ge > TOPK, 1.0, 0.0)) > 0.5

    @pl.when(jnp.logical_not(has_tie))
    def _():
        write_bias(lambda blk, idx: blk >= thr)

    @pl.when(has_tie)
    def _():
        need = TOPK - count_where(lambda blk, idx: blk > thr)
        cutoff = _tie_cutoff(
            lambda x: count_where(lambda blk, idx: jnp.logical_and(blk == thr, idx < x)),
            need, int(np.log2(SEQ)), lane_shape)
        write_bias(lambda blk, idx: jnp.logical_or(blk > thr, jnp.logical_and(blk == thr, idx <= cutoff)))

    q = q_ref[...]
    n_lanes = HEADS_PER_KV * Q_BLOCK
    q_rows = [
        jnp.concatenate([q[:, (g * HEADS_PER_KV + h) * HEAD_DIM:(g * HEADS_PER_KV + h + 1) * HEAD_DIM]
                         for h in range(HEADS_PER_KV)], axis=0)
        for g in range(N_KV_HEADS)]
    acc_s[...] = jnp.zeros(acc_s.shape, jnp.float32)
    subs = ATT_CHUNK // KEY_CHUNK

    ones_rows = jnp.ones((PACKED_ROWS, KEY_CHUNK), jnp.bfloat16)

    d_sl = [slice(g * HEAD_DIM, (g + 1) * HEAD_DIM) for g in range(N_KV_HEADS)]

    def logits(c, slot):
        maxima = []
        for g in range(N_KV_HEADS):
            mx = None
            for r in range(0, ATT_CHUNK, KEY_CHUNK):
                rows = pl.ds(pl.multiple_of(c * ATT_CHUNK + r, KEY_CHUNK), KEY_CHUNK)
                b = bias_s[rows, :]
                x = _dot_nt(kb_ref[rows, d_sl[g]], q_rows[g]) + jnp.concatenate([b] * HEADS_PER_KV, axis=1)
                s_scr[slot, g, r:r + KEY_CHUNK, :] = x
                f = _fold_rows(x, SUBLANES, jnp.maximum)
                mx = f if mx is None else jnp.maximum(mx, f)
            maxima.append(jnp.max(mx, axis=0, keepdims=True))
        return tuple(maxima)

    def accumulate(c, slot, m_old, m_step):
        m_out = []
        for g in range(N_KV_HEADS):
            m_new = jnp.maximum(m_old[g], m_step[g])
            m_safe = jnp.where(m_new == NEG_INF, 0.0, m_new)
            alpha = jnp.exp2(m_old[g] - m_safe)
            pv = None
            for sub in range(subs):
                r = sub * KEY_CHUNK
                pb = _bf16(jnp.exp2(s_scr[slot, g, r:r + KEY_CHUNK, :] - m_safe))
                lhs = jnp.concatenate([vt_ref[c * subs + sub, d_sl[g], :], ones_rows], axis=0)
                part = _dot(lhs, pb)
                pv = part if pv is None else pv + part
            acc_s[g] = acc_s[g] * alpha + pv
            m_out.append(m_new)
        return tuple(m_out)

    def attend_pair(p, carry):
        m_run, m_even = carry
        c = 2 * p
        m_odd = logits(c + 1, 1)
        m_run = accumulate(c, 0, m_run, m_even)
        m_even = logits(jnp.minimum(c + 2, n_steps - 1), 0)
        m_run = accumulate(c + 1, 1, m_run, m_odd)
        return m_run, m_even

    m_init = tuple(jnp.full((1, n_lanes), NEG_INF, jnp.float32) for _ in range(N_KV_HEADS))
    m_run, m_even = lax.fori_loop(0, n_steps // 2, attend_pair, (m_init, logits(0, 0)))

    @pl.when(n_steps % 2 == 1)
    def _():
        accumulate(n_steps - 1, 0, m_run, m_even)

    for g in range(N_KV_HEADS):
        o_t = acc_s[g, 0:HEAD_DIM, :] / acc_s[g, HEAD_DIM:HEAD_DIM + 1, :]
        for h in range(HEADS_PER_KV):
            col = (g * HEADS_PER_KV + h) * HEAD_DIM
            o_ref[:, col:col + HEAD_DIM] = _bf16(o_t[:, h * Q_BLOCK:(h + 1) * Q_BLOCK].T)


def _prompt_attention(q, qi, kiwi, kb, vt, kib):
    nq = SEQ // Q_BLOCK
    blk = lambda w: pl.BlockSpec((None, Q_BLOCK, w), lambda b, i: (b, i, 0))
    return pl.pallas_call(
        _attn_kernel,
        grid=(BATCH, nq),
        in_specs=[
            blk(D_ATT), blk(D_IDX), blk(LANES),
            pl.BlockSpec((None, SEQ, D_KV), lambda b, i: (b, 0, 0)),
            pl.BlockSpec((None, SEQ // KEY_CHUNK, D_KV, KEY_CHUNK), lambda b, i: (b, 0, 0, 0)),
            pl.BlockSpec((None, SEQ, 2 * LANES), lambda b, i: (b, 0, 0)),
        ],
        out_specs=blk(D_ATT),
        out_shape=jax.ShapeDtypeStruct((BATCH, SEQ, D_ATT), jnp.bfloat16),
        scratch_shapes=[
            pltpu.VMEM((SEQ, Q_BLOCK), jnp.float32),
            pltpu.VMEM((SEQ, Q_BLOCK), jnp.bfloat16),
            pltpu.VMEM((SEQ, Q_BLOCK), jnp.float32),
            pltpu.VMEM((2, N_KV_HEADS, ATT_CHUNK, HEADS_PER_KV * Q_BLOCK), jnp.float32),
            pltpu.VMEM((N_KV_HEADS, HEAD_DIM + PACKED_ROWS, HEADS_PER_KV * Q_BLOCK), jnp.float32),
        ],
        compiler_params=pltpu.CompilerParams(
            dimension_semantics=("arbitrary", "arbitrary"), vmem_limit_bytes=VMEM_LIMIT),
        name="prompt_attention",
    )(q, qi, kiwi, kb, vt, kib)


def _out_kernel(o_ref, sgb_ref, mpa_ref, smb_ref, x_ref, gate_ref, w_pb_ref, w_o_ref, gf_ref, y_ref):
    f32 = jnp.float32
    yb = _dot(_bf16(o_ref[...].astype(f32) * sgb_ref[...].astype(f32)), w_pb_ref[...])
    m = mpa_ref[...].astype(f32) + smb_ref[...].astype(f32) * yb
    r = x_ref[...] + gate_ref[...] * _dot(_bf16(m), w_o_ref[...])
    y_ref[...] = r * lax.rsqrt(jnp.mean(r * r, axis=-1, keepdims=True) + EPS) * gf_ref[...]


def _output_projection(o, sgb, mpa, smb, x, gate, wts, rows):
    ng, nr, _ = x.shape
    row_spec = pl.BlockSpec((None, rows, D_MODEL), lambda b, j: (b, j, 0))
    if gate.shape[1] == 1:
        gate_spec = pl.BlockSpec((None, 1, D_MODEL), lambda b, j: (b, 0, 0))
    else:
        gate_spec = row_spec
    return pl.pallas_call(
        _out_kernel,
        grid=(ng, nr // rows),
        in_specs=[row_spec] * 5 + [gate_spec, _const_spec((D_ATT, D_MODEL)), _const_spec((D_MODEL, D_MODEL)),
                                   _const_spec((1, D_MODEL))],
        out_specs=row_spec,
        out_shape=jax.ShapeDtypeStruct(x.shape, jnp.float32),
        compiler_params=pltpu.CompilerParams(
            dimension_semantics=("arbitrary", "arbitrary"), vmem_limit_bytes=VMEM_LIMIT),
        name="output_projection",
    )(o, sgb, mpa, smb, x, gate, wts["w_pb"], wts["w_o"], wts["g_final"])


def _sample_proj_kernel(x_ref, shift_ref, scale_ref, gn_ref, w_in_ref, wconv_ref, bconv_ref, w_ra_ref, b_ra_ref,
                        w_rx_ref, b_rx_ref, lam_ref, ig_ref, ib_ref, w_pa_ref,
                        cos_h_ref, sin_h_ref, cos_i_ref, sin_ia_ref, sin_ib_ref, buf_ref, h0_ref,
                        k_ref, v_ref, q_ref, qi_ref, kiwi_ref, mpa_ref, sgb_ref, smb_ref, conv_ref, lru_ref):
    xn = _modulated_norm(x_ref[...], gn_ref[...], scale_ref[...], shift_ref[...])
    xa = _dot(xn, w_in_ref[:, C_XA:C_XA + D_RNN])
    xc = bconv_ref[...]
    for t in range(CONV_W - 1):
        xc = xc + buf_ref[t] * wconv_ref[t:t + 1, :]
        if t > 0:
            conv_ref[t - 1] = buf_ref[t]
    xc = xc + xa * wconv_ref[CONV_W - 1:CONV_W, :]
    conv_ref[CONV_W - 2] = xa
    a, u = _lru_gates(xc, w_ra_ref, b_ra_ref[...], w_rx_ref, b_rx_ref[...], lam_ref[...])
    h = a * h0_ref[...] + u
    lru_ref[...] = h
    ga = _dot(xn, w_in_ref[:, C_GA:C_GA + D_RNN])
    ya = _dot(_bf16(h * _silu(ga)), w_pa_ref[...])
    mpa_ref[...] = _sigmoid(_dot(xn, w_in_ref[:, C_MA:C_MA + D_MODEL])) * ya
    sgb_ref[...] = _silu(_dot(xn, w_in_ref[:, C_GB:C_GB + D_ATT]))
    smb_ref[...] = _sigmoid(_dot(xn, w_in_ref[:, C_MB:C_MB + D_MODEL]))

    cos_h, sin_h = cos_h_ref[0:1, :], sin_h_ref[0:1, :]
    cos_i, sin_ia, sin_ib = cos_i_ref[0:1, :], sin_ia_ref[0:1, :], sin_ib_ref[0:1, :]
    zq = _dot(xn, w_in_ref[:, C_Q:C_Q + D_ATT])
    for hd in range(N_HEADS):
        sl = slice(hd * HEAD_DIM, (hd + 1) * HEAD_DIM)
        q_ref[:, sl] = _bf16(_rot_head(zq[:, sl], cos_h, sin_h))
    zk = _dot(xn, w_in_ref[:, C_K:C_K + D_KV])
    for g in range(N_KV_HEADS):
        sl = slice(g * HEAD_DIM, (g + 1) * HEAD_DIM)
        k_ref[:, sl] = _rot_head(zk[:, sl], cos_h, sin_h)
    v_ref[...] = _dot(xn, w_in_ref[:, C_V:C_V + D_KV])
    zqi = _dot(xn, w_in_ref[:, C_QI:C_QI + D_IDX])
    for p in range(D_IDX // LANES):
        sl = slice(p * LANES, (p + 1) * LANES)
        qi_ref[:, sl] = _bf16(_rot_idx(zqi[:, sl], cos_i, sin_ia, sin_ib))
    kiwi_ref[...] = _idx_key_slab(_dot(xn, w_in_ref[:, C_KW:C_KW + LANES]), ig_ref[...], ib_ref[...],
                                  cos_i, sin_ia, sin_ib)


def _sample_projection(x, shift, scale, wts, tabs, buf_t, h0):
    n = DEC_BATCH
    f32, bf16 = jnp.float32, jnp.bfloat16
    in_specs = [
        _const_spec((n, D_MODEL)), _const_spec((n, D_MODEL)), _const_spec((n, D_MODEL)), _const_spec((1, D_MODEL)),
        _const_spec((D_MODEL, D_IN_PACKED)), _const_spec((CONV_W, D_RNN)), _const_spec((1, D_RNN)),
        _const_spec((RNN_BLOCKS, RNN_BLOCK_W, RNN_BLOCK_W)), _const_spec((1, D_RNN)),
        _const_spec((RNN_BLOCKS, RNN_BLOCK_W, RNN_BLOCK_W)), _const_spec((1, D_RNN)), _const_spec((1, D_RNN)),
        _const_spec((1, LANES)), _const_spec((1, LANES)), _const_spec((D_RNN, D_MODEL)),
    ] + [_const_spec((SUBLANES, LANES))] * 5 + [_const_spec((CONV_W - 1, n, D_RNN)), _const_spec((n, D_RNN))]
    shapes = [
        ((n, D_KV), f32), ((n, D_KV), f32), ((n, D_ATT), bf16), ((n, D_IDX), bf16), ((n, LANES), f32),
        ((n, D_MODEL), f32), ((n, D_ATT), f32), ((n, D_MODEL), f32), ((CONV_W - 1, n, D_RNN), f32), ((n, D_RNN), f32),
    ]
    return pl.pallas_call(
        _sample_proj_kernel,
        grid=(1,),
        in_specs=in_specs,
        out_specs=[_const_spec(s, single=False) for s, _ in shapes],
        out_shape=[jax.ShapeDtypeStruct(s, d) for s, d in shapes],
        compiler_params=pltpu.CompilerParams(vmem_limit_bytes=VMEM_LIMIT),
        name="sample_projection",
    )(x, shift, scale, wts["g_norm"], wts["w_in"], wts["w_conv"], wts["b_conv"], wts["w_ra"], wts["b_ra"],
      wts["w_rx"], wts["b_rx"], wts["lam"], wts["idx_g"], wts["idx_b"], wts["w_pa"], *tabs, buf_t, h0)


def _sample_score_kernel(pt_ref, qi_ref, w_ref, kinew_ref, idx_hbm, o_ref, buf, sem):
    b = pl.program_id(0)
    slot = b % 2

    def page_copy(sample, p, sl):
        return pltpu.make_async_copy(idx_hbm.at[pt_ref[sample * N_PAGES + p]], buf.at[sl, p], sem.at[sl])

    def start_sample(sample, sl):
        def body(p, carry):
            page_copy(sample, p, sl).start()
            return carry
        lax.fori_loop(0, N_PAGES, body, 0)

    @pl.when(b == 0)
    def _():
        start_sample(0, 0)

    @pl.when(b + 1 < pl.num_programs(0))
    def _():
        start_sample(b + 1, 1 - slot)

    def wait_page(p, carry):
        page_copy(b, p, slot).wait()
        return carry

    lax.fori_loop(0, N_PAGES, wait_page, 0)

    qi = qi_ref[...]
    w = w_ref[...]

    def score_pages(i, carry):
        p0 = i * SCORE_PAGES
        kt = _bf16(jnp.concatenate([buf[slot, p0 + t] for t in range(SCORE_PAGES)], axis=1))
        s = _dot(qi, kt)
        score = jnp.sum(jnp.maximum(s, 0.0) * w, axis=0, keepdims=True)
        for t in range(SCORE_PAGES):
            o_ref[pl.ds(p0 + t, 1), :] = score[:, t * PAGE_SIZE:(t + 1) * PAGE_SIZE]
        return carry

    lax.fori_loop(0, N_PAGES // SCORE_PAGES, score_pages, 0)

    k_self = _bf16(kinew_ref[...][:, :IDX_DIM]).astype(jnp.float32)
    s_self = jnp.sum(qi.astype(jnp.float32) * k_self, axis=1, keepdims=True)
    score_self = jnp.sum(jnp.maximum(s_self, 0.0) * w, axis=0, keepdims=True)
    lane = lax.broadcasted_iota(jnp.int32, (1, PAGE_SIZE), 1)
    o_ref[N_PAGES:N_PAGES + 1, :] = jnp.where(lane == 0, score_self, NEG_INF)


def _sample_scores(page_table_flat, idx_pages, qi3, w_col, kiwi3):
    per_sample = lambda r, w: pl.BlockSpec((None, r, w), lambda b, pt: (b, 0, 0))
    return pl.pallas_call(
        _sample_score_kernel,
        grid_spec=pltpu.PrefetchScalarGridSpec(
            num_scalar_prefetch=1,
            grid=(DEC_BATCH,),
            in_specs=[per_sample(N_IDX_HEADS, IDX_DIM), per_sample(N_IDX_HEADS, 1), per_sample(1, LANES),
                      pl.BlockSpec(memory_space=pl.ANY)],
            out_specs=per_sample(N_PAGES + 1, PAGE_SIZE),
            scratch_shapes=[pltpu.VMEM((2, N_PAGES, IDX_DIM, PAGE_SIZE), jnp.float32),
                            pltpu.SemaphoreType.DMA((2,))],
        ),
        out_shape=jax.ShapeDtypeStruct((DEC_BATCH, N_PAGES + 1, PAGE_SIZE), jnp.float32),
        compiler_params=pltpu.CompilerParams(dimension_semantics=("arbitrary",), vmem_limit_bytes=VMEM_LIMIT),
        name="sample_scores",
    )(page_table_flat, qi3, w_col, kiwi3, idx_pages)


def _sample_select_kernel(score_ref, bias_ref):
    keys = score_ref[...]
    idx = lax.broadcasted_iota(jnp.int32, keys.shape, 1)
    col_shape = (keys.shape[0], 1)
    count = lambda pred: jnp.sum(jnp.where(pred, 1.0, 0.0), axis=1, keepdims=True)
    total = jnp.full(col_shape, float(keys.shape[1]), jnp.float32)
    thr_u, _ = _greedy_bits(lambda t: count(keys >= _threshold_value(t)), 32, col_shape, total)
    thr = _threshold_value(thr_u)
    need = TOPK - count(keys > thr)
    tie = keys == thr
    n_bits = int(np.ceil(np.log2(keys.shape[1])))
    cutoff = _tie_cutoff(lambda x: count(jnp.logical_and(tie, idx < x)), need, n_bits, col_shape)
    sel = jnp.logical_or(keys > thr, jnp.logical_and(tie, idx <= cutoff))
    bias_ref[...] = jnp.where(sel, 0.0, NEG_INF)


def _sample_select(scores):
    return pl.pallas_call(
        _sample_select_kernel,
        grid=(1,),
        in_specs=[_const_spec(scores.shape)],
        out_specs=_const_spec(scores.shape, single=False),
        out_shape=jax.ShapeDtypeStruct(scores.shape, jnp.float32),
        compiler_params=pltpu.CompilerParams(vmem_limit_bytes=VMEM_LIMIT),
        name="sample_select",
    )(scores)


def _sample_attn_kernel(pt_ref, bias_ref, bias_self_ref, q_ref, knew_ref, vnew_ref, k_hbm, v_hbm, o_ref,
                        kbuf, vbuf, sem, m_s, l_s, acc_s):
    b, j = pl.program_id(0), pl.program_id(1)
    n_j = pl.num_programs(1)
    step = b * n_j + j
    slot = step % 2

    def page_copies(st, sl):
        out = []
        for t in range(ATTN_PAGES):
            page = pt_ref[st * ATTN_PAGES + t]
            out.append(pltpu.make_async_copy(k_hbm.at[page], kbuf.at[sl, t], sem.at[0, sl]))
            out.append(pltpu.make_async_copy(v_hbm.at[page], vbuf.at[sl, t], sem.at[1, sl]))
        return out

    @pl.when(step == 0)
    def _():
        for c in page_copies(0, 0):
            c.start()

    @pl.when(step + 1 < pl.num_programs(0) * n_j)
    def _():
        for c in page_copies(step + 1, 1 - slot):
            c.start()

    for c in page_copies(step, slot):
        c.wait()

    @pl.when(j == 0)
    def _():
        m_s[...] = jnp.full(m_s.shape, NEG_INF, jnp.float32)
        l_s[...] = jnp.zeros(l_s.shape, jnp.float32)
        acc_s[...] = jnp.zeros(acc_s.shape, jnp.float32)

    def online_update(s, pv_of):
        m_old = m_s[...]
        m_new = jnp.maximum(m_old, jnp.max(s, axis=1, keepdims=True))
        m_safe = jnp.where(m_new == NEG_INF, 0.0, m_new)
        pr = jnp.exp2((s - m_safe) * SOFTMAX_SCALE_LOG2E)
        alpha = jnp.exp2((m_old - m_safe) * SOFTMAX_SCALE_LOG2E)
        l_s[...] = alpha * l_s[...] + jnp.sum(pr, axis=1, keepdims=True)
        acc_s[...] = acc_s[...] * alpha + pv_of(_bf16(pr))
        m_s[...] = m_new

    q = q_ref[...]
    rows_per_page = PAGE_SIZE * N_KV_HEADS
    n_cols = ATTN_PAGES * rows_per_page
    k_all = _bf16(kbuf[slot].reshape(n_cols, HEAD_DIM))
    v_all = _bf16(vbuf[slot].reshape(n_cols, HEAD_DIM))
    dup = jnp.where(lax.broadcasted_iota(jnp.int32, (PAGE_SIZE, rows_per_page), 1) // N_KV_HEADS
                    == lax.broadcasted_iota(jnp.int32, (PAGE_SIZE, rows_per_page), 0), 1.0, 0.0)
    sel_pages = _dot(_bf16(jnp.where(bias_ref[...] == 0.0, 1.0, 0.0)), _bf16(dup))
    sel_row = jnp.concatenate([sel_pages[t:t + 1, :] for t in range(ATTN_PAGES)], axis=1)
    head = lax.broadcasted_iota(jnp.int32, (N_HEADS, n_cols), 0)
    col = lax.broadcasted_iota(jnp.int32, (N_HEADS, n_cols), 1)
    own = (col % N_KV_HEADS) == (head // HEADS_PER_KV)
    s = jnp.where(jnp.logical_and(own, sel_row > 0.5), _dot_nt(q, k_all), NEG_INF)
    online_update(s, lambda pb: _dot(pb, v_all))

    @pl.when(j == n_j - 1)
    def _():
        head_d = lax.broadcasted_iota(jnp.int32, (N_HEADS, HEAD_DIM), 0) // HEADS_PER_KV

        def own_row(ref):
            rows = _bf16(ref[...]).astype(jnp.float32)
            out = jnp.broadcast_to(rows[N_KV_HEADS - 1:N_KV_HEADS, :], (N_HEADS, HEAD_DIM))
            for g in range(N_KV_HEADS - 2, -1, -1):
                out = jnp.where(head_d == g, rows[g:g + 1, :], out)
            return out

        s_self = jnp.sum(q.astype(jnp.float32) * own_row(knew_ref), axis=1, keepdims=True)
        v_own = own_row(vnew_ref)
        online_update(s_self + bias_self_ref[...][:, 0:1], lambda pb: pb.astype(jnp.float32) * v_own)
        o_ref[...] = acc_s[...] / l_s[...]


def _sample_attention(page_table_flat, k_pages, v_pages, bias3, bias_self, q3, knew3, vnew3):
    per_sample = lambda r, w: pl.BlockSpec((None, r, w), lambda b, j, pt: (b, 0, 0))
    rows_per_page = PAGE_SIZE * N_KV_HEADS
    return pl.pallas_call(
        _sample_attn_kernel,
        grid_spec=pltpu.PrefetchScalarGridSpec(
            num_scalar_prefetch=1,
            grid=(DEC_BATCH, N_PAGES // ATTN_PAGES),
            in_specs=[
                pl.BlockSpec((None, ATTN_PAGES, PAGE_SIZE), lambda b, j, pt: (b, j, 0)),
                per_sample(1, PAGE_SIZE), per_sample(N_HEADS, HEAD_DIM),
                per_sample(N_KV_HEADS, HEAD_DIM), per_sample(N_KV_HEADS, HEAD_DIM),
                pl.BlockSpec(memory_space=pl.ANY), pl.BlockSpec(memory_space=pl.ANY),
            ],
            out_specs=per_sample(N_HEADS, HEAD_DIM),
            scratch_shapes=[
                pltpu.VMEM((2, ATTN_PAGES, rows_per_page, HEAD_DIM), jnp.float32),
                pltpu.VMEM((2, ATTN_PAGES, rows_per_page, HEAD_DIM), jnp.float32),
                pltpu.SemaphoreType.DMA((2, 2)),
                pltpu.VMEM((N_HEADS, 1), jnp.float32), pltpu.VMEM((N_HEADS, 1), jnp.float32),
                pltpu.VMEM((N_HEADS, HEAD_DIM), jnp.float32),
            ],
        ),
        out_shape=jax.ShapeDtypeStruct((DEC_BATCH, N_HEADS, HEAD_DIM), jnp.float32),
        compiler_params=pltpu.CompilerParams(
            dimension_semantics=("arbitrary", "arbitrary"), vmem_limit_bytes=VMEM_LIMIT),
        name="sample_attention",
    )(page_table_flat, bias3, bias_self, q3, knew3, vnew3, k_pages, v_pages)


def _pack_w_in(w_in):
    xa, ga, q, k, v, gb, qi, ki, wi, ma, mb = jnp.split(w_in, np.cumsum(SPLITS)[:-1].tolist(), axis=-1)
    pad = jnp.zeros((D_MODEL, LANES - IDX_DIM - N_IDX_HEADS), w_in.dtype)
    return _bf16(jnp.concatenate([xa, ga, q, gb, ma, mb, k, v, qi, ki, wi, pad], axis=-1))


def _lane_pad(v):
    return jnp.pad(v.reshape(1, -1), ((0, 0), (0, LANES - v.shape[-1])))


def kernel(x_prompt, x_sample, cache_k, cache_v, cache_idx_k, state_conv, state_rglru, page_table, c_prompt, c_sample, w_ada, b_ada, g_norm, w_in, w_conv, b_conv, w_ra, b_ra, w_rx, b_rx, lru_lambda, idx_k_norm_g, idx_k_norm_b, w_pa, w_pb, w_o, g_final):
    assert w_in.shape[0] == 1, "one layer"
    wts = {
        "g_norm": g_norm[0].reshape(1, -1), "w_in": _pack_w_in(w_in[0]), "w_conv": w_conv[0],
        "b_conv": b_conv[0].reshape(1, -1), "w_ra": _bf16(w_ra[0]), "b_ra": b_ra[0].reshape(1, -1),
        "w_rx": _bf16(w_rx[0]), "b_rx": b_rx[0].reshape(1, -1), "lam": lru_lambda[0].reshape(1, -1),
        "idx_g": _lane_pad(idx_k_norm_g[0]), "idx_b": _lane_pad(idx_k_norm_b[0]),
        "w_pa": _bf16(w_pa[0]), "w_pb": _bf16(w_pb[0]), "w_o": _bf16(w_o[0]), "g_final": g_final.reshape(1, -1),
    }
    half_h, half_i = HEAD_DIM // 2, IDX_DIM // 2
    invf_h = ROPE_THETA ** (-jnp.arange(half_h, dtype=jnp.float32) / half_h)
    invf_i = ROPE_THETA ** (-jnp.arange(half_i, dtype=jnp.float32) / half_i)
    invf = jnp.zeros((SUBLANES, LANES), jnp.float32)
    invf = invf.at[0].set(jnp.tile(invf_h, LANES // half_h)).at[1].set(jnp.tile(invf_i, LANES // half_i))
    tabs_prompt = _rope_tables(invf, SEQ, 0, 1)
    tabs_sample = _rope_tables(invf, SUBLANES, PAST_LEN, 0)

    mod = _ada_modulation(jnp.concatenate([c_prompt, c_sample], axis=0), _bf16(w_ada[0]), b_ada[0].reshape(1, -1))
    shift, scale, gate = mod[:, :D_MODEL], mod[:, D_MODEL:2 * D_MODEL], mod[:, 2 * D_MODEL:]

    (k_p, v_p, ki_p, kb, vt, kib, q, qi, kiwi, mpa, sgb, smb, conv_p, lru_p) = _prompt_projection(
        x_prompt, shift[:BATCH, None, :], scale[:BATCH, None, :], wts, tabs_prompt)
    o = _prompt_attention(q, qi, kiwi, kb, vt, kib)
    y_prompt = _output_projection(o, sgb, mpa, smb, x_prompt, gate[:BATCH, None, :], wts, OUT_ROWS)

    xs = x_sample[:, 0, :]
    (k_s, v_s, q_s, qi_s, kiwi_s, mpa_s, sgb_s, smb_s, conv_s, lru_s) = _sample_projection(
        xs, shift[BATCH:], scale[BATCH:], wts, tabs_sample, jnp.swapaxes(state_conv[0], 0, 1), state_rglru[0])
    pt_flat = page_table.reshape(-1)
    w_col = kiwi_s[:, IDX_DIM:IDX_DIM + N_IDX_HEADS, None]
    idx_pages = jnp.swapaxes(cache_idx_k[0], 1, 2)
    kv_pages = lambda t: t[0].reshape(-1, PAGE_SIZE * N_KV_HEADS, HEAD_DIM)
    scores = _sample_scores(pt_flat, idx_pages, qi_s.reshape(DEC_BATCH, N_IDX_HEADS, IDX_DIM), w_col,
                            kiwi_s[:, None, :])
    bias = _sample_select(scores.reshape(DEC_BATCH, (N_PAGES + 1) * PAGE_SIZE))
    bias = bias.reshape(DEC_BATCH, N_PAGES + 1, PAGE_SIZE)
    o_s = _sample_attention(
        pt_flat, kv_pages(cache_k), kv_pages(cache_v), bias, bias[:, N_PAGES:, :],
        q_s.reshape(DEC_BATCH, N_HEADS, HEAD_DIM), k_s.reshape(DEC_BATCH, N_KV_HEADS, HEAD_DIM),
        v_s.reshape(DEC_BATCH, N_KV_HEADS, HEAD_DIM))
    y_sample = _output_projection(
        o_s.reshape(1, DEC_BATCH, D_ATT), sgb_s[None], mpa_s[None], smb_s[None], xs[None], gate[None, BATCH:],
        wts, DEC_BATCH)

    kv_p = lambda t: t.reshape(1, BATCH, SEQ, N_KV_HEADS, HEAD_DIM)
    kv_s = lambda t: t.reshape(1, DEC_BATCH, 1, N_KV_HEADS, HEAD_DIM)
    return (
        y_prompt, y_sample.reshape(DEC_BATCH, 1, D_MODEL),
        kv_p(k_p), kv_p(v_p), jnp.swapaxes(ki_p, 1, 2)[None], conv_p[None], lru_p.reshape(1, BATCH, D_RNN),
        kv_s(k_s), kv_s(v_s), kiwi_s[:, :IDX_DIM].reshape(1, DEC_BATCH, 1, IDX_DIM),
        jnp.swapaxes(conv_s, 0, 1)[None], lru_s[None],
    )
```

```python
import functools

import jax
import jax.numpy as jnp
import numpy as np
from jax import lax
from jax.experimental import pallas as pl
from jax.experimental.pallas import tpu as pltpu

D_MODEL = 1024
BATCH = 8
SEQ = 4096
DEC_BATCH = 32
PAST_LEN = 16384
PAGE_SIZE = 128
N_PAGES = PAST_LEN // PAGE_SIZE
D_RNN = D_MODEL
RNN_BLOCKS = 4
RNN_BLOCK_W = D_RNN // RNN_BLOCKS
CONV_W = 4
LRU_C = 8.0
N_HEADS = 8
HEAD_DIM = 128
N_KV_HEADS = 2
HEADS_PER_KV = N_HEADS // N_KV_HEADS
D_ATT = N_HEADS * HEAD_DIM
D_KV = N_KV_HEADS * HEAD_DIM
N_IDX_HEADS = 8
IDX_DIM = 64
D_IDX = N_IDX_HEADS * IDX_DIM
IDX_W_SCALE = (N_IDX_HEADS * IDX_DIM) ** -0.5
TOPK = 256
ROPE_THETA = 10000.0
EPS = 1e-6
SPLITS = (D_RNN, D_RNN, D_ATT, D_KV, D_KV, D_ATT, D_IDX, IDX_DIM, N_IDX_HEADS, D_MODEL, D_MODEL)

LANES = 128
SUBLANES = 8

C_XA, C_GA, C_Q, C_GB, C_MA, C_MB = 0, 1024, 2048, 3072, 4096, 5120
C_K, C_V, C_QI, C_KW = 6144, 6400, 6656, 7168
D_IN_PACKED = 7296

PROJ_ROWS = 256
Q_BLOCK = 256
KEY_CHUNK = 256
ATT_CHUNK = 512
PACKED_ROWS = 16
assert KEY_CHUNK >= TOPK and ATT_CHUNK % KEY_CHUNK == 0 and SEQ % ATT_CHUNK == 0 and ATT_CHUNK % Q_BLOCK == 0
assert PAGE_SIZE == HEAD_DIM == LANES
OUT_ROWS = 512
SCORE_PAGES = 16
ATTN_PAGES = 32
assert N_PAGES % SCORE_PAGES == 0 and N_PAGES % ATTN_PAGES == 0
SOFTMAX_SCALE_LOG2E = (HEAD_DIM ** -0.5) * float(np.log2(np.e))
NEG_INF = float("-inf")
INT_MIN = -2 ** 31
KEY_NEG_INF = INT_MIN + 0x7FFFFF
VMEM_LIMIT = 56 * 1024 * 1024


def _sigmoid(x):
    return 1.0 / (1.0 + jnp.exp(-x))


def _silu(x):
    return x * _sigmoid(x)


def _dot(a, b):
    return jnp.dot(a, b, preferred_element_type=jnp.float32)


def _dot_nt(a, b):
    return lax.dot_general(a, b, (((1,), (1,)), ((), ())), preferred_element_type=jnp.float32)


def _bf16(x):
    return x.astype(jnp.bfloat16)


def _const_spec(shape, single=True):
    nd = len(shape)
    kwargs = {"pipeline_mode": pl.Buffered(1)} if single else {}
    return pl.BlockSpec(shape, lambda *_: (0,) * nd, **kwargs)


def _rope_kernel(invf_ref, cos_h_ref, sin_h_ref, cos_i_ref, sin_ia_ref, sin_ib_ref, *, pos0, pos_step, rows):
    r0 = pl.program_id(0) * rows
    row = lax.broadcasted_iota(jnp.int32, (rows, LANES), 0) + r0
    lane = lax.broadcasted_iota(jnp.int32, (rows, LANES), 1)
    pos = (pos0 + pos_step * row).astype(jnp.float32)
    ang_h = pos * invf_ref[0:1, :]
    ang_i = pos * invf_ref[1:2, :]
    cos_h_ref[...] = jnp.cos(ang_h)
    sh = jnp.sin(ang_h)
    sin_h_ref[...] = jnp.where(lane < HEAD_DIM // 2, -sh, sh)
    cos_i_ref[...] = jnp.cos(ang_i)
    si = jnp.sin(ang_i)
    first_half = (lane % IDX_DIM) < IDX_DIM // 2
    sin_ia_ref[...] = jnp.where(first_half, -si, 0.0)
    sin_ib_ref[...] = jnp.where(first_half, 0.0, si)


def _rope_tables(invf, n, pos0, pos_step):
    rows = min(n, 512)
    out = jax.ShapeDtypeStruct((n, LANES), jnp.float32)
    spec = pl.BlockSpec((rows, LANES), lambda i: (i, 0))
    return pl.pallas_call(
        functools.partial(_rope_kernel, pos0=pos0, pos_step=pos_step, rows=rows),
        grid=(n // rows,),
        in_specs=[pl.BlockSpec((SUBLANES, LANES), lambda i: (0, 0))],
        out_specs=[spec] * 5,
        out_shape=[out] * 5,
        name="rope_tables",
    )(invf)


def _rot_head(z, cos, sin_signed):
    return z * cos + pltpu.roll(z, HEAD_DIM // 2, 1) * sin_signed


def _rot_idx(z, cos, sin_a, sin_b):
    return z * cos + pltpu.roll(z, LANES - IDX_DIM // 2, 1) * sin_a + pltpu.roll(z, IDX_DIM // 2, 1) * sin_b


def _ada_kernel(c_ref, w_ref, b_ref, o_ref):
    o_ref[...] = _dot(_bf16(_silu(c_ref[...])), w_ref[...]) + b_ref[...]


def _ada_modulation(c_all, w_ada, b_ada):
    n = c_all.shape[0]
    return pl.pallas_call(
        _ada_kernel,
        grid=(1,),
        in_specs=[_const_spec((n, D_MODEL)), _const_spec((D_MODEL, 3 * D_MODEL)), _const_spec((1, 3 * D_MODEL))],
        out_specs=_const_spec((n, 3 * D_MODEL), single=False),
        out_shape=jax.ShapeDtypeStruct((n, 3 * D_MODEL), jnp.float32),
        compiler_params=pltpu.CompilerParams(vmem_limit_bytes=VMEM_LIMIT),
        name="ada_modulation",
    )(c_all, w_ada, b_ada)


def _modulated_norm(x, g, scale, shift):
    y = x * lax.rsqrt(jnp.mean(x * x, axis=-1, keepdims=True) + EPS) * g
    return _bf16(y * (1.0 + scale) + shift)


def _lru_gates(xc, w_ra_ref, b_ra, w_rx_ref, b_rx, lam):
    xcb = _bf16(xc)
    r_parts, i_parts = [], []
    for n in range(RNN_BLOCKS):
        sl = slice(n * RNN_BLOCK_W, (n + 1) * RNN_BLOCK_W)
        r_parts.append(_dot(xcb[:, sl], w_ra_ref[n]))
        i_parts.append(_dot(xcb[:, sl], w_rx_ref[n]))
    r = _sigmoid(jnp.concatenate(r_parts, axis=1) + b_ra)
    i = _sigmoid(jnp.concatenate(i_parts, axis=1) + b_rx)
    neg_lam = -lam
    softplus = jnp.maximum(neg_lam, 0.0) + jnp.log1p(jnp.exp(-jnp.abs(neg_lam)))
    log_a = (-LRU_C) * r * softplus
    a = jnp.exp(log_a)
    u = jnp.sqrt(-jnp.tanh(log_a) * (a * a + 1.0)) * (i * xc)
    return a, u


def _idx_key_slab(z_kw, g, b, cos_i, sin_ia, sin_ib):
    lane = lax.broadcasted_iota(jnp.int32, z_kw.shape, 1)
    is_key = lane < IDX_DIM
    mu = jnp.sum(jnp.where(is_key, z_kw, 0.0), axis=-1, keepdims=True) * (1.0 / IDX_DIM)
    d = jnp.where(is_key, z_kw - mu, 0.0)
    var = jnp.sum(d * d, axis=-1, keepdims=True) * (1.0 / IDX_DIM)
    y = d * lax.rsqrt(var + EPS) * g + b
    key = _rot_idx(y, cos_i, sin_ia, sin_ib)
    is_w = jnp.logical_and(lane >= IDX_DIM, lane < IDX_DIM + N_IDX_HEADS)
    return key + jnp.where(is_w, z_kw * IDX_W_SCALE, 0.0)


def _proj_kernel(x_ref, shift_ref, scale_ref, gn_ref, w_in_ref, wconv_ref, bconv_ref, w_ra_ref, b_ra_ref,
                 w_rx_ref, b_rx_ref, lam_ref, ig_ref, ib_ref, w_pa_ref,
                 cos_h_ref, sin_h_ref, cos_i_ref, sin_ia_ref, sin_ib_ref,
                 k_ref, v_ref, ki_ref, kb_ref, vt_ref, kib_ref, q_ref, qi_ref, kiwi_ref,
                 mpa_ref, sgb_ref, smb_ref, conv_ref, lru_ref,
                 xa_ext, a_s, u_s, h_carry, tail_s, ga_s, ma_s):
    ts = PROJ_ROWS

    @pl.when(pl.program_id(1) == 0)
    def _():
        tail_s[...] = jnp.zeros(tail_s.shape, jnp.float32)
        h_carry[...] = jnp.zeros(h_carry.shape, jnp.float32)

    xn = _modulated_norm(x_ref[...], gn_ref[...], scale_ref[...], shift_ref[...])

    xa_ext[0:SUBLANES, :] = tail_s[...]
    xa_ext[SUBLANES:SUBLANES + ts, :] = _dot(xn, w_in_ref[:, C_XA:C_XA + D_RNN])
    sgb_ref[...] = _bf16(_silu(_dot(xn, w_in_ref[:, C_GB:C_GB + D_ATT])))
    smb_ref[...] = _bf16(_sigmoid(_dot(xn, w_in_ref[:, C_MB:C_MB + D_MODEL])))
    xc = bconv_ref[...]
    for t in range(CONV_W):
        off = SUBLANES - (CONV_W - 1) + t
        xc = xc + xa_ext[off:off + ts, :] * wconv_ref[t:t + 1, :]
    conv_ref[...] = xa_ext[ts + SUBLANES - (CONV_W - 1):ts + SUBLANES, :]
    tail_s[...] = xa_ext[ts:ts + SUBLANES, :]

    a, u = _lru_gates(xc, w_ra_ref, b_ra_ref[...], w_rx_ref, b_rx_ref[...], lam_ref[...])
    a_s[...] = a
    u_s[...] = u
    row = lax.broadcasted_iota(jnp.int32, (SUBLANES, D_RNN), 0)
    cos_h, sin_h = cos_h_ref[...], sin_h_ref[...]
    cos_i, sin_ia, sin_ib = cos_i_ref[...], sin_ia_ref[...], sin_ib_ref[...]

    def proj_gate_a():
        ga_s[...] = _silu(_dot(xn, w_in_ref[:, C_GA:C_GA + D_RNN]))

    def proj_merge_a():
        ma_s[...] = _sigmoid(_dot(xn, w_in_ref[:, C_MA:C_MA + D_MODEL]))

    def proj_q(lo, hi):
        def run():
            zq = _dot(xn, w_in_ref[:, C_Q + lo * HEAD_DIM:C_Q + hi * HEAD_DIM])
            for h in range(hi - lo):
                rot = _rot_head(zq[:, h * HEAD_DIM:(h + 1) * HEAD_DIM], cos_h, sin_h)
                q_ref[:, (lo + h) * HEAD_DIM:(lo + h + 1) * HEAD_DIM] = _bf16(rot * SOFTMAX_SCALE_LOG2E)
        return run

    def proj_kv():
        zk = _dot(xn, w_in_ref[:, C_K:C_K + D_KV])
        for g in range(N_KV_HEADS):
            sl = slice(g * HEAD_DIM, (g + 1) * HEAD_DIM)
            kr = _rot_head(zk[:, sl], cos_h, sin_h)
            k_ref[:, g, :] = kr
            kb_ref[:, sl] = _bf16(kr)
        zv = _dot(xn, w_in_ref[:, C_V:C_V + D_KV])
        for g in range(N_KV_HEADS):
            v_ref[:, g, :] = zv[:, g * HEAD_DIM:(g + 1) * HEAD_DIM]
        vt = _bf16(zv.T)
        for c in range(ts // KEY_CHUNK):
            vt_ref[c] = vt[:, c * KEY_CHUNK:(c + 1) * KEY_CHUNK]

    def proj_idx():
        zqi = _dot(xn, w_in_ref[:, C_QI:C_QI + D_IDX])
        for p in range(D_IDX // LANES):
            sl = slice(p * LANES, (p + 1) * LANES)
            qi_ref[:, sl] = _bf16(_rot_idx(zqi[:, sl], cos_i, sin_ia, sin_ib))
        slab = _idx_key_slab(_dot(xn, w_in_ref[:, C_KW:C_KW + LANES]), ig_ref[...], ib_ref[...],
                             cos_i, sin_ia, sin_ib)
        kiwi_ref[...] = slab
        ki_ref[...] = slab.T[:IDX_DIM, :]
        key_even = jnp.where(lax.broadcasted_iota(jnp.int32, slab.shape, 1) < IDX_DIM, slab, 0.0)
        kib_ref[:, 0:LANES] = _bf16(key_even)
        kib_ref[:, LANES:2 * LANES] = _bf16(pltpu.roll(key_even, IDX_DIM, 1))

    scan_work = [proj_gate_a, proj_merge_a, proj_q(0, N_HEADS // 2), proj_q(N_HEADS // 2, N_HEADS), proj_kv, proj_idx]

    hc = h_carry[...]
    n_groups = ts // SUBLANES
    per_chunk = n_groups // (len(scan_work) + 2)
    for g in range(n_groups):
        rows = slice(g * SUBLANES, (g + 1) * SUBLANES)
        a8 = a_s[rows, :]
        u8 = u_s[rows, :]
        for d in (1, 2, 4):
            keep = row >= d
            u8 = jnp.where(keep, a8 * pltpu.roll(u8, d, 0) + u8, u8)
            a8 = jnp.where(keep, a8 * pltpu.roll(a8, d, 0), a8)
        h8 = a8 * hc + u8
        u_s[rows, :] = h8
        hc = h8[SUBLANES - 1:SUBLANES, :]
        if g % per_chunk == per_chunk - 1 and g // per_chunk < len(scan_work):
            scan_work[g // per_chunk]()
    h_carry[...] = hc
    lru_ref[...] = hc

    ya = _dot(_bf16(u_s[...] * ga_s[...]), w_pa_ref[...])
    mpa_ref[...] = _bf16(ma_s[...] * ya)


def _prompt_projection(x, shift, scale, wts, tabs):
    ts = PROJ_ROWS
    nt = SEQ // ts
    f32, bf16 = jnp.float32, jnp.bfloat16
    row_spec = lambda w: pl.BlockSpec((None, ts, w), lambda b, j: (b, j, 0))
    bvec_spec = pl.BlockSpec((None, 1, D_MODEL), lambda b, j: (b, 0, 0))
    kv_spec = pl.BlockSpec((None, ts, N_KV_HEADS, HEAD_DIM), lambda b, j: (b, j, 0, 0))
    tab_spec = pl.BlockSpec((ts, LANES), lambda b, j: (j, 0))
    in_specs = [
        row_spec(D_MODEL), bvec_spec, bvec_spec, _const_spec((1, D_MODEL)),
        _const_spec((D_MODEL, D_IN_PACKED)), _const_spec((CONV_W, D_RNN)), _const_spec((1, D_RNN)),
        _const_spec((RNN_BLOCKS, RNN_BLOCK_W, RNN_BLOCK_W)), _const_spec((1, D_RNN)),
        _const_spec((RNN_BLOCKS, RNN_BLOCK_W, RNN_BLOCK_W)), _const_spec((1, D_RNN)), _const_spec((1, D_RNN)),
        _const_spec((1, LANES)), _const_spec((1, LANES)), _const_spec((D_RNN, D_MODEL)),
    ] + [tab_spec] * 5
    out_shape = [
        jax.ShapeDtypeStruct((BATCH, SEQ, N_KV_HEADS, HEAD_DIM), f32),
        jax.ShapeDtypeStruct((BATCH, SEQ, N_KV_HEADS, HEAD_DIM), f32),
        jax.ShapeDtypeStruct((BATCH, IDX_DIM, SEQ), f32),
        jax.ShapeDtypeStruct((BATCH, SEQ, D_KV), bf16),
        jax.ShapeDtypeStruct((BATCH, SEQ // KEY_CHUNK, D_KV, KEY_CHUNK), bf16),
        jax.ShapeDtypeStruct((BATCH, SEQ, 2 * LANES), bf16),
        jax.ShapeDtypeStruct((BATCH, SEQ, D_ATT), bf16),
        jax.ShapeDtypeStruct((BATCH, SEQ, D_IDX), bf16),
        jax.ShapeDtypeStruct((BATCH, SEQ, LANES), f32),
        jax.ShapeDtypeStruct((BATCH, SEQ, D_MODEL), bf16),
        jax.ShapeDtypeStruct((BATCH, SEQ, D_ATT), bf16),
        jax.ShapeDtypeStruct((BATCH, SEQ, D_MODEL), bf16),
        jax.ShapeDtypeStruct((BATCH, CONV_W - 1, D_RNN), f32),
        jax.ShapeDtypeStruct((BATCH, 1, D_RNN), f32),
    ]
    out_specs = [
        kv_spec, kv_spec, pl.BlockSpec((None, IDX_DIM, ts), lambda b, j: (b, 0, j)), row_spec(D_KV),
        pl.BlockSpec((None, ts // KEY_CHUNK, D_KV, KEY_CHUNK), lambda b, j: (b, j, 0, 0)),
        row_spec(2 * LANES), row_spec(D_ATT), row_spec(D_IDX), row_spec(LANES),
        row_spec(D_MODEL), row_spec(D_ATT), row_spec(D_MODEL),
        pl.BlockSpec((None, CONV_W - 1, D_RNN), lambda b, j: (b, 0, 0)),
        pl.BlockSpec((None, 1, D_RNN), lambda b, j: (b, 0, 0)),
    ]
    scratch = [
        pltpu.VMEM((ts + SUBLANES, D_RNN), f32), pltpu.VMEM((ts, D_RNN), f32),
        pltpu.VMEM((ts, D_RNN), f32), pltpu.VMEM((1, D_RNN), f32), pltpu.VMEM((SUBLANES, D_RNN), f32),
        pltpu.VMEM((ts, D_RNN), f32), pltpu.VMEM((ts, D_MODEL), f32),
    ]
    return pl.pallas_call(
        _proj_kernel,
        grid=(BATCH, nt),
        in_specs=in_specs, out_specs=out_specs, out_shape=out_shape, scratch_shapes=scratch,
        compiler_params=pltpu.CompilerParams(
            dimension_semantics=("arbitrary", "arbitrary"), vmem_limit_bytes=VMEM_LIMIT),
        name="prompt_projection",
    )(x, shift, scale, wts["g_norm"], wts["w_in"], wts["w_conv"], wts["b_conv"], wts["w_ra"], wts["b_ra"],
      wts["w_rx"], wts["b_rx"], wts["lam"], wts["idx_g"], wts["idx_b"], wts["w_pa"], *tabs)


def _threshold_value(t_unsigned):
    key = jnp.maximum(jnp.bitwise_xor(t_unsigned, jnp.int32(INT_MIN)), jnp.int32(KEY_NEG_INF))
    bits = key ^ jnp.bitwise_and(jnp.right_shift(key, 31), jnp.int32(0x7FFFFFFF))
    return pltpu.bitcast(bits, jnp.float32)


def _greedy_bits(count_ge, n_bits, shape, count_all):
    def bit_step(b, carry):
        t, cnt = carry
        cand = jnp.bitwise_or(t, jnp.left_shift(jnp.int32(1), n_bits - 1 - b))
        c = count_ge(cand)
        ok = c >= TOPK
        return jnp.where(ok, cand, t), jnp.where(ok, c, cnt)
    return lax.fori_loop(0, n_bits, bit_step, (jnp.zeros(shape, jnp.int32), count_all))


def _fold_rows(x, rows, op=jnp.add, chains=4):
    parts = [x[r:r + rows] for r in range(0, x.shape[0], rows)]
    acc = parts[:chains]
    for k, part in enumerate(parts[chains:]):
        acc[k % len(acc)] = op(acc[k % len(acc)], part)
    while len(acc) > 1:
        acc = [op(acc[k], acc[k + 1]) for k in range(0, len(acc) - 1, 2)] + ([acc[-1]] if len(acc) % 2 else [])
    return acc[0]


def _tie_cutoff(count_tie_below, need, n_bits, shape):
    def bit_step(b, x):
        cand = jnp.bitwise_or(x, jnp.left_shift(jnp.int32(1), n_bits - 1 - b))
        return jnp.where(count_tie_below(cand) < need, cand, x)
    return lax.fori_loop(0, n_bits, bit_step, jnp.zeros(shape, jnp.int32))


def _attn_kernel(q_ref, qi_ref, kiwi_ref, kb_ref, vt_ref, kib_ref, o_ref,
                 score_s, hi_s, bias_s, s_scr, acc_s):
    i = pl.program_id(1)
    t0 = i * Q_BLOCK
    n_steps = (t0 + Q_BLOCK + ATT_CHUNK - 1) // ATT_CHUNK
    step_iota = lax.broadcasted_iota(jnp.int32, (ATT_CHUNK, Q_BLOCK), 0)
    sub_iota = lax.broadcasted_iota(jnp.int32, (KEY_CHUNK, Q_BLOCK), 0)
    q_pos = t0 + lax.broadcasted_iota(jnp.int32, (1, Q_BLOCK), 1)
    lane_shape = (1, Q_BLOCK)

    def step_rows(c):
        return pl.ds(pl.multiple_of(c * ATT_CHUNK, ATT_CHUNK), ATT_CHUNK)

    w_t = kiwi_ref[...].T[IDX_DIM:IDX_DIM + N_IDX_HEADS, :]
    qi = qi_ref[...]
    n_pairs = D_IDX // LANES
    qi_rows = jnp.concatenate([qi[:, p * LANES:(p + 1) * LANES] for p in range(n_pairs)], axis=0)

    def score_step(c, causal):
        for sub in range(ATT_CHUNK // KEY_CHUNK):
            r0 = pl.multiple_of(c * ATT_CHUNK + sub * KEY_CHUNK, KEY_CHUNK)
            rows = pl.ds(r0, KEY_CHUNK)
            s_par = [_dot_nt(kib_ref[rows, par * LANES:(par + 1) * LANES], qi_rows) for par in range(2)]
            score = jnp.zeros((KEY_CHUNK, Q_BLOCK), jnp.float32)
            for h in range(N_IDX_HEADS):
                s_h = s_par[h % 2][:, (h // 2) * Q_BLOCK:(h // 2 + 1) * Q_BLOCK]
                score = score + jnp.maximum(s_h, 0.0) * w_t[h:h + 1, :]
            if causal:
                score = jnp.where((r0 + sub_iota) <= q_pos, score, NEG_INF)
            score_s[rows, :] = score
            hi_s[rows, :] = _bf16(score)

    def early_score_step(c, carry):
        score_step(c, False)
        return carry

    lax.fori_loop(0, n_steps - 1, early_score_step, 0)
    score_step(n_steps - 1, True)

    def count_where(pred):
        def body(c, acc):
            sel = pred(score_s[step_rows(c), :], c * ATT_CHUNK + step_iota)
            parts = [_fold_rows(jnp.where(sel[:, t * LANES:(t + 1) * LANES], 1.0, 0.0), SUBLANES, chains=2)
                     for t in range(Q_BLOCK // LANES)]
            return acc + jnp.concatenate(parts, axis=1)
        acc = lax.fori_loop(0, n_steps, body, jnp.zeros((SUBLANES, Q_BLOCK), jnp.float32))
        return jnp.sum(acc, axis=0, keepdims=True)

    def count_rounded_ge(t16):
        cand = _threshold_value(jnp.left_shift(t16, 16)).astype(jnp.bfloat16)
        def body(c, acc):
            one = jnp.where(hi_s[step_rows(c), :] >= cand, jnp.bfloat16(1), jnp.bfloat16(0))
            return acc + _fold_rows(one, PACKED_ROWS, chains=2)
        acc = lax.fori_loop(0, n_steps, body, jnp.zeros((PACKED_ROWS, Q_BLOCK), jnp.bfloat16))
        return jnp.sum(acc.astype(jnp.float32), axis=0, keepdims=True)

    count_all = jnp.full(lane_shape, 1.0, jnp.float32) * (n_steps * ATT_CHUNK).astype(jnp.float32)
    t1, _ = _greedy_bits(count_rounded_ge, 16, lane_shape, count_all)

    base = jnp.left_shift(jnp.maximum(t1 - 1, 0), 16)

    def count_ge(offset):
        cand = _threshold_value(base + offset)
        return count_where(lambda blk, idx: blk >= cand)

    off, cnt_ge = _greedy_bits(count_ge, 17, lane_shape, count_ge(jnp.zeros(lane_shape, jnp.int32)))
    thr = _threshold_value(base + off)

    def write_bias(select):
        def step(c, causal):
            blk = score_s[step_rows(c), :]
            idx = c * ATT_CHUNK + step_iota
            sel = select(blk, idx)
            if causal:
                sel = jnp.logical_and(sel, idx <= q_pos)
            bias_s[step_rows(c), :] = jnp.where(sel, 0.0, NEG_INF)

        def early_step(c, carry):
            step(c, False)
            return carry

        lax.fori_loop(0, n_steps - 1, early_step, 0)
        step(n_steps - 1, True)

    has_tie = jnp.max(jnp.where(cnt_ge > TOPK, 1.0, 0.0)) > 0.5

    @pl.when(jnp.logical_not(has_tie))
    def _():
        write_bias(lambda blk, idx: blk >= thr)

    @pl.when(has_tie)
    def _():
        need = TOPK - count_where(lambda blk, idx: blk > thr)
        cutoff = _tie_cutoff(
            lambda x: count_where(lambda blk, idx: jnp.logical_and(blk == thr, idx < x)),
            need, int(np.log2(SEQ)), lane_shape)
        write_bias(lambda blk, idx: jnp.logical_or(blk > thr, jnp.logical_and(blk == thr, idx <= cutoff)))

    q = q_ref[...]
    n_lanes = HEADS_PER_KV * Q_BLOCK
    q_rows = [
        jnp.concatenate([q[:, (g * HEADS_PER_KV + h) * HEAD_DIM:(g * HEADS_PER_KV + h + 1) * HEAD_DIM]
                         for h in range(HEADS_PER_KV)], axis=0)
        for g in range(N_KV_HEADS)]
    acc_s[...] = jnp.zeros(acc_s.shape, jnp.float32)
    subs = ATT_CHUNK // KEY_CHUNK

    ones_rows = jnp.ones((PACKED_ROWS, KEY_CHUNK), jnp.bfloat16)

    d_sl = [slice(g * HEAD_DIM, (g + 1) * HEAD_DIM) for g in range(N_KV_HEADS)]

    def logits(c, slot):
        maxima = []
        for g in range(N_KV_HEADS):
            mx = None
            for r in range(0, ATT_CHUNK, KEY_CHUNK):
                rows = pl.ds(pl.multiple_of(c * ATT_CHUNK + r, KEY_CHUNK), KEY_CHUNK)
                b = bias_s[rows, :]
                x = _dot_nt(kb_ref[rows, d_sl[g]], q_rows[g]) + jnp.concatenate([b] * HEADS_PER_KV, axis=1)
                s_scr[slot, g, r:r + KEY_CHUNK, :] = x
                f = _fold_rows(x, SUBLANES, jnp.maximum)
                mx = f if mx is None else jnp.maximum(mx, f)
            maxima.append(jnp.max(mx, axis=0, keepdims=True))
        return tuple(maxima)

    def accumulate(c, slot, m_old, m_step):
        m_out = []
        for g in range(N_KV_HEADS):
            m_new = jnp.maximum(m_old[g], m_step[g])
            m_safe = jnp.where(m_new == NEG_INF, 0.0, m_new)
            alpha = jnp.exp2(m_old[g] - m_safe)
            pv = None
            for sub in range(subs):
                r = sub * KEY_CHUNK
                pb = _bf16(jnp.exp2(s_scr[slot, g, r:r + KEY_CHUNK, :] - m_safe))
                lhs = jnp.concatenate([vt_ref[c * subs + sub, d_sl[g], :], ones_rows], axis=0)
                part = _dot(lhs, pb)
                pv = part if pv is None else pv + part
            acc_s[g] = acc_s[g] * alpha + pv
            m_out.append(m_new)
        return tuple(m_out)

    def attend_pair(p, carry):
        m_run, m_even = carry
        c = 2 * p
        m_odd = logits(c + 1, 1)
        m_run = accumulate(c, 0, m_run, m_even)
        m_even = logits(jnp.minimum(c + 2, n_steps - 1), 0)
        m_run = accumulate(c + 1, 1, m_run, m_odd)
        return m_run, m_even

    m_init = tuple(jnp.full((1, n_lanes), NEG_INF, jnp.float32) for _ in range(N_KV_HEADS))
    m_run, m_even = lax.fori_loop(0, n_steps // 2, attend_pair, (m_init, logits(0, 0)))

    @pl.when(n_steps % 2 == 1)
    def _():
        accumulate(n_steps - 1, 0, m_run, m_even)

    for g in range(N_KV_HEADS):
        o_t = acc_s[g, 0:HEAD_DIM, :] / acc_s[g, HEAD_DIM:HEAD_DIM + 1, :]
        for h in range(HEADS_PER_KV):
            col = (g * HEADS_PER_KV + h) * HEAD_DIM
            o_ref[:, col:col + HEAD_DIM] = _bf16(o_t[:, h * Q_BLOCK:(h + 1) * Q_BLOCK].T)


def _prompt_attention(q, qi, kiwi, kb, vt, kib):
    nq = SEQ // Q_BLOCK
    blk = lambda w: pl.BlockSpec((None, Q_BLOCK, w), lambda b, i: (b, i, 0))
    return pl.pallas_call(
        _attn_kernel,
        grid=(BATCH, nq),
        in_specs=[
            blk(D_ATT), blk(D_IDX), blk(LANES),
            pl.BlockSpec((None, SEQ, D_KV), lambda b, i: (b, 0, 0)),
            pl.BlockSpec((None, SEQ // KEY_CHUNK, D_KV, KEY_CHUNK), lambda b, i: (b, 0, 0, 0)),
            pl.BlockSpec((None, SEQ, 2 * LANES), lambda b, i: (b, 0, 0)),
        ],
        out_specs=blk(D_ATT),
        out_shape=jax.ShapeDtypeStruct((BATCH, SEQ, D_ATT), jnp.bfloat16),
        scratch_shapes=[
            pltpu.VMEM((SEQ, Q_BLOCK), jnp.float32),
            pltpu.VMEM((SEQ, Q_BLOCK), jnp.bfloat16),
            pltpu.VMEM((SEQ, Q_BLOCK), jnp.float32),
            pltpu.VMEM((2, N_KV_HEADS, ATT_CHUNK, HEADS_PER_KV * Q_BLOCK), jnp.float32),
            pltpu.VMEM((N_KV_HEADS, HEAD_DIM + PACKED_ROWS, HEADS_PER_KV * Q_BLOCK), jnp.float32),
        ],
        compiler_params=pltpu.CompilerParams(
            dimension_semantics=("arbitrary", "arbitrary"), vmem_limit_bytes=VMEM_LIMIT),
        name="prompt_attention",
    )(q, qi, kiwi, kb, vt, kib)


def _out_kernel(o_ref, sgb_ref, mpa_ref, smb_ref, x_ref, gate_ref, w_pb_ref, w_o_ref, gf_ref, y_ref):
    f32 = jnp.float32
    yb = _dot(_bf16(o_ref[...].astype(f32) * sgb_ref[...].astype(f32)), w_pb_ref[...])
    m = mpa_ref[...].astype(f32) + smb_ref[...].astype(f32) * yb
    r = x_ref[...] + gate_ref[...] * _dot(_bf16(m), w_o_ref[...])
    y_ref[...] = r * lax.rsqrt(jnp.mean(r * r, axis=-1, keepdims=True) + EPS) * gf_ref[...]


def _output_projection(o, sgb, mpa, smb, x, gate, wts, rows):
    ng, nr, _ = x.shape
    row_spec = pl.BlockSpec((None, rows, D_MODEL), lambda b, j: (b, j, 0))
    if gate.shape[1] == 1:
        gate_spec = pl.BlockSpec((None, 1, D_MODEL), lambda b, j: (b, 0, 0))
    else:
        gate_spec = row_spec
    return pl.pallas_call(
        _out_kernel,
        grid=(ng, nr // rows),
        in_specs=[row_spec] * 5 + [gate_spec, _const_spec((D_ATT, D_MODEL)), _const_spec((D_MODEL, D_MODEL)),
                                   _const_spec((1, D_MODEL))],
        out_specs=row_spec,
        out_shape=jax.ShapeDtypeStruct(x.shape, jnp.float32),
        compiler_params=pltpu.CompilerParams(
            dimension_semantics=("arbitrary", "arbitrary"), vmem_limit_bytes=VMEM_LIMIT),
        name="output_projection",
    )(o, sgb, mpa, smb, x, gate, wts["w_pb"], wts["w_o"], wts["g_final"])


def _sample_proj_kernel(x_ref, shift_ref, scale_ref, gn_ref, w_in_ref, wconv_ref, bconv_ref, w_ra_ref, b_ra_ref,
                        w_rx_ref, b_rx_ref, lam_ref, ig_ref, ib_ref, w_pa_ref,
                        cos_h_ref, sin_h_ref, cos_i_ref, sin_ia_ref, sin_ib_ref, buf_ref, h0_ref,
                        k_ref, v_ref, q_ref, qi_ref, kiwi_ref, mpa_ref, sgb_ref, smb_ref, conv_ref, lru_ref):
    xn = _modulated_norm(x_ref[...], gn_ref[...], scale_ref[...], shift_ref[...])
    xa = _dot(xn, w_in_ref[:, C_XA:C_XA + D_RNN])
    xc = bconv_ref[...]
    for t in range(CONV_W - 1):
        xc = xc + buf_ref[t] * wconv_ref[t:t + 1, :]
        if t > 0:
            conv_ref[t - 1] = buf_ref[t]
    xc = xc + xa * wconv_ref[CONV_W - 1:CONV_W, :]
    conv_ref[CONV_W - 2] = xa
    a, u = _lru_gates(xc, w_ra_ref, b_ra_ref[...], w_rx_ref, b_rx_ref[...], lam_ref[...])
    h = a * h0_ref[...] + u
    lru_ref[...] = h
    ga = _dot(xn, w_in_ref[:, C_GA:C_GA + D_RNN])
    ya = _dot(_bf16(h * _silu(ga)), w_pa_ref[...])
    mpa_ref[...] = _sigmoid(_dot(xn, w_in_ref[:, C_MA:C_MA + D_MODEL])) * ya
    sgb_ref[...] = _silu(_dot(xn, w_in_ref[:, C_GB:C_GB + D_ATT]))
    smb_ref[...] = _sigmoid(_dot(xn, w_in_ref[:, C_MB:C_MB + D_MODEL]))

    cos_h, sin_h = cos_h_ref[0:1, :], sin_h_ref[0:1, :]
    cos_i, sin_ia, sin_ib = cos_i_ref[0:1, :], sin_ia_ref[0:1, :], sin_ib_ref[0:1, :]
    zq = _dot(xn, w_in_ref[:, C_Q:C_Q + D_ATT])
    for hd in range(N_HEADS):
        sl = slice(hd * HEAD_DIM, (hd + 1) * HEAD_DIM)
        q_ref[:, sl] = _bf16(_rot_head(zq[:, sl], cos_h, sin_h))
    zk = _dot(xn, w_in_ref[:, C_K:C_K + D_KV])
    for g in range(N_KV_HEADS):
        sl = slice(g * HEAD_DIM, (g + 1) * HEAD_DIM)
        k_ref[:, sl] = _rot_head(zk[:, sl], cos_h, sin_h)
    v_ref[...] = _dot(xn, w_in_ref[:, C_V:C_V + D_KV])
    zqi = _dot(xn, w_in_ref[:, C_QI:C_QI + D_IDX])
    for p in range(D_IDX // LANES):
        sl = slice(p * LANES, (p + 1) * LANES)
        qi_ref[:, sl] = _bf16(_rot_idx(zqi[:, sl], cos_i, sin_ia, sin_ib))
    kiwi_ref[...] = _idx_key_slab(_dot(xn, w_in_ref[:, C_KW:C_KW + LANES]), ig_ref[...], ib_ref[...],
                                  cos_i, sin_ia, sin_ib)


def _sample_projection(x, shift, scale, wts, tabs, buf_t, h0):
    n = DEC_BATCH
    f32, bf16 = jnp.float32, jnp.bfloat16
    in_specs = [
        _const_spec((n, D_MODEL)), _const_spec((n, D_MODEL)), _const_spec((n, D_MODEL)), _const_spec((1, D_MODEL)),
        _const_spec((D_MODEL, D_IN_PACKED)), _const_spec((CONV_W, D_RNN)), _const_spec((1, D_RNN)),
        _const_spec((RNN_BLOCKS, RNN_BLOCK_W, RNN_BLOCK_W)), _const_spec((1, D_RNN)),
        _const_spec((RNN_BLOCKS, RNN_BLOCK_W, RNN_BLOCK_W)), _const_spec((1, D_RNN)), _const_spec((1, D_RNN)),
        _const_spec((1, LANES)), _const_spec((1, LANES)), _const_spec((D_RNN, D_MODEL)),
    ] + [_const_spec((SUBLANES, LANES))] * 5 + [_const_spec((CONV_W - 1, n, D_RNN)), _const_spec((n, D_RNN))]
    shapes = [
        ((n, D_KV), f32), ((n, D_KV), f32), ((n, D_ATT), bf16), ((n, D_IDX), bf16), ((n, LANES), f32),
        ((n, D_MODEL), f32), ((n, D_ATT), f32), ((n, D_MODEL), f32), ((CONV_W - 1, n, D_RNN), f32), ((n, D_RNN), f32),
    ]
    return pl.pallas_call(
        _sample_proj_kernel,
        grid=(1,),
        in_specs=in_specs,
        out_specs=[_const_spec(s, single=False) for s, _ in shapes],
        out_shape=[jax.ShapeDtypeStruct(s, d) for s, d in shapes],
        compiler_params=pltpu.CompilerParams(vmem_limit_bytes=VMEM_LIMIT),
        name="sample_projection",
    )(x, shift, scale, wts["g_norm"], wts["w_in"], wts["w_conv"], wts["b_conv"], wts["w_ra"], wts["b_ra"],
      wts["w_rx"], wts["b_rx"], wts["lam"], wts["idx_g"], wts["idx_b"], wts["w_pa"], *tabs, buf_t, h0)


def _sample_score_kernel(pt_ref, qi_ref, w_ref, kinew_ref, idx_hbm, o_ref, buf, sem):
    b = pl.program_id(0)
    slot = b % 2

    def page_copy(sample, p, sl):
        return pltpu.make_async_copy(idx_hbm.at[pt_ref[sample * N_PAGES + p]], buf.at[sl, p], sem.at[sl])

    def start_sample(sample, sl):
        def body(p, carry):
            page_copy(sample, p, sl).start()
            return carry
        lax.fori_loop(0, N_PAGES, body, 0)

    @pl.when(b == 0)
    def _():
        start_sample(0, 0)

    @pl.when(b + 1 < pl.num_programs(0))
    def _():
        start_sample(b + 1, 1 - slot)

    def wait_page(p, carry):
        page_copy(b, p, slot).wait()
        return carry

    lax.fori_loop(0, N_PAGES, wait_page, 0)

    qi = qi_ref[...]
    w = w_ref[...]

    def score_pages(i, carry):
        p0 = i * SCORE_PAGES
        kt = _bf16(jnp.concatenate([buf[slot, p0 + t] for t in range(SCORE_PAGES)], axis=1))
        s = _dot(qi, kt)
        score = jnp.sum(jnp.maximum(s, 0.0) * w, axis=0, keepdims=True)
        for t in range(SCORE_PAGES):
            o_ref[pl.ds(p0 + t, 1), :] = score[:, t * PAGE_SIZE:(t + 1) * PAGE_SIZE]
        return carry

    lax.fori_loop(0, N_PAGES // SCORE_PAGES, score_pages, 0)

    k_self = _bf16(kinew_ref[...][:, :IDX_DIM]).astype(jnp.float32)
    s_self = jnp.sum(qi.astype(jnp.float32) * k_self, axis=1, keepdims=True)
    score_self = jnp.sum(jnp.maximum(s_self, 0.0) * w, axis=0, keepdims=True)
    lane = lax.broadcasted_iota(jnp.int32, (1, PAGE_SIZE), 1)
    o_ref[N_PAGES:N_PAGES + 1, :] = jnp.where(lane == 0, score_self, NEG_INF)


def _sample_scores(page_table_flat, idx_pages, qi3, w_col, kiwi3):
    per_sample = lambda r, w: pl.BlockSpec((None, r, w), lambda b, pt: (b, 0, 0))
    return pl.pallas_call(
        _sample_score_kernel,
        grid_spec=pltpu.PrefetchScalarGridSpec(
            num_scalar_prefetch=1,
            grid=(DEC_BATCH,),
            in_specs=[per_sample(N_IDX_HEADS, IDX_DIM), per_sample(N_IDX_HEADS, 1), per_sample(1, LANES),
                      pl.BlockSpec(memory_space=pl.ANY)],
            out_specs=per_sample(N_PAGES + 1, PAGE_SIZE),
            scratch_shapes=[pltpu.VMEM((2, N_PAGES, IDX_DIM, PAGE_SIZE), jnp.float32),
                            pltpu.SemaphoreType.DMA((2,))],
        ),
        out_shape=jax.ShapeDtypeStruct((DEC_BATCH, N_PAGES + 1, PAGE_SIZE), jnp.float32),
        compiler_params=pltpu.CompilerParams(dimension_semantics=("arbitrary",), vmem_limit_bytes=VMEM_LIMIT),
        name="sample_scores",
    )(page_table_flat, qi3, w_col, kiwi3, idx_pages)


def _sample_select_kernel(score_ref, bias_ref):
    keys = score_ref[...]
    idx = lax.broadcasted_iota(jnp.int32, keys.shape, 1)
    col_shape = (keys.shape[0], 1)
    count = lambda pred: jnp.sum(jnp.where(pred, 1.0, 0.0), axis=1, keepdims=True)
    total = jnp.full(col_shape, float(keys.shape[1]), jnp.float32)
    thr_u, _ = _greedy_bits(lambda t: count(keys >= _threshold_value(t)), 32, col_shape, total)
    thr = _threshold_value(thr_u)
    need = TOPK - count(keys > thr)
    tie = keys == thr
    n_bits = int(np.ceil(np.log2(keys.shape[1])))
    cutoff = _tie_cutoff(lambda x: count(jnp.logical_and(tie, idx < x)), need, n_bits, col_shape)
    sel = jnp.logical_or(keys > thr, jnp.logical_and(tie, idx <= cutoff))
    bias_ref[...] = jnp.where(sel, 0.0, NEG_INF)


def _sample_select(scores):
    return pl.pallas_call(
        _sample_select_kernel,
        grid=(1,),
        in_specs=[_const_spec(scores.shape)],
        out_specs=_const_spec(scores.shape, single=False),
        out_shape=jax.ShapeDtypeStruct(scores.shape, jnp.float32),
        compiler_params=pltpu.CompilerParams(vmem_limit_bytes=VMEM_LIMIT),
        name="sample_select",
    )(scores)


def _sample_attn_kernel(pt_ref, bias_ref, bias_self_ref, q_ref, knew_ref, vnew_ref, k_hbm, v_hbm, o_ref,
                        kbuf, vbuf, sem, m_s, l_s, acc_s):
    b, j = pl.program_id(0), pl.program_id(1)
    n_j = pl.num_programs(1)
    step = b * n_j + j
    slot = step % 2

    def page_copies(st, sl):
        out = []
        for t in range(ATTN_PAGES):
            page = pt_ref[st * ATTN_PAGES + t]
            out.append(pltpu.make_async_copy(k_hbm.at[page], kbuf.at[sl, t], sem.at[0, sl]))
            out.append(pltpu.make_async_copy(v_hbm.at[page], vbuf.at[sl, t], sem.at[1, sl]))
        return out

    @pl.when(step == 0)
    def _():
        for c in page_copies(0, 0):
            c.start()

    @pl.when(step + 1 < pl.num_programs(0) * n_j)
    def _():
        for c in page_copies(step + 1, 1 - slot):
            c.start()

    for c in page_copies(step, slot):
        c.wait()

    @pl.when(j == 0)
    def _():
        m_s[...] = jnp.full(m_s.shape, NEG_INF, jnp.float32)
        l_s[...] = jnp.zeros(l_s.shape, jnp.float32)
        acc_s[...] = jnp.zeros(acc_s.shape, jnp.float32)

    def online_update(s, pv_of):
        m_old = m_s[...]
        m_new = jnp.maximum(m_old, jnp.max(s, axis=1, keepdims=True))
        m_safe = jnp.where(m_new == NEG_INF, 0.0, m_new)
        pr = jnp.exp2((s - m_safe) * SOFTMAX_SCALE_LOG2E)
        alpha = jnp.exp2((m_old - m_safe) * SOFTMAX_SCALE_LOG2E)
        l_s[...] = alpha * l_s[...] + jnp.sum(pr, axis=1, keepdims=True)
        acc_s[...] = acc_s[...] * alpha + pv_of(_bf16(pr))
        m_s[...] = m_new

    q = q_ref[...]
    rows_per_page = PAGE_SIZE * N_KV_HEADS
    n_cols = ATTN_PAGES * rows_per_page
    k_all = _bf16(kbuf[slot].reshape(n_cols, HEAD_DIM))
    v_all = _bf16(vbuf[slot].reshape(n_cols, HEAD_DIM))
    dup = jnp.where(lax.broadcasted_iota(jnp.int32, (PAGE_SIZE, rows_per_page), 1) // N_KV_HEADS
                    == lax.broadcasted_iota(jnp.int32, (PAGE_SIZE, rows_per_page), 0), 1.0, 0.0)
    sel_pages = _dot(_bf16(jnp.where(bias_ref[...] == 0.0, 1.0, 0.0)), _bf16(dup))
    sel_row = jnp.concatenate([sel_pages[t:t + 1, :] for t in range(ATTN_PAGES)], axis=1)
    head = lax.broadcasted_iota(jnp.int32, (N_HEADS, n_cols), 0)
    col = lax.broadcasted_iota(jnp.int32, (N_HEADS, n_cols), 1)
    own = (col % N_KV_HEADS) == (head // HEADS_PER_KV)
    s = jnp.where(jnp.logical_and(own, sel_row > 0.5), _dot_nt(q, k_all), NEG_INF)
    online_update(s, lambda pb: _dot(pb, v_all))

    @pl.when(j == n_j - 1)
    def _():
        head_d = lax.broadcasted_iota(jnp.int32, (N_HEADS, HEAD_DIM), 0) // HEADS_PER_KV

        def own_row(ref):
            rows = _bf16(ref[...]).astype(jnp.float32)
            out = jnp.broadcast_to(rows[N_KV_HEADS - 1:N_KV_HEADS, :], (N_HEADS, HEAD_DIM))
            for g in range(N_KV_HEADS - 2, -1, -1):
                out = jnp.where(head_d == g, rows[g:g + 1, :], out)
            return out

        s_self = jnp.sum(q.astype(jnp.float32) * own_row(knew_ref), axis=1, keepdims=True)
        v_own = own_row(vnew_ref)
        online_update(s_self + bias_self_ref[...][:, 0:1], lambda pb: pb.astype(jnp.float32) * v_own)
        o_ref[...] = acc_s[...] / l_s[...]


def _sample_attention(page_table_flat, k_pages, v_pages, bias3, bias_self, q3, knew3, vnew3):
    per_sample = lambda r, w: pl.BlockSpec((None, r, w), lambda b, j, pt: (b, 0, 0))
    rows_per_page = PAGE_SIZE * N_KV_HEADS
    return pl.pallas_call(
        _sample_attn_kernel,
        grid_spec=pltpu.PrefetchScalarGridSpec(
            num_scalar_prefetch=1,
            grid=(DEC_BATCH, N_PAGES // ATTN_PAGES),
            in_specs=[
                pl.BlockSpec((None, ATTN_PAGES, PAGE_SIZE), lambda b, j, pt: (b, j, 0)),
                per_sample(1, PAGE_SIZE), per_sample(N_HEADS, HEAD_DIM),
                per_sample(N_KV_HEADS, HEAD_DIM), per_sample(N_KV_HEADS, HEAD_DIM),
                pl.BlockSpec(memory_space=pl.ANY), pl.BlockSpec(memory_space=pl.ANY),
            ],
            out_specs=per_sample(N_HEADS, HEAD_DIM),
            scratch_shapes=[
                pltpu.VMEM((2, ATTN_PAGES, rows_per_page, HEAD_DIM), jnp.float32),
                pltpu.VMEM((2, ATTN_PAGES, rows_per_page, HEAD_DIM), jnp.float32),
                pltpu.SemaphoreType.DMA((2, 2)),
                pltpu.VMEM((N_HEADS, 1), jnp.float32), pltpu.VMEM((N_HEADS, 1), jnp.float32),
                pltpu.VMEM((N_HEADS, HEAD_DIM), jnp.float32),
            ],
        ),
        out_shape=jax.ShapeDtypeStruct((DEC_BATCH, N_HEADS, HEAD_DIM), jnp.float32),
        compiler_params=pltpu.CompilerParams(
            dimension_semantics=("arbitrary", "arbitrary"), vmem_limit_bytes=VMEM_LIMIT),
        name="sample_attention",
    )(page_table_flat, bias3, bias_self, q3, knew3, vnew3, k_pages, v_pages)


def _pack_w_in(w_in):
    xa, ga, q, k, v, gb, qi, ki, wi, ma, mb = jnp.split(w_in, np.cumsum(SPLITS)[:-1].tolist(), axis=-1)
    pad = jnp.zeros((D_MODEL, LANES - IDX_DIM - N_IDX_HEADS), w_in.dtype)
    return _bf16(jnp.concatenate([xa, ga, q, gb, ma, mb, k, v, qi, ki, wi, pad], axis=-1))


def _lane_pad(v):
    return jnp.pad(v.reshape(1, -1), ((0, 0), (0, LANES - v.shape[-1])))


def kernel(x_prompt, x_sample, cache_k, cache_v, cache_idx_k, state_conv, state_rglru, page_table, c_prompt, c_sample, w_ada, b_ada, g_norm, w_in, w_conv, b_conv, w_ra, b_ra, w_rx, b_rx, lru_lambda, idx_k_norm_g, idx_k_norm_b, w_pa, w_pb, w_o, g_final):
    assert w_in.shape[0] == 1, "one layer"
    wts = {
        "g_norm": g_norm[0].reshape(1, -1), "w_in": _pack_w_in(w_in[0]), "w_conv": w_conv[0],
        "b_conv": b_conv[0].reshape(1, -1), "w_ra": _bf16(w_ra[0]), "b_ra": b_ra[0].reshape(1, -1),
        "w_rx": _bf16(w_rx[0]), "b_rx": b_rx[0].reshape(1, -1), "lam": lru_lambda[0].reshape(1, -1),
        "idx_g": _lane_pad(idx_k_norm_g[0]), "idx_b": _lane_pad(idx_k_norm_b[0]),
        "w_pa": _bf16(w_pa[0]), "w_pb": _bf16(w_pb[0]), "w_o": _bf16(w_o[0]), "g_final": g_final.reshape(1, -1),
    }
    half_h, half_i = HEAD_DIM // 2, IDX_DIM // 2
    invf_h = ROPE_THETA ** (-jnp.arange(half_h, dtype=jnp.float32) / half_h)
    invf_i = ROPE_THETA ** (-jnp.arange(half_i, dtype=jnp.float32) / half_i)
    invf = jnp.zeros((SUBLANES, LANES), jnp.float32)
    invf = invf.at[0].set(jnp.tile(invf_h, LANES // half_h)).at[1].set(jnp.tile(invf_i, LANES // half_i))
    tabs_prompt = _rope_tables(invf, SEQ, 0, 1)
    tabs_sample = _rope_tables(invf, SUBLANES, PAST_LEN, 0)

    mod = _ada_modulation(jnp.concatenate([c_prompt, c_sample], axis=0), _bf16(w_ada[0]), b_ada[0].reshape(1, -1))
    shift, scale, gate = mod[:, :D_MODEL], mod[:, D_MODEL:2 * D_MODEL], mod[:, 2 * D_MODEL:]

    (k_p, v_p, ki_p, kb, vt, kib, q, qi, kiwi, mpa, sgb, smb, conv_p, lru_p) = _prompt_projection(
        x_prompt, shift[:BATCH, None, :], scale[:BATCH, None, :], wts, tabs_prompt)
    o = _prompt_attention(q, qi, kiwi, kb, vt, kib)
    y_prompt = _output_projection(o, sgb, mpa, smb, x_prompt, gate[:BATCH, None, :], wts, OUT_ROWS)

    xs = x_sample[:, 0, :]
    (k_s, v_s, q_s, qi_s, kiwi_s, mpa_s, sgb_s, smb_s, conv_s, lru_s) = _sample_projection(
        xs, shift[BATCH:], scale[BATCH:], wts, tabs_sample, jnp.swapaxes(state_conv[0], 0, 1), state_rglru[0])
    pt_flat = page_table.reshape(-1)
    w_col = kiwi_s[:, IDX_DIM:IDX_DIM + N_IDX_HEADS, None]
    idx_pages = jnp.swapaxes(cache_idx_k[0], 1, 2)
    kv_pages = lambda t: t[0].reshape(-1, PAGE_SIZE * N_KV_HEADS, HEAD_DIM)
    scores = _sample_scores(pt_flat, idx_pages, qi_s.reshape(DEC_BATCH, N_IDX_HEADS, IDX_DIM), w_col,
                            kiwi_s[:, None, :])
    bias = _sample_select(scores.reshape(DEC_BATCH, (N_PAGES + 1) * PAGE_SIZE))
    bias = bias.reshape(DEC_BATCH, N_PAGES + 1, PAGE_SIZE)
    o_s = _sample_attention(
        pt_flat, kv_pages(cache_k), kv_pages(cache_v), bias, bias[:, N_PAGES:, :],
        q_s.reshape(DEC_BATCH, N_HEADS, HEAD_DIM), k_s.reshape(DEC_BATCH, N_KV_HEADS, HEAD_DIM),
        v_s.reshape(DEC_BATCH, N_KV_HEADS, HEAD_DIM))
    y_sample = _output_projection(
        o_s.reshape(1, DEC_BATCH, D_ATT), sgb_s[None], mpa_s[None], smb_s[None], xs[None], gate[None, BATCH:],
        wts, DEC_BATCH)

    kv_s = lambda t: t.reshape(1, DEC_BATCH, 1, N_KV_HEADS, HEAD_DIM)
    return (
        y_prompt, y_sample.reshape(DEC_BATCH, 1, D_MODEL),
        k_p[None], v_p[None], jnp.swapaxes(ki_p, 1, 2)[None], conv_p[None], lru_p.reshape(1, BATCH, D_RNN),
        kv_s(k_s), kv_s(v_s), kiwi_s[:, :IDX_DIM].reshape(1, DEC_BATCH, 1, IDX_DIM),
        jnp.swapaxes(conv_s, 0, 1)[None], lru_s[None],
    )
```

```python
import functools

import jax
import jax.numpy as jnp
import numpy as np
from jax import lax
from jax.experimental import pallas as pl
from jax.experimental.pallas import tpu as pltpu

D_MODEL = 1024
BATCH = 8
SEQ = 4096
DEC_BATCH = 32
PAST_LEN = 16384
PAGE_SIZE = 128
N_PAGES = PAST_LEN // PAGE_SIZE
D_RNN = D_MODEL
RNN_BLOCKS = 4
RNN_BLOCK_W = D_RNN // RNN_BLOCKS
CONV_W = 4
LRU_C = 8.0
N_HEADS = 8
HEAD_DIM = 128
N_KV_HEADS = 2
HEADS_PER_KV = N_HEADS // N_KV_HEADS
D_ATT = N_HEADS * HEAD_DIM
D_KV = N_KV_HEADS * HEAD_DIM
N_IDX_HEADS = 8
IDX_DIM = 64
D_IDX = N_IDX_HEADS * IDX_DIM
IDX_W_SCALE = (N_IDX_HEADS * IDX_DIM) ** -0.5
TOPK = 256
ROPE_THETA = 10000.0
EPS = 1e-6
SPLITS = (D_RNN, D_RNN, D_ATT, D_KV, D_KV, D_ATT, D_IDX, IDX_DIM, N_IDX_HEADS, D_MODEL, D_MODEL)

LANES = 128
SUBLANES = 8

C_XA, C_GA, C_Q, C_K, C_V, C_GB, C_QI, C_KW = (int(c) for c in np.cumsum((0,) + SPLITS[:7]))
W_PAD_AT = C_KW + IDX_DIM + N_IDX_HEADS
W_PAD = LANES - IDX_DIM - N_IDX_HEADS
C_MA = C_KW + LANES
C_MB = C_MA + D_MODEL
D_IN_PACKED = C_MB + D_MODEL
assert all(c % LANES == 0 for c in (C_XA, C_GA, C_Q, C_K, C_V, C_GB, C_QI, C_KW, C_MA, C_MB))

PROJ_ROWS = 256
Q_BLOCK = 256
KEY_CHUNK = 256
ATT_CHUNK = 512
PACKED_ROWS = 16
assert KEY_CHUNK >= TOPK and ATT_CHUNK % KEY_CHUNK == 0 and SEQ % ATT_CHUNK == 0 and ATT_CHUNK % Q_BLOCK == 0
assert PAGE_SIZE == HEAD_DIM == LANES
OUT_ROWS = 512
SCORE_PAGES = 16
ATTN_PAGES = 32
assert N_PAGES % SCORE_PAGES == 0 and N_PAGES % ATTN_PAGES == 0
SOFTMAX_SCALE_LOG2E = (HEAD_DIM ** -0.5) * float(np.log2(np.e))
NEG_INF = float("-inf")
INT_MIN = -2 ** 31
KEY_NEG_INF = INT_MIN + 0x7FFFFF
VMEM_LIMIT = 56 * 1024 * 1024


def _sigmoid(x):
    return 1.0 / (1.0 + jnp.exp(-x))


def _silu(x):
    return x * _sigmoid(x)


def _dot(a, b):
    return jnp.dot(a, b, preferred_element_type=jnp.float32)


def _dot_nt(a, b):
    return lax.dot_general(a, b, (((1,), (1,)), ((), ())), preferred_element_type=jnp.float32)


def _bf16(x):
    return x.astype(jnp.bfloat16)


def _const_spec(shape, single=True):
    nd = len(shape)
    kwargs = {"pipeline_mode": pl.Buffered(1)} if single else {}
    return pl.BlockSpec(shape, lambda *_: (0,) * nd, **kwargs)


def _rope_kernel(invf_ref, cos_h_ref, sin_h_ref, cos_i_ref, sin_ia_ref, sin_ib_ref, *, pos0, pos_step, rows):
    r0 = pl.program_id(0) * rows
    row = lax.broadcasted_iota(jnp.int32, (rows, LANES), 0) + r0
    lane = lax.broadcasted_iota(jnp.int32, (rows, LANES), 1)
    pos = (pos0 + pos_step * row).astype(jnp.float32)
    ang_h = pos * invf_ref[0:1, :]
    ang_i = pos * invf_ref[1:2, :]
    cos_h_ref[...] = jnp.cos(ang_h)
    sh = jnp.sin(ang_h)
    sin_h_ref[...] = jnp.where(lane < HEAD_DIM // 2, -sh, sh)
    cos_i_ref[...] = jnp.cos(ang_i)
    si = jnp.sin(ang_i)
    first_half = (lane % IDX_DIM) < IDX_DIM // 2
    sin_ia_ref[...] = jnp.where(first_half, -si, 0.0)
    sin_ib_ref[...] = jnp.where(first_half, 0.0, si)


def _rope_tables(invf, n, pos0, pos_step):
    rows = min(n, 512)
    out = jax.ShapeDtypeStruct((n, LANES), jnp.float32)
    spec = pl.BlockSpec((rows, LANES), lambda i: (i, 0))
    return pl.pallas_call(
        functools.partial(_rope_kernel, pos0=pos0, pos_step=pos_step, rows=rows),
        grid=(n // rows,),
        in_specs=[pl.BlockSpec((SUBLANES, LANES), lambda i: (0, 0))],
        out_specs=[spec] * 5,
        out_shape=[out] * 5,
        name="rope_tables",
    )(invf)


def _rot_head(z, cos, sin_signed):
    return z * cos + pltpu.roll(z, HEAD_DIM // 2, 1) * sin_signed


def _rot_idx(z, cos, sin_a, sin_b):
    return z * cos + pltpu.roll(z, LANES - IDX_DIM // 2, 1) * sin_a + pltpu.roll(z, IDX_DIM // 2, 1) * sin_b


def _ada_kernel(c_ref, w_ref, b_ref, o_ref):
    o_ref[...] = _dot(_bf16(_silu(c_ref[...])), w_ref[...]) + b_ref[...]


def _ada_modulation(c_all, w_ada, b_ada):
    n = c_all.shape[0]
    return pl.pallas_call(
        _ada_kernel,
        grid=(1,),
        in_specs=[_const_spec((n, D_MODEL)), _const_spec((D_MODEL, 3 * D_MODEL)), _const_spec((1, 3 * D_MODEL))],
        out_specs=_const_spec((n, 3 * D_MODEL), single=False),
        out_shape=jax.ShapeDtypeStruct((n, 3 * D_MODEL), jnp.float32),
        compiler_params=pltpu.CompilerParams(vmem_limit_bytes=VMEM_LIMIT),
        name="ada_modulation",
    )(c_all, w_ada, b_ada)


def _modulated_norm(x, g, scale, shift):
    y = x * lax.rsqrt(jnp.mean(x * x, axis=-1, keepdims=True) + EPS) * g
    return _bf16(y * (1.0 + scale) + shift)


def _lru_gates(xc, w_ra_ref, b_ra, w_rx_ref, b_rx, lam):
    xcb = _bf16(xc)
    r_parts, i_parts = [], []
    for n in range(RNN_BLOCKS):
        sl = slice(n * RNN_BLOCK_W, (n + 1) * RNN_BLOCK_W)
        r_parts.append(_dot(xcb[:, sl], w_ra_ref[n]))
        i_parts.append(_dot(xcb[:, sl], w_rx_ref[n]))
    r = _sigmoid(jnp.concatenate(r_parts, axis=1) + b_ra)
    i = _sigmoid(jnp.concatenate(i_parts, axis=1) + b_rx)
    neg_lam = -lam
    softplus = jnp.maximum(neg_lam, 0.0) + jnp.log1p(jnp.exp(-jnp.abs(neg_lam)))
    log_a = (-LRU_C) * r * softplus
    a = jnp.exp(log_a)
    u = jnp.sqrt(-jnp.tanh(log_a) * (a * a + 1.0)) * (i * xc)
    return a, u


def _idx_key_slab(z_kw, g, b, cos_i, sin_ia, sin_ib):
    lane = lax.broadcasted_iota(jnp.int32, z_kw.shape, 1)
    is_key = lane < IDX_DIM
    mu = jnp.sum(jnp.where(is_key, z_kw, 0.0), axis=-1, keepdims=True) * (1.0 / IDX_DIM)
    d = jnp.where(is_key, z_kw - mu, 0.0)
    var = jnp.sum(d * d, axis=-1, keepdims=True) * (1.0 / IDX_DIM)
    y = d * lax.rsqrt(var + EPS) * g + b
    key = _rot_idx(y, cos_i, sin_ia, sin_ib)
    is_w = jnp.logical_and(lane >= IDX_DIM, lane < IDX_DIM + N_IDX_HEADS)
    return key + jnp.where(is_w, z_kw * IDX_W_SCALE, 0.0)


def _proj_kernel(x_ref, shift_ref, scale_ref, gn_ref, w_in_ref, wconv_ref, bconv_ref, w_ra_ref, b_ra_ref,
                 w_rx_ref, b_rx_ref, lam_ref, ig_ref, ib_ref, w_pa_ref,
                 cos_h_ref, sin_h_ref, cos_i_ref, sin_ia_ref, sin_ib_ref,
                 k_ref, v_ref, ki_ref, kb_ref, vt_ref, kib_ref, q_ref, qi_ref, kiwi_ref,
                 mpa_ref, sgb_ref, smb_ref, conv_ref, lru_ref,
                 xa_ext, a_s, u_s, h_carry, tail_s, ga_s, ma_s):
    ts = PROJ_ROWS

    @pl.when(pl.program_id(1) == 0)
    def _():
        tail_s[...] = jnp.zeros(tail_s.shape, jnp.float32)
        h_carry[...] = jnp.zeros(h_carry.shape, jnp.float32)

    xn = _modulated_norm(x_ref[...], gn_ref[...], scale_ref[...], shift_ref[...])

    xa_ext[0:SUBLANES, :] = tail_s[...]
    xa_ext[SUBLANES:SUBLANES + ts, :] = _dot(xn, w_in_ref[:, C_XA:C_XA + D_RNN])
    sgb_ref[...] = _bf16(_silu(_dot(xn, w_in_ref[:, C_GB:C_GB + D_ATT])))
    smb_ref[...] = _bf16(_sigmoid(_dot(xn, w_in_ref[:, C_MB:C_MB + D_MODEL])))
    xc = bconv_ref[...]
    for t in range(CONV_W):
        off = SUBLANES - (CONV_W - 1) + t
        xc = xc + xa_ext[off:off + ts, :] * wconv_ref[t:t + 1, :]
    conv_ref[...] = xa_ext[ts + SUBLANES - (CONV_W - 1):ts + SUBLANES, :]
    tail_s[...] = xa_ext[ts:ts + SUBLANES, :]

    a, u = _lru_gates(xc, w_ra_ref, b_ra_ref[...], w_rx_ref, b_rx_ref[...], lam_ref[...])
    a_s[...] = a
    u_s[...] = u
    row = lax.broadcasted_iota(jnp.int32, (SUBLANES, D_RNN), 0)
    cos_h, sin_h = cos_h_ref[...], sin_h_ref[...]
    cos_i, sin_ia, sin_ib = cos_i_ref[...], sin_ia_ref[...], sin_ib_ref[...]

    def proj_gate_a():
        ga_s[...] = _silu(_dot(xn, w_in_ref[:, C_GA:C_GA + D_RNN]))

    def proj_merge_a():
        ma_s[...] = _sigmoid(_dot(xn, w_in_ref[:, C_MA:C_MA + D_MODEL]))

    def proj_q(lo, hi):
        def run():
            zq = _dot(xn, w_in_ref[:, C_Q + lo * HEAD_DIM:C_Q + hi * HEAD_DIM])
            for h in range(hi - lo):
                rot = _rot_head(zq[:, h * HEAD_DIM:(h + 1) * HEAD_DIM], cos_h, sin_h)
                q_ref[:, (lo + h) * HEAD_DIM:(lo + h + 1) * HEAD_DIM] = _bf16(rot * SOFTMAX_SCALE_LOG2E)
        return run

    def proj_kv():
        zk = _dot(xn, w_in_ref[:, C_K:C_K + D_KV])
        for g in range(N_KV_HEADS):
            sl = slice(g * HEAD_DIM, (g + 1) * HEAD_DIM)
            kr = _rot_head(zk[:, sl], cos_h, sin_h)
            k_ref[:, g, :] = kr
            kb_ref[:, sl] = _bf16(kr)
        zv = _dot(xn, w_in_ref[:, C_V:C_V + D_KV])
        for g in range(N_KV_HEADS):
            v_ref[:, g, :] = zv[:, g * HEAD_DIM:(g + 1) * HEAD_DIM]
        vt = _bf16(zv.T)
        for c in range(ts // KEY_CHUNK):
            vt_ref[c] = vt[:, c * KEY_CHUNK:(c + 1) * KEY_CHUNK]

    def proj_idx():
        zqi = _dot(xn, w_in_ref[:, C_QI:C_QI + D_IDX])
        for p in range(D_IDX // LANES):
            sl = slice(p * LANES, (p + 1) * LANES)
            qi_ref[:, sl] = _bf16(_rot_idx(zqi[:, sl], cos_i, sin_ia, sin_ib))
        slab = _idx_key_slab(_dot(xn, w_in_ref[:, C_KW:C_KW + LANES]), ig_ref[...], ib_ref[...],
                             cos_i, sin_ia, sin_ib)
        kiwi_ref[...] = slab
        ki_ref[...] = slab.T[:IDX_DIM, :]
        key_even = jnp.where(lax.broadcasted_iota(jnp.int32, slab.shape, 1) < IDX_DIM, slab, 0.0)
        kib_ref[:, 0:LANES] = _bf16(key_even)
        kib_ref[:, LANES:2 * LANES] = _bf16(pltpu.roll(key_even, IDX_DIM, 1))

    scan_work = [proj_gate_a, proj_merge_a, proj_q(0, N_HEADS // 2), proj_q(N_HEADS // 2, N_HEADS), proj_kv, proj_idx]

    hc = h_carry[...]
    n_groups = ts // SUBLANES
    per_chunk = n_groups // (len(scan_work) + 2)
    for g in range(n_groups):
        rows = slice(g * SUBLANES, (g + 1) * SUBLANES)
        a8 = a_s[rows, :]
        u8 = u_s[rows, :]
        for d in (1, 2, 4):
            keep = row >= d
            u8 = jnp.where(keep, a8 * pltpu.roll(u8, d, 0) + u8, u8)
            a8 = jnp.where(keep, a8 * pltpu.roll(a8, d, 0), a8)
        h8 = a8 * hc + u8
        u_s[rows, :] = h8
        hc = h8[SUBLANES - 1:SUBLANES, :]
        if g % per_chunk == per_chunk - 1 and g // per_chunk < len(scan_work):
            scan_work[g // per_chunk]()
    h_carry[...] = hc
    lru_ref[...] = hc

    ya = _dot(_bf16(u_s[...] * ga_s[...]), w_pa_ref[...])
    mpa_ref[...] = _bf16(ma_s[...] * ya)


def _prompt_projection(x, shift, scale, wts, tabs):
    ts = PROJ_ROWS
    nt = SEQ // ts
    f32, bf16 = jnp.float32, jnp.bfloat16
    row_spec = lambda w: pl.BlockSpec((None, ts, w), lambda b, j: (b, j, 0))
    bvec_spec = pl.BlockSpec((None, 1, D_MODEL), lambda b, j: (b, 0, 0))
    kv_spec = pl.BlockSpec((None, ts, N_KV_HEADS, HEAD_DIM), lambda b, j: (b, j, 0, 0))
    tab_spec = pl.BlockSpec((ts, LANES), lambda b, j: (j, 0))
    in_specs = [
        row_spec(D_MODEL), bvec_spec, bvec_spec, _const_spec((1, D_MODEL)),
        _const_spec((D_MODEL, D_IN_PACKED)), _const_spec((CONV_W, D_RNN)), _const_spec((1, D_RNN)),
        _const_spec((RNN_BLOCKS, RNN_BLOCK_W, RNN_BLOCK_W)), _const_spec((1, D_RNN)),
        _const_spec((RNN_BLOCKS, RNN_BLOCK_W, RNN_BLOCK_W)), _const_spec((1, D_RNN)), _const_spec((1, D_RNN)),
        _const_spec((1, LANES)), _const_spec((1, LANES)), _const_spec((D_RNN, D_MODEL)),
    ] + [tab_spec] * 5
    out_shape = [
        jax.ShapeDtypeStruct((BATCH, SEQ, N_KV_HEADS, HEAD_DIM), f32),
        jax.ShapeDtypeStruct((BATCH, SEQ, N_KV_HEADS, HEAD_DIM), f32),
        jax.ShapeDtypeStruct((BATCH, IDX_DIM, SEQ), f32),
        jax.ShapeDtypeStruct((BATCH, SEQ, D_KV), bf16),
        jax.ShapeDtypeStruct((BATCH, SEQ // KEY_CHUNK, D_KV, KEY_CHUNK), bf16),
        jax.ShapeDtypeStruct((BATCH, SEQ, 2 * LANES), bf16),
        jax.ShapeDtypeStruct((BATCH, SEQ, D_ATT), bf16),
        jax.ShapeDtypeStruct((BATCH, SEQ, D_IDX), bf16),
        jax.ShapeDtypeStruct((BATCH, SEQ, LANES), f32),
        jax.ShapeDtypeStruct((BATCH, SEQ, D_MODEL), bf16),
        jax.ShapeDtypeStruct((BATCH, SEQ, D_ATT), bf16),
        jax.ShapeDtypeStruct((BATCH, SEQ, D_MODEL), bf16),
        jax.ShapeDtypeStruct((BATCH, CONV_W - 1, D_RNN), f32),
        jax.ShapeDtypeStruct((BATCH, 1, D_RNN), f32),
    ]
    out_specs = [
        kv_spec, kv_spec, pl.BlockSpec((None, IDX_DIM, ts), lambda b, j: (b, 0, j)), row_spec(D_KV),
        pl.BlockSpec((None, ts // KEY_CHUNK, D_KV, KEY_CHUNK), lambda b, j: (b, j, 0, 0)),
        row_spec(2 * LANES), row_spec(D_ATT), row_spec(D_IDX), row_spec(LANES),
        row_spec(D_MODEL), row_spec(D_ATT), row_spec(D_MODEL),
        pl.BlockSpec((None, CONV_W - 1, D_RNN), lambda b, j: (b, 0, 0)),
        pl.BlockSpec((None, 1, D_RNN), lambda b, j: (b, 0, 0)),
    ]
    scratch = [
        pltpu.VMEM((ts + SUBLANES, D_RNN), f32), pltpu.VMEM((ts, D_RNN), f32),
        pltpu.VMEM((ts, D_RNN), f32), pltpu.VMEM((1, D_RNN), f32), pltpu.VMEM((SUBLANES, D_RNN), f32),
        pltpu.VMEM((ts, D_RNN), f32), pltpu.VMEM((ts, D_MODEL), f32),
    ]
    return pl.pallas_call(
        _proj_kernel,
        grid=(BATCH, nt),
        in_specs=in_specs, out_specs=out_specs, out_shape=out_shape, scratch_shapes=scratch,
        compiler_params=pltpu.CompilerParams(
            dimension_semantics=("arbitrary", "arbitrary"), vmem_limit_bytes=VMEM_LIMIT),
        name="prompt_projection",
    )(x, shift, scale, wts["g_norm"], wts["w_in"], wts["w_conv"], wts["b_conv"], wts["w_ra"], wts["b_ra"],
      wts["w_rx"], wts["b_rx"], wts["lam"], wts["idx_g"], wts["idx_b"], wts["w_pa"], *tabs)


def _threshold_value(t_unsigned):
    key = jnp.maximum(jnp.bitwise_xor(t_unsigned, jnp.int32(INT_MIN)), jnp.int32(KEY_NEG_INF))
    bits = key ^ jnp.bitwise_and(jnp.right_shift(key, 31), jnp.int32(0x7FFFFFFF))
    return pltpu.bitcast(bits, jnp.float32)


def _greedy_bits(count_ge, n_bits, shape, count_all):
    def bit_step(b, carry):
        t, cnt = carry
        cand = jnp.bitwise_or(t, jnp.left_shift(jnp.int32(1), n_bits - 1 - b))
        c = count_ge(cand)
        ok = c >= TOPK
        return jnp.where(ok, cand, t), jnp.where(ok, c, cnt)
    return lax.fori_loop(0, n_bits, bit_step, (jnp.zeros(shape, jnp.int32), count_all))


def _fold_rows(x, rows, op=jnp.add, chains=4):
    parts = [x[r:r + rows] for r in range(0, x.shape[0], rows)]
    acc = parts[:chains]
    for k, part in enumerate(parts[chains:]):
        acc[k % len(acc)] = op(acc[k % len(acc)], part)
    while len(acc) > 1:
        acc = [op(acc[k], acc[k + 1]) for k in range(0, len(acc) - 1, 2)] + ([acc[-1]] if len(acc) % 2 else [])
    return acc[0]


def _tie_cutoff(count_tie_below, need, n_bits, shape):
    def bit_step(b, x):
        cand = jnp.bitwise_or(x, jnp.left_shift(jnp.int32(1), n_bits - 1 - b))
        return jnp.where(count_tie_below(cand) < need, cand, x)
    return lax.fori_loop(0, n_bits, bit_step, jnp.zeros(shape, jnp.int32))


def _attn_kernel(q_ref, qi_ref, kiwi_ref, kb_ref, vt_ref, kib_ref, o_ref,
                 score_s, hi_s, bias_s, s_scr, acc_s):
    i = pl.program_id(1)
    t0 = i * Q_BLOCK
    n_steps = (t0 + Q_BLOCK + ATT_CHUNK - 1) // ATT_CHUNK
    step_iota = lax.broadcasted_iota(jnp.int32, (ATT_CHUNK, Q_BLOCK), 0)
    sub_iota = lax.broadcasted_iota(jnp.int32, (KEY_CHUNK, Q_BLOCK), 0)
    q_pos = t0 + lax.broadcasted_iota(jnp.int32, (1, Q_BLOCK), 1)
    lane_shape = (1, Q_BLOCK)

    def step_rows(c):
        return pl.ds(pl.multiple_of(c * ATT_CHUNK, ATT_CHUNK), ATT_CHUNK)

    w_t = kiwi_ref[...].T[IDX_DIM:IDX_DIM + N_IDX_HEADS, :]
    qi = qi_ref[...]
    n_pairs = D_IDX // LANES
    qi_rows = jnp.concatenate([qi[:, p * LANES:(p + 1) * LANES] for p in range(n_pairs)], axis=0)

    def score_step(c, causal):
        for sub in range(ATT_CHUNK // KEY_CHUNK):
            r0 = pl.multiple_of(c * ATT_CHUNK + sub * KEY_CHUNK, KEY_CHUNK)
            rows = pl.ds(r0, KEY_CHUNK)
            s_par = [_dot_nt(kib_ref[rows, par * LANES:(par + 1) * LANES], qi_rows) for par in range(2)]
            score = jnp.zeros((KEY_CHUNK, Q_BLOCK), jnp.float32)
            for h in range(N_IDX_HEADS):
                s_h = s_par[h % 2][:, (h // 2) * Q_BLOCK:(h // 2 + 1) * Q_BLOCK]
                score = score + jnp.maximum(s_h, 0.0) * w_t[h:h + 1, :]
            if causal:
                score = jnp.where((r0 + sub_iota) <= q_pos, score, NEG_INF)
            score_s[rows, :] = score
            hi_s[rows, :] = _bf16(score)

    def early_score_step(c, carry):
        score_step(c, False)
        return carry

    lax.fori_loop(0, n_steps - 1, early_score_step, 0)
    score_step(n_steps - 1, True)

    def count_where(pred):
        def body(c, acc):
            sel = pred(score_s[step_rows(c), :], c * ATT_CHUNK + step_iota)
            parts = [_fold_rows(jnp.where(sel[:, t * LANES:(t + 1) * LANES], 1.0, 0.0), SUBLANES, chains=2)
                     for t in range(Q_BLOCK // LANES)]
            return acc + jnp.concatenate(parts, axis=1)
        acc = lax.fori_loop(0, n_steps, body, jnp.zeros((SUBLANES, Q_BLOCK), jnp.float32))
        return jnp.sum(acc, axis=0, keepdims=True)

    def count_rounded_ge(t16):
        cand = _threshold_value(jnp.left_shift(t16, 16)).astype(jnp.bfloat16)
        def body(c, acc):
            one = jnp.where(hi_s[step_rows(c), :] >= cand, jnp.bfloat16(1), jnp.bfloat16(0))
            return acc + _fold_rows(one, PACKED_ROWS, chains=2)
        acc = lax.fori_loop(0, n_steps, body, jnp.zeros((PACKED_ROWS, Q_BLOCK), jnp.bfloat16))
        return jnp.sum(acc.astype(jnp.float32), axis=0, keepdims=True)

    count_all = jnp.full(lane_shape, 1.0, jnp.float32) * (n_steps * ATT_CHUNK).astype(jnp.float32)
    t1, _ = _greedy_bits(count_rounded_ge, 16, lane_shape, count_all)

    base = jnp.left_shift(jnp.maximum(t1 - 1, 0), 16)

    def count_ge(offset):
        cand = _threshold_value(base + offset)
        return count_where(lambda blk, idx: blk >= cand)

    off, cnt_ge = _greedy_bits(count_ge, 17, lane_shape, count_ge(jnp.zeros(lane_shape, jnp.int32)))
    thr = _threshold_value(base + off)

    def write_bias(select):
        def step(c, causal):
            blk = score_s[step_rows(c), :]
            idx = c * ATT_CHUNK + step_iota
            sel = select(blk, idx)
            if causal:
                sel = jnp.logical_and(sel, idx <= q_pos)
            bias_s[step_rows(c), :] = jnp.where(sel, 0.0, NEG_INF)

        def early_step(c, carry):
            step(c, False)
            return carry

        lax.fori_loop(0, n_steps - 1, early_step, 0)
        step(n_steps - 1, True)

    has_tie = jnp.max(jnp.where(cnt_ge > TOPK, 1.0, 0.0)) > 0.5

    @pl.when(jnp.logical_not(has_tie))
    def _():
        write_bias(lambda blk, idx: blk >= thr)

    @pl.when(has_tie)
    def _():
        need = TOPK - count_where(lambda blk, idx: blk > thr)
        cutoff = _tie_cutoff(
            lambda x: count_where(lambda blk, idx: jnp.logical_and(blk == thr, idx < x)),
            need, int(np.log2(SEQ)), lane_shape)
        write_bias(lambda blk, idx: jnp.logical_or(blk > thr, jnp.logical_and(blk == thr, idx <= cutoff)))

    q = q_ref[...]
    n_lanes = HEADS_PER_KV * Q_BLOCK
    q_rows = [
        jnp.concatenate([q[:, (g * HEADS_PER_KV + h) * HEAD_DIM:(g * HEADS_PER_KV + h + 1) * HEAD_DIM]
                         for h in range(HEADS_PER_KV)], axis=0)
        for g in range(N_KV_HEADS)]
    acc_s[...] = jnp.zeros(acc_s.shape, jnp.float32)
    subs = ATT_CHUNK // KEY_CHUNK

    ones_rows = jnp.ones((PACKED_ROWS, KEY_CHUNK), jnp.bfloat16)

    d_sl = [slice(g * HEAD_DIM, (g + 1) * HEAD_DIM) for g in range(N_KV_HEADS)]

    def logits(c, slot):
        maxima = []
        for g in range(N_KV_HEADS):
            mx = None
            for r in range(0, ATT_CHUNK, KEY_CHUNK):
                rows = pl.ds(pl.multiple_of(c * ATT_CHUNK + r, KEY_CHUNK), KEY_CHUNK)
                b = bias_s[rows, :]
                x = _dot_nt(kb_ref[rows, d_sl[g]], q_rows[g]) + jnp.concatenate([b] * HEADS_PER_KV, axis=1)
                s_scr[slot, g, r:r + KEY_CHUNK, :] = x
                f = _fold_rows(x, SUBLANES, jnp.maximum)
                mx = f if mx is None else jnp.maximum(mx, f)
            maxima.append(jnp.max(mx, axis=0, keepdims=True))
        return tuple(maxima)

    def accumulate(c, slot, m_old, m_step):
        m_out = []
        for g in range(N_KV_HEADS):
            m_new = jnp.maximum(m_old[g], m_step[g])
            m_safe = jnp.where(m_new == NEG_INF, 0.0, m_new)
            alpha = jnp.exp2(m_old[g] - m_safe)
            pv = None
            for sub in range(subs):
                r = sub * KEY_CHUNK
                pb = _bf16(jnp.exp2(s_scr[slot, g, r:r + KEY_CHUNK, :] - m_safe))
                lhs = jnp.concatenate([vt_ref[c * subs + sub, d_sl[g], :], ones_rows], axis=0)
                part = _dot(lhs, pb)
                pv = part if pv is None else pv + part
            acc_s[g] = acc_s[g] * alpha + pv
            m_out.append(m_new)
        return tuple(m_out)

    def attend_pair(p, carry):
        m_run, m_even = carry
        c = 2 * p
        m_odd = logits(c + 1, 1)
        m_run = accumulate(c, 0, m_run, m_even)
        m_even = logits(jnp.minimum(c + 2, n_steps - 1), 0)
        m_run = accumulate(c + 1, 1, m_run, m_odd)
        return m_run, m_even

    m_init = tuple(jnp.full((1, n_lanes), NEG_INF, jnp.float32) for _ in range(N_KV_HEADS))
    m_run, m_even = lax.fori_loop(0, n_steps // 2, attend_pair, (m_init, logits(0, 0)))

    @pl.when(n_steps % 2 == 1)
    def _():
        accumulate(n_steps - 1, 0, m_run, m_even)

    for g in range(N_KV_HEADS):
        o_t = acc_s[g, 0:HEAD_DIM, :] / acc_s[g, HEAD_DIM:HEAD_DIM + 1, :]
        for h in range(HEADS_PER_KV):
            col = (g * HEADS_PER_KV + h) * HEAD_DIM
            o_ref[:, col:col + HEAD_DIM] = _bf16(o_t[:, h * Q_BLOCK:(h + 1) * Q_BLOCK].T)


def _prompt_attention(q, qi, kiwi, kb, vt, kib):
    nq = SEQ // Q_BLOCK
    blk = lambda w: pl.BlockSpec((None, Q_BLOCK, w), lambda b, i: (b, i, 0))
    return pl.pallas_call(
        _attn_kernel,
        grid=(BATCH, nq),
        in_specs=[
            blk(D_ATT), blk(D_IDX), blk(LANES),
            pl.BlockSpec((None, SEQ, D_KV), lambda b, i: (b, 0, 0)),
            pl.BlockSpec((None, SEQ // KEY_CHUNK, D_KV, KEY_CHUNK), lambda b, i: (b, 0, 0, 0)),
            pl.BlockSpec((None, SEQ, 2 * LANES), lambda b, i: (b, 0, 0)),
        ],
        out_specs=blk(D_ATT),
        out_shape=jax.ShapeDtypeStruct((BATCH, SEQ, D_ATT), jnp.bfloat16),
        scratch_shapes=[
            pltpu.VMEM((SEQ, Q_BLOCK), jnp.float32),
            pltpu.VMEM((SEQ, Q_BLOCK), jnp.bfloat16),
            pltpu.VMEM((SEQ, Q_BLOCK), jnp.float32),
            pltpu.VMEM((2, N_KV_HEADS, ATT_CHUNK, HEADS_PER_KV * Q_BLOCK), jnp.float32),
            pltpu.VMEM((N_KV_HEADS, HEAD_DIM + PACKED_ROWS, HEADS_PER_KV * Q_BLOCK), jnp.float32),
        ],
        compiler_params=pltpu.CompilerParams(
            dimension_semantics=("arbitrary", "arbitrary"), vmem_limit_bytes=VMEM_LIMIT),
        name="prompt_attention",
    )(q, qi, kiwi, kb, vt, kib)


def _out_kernel(o_ref, sgb_ref, mpa_ref, smb_ref, x_ref, gate_ref, w_pb_ref, w_o_ref, gf_ref, y_ref):
    f32 = jnp.float32
    yb = _dot(_bf16(o_ref[...].astype(f32) * sgb_ref[...].astype(f32)), w_pb_ref[...])
    m = mpa_ref[...].astype(f32) + smb_ref[...].astype(f32) * yb
    r = x_ref[...] + gate_ref[...] * _dot(_bf16(m), w_o_ref[...])
    y_ref[...] = r * lax.rsqrt(jnp.mean(r * r, axis=-1, keepdims=True) + EPS) * gf_ref[...]


def _output_projection(o, sgb, mpa, smb, x, gate, wts, rows):
    ng, nr, _ = x.shape
    row_spec = pl.BlockSpec((None, rows, D_MODEL), lambda b, j: (b, j, 0))
    if gate.shape[1] == 1:
        gate_spec = pl.BlockSpec((None, 1, D_MODEL), lambda b, j: (b, 0, 0))
    else:
        gate_spec = row_spec
    return pl.pallas_call(
        _out_kernel,
        grid=(ng, nr // rows),
        in_specs=[row_spec] * 5 + [gate_spec, _const_spec((D_ATT, D_MODEL)), _const_spec((D_MODEL, D_MODEL)),
                                   _const_spec((1, D_MODEL))],
        out_specs=row_spec,
        out_shape=jax.ShapeDtypeStruct(x.shape, jnp.float32),
        compiler_params=pltpu.CompilerParams(
            dimension_semantics=("arbitrary", "arbitrary"), vmem_limit_bytes=VMEM_LIMIT),
        name="output_projection",
    )(o, sgb, mpa, smb, x, gate, wts["w_pb"], wts["w_o"], wts["g_final"])


def _sample_proj_kernel(x_ref, shift_ref, scale_ref, gn_ref, w_in_ref, wconv_ref, bconv_ref, w_ra_ref, b_ra_ref,
                        w_rx_ref, b_rx_ref, lam_ref, ig_ref, ib_ref, w_pa_ref,
                        cos_h_ref, sin_h_ref, cos_i_ref, sin_ia_ref, sin_ib_ref, buf_ref, h0_ref,
                        k_ref, v_ref, q_ref, qi_ref, kiwi_ref, mpa_ref, sgb_ref, smb_ref, conv_ref, lru_ref):
    xn = _modulated_norm(x_ref[...], gn_ref[...], scale_ref[...], shift_ref[...])
    xa = _dot(xn, w_in_ref[:, C_XA:C_XA + D_RNN])
    xc = bconv_ref[...]
    for t in range(CONV_W - 1):
        xc = xc + buf_ref[t] * wconv_ref[t:t + 1, :]
        if t > 0:
            conv_ref[t - 1] = buf_ref[t]
    xc = xc + xa * wconv_ref[CONV_W - 1:CONV_W, :]
    conv_ref[CONV_W - 2] = xa
    a, u = _lru_gates(xc, w_ra_ref, b_ra_ref[...], w_rx_ref, b_rx_ref[...], lam_ref[...])
    h = a * h0_ref[...] + u
    lru_ref[...] = h
    ga = _dot(xn, w_in_ref[:, C_GA:C_GA + D_RNN])
    ya = _dot(_bf16(h * _silu(ga)), w_pa_ref[...])
    mpa_ref[...] = _sigmoid(_dot(xn, w_in_ref[:, C_MA:C_MA + D_MODEL])) * ya
    sgb_ref[...] = _silu(_dot(xn, w_in_ref[:, C_GB:C_GB + D_ATT]))
    smb_ref[...] = _sigmoid(_dot(xn, w_in_ref[:, C_MB:C_MB + D_MODEL]))

    cos_h, sin_h = cos_h_ref[0:1, :], sin_h_ref[0:1, :]
    cos_i, sin_ia, sin_ib = cos_i_ref[0:1, :], sin_ia_ref[0:1, :], sin_ib_ref[0:1, :]
    zq = _dot(xn, w_in_ref[:, C_Q:C_Q + D_ATT])
    for hd in range(N_HEADS):
        sl = slice(hd * HEAD_DIM, (hd + 1) * HEAD_DIM)
        q_ref[:, sl] = _bf16(_rot_head(zq[:, sl], cos_h, sin_h))
    zk = _dot(xn, w_in_ref[:, C_K:C_K + D_KV])
    for g in range(N_KV_HEADS):
        sl = slice(g * HEAD_DIM, (g + 1) * HEAD_DIM)
        k_ref[:, sl] = _rot_head(zk[:, sl], cos_h, sin_h)
    v_ref[...] = _dot(xn, w_in_ref[:, C_V:C_V + D_KV])
    zqi = _dot(xn, w_in_ref[:, C_QI:C_QI + D_IDX])
    for p in range(D_IDX // LANES):
        sl = slice(p * LANES, (p + 1) * LANES)
        qi_ref[:, sl] = _bf16(_rot_idx(zqi[:, sl], cos_i, sin_ia, sin_ib))
    kiwi_ref[...] = _idx_key_slab(_dot(xn, w_in_ref[:, C_KW:C_KW + LANES]), ig_ref[...], ib_ref[...],
                                  cos_i, sin_ia, sin_ib)


def _sample_projection(x, shift, scale, wts, tabs, buf_t, h0):
    n = DEC_BATCH
    f32, bf16 = jnp.float32, jnp.bfloat16
    in_specs = [
        _const_spec((n, D_MODEL)), _const_spec((n, D_MODEL)), _const_spec((n, D_MODEL)), _const_spec((1, D_MODEL)),
        _const_spec((D_MODEL, D_IN_PACKED)), _const_spec((CONV_W, D_RNN)), _const_spec((1, D_RNN)),
        _const_spec((RNN_BLOCKS, RNN_BLOCK_W, RNN_BLOCK_W)), _const_spec((1, D_RNN)),
        _const_spec((RNN_BLOCKS, RNN_BLOCK_W, RNN_BLOCK_W)), _const_spec((1, D_RNN)), _const_spec((1, D_RNN)),
        _const_spec((1, LANES)), _const_spec((1, LANES)), _const_spec((D_RNN, D_MODEL)),
    ] + [_const_spec((SUBLANES, LANES))] * 5 + [_const_spec((CONV_W - 1, n, D_RNN)), _const_spec((n, D_RNN))]
    shapes = [
        ((n, D_KV), f32), ((n, D_KV), f32), ((n, D_ATT), bf16), ((n, D_IDX), bf16), ((n, LANES), f32),
        ((n, D_MODEL), f32), ((n, D_ATT), f32), ((n, D_MODEL), f32), ((CONV_W - 1, n, D_RNN), f32), ((n, D_RNN), f32),
    ]
    return pl.pallas_call(
        _sample_proj_kernel,
        grid=(1,),
        in_specs=in_specs,
        out_specs=[_const_spec(s, single=False) for s, _ in shapes],
        out_shape=[jax.ShapeDtypeStruct(s, d) for s, d in shapes],
        compiler_params=pltpu.CompilerParams(vmem_limit_bytes=VMEM_LIMIT),
        name="sample_projection",
    )(x, shift, scale, wts["g_norm"], wts["w_in"], wts["w_conv"], wts["b_conv"], wts["w_ra"], wts["b_ra"],
      wts["w_rx"], wts["b_rx"], wts["lam"], wts["idx_g"], wts["idx_b"], wts["w_pa"], *tabs, buf_t, h0)


def _sample_score_kernel(pt_ref, qi_ref, w_ref, kinew_ref, idx_hbm, o_ref, buf, sem):
    b = pl.program_id(0)
    slot = b % 2

    def page_copy(sample, p, sl):
        return pltpu.make_async_copy(idx_hbm.at[pt_ref[sample * N_PAGES + p]], buf.at[sl, p], sem.at[sl])

    def start_sample(sample, sl):
        def body(p, carry):
            page_copy(sample, p, sl).start()
            return carry
        lax.fori_loop(0, N_PAGES, body, 0)

    @pl.when(b == 0)
    def _():
        start_sample(0, 0)

    @pl.when(b + 1 < pl.num_programs(0))
    def _():
        start_sample(b + 1, 1 - slot)

    def wait_page(p, carry):
        page_copy(b, p, slot).wait()
        return carry

    lax.fori_loop(0, N_PAGES, wait_page, 0)

    qi = qi_ref[...]
    w = w_ref[...]

    def score_pages(i, carry):
        p0 = i * SCORE_PAGES
        kt = _bf16(jnp.concatenate([buf[slot, p0 + t] for t in range(SCORE_PAGES)], axis=1))
        s = _dot(qi, kt)
        score = jnp.sum(jnp.maximum(s, 0.0) * w, axis=0, keepdims=True)
        for t in range(SCORE_PAGES):
            o_ref[pl.ds(p0 + t, 1), :] = score[:, t * PAGE_SIZE:(t + 1) * PAGE_SIZE]
        return carry

    for i in range(N_PAGES // SCORE_PAGES):
        score_pages(i, 0)

    k_self = _bf16(kinew_ref[...][:, :IDX_DIM]).astype(jnp.float32)
    s_self = jnp.sum(qi.astype(jnp.float32) * k_self, axis=1, keepdims=True)
    score_self = jnp.sum(jnp.maximum(s_self, 0.0) * w, axis=0, keepdims=True)
    lane = lax.broadcasted_iota(jnp.int32, (1, PAGE_SIZE), 1)
    o_ref[N_PAGES:N_PAGES + 1, :] = jnp.where(lane == 0, score_self, NEG_INF)


def _sample_scores(page_table_flat, idx_pages, qi3, w_col, kiwi3):
    per_sample = lambda r, w: pl.BlockSpec((None, r, w), lambda b, pt: (b, 0, 0))
    return pl.pallas_call(
        _sample_score_kernel,
        grid_spec=pltpu.PrefetchScalarGridSpec(
            num_scalar_prefetch=1,
            grid=(DEC_BATCH,),
            in_specs=[per_sample(N_IDX_HEADS, IDX_DIM), per_sample(N_IDX_HEADS, 1), per_sample(1, LANES),
                      pl.BlockSpec(memory_space=pl.ANY)],
            out_specs=per_sample(N_PAGES + 1, PAGE_SIZE),
            scratch_shapes=[pltpu.VMEM((2, N_PAGES, IDX_DIM, PAGE_SIZE), jnp.float32),
                            pltpu.SemaphoreType.DMA((2,))],
        ),
        out_shape=jax.ShapeDtypeStruct((DEC_BATCH, N_PAGES + 1, PAGE_SIZE), jnp.float32),
        compiler_params=pltpu.CompilerParams(dimension_semantics=("arbitrary",), vmem_limit_bytes=VMEM_LIMIT),
        name="sample_scores",
    )(page_table_flat, qi3, w_col, kiwi3, idx_pages)


def _sample_select_kernel(score_ref, bias_ref):
    keys = score_ref[...]
    idx = lax.broadcasted_iota(jnp.int32, keys.shape, 1)
    col_shape = (keys.shape[0], 1)
    count = lambda pred: jnp.sum(jnp.where(pred, 1.0, 0.0), axis=1, keepdims=True)
    total = jnp.full(col_shape, float(keys.shape[1]), jnp.float32)
    thr_u, _ = _greedy_bits(lambda t: count(keys >= _threshold_value(t)), 32, col_shape, total)
    thr = _threshold_value(thr_u)
    need = TOPK - count(keys > thr)
    tie = keys == thr
    n_bits = int(np.ceil(np.log2(keys.shape[1])))
    cutoff = _tie_cutoff(lambda x: count(jnp.logical_and(tie, idx < x)), need, n_bits, col_shape)
    sel = jnp.logical_or(keys > thr, jnp.logical_and(tie, idx <= cutoff))
    bias_ref[...] = jnp.where(sel, 0.0, NEG_INF)


def _sample_select(scores):
    return pl.pallas_call(
        _sample_select_kernel,
        grid=(1,),
        in_specs=[_const_spec(scores.shape)],
        out_specs=_const_spec(scores.shape, single=False),
        out_shape=jax.ShapeDtypeStruct(scores.shape, jnp.float32),
        compiler_params=pltpu.CompilerParams(vmem_limit_bytes=VMEM_LIMIT),
        name="sample_select",
    )(scores)


def _sample_attn_kernel(pt_ref, bias_ref, bias_self_ref, q_ref, knew_ref, vnew_ref, k_hbm, v_hbm, o_ref,
                        kbuf, vbuf, sem, m_s, l_s, acc_s):
    b, j = pl.program_id(0), pl.program_id(1)
    n_j = pl.num_programs(1)
    step = b * n_j + j
    slot = step % 2

    def page_copies(st, sl):
        out = []
        for t in range(ATTN_PAGES):
            page = pt_ref[st * ATTN_PAGES + t]
            out.append(pltpu.make_async_copy(k_hbm.at[page], kbuf.at[sl, t], sem.at[0, sl]))
            out.append(pltpu.make_async_copy(v_hbm.at[page], vbuf.at[sl, t], sem.at[1, sl]))
        return out

    @pl.when(step == 0)
    def _():
        for c in page_copies(0, 0):
            c.start()

    @pl.when(step + 1 < pl.num_programs(0) * n_j)
    def _():
        for c in page_copies(step + 1, 1 - slot):
            c.start()

    for c in page_copies(step, slot):
        c.wait()

    @pl.when(j == 0)
    def _():
        m_s[...] = jnp.full(m_s.shape, NEG_INF, jnp.float32)
        l_s[...] = jnp.zeros(l_s.shape, jnp.float32)
        acc_s[...] = jnp.zeros(acc_s.shape, jnp.float32)

    def online_update(s, pv_of):
        m_old = m_s[...]
        m_new = jnp.maximum(m_old, jnp.max(s, axis=1, keepdims=True))
        m_safe = jnp.where(m_new == NEG_INF, 0.0, m_new)
        pr = jnp.exp2((s - m_safe) * SOFTMAX_SCALE_LOG2E)
        alpha = jnp.exp2((m_old - m_safe) * SOFTMAX_SCALE_LOG2E)
        l_s[...] = alpha * l_s[...] + jnp.sum(pr, axis=1, keepdims=True)
        acc_s[...] = acc_s[...] * alpha + pv_of(_bf16(pr))
        m_s[...] = m_new

    q = q_ref[...]
    rows_per_page = PAGE_SIZE * N_KV_HEADS
    n_cols = ATTN_PAGES * rows_per_page
    k_all = _bf16(kbuf[slot].reshape(n_cols, HEAD_DIM))
    v_all = _bf16(vbuf[slot].reshape(n_cols, HEAD_DIM))
    dup = jnp.where(lax.broadcasted_iota(jnp.int32, (PAGE_SIZE, rows_per_page), 1) // N_KV_HEADS
                    == lax.broadcasted_iota(jnp.int32, (PAGE_SIZE, rows_per_page), 0), 1.0, 0.0)
    sel_pages = _dot(_bf16(jnp.where(bias_ref[...] == 0.0, 1.0, 0.0)), _bf16(dup))
    sel_row = jnp.concatenate([sel_pages[t:t + 1, :] for t in range(ATTN_PAGES)], axis=1)
    head = lax.broadcasted_iota(jnp.int32, (N_HEADS, n_cols), 0)
    col = lax.broadcasted_iota(jnp.int32, (N_HEADS, n_cols), 1)
    own = (col % N_KV_HEADS) == (head // HEADS_PER_KV)
    s = jnp.where(jnp.logical_and(own, sel_row > 0.5), _dot_nt(q, k_all), NEG_INF)
    online_update(s, lambda pb: _dot(pb, v_all))

    @pl.when(j == n_j - 1)
    def _():
        head_d = lax.broadcasted_iota(jnp.int32, (N_HEADS, HEAD_DIM), 0) // HEADS_PER_KV

        def own_row(ref):
            rows = _bf16(ref[...]).astype(jnp.float32)
            out = jnp.broadcast_to(rows[N_KV_HEADS - 1:N_KV_HEADS, :], (N_HEADS, HEAD_DIM))
            for g in range(N_KV_HEADS - 2, -1, -1):
                out = jnp.where(head_d == g, rows[g:g + 1, :], out)
            return out

        s_self = jnp.sum(q.astype(jnp.float32) * own_row(knew_ref), axis=1, keepdims=True)
        v_own = own_row(vnew_ref)
        online_update(s_self + bias_self_ref[...][:, 0:1], lambda pb: pb.astype(jnp.float32) * v_own)
        o_ref[...] = acc_s[...] / l_s[...]


def _sample_attention(page_table_flat, k_pages, v_pages, bias3, bias_self, q3, knew3, vnew3):
    per_sample = lambda r, w: pl.BlockSpec((None, r, w), lambda b, j, pt: (b, 0, 0))
    rows_per_page = PAGE_SIZE * N_KV_HEADS
    return pl.pallas_call(
        _sample_attn_kernel,
        grid_spec=pltpu.PrefetchScalarGridSpec(
            num_scalar_prefetch=1,
            grid=(DEC_BATCH, N_PAGES // ATTN_PAGES),
            in_specs=[
                pl.BlockSpec((None, ATTN_PAGES, PAGE_SIZE), lambda b, j, pt: (b, j, 0)),
                per_sample(1, PAGE_SIZE), per_sample(N_HEADS, HEAD_DIM),
                per_sample(N_KV_HEADS, HEAD_DIM), per_sample(N_KV_HEADS, HEAD_DIM),
                pl.BlockSpec(memory_space=pl.ANY), pl.BlockSpec(memory_space=pl.ANY),
            ],
            out_specs=per_sample(N_HEADS, HEAD_DIM),
            scratch_shapes=[
                pltpu.VMEM((2, ATTN_PAGES, rows_per_page, HEAD_DIM), jnp.float32),
                pltpu.VMEM((2, ATTN_PAGES, rows_per_page, HEAD_DIM), jnp.float32),
                pltpu.SemaphoreType.DMA((2, 2)),
                pltpu.VMEM((N_HEADS, 1), jnp.float32), pltpu.VMEM((N_HEADS, 1), jnp.float32),
                pltpu.VMEM((N_HEADS, HEAD_DIM), jnp.float32),
            ],
        ),
        out_shape=jax.ShapeDtypeStruct((DEC_BATCH, N_HEADS, HEAD_DIM), jnp.float32),
        compiler_params=pltpu.CompilerParams(
            dimension_semantics=("arbitrary", "arbitrary"), vmem_limit_bytes=VMEM_LIMIT),
        name="sample_attention",
    )(page_table_flat, bias3, bias_self, q3, knew3, vnew3, k_pages, v_pages)


def _pack_w_in(w_in):
    pad = jnp.zeros((D_MODEL, W_PAD), jnp.bfloat16)
    return jnp.concatenate([_bf16(w_in[:, :W_PAD_AT]), pad, _bf16(w_in[:, W_PAD_AT:])], axis=-1)


def _lane_pad(v):
    return jnp.pad(v.reshape(1, -1), ((0, 0), (0, LANES - v.shape[-1])))


def kernel(x_prompt, x_sample, cache_k, cache_v, cache_idx_k, state_conv, state_rglru, page_table, c_prompt, c_sample, w_ada, b_ada, g_norm, w_in, w_conv, b_conv, w_ra, b_ra, w_rx, b_rx, lru_lambda, idx_k_norm_g, idx_k_norm_b, w_pa, w_pb, w_o, g_final):
    assert w_in.shape[0] == 1, "one layer"
    wts = {
        "g_norm": g_norm[0].reshape(1, -1), "w_in": _pack_w_in(w_in[0]), "w_conv": w_conv[0],
        "b_conv": b_conv[0].reshape(1, -1), "w_ra": _bf16(w_ra[0]), "b_ra": b_ra[0].reshape(1, -1),
        "w_rx": _bf16(w_rx[0]), "b_rx": b_rx[0].reshape(1, -1), "lam": lru_lambda[0].reshape(1, -1),
        "idx_g": _lane_pad(idx_k_norm_g[0]), "idx_b": _lane_pad(idx_k_norm_b[0]),
        "w_pa": _bf16(w_pa[0]), "w_pb": _bf16(w_pb[0]), "w_o": _bf16(w_o[0]), "g_final": g_final.reshape(1, -1),
    }
    half_h, half_i = HEAD_DIM // 2, IDX_DIM // 2
    invf_h = ROPE_THETA ** (-jnp.arange(half_h, dtype=jnp.float32) / half_h)
    invf_i = ROPE_THETA ** (-jnp.arange(half_i, dtype=jnp.float32) / half_i)
    invf = jnp.zeros((SUBLANES, LANES), jnp.float32)
    invf = invf.at[0].set(jnp.tile(invf_h, LANES // half_h)).at[1].set(jnp.tile(invf_i, LANES // half_i))
    tabs_prompt = _rope_tables(invf, SEQ, 0, 1)
    tabs_sample = _rope_tables(invf, SUBLANES, PAST_LEN, 0)

    mod = _ada_modulation(jnp.concatenate([c_prompt, c_sample], axis=0), _bf16(w_ada[0]), b_ada[0].reshape(1, -1))
    shift, scale, gate = mod[:, :D_MODEL], mod[:, D_MODEL:2 * D_MODEL], mod[:, 2 * D_MODEL:]

    (k_p, v_p, ki_p, kb, vt, kib, q, qi, kiwi, mpa, sgb, smb, conv_p, lru_p) = _prompt_projection(
        x_prompt, shift[:BATCH, None, :], scale[:BATCH, None, :], wts, tabs_prompt)
    o = _prompt_attention(q, qi, kiwi, kb, vt, kib)
    y_prompt = _output_projection(o, sgb, mpa, smb, x_prompt, gate[:BATCH, None, :], wts, OUT_ROWS)

    xs = x_sample[:, 0, :]
    (k_s, v_s, q_s, qi_s, kiwi_s, mpa_s, sgb_s, smb_s, conv_s, lru_s) = _sample_projection(
        xs, shift[BATCH:], scale[BATCH:], wts, tabs_sample, jnp.swapaxes(state_conv[0], 0, 1), state_rglru[0])
    pt_flat = page_table.reshape(-1)
    w_col = kiwi_s[:, IDX_DIM:IDX_DIM + N_IDX_HEADS, None]
    idx_pages = jnp.swapaxes(cache_idx_k[0], 1, 2)
    kv_pages = lambda t: t[0].reshape(-1, PAGE_SIZE * N_KV_HEADS, HEAD_DIM)
    scores = _sample_scores(pt_flat, idx_pages, qi_s.reshape(DEC_BATCH, N_IDX_HEADS, IDX_DIM), w_col,
                            kiwi_s[:, None, :])
    bias = _sample_select(scores.reshape(DEC_BATCH, (N_PAGES + 1) * PAGE_SIZE))
    bias = bias.reshape(DEC_BATCH, N_PAGES + 1, PAGE_SIZE)
    o_s = _sample_attention(
        pt_flat, kv_pages(cache_k), kv_pages(cache_v), bias, bias[:, N_PAGES:, :],
        q_s.reshape(DEC_BATCH, N_HEADS, HEAD_DIM), k_s.reshape(DEC_BATCH, N_KV_HEADS, HEAD_DIM),
        v_s.reshape(DEC_BATCH, N_KV_HEADS, HEAD_DIM))
    y_sample = _output_projection(
        o_s.reshape(1, DEC_BATCH, D_ATT), sgb_s[None], mpa_s[None], smb_s[None], xs[None], gate[None, BATCH:],
        wts, DEC_BATCH)

    kv_s = lambda t: t.reshape(1, DEC_BATCH, 1, N_KV_HEADS, HEAD_DIM)
    return (
        y_prompt, y_sample.reshape(DEC_BATCH, 1, D_MODEL),
        k_p[None], v_p[None], jnp.swapaxes(ki_p, 1, 2)[None], conv_p[None], lru_p.reshape(1, BATCH, D_RNN),
        kv_s(k_s), kv_s(v_s), kiwi_s[:, :IDX_DIM].reshape(1, DEC_BATCH, 1, IDX_DIM),
        jnp.swapaxes(conv_s, 0, 1)[None], lru_s[None],
    )
```

```python
import functools

import jax
import jax.numpy as jnp
import numpy as np
from jax import lax
from jax.experimental import pallas as pl
from jax.experimental.pallas import tpu as pltpu

D_MODEL = 1024
BATCH = 8
SEQ = 4096
DEC_BATCH = 32
PAST_LEN = 16384
PAGE_SIZE = 128
N_PAGES = PAST_LEN // PAGE_SIZE
D_RNN = D_MODEL
RNN_BLOCKS = 4
RNN_BLOCK_W = D_RNN // RNN_BLOCKS
CONV_W = 4
LRU_C = 8.0
N_HEADS = 8
HEAD_DIM = 128
N_KV_HEADS = 2
HEADS_PER_KV = N_HEADS // N_KV_HEADS
D_ATT = N_HEADS * HEAD_DIM
D_KV = N_KV_HEADS * HEAD_DIM
N_IDX_HEADS = 8
IDX_DIM = 64
D_IDX = N_IDX_HEADS * IDX_DIM
IDX_W_SCALE = (N_IDX_HEADS * IDX_DIM) ** -0.5
TOPK = 256
ROPE_THETA = 10000.0
EPS = 1e-6
SPLITS = (D_RNN, D_RNN, D_ATT, D_KV, D_KV, D_ATT, D_IDX, IDX_DIM, N_IDX_HEADS, D_MODEL, D_MODEL)

LANES = 128
SUBLANES = 8

C_XA, C_GA, C_Q, C_K, C_V, C_GB, C_QI, C_KW = (int(c) for c in np.cumsum((0,) + SPLITS[:7]))
W_PAD_AT = C_KW + IDX_DIM + N_IDX_HEADS
W_PAD = LANES - IDX_DIM - N_IDX_HEADS
C_MA = C_KW + LANES
C_MB = C_MA + D_MODEL
D_IN_PACKED = C_MB + D_MODEL
assert all(c % LANES == 0 for c in (C_XA, C_GA, C_Q, C_K, C_V, C_GB, C_QI, C_KW, C_MA, C_MB))

PROJ_ROWS = 256
Q_BLOCK = 256
KEY_CHUNK = 256
ATT_CHUNK = 512
PACKED_ROWS = 16
assert KEY_CHUNK >= TOPK and ATT_CHUNK % KEY_CHUNK == 0 and SEQ % ATT_CHUNK == 0 and ATT_CHUNK % Q_BLOCK == 0
assert PAGE_SIZE == HEAD_DIM == LANES
OUT_ROWS = 1024
SCORE_PAGES = 16
ATTN_PAGES = 32
assert N_PAGES % SCORE_PAGES == 0 and N_PAGES % ATTN_PAGES == 0
SOFTMAX_SCALE_LOG2E = (HEAD_DIM ** -0.5) * float(np.log2(np.e))
NEG_INF = float("-inf")
INT_MIN = -2 ** 31
KEY_NEG_INF = INT_MIN + 0x7FFFFF
V7X_VMEM_BYTES = 64 * 1024 * 1024
VMEM_LIMIT = V7X_VMEM_BYTES * 7 // 8


def _sigmoid(x):
    return 1.0 / (1.0 + jnp.exp(-x))


def _silu(x):
    return x * _sigmoid(x)


def _dot(a, b):
    return jnp.dot(a, b, preferred_element_type=jnp.float32)


def _dot_nt(a, b):
    return lax.dot_general(a, b, (((1,), (1,)), ((), ())), preferred_element_type=jnp.float32)


def _bf16(x):
    return x.astype(jnp.bfloat16)


def _const_spec(shape, single=True):
    nd = len(shape)
    kwargs = {"pipeline_mode": pl.Buffered(1)} if single else {}
    return pl.BlockSpec(shape, lambda *_: (0,) * nd, **kwargs)


def _rope_kernel(invf_ref, cos_h_ref, sin_h_ref, cos_i_ref, sin_ia_ref, sin_ib_ref, *, pos0, pos_step, rows):
    r0 = pl.program_id(0) * rows
    row = lax.broadcasted_iota(jnp.int32, (rows, LANES), 0) + r0
    lane = lax.broadcasted_iota(jnp.int32, (rows, LANES), 1)
    pos = (pos0 + pos_step * row).astype(jnp.float32)
    ang_h = pos * invf_ref[0:1, :]
    ang_i = pos * invf_ref[1:2, :]
    cos_h_ref[...] = jnp.cos(ang_h)
    sh = jnp.sin(ang_h)
    sin_h_ref[...] = jnp.where(lane < HEAD_DIM // 2, -sh, sh)
    cos_i_ref[...] = jnp.cos(ang_i)
    si = jnp.sin(ang_i)
    first_half = (lane % IDX_DIM) < IDX_DIM // 2
    sin_ia_ref[...] = jnp.where(first_half, -si, 0.0)
    sin_ib_ref[...] = jnp.where(first_half, 0.0, si)


def _rope_tables(invf, n, pos0, pos_step):
    rows = min(n, 512)
    out = jax.ShapeDtypeStruct((n, LANES), jnp.float32)
    spec = pl.BlockSpec((rows, LANES), lambda i: (i, 0))
    return pl.pallas_call(
        functools.partial(_rope_kernel, pos0=pos0, pos_step=pos_step, rows=rows),
        grid=(n // rows,),
        in_specs=[pl.BlockSpec((SUBLANES, LANES), lambda i: (0, 0))],
        out_specs=[spec] * 5,
        out_shape=[out] * 5,
        name="rope_tables",
    )(invf)


def _rot_head(z, cos, sin_signed):
    return z * cos + pltpu.roll(z, HEAD_DIM // 2, 1) * sin_signed


def _rot_idx(z, cos, sin_a, sin_b):
    return z * cos + pltpu.roll(z, LANES - IDX_DIM // 2, 1) * sin_a + pltpu.roll(z, IDX_DIM // 2, 1) * sin_b


def _ada_kernel(c_ref, w_ref, b_ref, o_ref):
    o_ref[...] = _dot(_bf16(_silu(c_ref[...])), w_ref[...]) + b_ref[...]


def _ada_modulation(c_all, w_ada, b_ada):
    n = c_all.shape[0]
    return pl.pallas_call(
        _ada_kernel,
        grid=(1,),
        in_specs=[_const_spec((n, D_MODEL)), _const_spec((D_MODEL, 3 * D_MODEL)), _const_spec((1, 3 * D_MODEL))],
        out_specs=_const_spec((n, 3 * D_MODEL), single=False),
        out_shape=jax.ShapeDtypeStruct((n, 3 * D_MODEL), jnp.float32),
        compiler_params=pltpu.CompilerParams(vmem_limit_bytes=VMEM_LIMIT),
        name="ada_modulation",
    )(c_all, w_ada, b_ada)


def _modulated_norm(x, g, scale, shift):
    y = x * lax.rsqrt(jnp.mean(x * x, axis=-1, keepdims=True) + EPS) * g
    return _bf16(y * (1.0 + scale) + shift)


def _lru_gates(xc, w_ra_ref, b_ra, w_rx_ref, b_rx, lam):
    xcb = _bf16(xc)
    r_parts, i_parts = [], []
    for n in range(RNN_BLOCKS):
        sl = slice(n * RNN_BLOCK_W, (n + 1) * RNN_BLOCK_W)
        r_parts.append(_dot(xcb[:, sl], w_ra_ref[n]))
        i_parts.append(_dot(xcb[:, sl], w_rx_ref[n]))
    r = _sigmoid(jnp.concatenate(r_parts, axis=1) + b_ra)
    i = _sigmoid(jnp.concatenate(i_parts, axis=1) + b_rx)
    neg_lam = -lam
    softplus = jnp.maximum(neg_lam, 0.0) + jnp.log1p(jnp.exp(-jnp.abs(neg_lam)))
    log_a = (-LRU_C) * r * softplus
    a = jnp.exp(log_a)
    u = jnp.sqrt(-jnp.tanh(log_a) * (a * a + 1.0)) * (i * xc)
    return a, u


def _idx_key_slab(z_kw, g, b, cos_i, sin_ia, sin_ib):
    lane = lax.broadcasted_iota(jnp.int32, z_kw.shape, 1)
    is_key = lane < IDX_DIM
    mu = jnp.sum(jnp.where(is_key, z_kw, 0.0), axis=-1, keepdims=True) * (1.0 / IDX_DIM)
    d = jnp.where(is_key, z_kw - mu, 0.0)
    var = jnp.sum(d * d, axis=-1, keepdims=True) * (1.0 / IDX_DIM)
    y = d * lax.rsqrt(var + EPS) * g + b
    key = _rot_idx(y, cos_i, sin_ia, sin_ib)
    is_w = jnp.logical_and(lane >= IDX_DIM, lane < IDX_DIM + N_IDX_HEADS)
    return key + jnp.where(is_w, z_kw * IDX_W_SCALE, 0.0)


def _proj_kernel(x_ref, shift_ref, scale_ref, gn_ref, w_in_ref, wconv_ref, bconv_ref, w_ra_ref, b_ra_ref,
                 w_rx_ref, b_rx_ref, lam_ref, ig_ref, ib_ref, w_pa_ref,
                 cos_h_ref, sin_h_ref, cos_i_ref, sin_ia_ref, sin_ib_ref,
                 k_ref, v_ref, ki_ref, kb_ref, vt_ref, kib_ref, q_ref, qi_ref, kiwi_ref,
                 mpa_ref, sgb_ref, smb_ref, conv_ref, lru_ref,
                 xa_ext, a_s, u_s, h_carry, tail_s, ga_s, ma_s):
    ts = PROJ_ROWS

    @pl.when(pl.program_id(1) == 0)
    def _():
        tail_s[...] = jnp.zeros(tail_s.shape, jnp.float32)
        h_carry[...] = jnp.zeros(h_carry.shape, jnp.float32)

    xn = _modulated_norm(x_ref[...], gn_ref[...], scale_ref[...], shift_ref[...])

    xa_ext[0:SUBLANES, :] = tail_s[...]
    xa_ext[SUBLANES:SUBLANES + ts, :] = _dot(xn, w_in_ref[:, C_XA:C_XA + D_RNN])
    sgb_ref[...] = _bf16(_silu(_dot(xn, w_in_ref[:, C_GB:C_GB + D_ATT])))
    smb_ref[...] = _bf16(_sigmoid(_dot(xn, w_in_ref[:, C_MB:C_MB + D_MODEL])))
    xc = bconv_ref[...]
    for t in range(CONV_W):
        off = SUBLANES - (CONV_W - 1) + t
        xc = xc + xa_ext[off:off + ts, :] * wconv_ref[t:t + 1, :]
    conv_ref[...] = xa_ext[ts + SUBLANES - (CONV_W - 1):ts + SUBLANES, :]
    tail_s[...] = xa_ext[ts:ts + SUBLANES, :]

    a, u = _lru_gates(xc, w_ra_ref, b_ra_ref[...], w_rx_ref, b_rx_ref[...], lam_ref[...])
    a_s[...] = a
    u_s[...] = u
    row = lax.broadcasted_iota(jnp.int32, (SUBLANES, D_RNN), 0)
    cos_h, sin_h = cos_h_ref[...], sin_h_ref[...]
    cos_i, sin_ia, sin_ib = cos_i_ref[...], sin_ia_ref[...], sin_ib_ref[...]

    def proj_gate_a():
        ga_s[...] = _silu(_dot(xn, w_in_ref[:, C_GA:C_GA + D_RNN]))

    def proj_merge_a():
        ma_s[...] = _sigmoid(_dot(xn, w_in_ref[:, C_MA:C_MA + D_MODEL]))

    def proj_q(lo, hi):
        def run():
            zq = _dot(xn, w_in_ref[:, C_Q + lo * HEAD_DIM:C_Q + hi * HEAD_DIM])
            for h in range(hi - lo):
                rot = _rot_head(zq[:, h * HEAD_DIM:(h + 1) * HEAD_DIM], cos_h, sin_h)
                q_ref[:, (lo + h) * HEAD_DIM:(lo + h + 1) * HEAD_DIM] = _bf16(rot * SOFTMAX_SCALE_LOG2E)
        return run

    def proj_kv():
        zk = _dot(xn, w_in_ref[:, C_K:C_K + D_KV])
        for g in range(N_KV_HEADS):
            sl = slice(g * HEAD_DIM, (g + 1) * HEAD_DIM)
            kr = _rot_head(zk[:, sl], cos_h, sin_h)
            k_ref[:, g, :] = kr
            kb_ref[:, sl] = _bf16(kr)
        zv = _dot(xn, w_in_ref[:, C_V:C_V + D_KV])
        for g in range(N_KV_HEADS):
            v_ref[:, g, :] = zv[:, g * HEAD_DIM:(g + 1) * HEAD_DIM]
        vt = _bf16(zv.T)
        for c in range(ts // KEY_CHUNK):
            vt_ref[c] = vt[:, c * KEY_CHUNK:(c + 1) * KEY_CHUNK]

    def proj_idx():
        zqi = _dot(xn, w_in_ref[:, C_QI:C_QI + D_IDX])
        for p in range(D_IDX // LANES):
            sl = slice(p * LANES, (p + 1) * LANES)
            qi_ref[:, sl] = _bf16(_rot_idx(zqi[:, sl], cos_i, sin_ia, sin_ib))
        slab = _idx_key_slab(_dot(xn, w_in_ref[:, C_KW:C_KW + LANES]), ig_ref[...], ib_ref[...],
                             cos_i, sin_ia, sin_ib)
        kiwi_ref[...] = slab
        ki_ref[...] = slab.T[:IDX_DIM, :]
        key_even = jnp.where(lax.broadcasted_iota(jnp.int32, slab.shape, 1) < IDX_DIM, slab, 0.0)
        kib_ref[:, 0:LANES] = _bf16(key_even)
        kib_ref[:, LANES:2 * LANES] = _bf16(pltpu.roll(key_even, IDX_DIM, 1))

    scan_work = [proj_gate_a, proj_merge_a, proj_q(0, N_HEADS // 2), proj_q(N_HEADS // 2, N_HEADS), proj_kv, proj_idx]

    hc = h_carry[...]
    n_groups = ts // SUBLANES
    per_chunk = n_groups // (len(scan_work) + 2)
    for g in range(n_groups):
        rows = slice(g * SUBLANES, (g + 1) * SUBLANES)
        a8 = a_s[rows, :]
        u8 = u_s[rows, :]
        for d in (1, 2, 4):
            keep = row >= d
            u8 = jnp.where(keep, a8 * pltpu.roll(u8, d, 0) + u8, u8)
            a8 = jnp.where(keep, a8 * pltpu.roll(a8, d, 0), a8)
        h8 = a8 * hc + u8
        u_s[rows, :] = h8
        hc = h8[SUBLANES - 1:SUBLANES, :]
        if g % per_chunk == per_chunk - 1 and g // per_chunk < len(scan_work):
            scan_work[g // per_chunk]()
    h_carry[...] = hc
    lru_ref[...] = hc

    ya = _dot(_bf16(u_s[...] * ga_s[...]), w_pa_ref[...])
    mpa_ref[...] = _bf16(ma_s[...] * ya)


def _prompt_projection(x, shift, scale, wts, tabs):
    ts = PROJ_ROWS
    nt = SEQ // ts
    f32, bf16 = jnp.float32, jnp.bfloat16
    row_spec = lambda w: pl.BlockSpec((None, ts, w), lambda b, j: (b, j, 0))
    bvec_spec = pl.BlockSpec((None, 1, D_MODEL), lambda b, j: (b, 0, 0))
    kv_spec = pl.BlockSpec((None, ts, N_KV_HEADS, HEAD_DIM), lambda b, j: (b, j, 0, 0))
    tab_spec = pl.BlockSpec((ts, LANES), lambda b, j: (j, 0))
    in_specs = [
        row_spec(D_MODEL), bvec_spec, bvec_spec, _const_spec((1, D_MODEL)),
        _const_spec((D_MODEL, D_IN_PACKED)), _const_spec((CONV_W, D_RNN)), _const_spec((1, D_RNN)),
        _const_spec((RNN_BLOCKS, RNN_BLOCK_W, RNN_BLOCK_W)), _const_spec((1, D_RNN)),
        _const_spec((RNN_BLOCKS, RNN_BLOCK_W, RNN_BLOCK_W)), _const_spec((1, D_RNN)), _const_spec((1, D_RNN)),
        _const_spec((1, LANES)), _const_spec((1, LANES)), _const_spec((D_RNN, D_MODEL)),
    ] + [tab_spec] * 5
    out_shape = [
        jax.ShapeDtypeStruct((BATCH, SEQ, N_KV_HEADS, HEAD_DIM), f32),
        jax.ShapeDtypeStruct((BATCH, SEQ, N_KV_HEADS, HEAD_DIM), f32),
        jax.ShapeDtypeStruct((BATCH, IDX_DIM, SEQ), f32),
        jax.ShapeDtypeStruct((BATCH, SEQ, D_KV), bf16),
        jax.ShapeDtypeStruct((BATCH, SEQ // KEY_CHUNK, D_KV, KEY_CHUNK), bf16),
        jax.ShapeDtypeStruct((BATCH, SEQ, 2 * LANES), bf16),
        jax.ShapeDtypeStruct((BATCH, SEQ, D_ATT), bf16),
        jax.ShapeDtypeStruct((BATCH, SEQ, D_IDX), bf16),
        jax.ShapeDtypeStruct((BATCH, SEQ, LANES), f32),
        jax.ShapeDtypeStruct((BATCH, SEQ, D_MODEL), bf16),
        jax.ShapeDtypeStruct((BATCH, SEQ, D_ATT), bf16),
        jax.ShapeDtypeStruct((BATCH, SEQ, D_MODEL), bf16),
        jax.ShapeDtypeStruct((BATCH, CONV_W - 1, D_RNN), f32),
        jax.ShapeDtypeStruct((BATCH, 1, D_RNN), f32),
    ]
    out_specs = [
        kv_spec, kv_spec, pl.BlockSpec((None, IDX_DIM, ts), lambda b, j: (b, 0, j)), row_spec(D_KV),
        pl.BlockSpec((None, ts // KEY_CHUNK, D_KV, KEY_CHUNK), lambda b, j: (b, j, 0, 0)),
        row_spec(2 * LANES), row_spec(D_ATT), row_spec(D_IDX), row_spec(LANES),
        row_spec(D_MODEL), row_spec(D_ATT), row_spec(D_MODEL),
        pl.BlockSpec((None, CONV_W - 1, D_RNN), lambda b, j: (b, 0, 0)),
        pl.BlockSpec((None, 1, D_RNN), lambda b, j: (b, 0, 0)),
    ]
    scratch = [
        pltpu.VMEM((ts + SUBLANES, D_RNN), f32), pltpu.VMEM((ts, D_RNN), f32),
        pltpu.VMEM((ts, D_RNN), f32), pltpu.VMEM((1, D_RNN), f32), pltpu.VMEM((SUBLANES, D_RNN), f32),
        pltpu.VMEM((ts, D_RNN), f32), pltpu.VMEM((ts, D_MODEL), f32),
    ]
    return pl.pallas_call(
        _proj_kernel,
        grid=(BATCH, nt),
        in_specs=in_specs, out_specs=out_specs, out_shape=out_shape, scratch_shapes=scratch,
        compiler_params=pltpu.CompilerParams(
            dimension_semantics=("arbitrary", "arbitrary"), vmem_limit_bytes=VMEM_LIMIT),
        name="prompt_projection",
    )(x, shift, scale, wts["g_norm"], wts["w_in"], wts["w_conv"], wts["b_conv"], wts["w_ra"], wts["b_ra"],
      wts["w_rx"], wts["b_rx"], wts["lam"], wts["idx_g"], wts["idx_b"], wts["w_pa"], *tabs)


def _threshold_value(t_unsigned):
    key = jnp.maximum(jnp.bitwise_xor(t_unsigned, jnp.int32(INT_MIN)), jnp.int32(KEY_NEG_INF))
    bits = key ^ jnp.bitwise_and(jnp.right_shift(key, 31), jnp.int32(0x7FFFFFFF))
    return pltpu.bitcast(bits, jnp.float32)


def _greedy_bits(count_ge, n_bits, shape, count_all):
    def bit_step(b, carry):
        t, cnt = carry
        cand = jnp.bitwise_or(t, jnp.left_shift(jnp.int32(1), n_bits - 1 - b))
        c = count_ge(cand)
        ok = c >= TOPK
        return jnp.where(ok, cand, t), jnp.where(ok, c, cnt)
    return lax.fori_loop(0, n_bits, bit_step, (jnp.zeros(shape, jnp.int32), count_all))


def _fold_rows(x, rows, op=jnp.add, chains=4):
    parts = [x[r:r + rows] for r in range(0, x.shape[0], rows)]
    acc = parts[:chains]
    for k, part in enumerate(parts[chains:]):
        acc[k % len(acc)] = op(acc[k % len(acc)], part)
    while len(acc) > 1:
        acc = [op(acc[k], acc[k + 1]) for k in range(0, len(acc) - 1, 2)] + ([acc[-1]] if len(acc) % 2 else [])
    return acc[0]


def _tie_cutoff(count_tie_below, need, n_bits, shape):
    def bit_step(b, x):
        cand = jnp.bitwise_or(x, jnp.left_shift(jnp.int32(1), n_bits - 1 - b))
        return jnp.where(count_tie_below(cand) < need, cand, x)
    return lax.fori_loop(0, n_bits, bit_step, jnp.zeros(shape, jnp.int32))


def _attn_kernel(q_ref, qi_ref, kiwi_ref, kb_ref, vt_ref, kib_ref, o_ref,
                 score_s, hi_s, bias_s, s_scr, acc_s):
    i = pl.program_id(1)
    t0 = i * Q_BLOCK
    n_steps = (t0 + Q_BLOCK + ATT_CHUNK - 1) // ATT_CHUNK
    step_iota = lax.broadcasted_iota(jnp.int32, (ATT_CHUNK, Q_BLOCK), 0)
    sub_iota = lax.broadcasted_iota(jnp.int32, (KEY_CHUNK, Q_BLOCK), 0)
    q_pos = t0 + lax.broadcasted_iota(jnp.int32, (1, Q_BLOCK), 1)
    lane_shape = (1, Q_BLOCK)

    def step_rows(c):
        return pl.ds(pl.multiple_of(c * ATT_CHUNK, ATT_CHUNK), ATT_CHUNK)

    w_t = kiwi_ref[...].T[IDX_DIM:IDX_DIM + N_IDX_HEADS, :]
    qi = qi_ref[...]
    n_pairs = D_IDX // LANES
    qi_rows = jnp.concatenate([qi[:, p * LANES:(p + 1) * LANES] for p in range(n_pairs)], axis=0)

    def score_step(c, causal):
        for sub in range(ATT_CHUNK // KEY_CHUNK):
            r0 = pl.multiple_of(c * ATT_CHUNK + sub * KEY_CHUNK, KEY_CHUNK)
            rows = pl.ds(r0, KEY_CHUNK)
            s_par = [_dot_nt(kib_ref[rows, par * LANES:(par + 1) * LANES], qi_rows) for par in range(2)]
            score = jnp.zeros((KEY_CHUNK, Q_BLOCK), jnp.float32)
            for h in range(N_IDX_HEADS):
                s_h = s_par[h % 2][:, (h // 2) * Q_BLOCK:(h // 2 + 1) * Q_BLOCK]
                score = score + jnp.maximum(s_h, 0.0) * w_t[h:h + 1, :]
            if causal:
                score = jnp.where((r0 + sub_iota) <= q_pos, score, NEG_INF)
            score_s[rows, :] = score
            hi_s[rows, :] = _bf16(score)

    def early_score_step(c, carry):
        score_step(c, False)
        return carry

    lax.fori_loop(0, n_steps - 1, early_score_step, 0)
    score_step(n_steps - 1, True)

    def count_where(pred):
        def body(c, acc):
            sel = pred(score_s[step_rows(c), :], c * ATT_CHUNK + step_iota)
            parts = [_fold_rows(jnp.where(sel[:, t * LANES:(t + 1) * LANES], 1.0, 0.0), SUBLANES, chains=2)
                     for t in range(Q_BLOCK // LANES)]
            return acc + jnp.concatenate(parts, axis=1)
        acc = lax.fori_loop(0, n_steps, body, jnp.zeros((SUBLANES, Q_BLOCK), jnp.float32))
        return jnp.sum(acc, axis=0, keepdims=True)

    def count_rounded_ge(t16):
        cand = _threshold_value(jnp.left_shift(t16, 16)).astype(jnp.bfloat16)
        def body(c, acc):
            one = jnp.where(hi_s[step_rows(c), :] >= cand, jnp.bfloat16(1), jnp.bfloat16(0))
            return acc + _fold_rows(one, PACKED_ROWS, chains=2)
        acc = lax.fori_loop(0, n_steps, body, jnp.zeros((PACKED_ROWS, Q_BLOCK), jnp.bfloat16))
        return jnp.sum(acc.astype(jnp.float32), axis=0, keepdims=True)

    count_all = jnp.full(lane_shape, 1.0, jnp.float32) * (n_steps * ATT_CHUNK).astype(jnp.float32)
    t1, cnt_t1 = _greedy_bits(count_rounded_ge, 16, lane_shape, count_all)

    base = jnp.left_shift(jnp.maximum(t1 - 1, 0), 16)

    def count_ge(offset):
        cand = _threshold_value(base + offset)
        return count_where(lambda blk, idx: blk >= cand)

    off, cnt_ge = _greedy_bits(count_ge, 17, lane_shape, cnt_t1)
    thr = _threshold_value(base + off)

    def write_bias(select):
        def step(c, causal):
            blk = score_s[step_rows(c), :]
            idx = c * ATT_CHUNK + step_iota
            sel = select(blk, idx)
            if causal:
                sel = jnp.logical_and(sel, idx <= q_pos)
            bias_s[step_rows(c), :] = jnp.where(sel, 0.0, NEG_INF)

        def early_step(c, carry):
            step(c, False)
            return carry

        lax.fori_loop(0, n_steps - 1, early_step, 0)
        step(n_steps - 1, True)

    has_tie = jnp.max(jnp.where(cnt_ge > TOPK, 1.0, 0.0)) > 0.5

    @pl.when(jnp.logical_not(has_tie))
    def _():
        write_bias(lambda blk, idx: blk >= thr)

    @pl.when(has_tie)
    def _():
        need = TOPK - count_where(lambda blk, idx: blk > thr)
        cutoff = _tie_cutoff(
            lambda x: count_where(lambda blk, idx: jnp.logical_and(blk == thr, idx < x)),
            need, int(np.log2(SEQ)), lane_shape)
        write_bias(lambda blk, idx: jnp.logical_or(blk > thr, jnp.logical_and(blk == thr, idx <= cutoff)))

    q = q_ref[...]
    n_lanes = HEADS_PER_KV * Q_BLOCK
    q_rows = [
        jnp.concatenate([q[:, (g * HEADS_PER_KV + h) * HEAD_DIM:(g * HEADS_PER_KV + h + 1) * HEAD_DIM]
                         for h in range(HEADS_PER_KV)], axis=0)
        for g in range(N_KV_HEADS)]
    acc_s[...] = jnp.zeros(acc_s.shape, jnp.float32)
    subs = ATT_CHUNK // KEY_CHUNK

    ones_rows = jnp.ones((PACKED_ROWS, KEY_CHUNK), jnp.bfloat16)

    d_sl = [slice(g * HEAD_DIM, (g + 1) * HEAD_DIM) for g in range(N_KV_HEADS)]

    def logits(c, slot):
        maxima = []
        for g in range(N_KV_HEADS):
            b = bias_s[step_rows(c), :]
            x = _dot_nt(kb_ref[step_rows(c), d_sl[g]], q_rows[g]) + jnp.concatenate([b] * HEADS_PER_KV, axis=1)
            s_scr[slot, g] = x
            maxima.append(jnp.max(_fold_rows(x, SUBLANES, jnp.maximum), axis=0, keepdims=True))
        return tuple(maxima)

    def accumulate(c, slot, m_old, m_step):
        m_out = []
        for g in range(N_KV_HEADS):
            m_new = jnp.maximum(m_old[g], m_step[g])
            m_safe = jnp.where(m_new == NEG_INF, 0.0, m_new)
            alpha = jnp.exp2(m_old[g] - m_safe)
            pv = None
            for sub in range(subs):
                r = sub * KEY_CHUNK
                pb = _bf16(jnp.exp2(s_scr[slot, g, r:r + KEY_CHUNK, :] - m_safe))
                lhs = jnp.concatenate([vt_ref[c * subs + sub, d_sl[g], :], ones_rows], axis=0)
                part = _dot(lhs, pb)
                pv = part if pv is None else pv + part
            acc_s[g] = acc_s[g] * alpha + pv
            m_out.append(m_new)
        return tuple(m_out)

    def attend_pair(p, carry):
        m_run, m_even = carry
        c = 2 * p
        m_odd = logits(c + 1, 1)
        m_run = accumulate(c, 0, m_run, m_even)
        m_even = logits(jnp.minimum(c + 2, n_steps - 1), 0)
        m_run = accumulate(c + 1, 1, m_run, m_odd)
        return m_run, m_even

    m_init = tuple(jnp.full((1, n_lanes), NEG_INF, jnp.float32) for _ in range(N_KV_HEADS))
    m_run, m_even = lax.fori_loop(0, n_steps // 2, attend_pair, (m_init, logits(0, 0)))

    @pl.when(n_steps % 2 == 1)
    def _():
        accumulate(n_steps - 1, 0, m_run, m_even)

    for g in range(N_KV_HEADS):
        o_t = acc_s[g, 0:HEAD_DIM, :] / acc_s[g, HEAD_DIM:HEAD_DIM + 1, :]
        for h in range(HEADS_PER_KV):
            col = (g * HEADS_PER_KV + h) * HEAD_DIM
            o_ref[:, col:col + HEAD_DIM] = _bf16(o_t[:, h * Q_BLOCK:(h + 1) * Q_BLOCK].T)


def _prompt_attention(q, qi, kiwi, kb, vt, kib):
    nq = SEQ // Q_BLOCK
    blk = lambda w: pl.BlockSpec((None, Q_BLOCK, w), lambda b, i: (b, i, 0))
    return pl.pallas_call(
        _attn_kernel,
        grid=(BATCH, nq),
        in_specs=[
            blk(D_ATT), blk(D_IDX), blk(LANES),
            pl.BlockSpec((None, SEQ, D_KV), lambda b, i: (b, 0, 0)),
            pl.BlockSpec((None, SEQ // KEY_CHUNK, D_KV, KEY_CHUNK), lambda b, i: (b, 0, 0, 0)),
            pl.BlockSpec((None, SEQ, 2 * LANES), lambda b, i: (b, 0, 0)),
        ],
        out_specs=blk(D_ATT),
        out_shape=jax.ShapeDtypeStruct((BATCH, SEQ, D_ATT), jnp.bfloat16),
        scratch_shapes=[
            pltpu.VMEM((SEQ, Q_BLOCK), jnp.float32),
            pltpu.VMEM((SEQ, Q_BLOCK), jnp.bfloat16),
            pltpu.VMEM((SEQ, Q_BLOCK), jnp.float32),
            pltpu.VMEM((2, N_KV_HEADS, ATT_CHUNK, HEADS_PER_KV * Q_BLOCK), jnp.float32),
            pltpu.VMEM((N_KV_HEADS, HEAD_DIM + PACKED_ROWS, HEADS_PER_KV * Q_BLOCK), jnp.float32),
        ],
        compiler_params=pltpu.CompilerParams(
            dimension_semantics=("arbitrary", "arbitrary"), vmem_limit_bytes=VMEM_LIMIT),
        name="prompt_attention",
    )(q, qi, kiwi, kb, vt, kib)


def _out_kernel(o_ref, sgb_ref, mpa_ref, smb_ref, x_ref, gate_ref, w_pb_ref, w_o_ref, gf_ref, y_ref):
    f32 = jnp.float32
    yb = _dot(_bf16(o_ref[...].astype(f32) * sgb_ref[...].astype(f32)), w_pb_ref[...])
    m = mpa_ref[...].astype(f32) + smb_ref[...].astype(f32) * yb
    r = x_ref[...] + gate_ref[...] * _dot(_bf16(m), w_o_ref[...])
    y_ref[...] = r * lax.rsqrt(jnp.mean(r * r, axis=-1, keepdims=True) + EPS) * gf_ref[...]


def _output_projection(o, sgb, mpa, smb, x, gate, wts, rows):
    ng, nr, _ = x.shape
    row_spec = pl.BlockSpec((None, rows, D_MODEL), lambda b, j: (b, j, 0))
    if gate.shape[1] == 1:
        gate_spec = pl.BlockSpec((None, 1, D_MODEL), lambda b, j: (b, 0, 0))
    else:
        gate_spec = row_spec
    return pl.pallas_call(
        _out_kernel,
        grid=(ng, nr // rows),
        in_specs=[row_spec] * 5 + [gate_spec, _const_spec((D_ATT, D_MODEL)), _const_spec((D_MODEL, D_MODEL)),
                                   _const_spec((1, D_MODEL))],
        out_specs=row_spec,
        out_shape=jax.ShapeDtypeStruct(x.shape, jnp.float32),
        compiler_params=pltpu.CompilerParams(
            dimension_semantics=("arbitrary", "arbitrary"), vmem_limit_bytes=VMEM_LIMIT),
        name="output_projection",
    )(o, sgb, mpa, smb, x, gate, wts["w_pb"], wts["w_o"], wts["g_final"])


def _sample_proj_kernel(x_ref, shift_ref, scale_ref, gn_ref, w_in_ref, wconv_ref, bconv_ref, w_ra_ref, b_ra_ref,
                        w_rx_ref, b_rx_ref, lam_ref, ig_ref, ib_ref, w_pa_ref,
                        cos_h_ref, sin_h_ref, cos_i_ref, sin_ia_ref, sin_ib_ref, buf_ref, h0_ref,
                        k_ref, v_ref, q_ref, qi_ref, kiwi_ref, mpa_ref, sgb_ref, smb_ref, conv_ref, lru_ref):
    xn = _modulated_norm(x_ref[...], gn_ref[...], scale_ref[...], shift_ref[...])
    xa = _dot(xn, w_in_ref[:, C_XA:C_XA + D_RNN])
    xc = bconv_ref[...]
    for t in range(CONV_W - 1):
        xc = xc + buf_ref[t] * wconv_ref[t:t + 1, :]
        if t > 0:
            conv_ref[t - 1] = buf_ref[t]
    xc = xc + xa * wconv_ref[CONV_W - 1:CONV_W, :]
    conv_ref[CONV_W - 2] = xa
    a, u = _lru_gates(xc, w_ra_ref, b_ra_ref[...], w_rx_ref, b_rx_ref[...], lam_ref[...])
    h = a * h0_ref[...] + u
    lru_ref[...] = h
    ga = _dot(xn, w_in_ref[:, C_GA:C_GA + D_RNN])
    ya = _dot(_bf16(h * _silu(ga)), w_pa_ref[...])
    mpa_ref[...] = _sigmoid(_dot(xn, w_in_ref[:, C_MA:C_MA + D_MODEL])) * ya
    sgb_ref[...] = _silu(_dot(xn, w_in_ref[:, C_GB:C_GB + D_ATT]))
    smb_ref[...] = _sigmoid(_dot(xn, w_in_ref[:, C_MB:C_MB + D_MODEL]))

    cos_h, sin_h = cos_h_ref[0:1, :], sin_h_ref[0:1, :]
    cos_i, sin_ia, sin_ib = cos_i_ref[0:1, :], sin_ia_ref[0:1, :], sin_ib_ref[0:1, :]
    zq = _dot(xn, w_in_ref[:, C_Q:C_Q + D_ATT])
    for hd in range(N_HEADS):
        sl = slice(hd * HEAD_DIM, (hd + 1) * HEAD_DIM)
        q_ref[:, sl] = _bf16(_rot_head(zq[:, sl], cos_h, sin_h))
    zk = _dot(xn, w_in_ref[:, C_K:C_K + D_KV])
    for g in range(N_KV_HEADS):
        sl = slice(g * HEAD_DIM, (g + 1) * HEAD_DIM)
        k_ref[:, sl] = _rot_head(zk[:, sl], cos_h, sin_h)
    v_ref[...] = _dot(xn, w_in_ref[:, C_V:C_V + D_KV])
    zqi = _dot(xn, w_in_ref[:, C_QI:C_QI + D_IDX])
    for p in range(D_IDX // LANES):
        sl = slice(p * LANES, (p + 1) * LANES)
        qi_ref[:, sl] = _bf16(_rot_idx(zqi[:, sl], cos_i, sin_ia, sin_ib))
    kiwi_ref[...] = _idx_key_slab(_dot(xn, w_in_ref[:, C_KW:C_KW + LANES]), ig_ref[...], ib_ref[...],
                                  cos_i, sin_ia, sin_ib)


def _sample_projection(x, shift, scale, wts, tabs, buf_t, h0):
    n = DEC_BATCH
    f32, bf16 = jnp.float32, jnp.bfloat16
    in_specs = [
        _const_spec((n, D_MODEL)), _const_spec((n, D_MODEL)), _const_spec((n, D_MODEL)), _const_spec((1, D_MODEL)),
        _const_spec((D_MODEL, D_IN_PACKED)), _const_spec((CONV_W, D_RNN)), _const_spec((1, D_RNN)),
        _const_spec((RNN_BLOCKS, RNN_BLOCK_W, RNN_BLOCK_W)), _const_spec((1, D_RNN)),
        _const_spec((RNN_BLOCKS, RNN_BLOCK_W, RNN_BLOCK_W)), _const_spec((1, D_RNN)), _const_spec((1, D_RNN)),
        _const_spec((1, LANES)), _const_spec((1, LANES)), _const_spec((D_RNN, D_MODEL)),
    ] + [_const_spec((SUBLANES, LANES))] * 5 + [_const_spec((CONV_W - 1, n, D_RNN)), _const_spec((n, D_RNN))]
    shapes = [
        ((n, D_KV), f32), ((n, D_KV), f32), ((n, D_ATT), bf16), ((n, D_IDX), bf16), ((n, LANES), f32),
        ((n, D_MODEL), f32), ((n, D_ATT), f32), ((n, D_MODEL), f32), ((CONV_W - 1, n, D_RNN), f32), ((n, D_RNN), f32),
    ]
    return pl.pallas_call(
        _sample_proj_kernel,
        grid=(1,),
        in_specs=in_specs,
        out_specs=[_const_spec(s, single=False) for s, _ in shapes],
        out_shape=[jax.ShapeDtypeStruct(s, d) for s, d in shapes],
        compiler_params=pltpu.CompilerParams(vmem_limit_bytes=VMEM_LIMIT),
        name="sample_projection",
    )(x, shift, scale, wts["g_norm"], wts["w_in"], wts["w_conv"], wts["b_conv"], wts["w_ra"], wts["b_ra"],
      wts["w_rx"], wts["b_rx"], wts["lam"], wts["idx_g"], wts["idx_b"], wts["w_pa"], *tabs, buf_t, h0)


def _sample_score_kernel(pt_ref, qi_ref, w_ref, kinew_ref, idx_hbm, o_ref, buf, sem):
    b = pl.program_id(0)
    slot = b % 2

    def page_copy(sample, p, sl):
        return pltpu.make_async_copy(idx_hbm.at[pt_ref[sample * N_PAGES + p]], buf.at[sl, p], sem.at[sl])

    def start_sample(sample, sl):
        def body(p, carry):
            page_copy(sample, p, sl).start()
            return carry
        lax.fori_loop(0, N_PAGES, body, 0)

    @pl.when(b == 0)
    def _():
        start_sample(0, 0)

    @pl.when(b + 1 < pl.num_programs(0))
    def _():
        start_sample(b + 1, 1 - slot)

    def wait_page(p, carry):
        page_copy(b, p, slot).wait()
        return carry

    lax.fori_loop(0, N_PAGES, wait_page, 0)

    qi = qi_ref[...]
    w = w_ref[...]

    def score_pages(i, carry):
        p0 = i * SCORE_PAGES
        kt = _bf16(jnp.concatenate([buf[slot, p0 + t] for t in range(SCORE_PAGES)], axis=1))
        s = _dot(qi, kt)
        score = jnp.sum(jnp.maximum(s, 0.0) * w, axis=0, keepdims=True)
        for t in range(SCORE_PAGES):
            o_ref[pl.ds(p0 + t, 1), :] = score[:, t * PAGE_SIZE:(t + 1) * PAGE_SIZE]
        return carry

    for i in range(N_PAGES // SCORE_PAGES):
        score_pages(i, 0)

    k_self = _bf16(kinew_ref[...][:, :IDX_DIM]).astype(jnp.float32)
    s_self = jnp.sum(qi.astype(jnp.float32) * k_self, axis=1, keepdims=True)
    score_self = jnp.sum(jnp.maximum(s_self, 0.0) * w, axis=0, keepdims=True)
    lane = lax.broadcasted_iota(jnp.int32, (1, PAGE_SIZE), 1)
    o_ref[N_PAGES:N_PAGES + 1, :] = jnp.where(lane == 0, score_self, NEG_INF)


def _sample_scores(page_table_flat, idx_pages, qi3, w_col, kiwi3):
    per_sample = lambda r, w: pl.BlockSpec((None, r, w), lambda b, pt: (b, 0, 0))
    return pl.pallas_call(
        _sample_score_kernel,
        grid_spec=pltpu.PrefetchScalarGridSpec(
            num_scalar_prefetch=1,
            grid=(DEC_BATCH,),
            in_specs=[per_sample(N_IDX_HEADS, IDX_DIM), per_sample(N_IDX_HEADS, 1), per_sample(1, LANES),
                      pl.BlockSpec(memory_space=pl.ANY)],
            out_specs=per_sample(N_PAGES + 1, PAGE_SIZE),
            scratch_shapes=[pltpu.VMEM((2, N_PAGES, IDX_DIM, PAGE_SIZE), jnp.float32),
                            pltpu.SemaphoreType.DMA((2,))],
        ),
        out_shape=jax.ShapeDtypeStruct((DEC_BATCH, N_PAGES + 1, PAGE_SIZE), jnp.float32),
        compiler_params=pltpu.CompilerParams(dimension_semantics=("arbitrary",), vmem_limit_bytes=VMEM_LIMIT),
        name="sample_scores",
    )(page_table_flat, qi3, w_col, kiwi3, idx_pages)


def _sample_select_kernel(score_ref, bias_ref):
    keys = score_ref[...]
    idx = lax.broadcasted_iota(jnp.int32, keys.shape, 1)
    col_shape = (keys.shape[0], 1)
    count = lambda pred: jnp.sum(jnp.where(pred, 1.0, 0.0), axis=1, keepdims=True)
    total = jnp.full(col_shape, float(keys.shape[1]), jnp.float32)
    thr_u, _ = _greedy_bits(lambda t: count(keys >= _threshold_value(t)), 32, col_shape, total)
    thr = _threshold_value(thr_u)
    need = TOPK - count(keys > thr)
    tie = keys == thr
    n_bits = int(np.ceil(np.log2(keys.shape[1])))
    cutoff = _tie_cutoff(lambda x: count(jnp.logical_and(tie, idx < x)), need, n_bits, col_shape)
    sel = jnp.logical_or(keys > thr, jnp.logical_and(tie, idx <= cutoff))
    bias_ref[...] = jnp.where(sel, 0.0, NEG_INF)


def _sample_select(scores):
    return pl.pallas_call(
        _sample_select_kernel,
        grid=(1,),
        in_specs=[_const_spec(scores.shape)],
        out_specs=_const_spec(scores.shape, single=False),
        out_shape=jax.ShapeDtypeStruct(scores.shape, jnp.float32),
        compiler_params=pltpu.CompilerParams(vmem_limit_bytes=VMEM_LIMIT),
        name="sample_select",
    )(scores)


def _sample_attn_kernel(pt_ref, bias_ref, bias_self_ref, q_ref, knew_ref, vnew_ref, k_hbm, v_hbm, o_ref,
                        kbuf, vbuf, sem, m_s, l_s, acc_s):
    b, j = pl.program_id(0), pl.program_id(1)
    n_j = pl.num_programs(1)
    step = b * n_j + j
    slot = step % 2

    def page_copies(st, sl):
        out = []
        for t in range(ATTN_PAGES):
            page = pt_ref[st * ATTN_PAGES + t]
            out.append(pltpu.make_async_copy(k_hbm.at[page], kbuf.at[sl, t], sem.at[0, sl]))
            out.append(pltpu.make_async_copy(v_hbm.at[page], vbuf.at[sl, t], sem.at[1, sl]))
        return out

    @pl.when(step == 0)
    def _():
        for c in page_copies(0, 0):
            c.start()

    @pl.when(step + 1 < pl.num_programs(0) * n_j)
    def _():
        for c in page_copies(step + 1, 1 - slot):
            c.start()

    for c in page_copies(step, slot):
        c.wait()

    @pl.when(j == 0)
    def _():
        m_s[...] = jnp.full(m_s.shape, NEG_INF, jnp.float32)
        l_s[...] = jnp.zeros(l_s.shape, jnp.float32)
        acc_s[...] = jnp.zeros(acc_s.shape, jnp.float32)

    def online_update(s, pv_of):
        m_old = m_s[...]
        m_new = jnp.maximum(m_old, jnp.max(s, axis=1, keepdims=True))
        m_safe = jnp.where(m_new == NEG_INF, 0.0, m_new)
        pr = jnp.exp2((s - m_safe) * SOFTMAX_SCALE_LOG2E)
        alpha = jnp.exp2((m_old - m_safe) * SOFTMAX_SCALE_LOG2E)
        l_s[...] = alpha * l_s[...] + jnp.sum(pr, axis=1, keepdims=True)
        acc_s[...] = acc_s[...] * alpha + pv_of(_bf16(pr))
        m_s[...] = m_new

    q = q_ref[...]
    rows_per_page = PAGE_SIZE * N_KV_HEADS
    n_cols = ATTN_PAGES * rows_per_page
    k_all = _bf16(kbuf[slot].reshape(n_cols, HEAD_DIM))
    v_all = _bf16(vbuf[slot].reshape(n_cols, HEAD_DIM))
    dup = jnp.where(lax.broadcasted_iota(jnp.int32, (PAGE_SIZE, rows_per_page), 1) // N_KV_HEADS
                    == lax.broadcasted_iota(jnp.int32, (PAGE_SIZE, rows_per_page), 0), 1.0, 0.0)
    sel_pages = _dot(_bf16(jnp.where(bias_ref[...] == 0.0, 1.0, 0.0)), _bf16(dup))
    sel_row = jnp.concatenate([sel_pages[t:t + 1, :] for t in range(ATTN_PAGES)], axis=1)
    head = lax.broadcasted_iota(jnp.int32, (N_HEADS, n_cols), 0)
    col = lax.broadcasted_iota(jnp.int32, (N_HEADS, n_cols), 1)
    own = (col % N_KV_HEADS) == (head // HEADS_PER_KV)
    s = jnp.where(jnp.logical_and(own, sel_row > 0.5), _dot_nt(q, k_all), NEG_INF)
    online_update(s, lambda pb: _dot(pb, v_all))

    @pl.when(j == n_j - 1)
    def _():
        head_d = lax.broadcasted_iota(jnp.int32, (N_HEADS, HEAD_DIM), 0) // HEADS_PER_KV

        def own_row(ref):
            rows = _bf16(ref[...]).astype(jnp.float32)
            out = jnp.broadcast_to(rows[N_KV_HEADS - 1:N_KV_HEADS, :], (N_HEADS, HEAD_DIM))
            for g in range(N_KV_HEADS - 2, -1, -1):
                out = jnp.where(head_d == g, rows[g:g + 1, :], out)
            return out

        s_self = jnp.sum(q.astype(jnp.float32) * own_row(knew_ref), axis=1, keepdims=True)
        v_own = own_row(vnew_ref)
        online_update(s_self + bias_self_ref[...][:, 0:1], lambda pb: pb.astype(jnp.float32) * v_own)
        o_ref[...] = acc_s[...] / l_s[...]


def _sample_attention(page_table_flat, k_pages, v_pages, bias3, bias_self, q3, knew3, vnew3):
    per_sample = lambda r, w: pl.BlockSpec((None, r, w), lambda b, j, pt: (b, 0, 0))
    rows_per_page = PAGE_SIZE * N_KV_HEADS
    return pl.pallas_call(
        _sample_attn_kernel,
        grid_spec=pltpu.PrefetchScalarGridSpec(
            num_scalar_prefetch=1,
            grid=(DEC_BATCH, N_PAGES // ATTN_PAGES),
            in_specs=[
                pl.BlockSpec((None, ATTN_PAGES, PAGE_SIZE), lambda b, j, pt: (b, j, 0)),
                per_sample(1, PAGE_SIZE), per_sample(N_HEADS, HEAD_DIM),
                per_sample(N_KV_HEADS, HEAD_DIM), per_sample(N_KV_HEADS, HEAD_DIM),
                pl.BlockSpec(memory_space=pl.ANY), pl.BlockSpec(memory_space=pl.ANY),
            ],
            out_specs=per_sample(N_HEADS, HEAD_DIM),
            scratch_shapes=[
                pltpu.VMEM((2, ATTN_PAGES, rows_per_page, HEAD_DIM), jnp.float32),
                pltpu.VMEM((2, ATTN_PAGES, rows_per_page, HEAD_DIM), jnp.float32),
                pltpu.SemaphoreType.DMA((2, 2)),
                pltpu.VMEM((N_HEADS, 1), jnp.float32), pltpu.VMEM((N_HEADS, 1), jnp.float32),
                pltpu.VMEM((N_HEADS, HEAD_DIM), jnp.float32),
            ],
        ),
        out_shape=jax.ShapeDtypeStruct((DEC_BATCH, N_HEADS, HEAD_DIM), jnp.float32),
        compiler_params=pltpu.CompilerParams(
            dimension_semantics=("arbitrary", "arbitrary"), vmem_limit_bytes=VMEM_LIMIT),
        name="sample_attention",
    )(page_table_flat, bias3, bias_self, q3, knew3, vnew3, k_pages, v_pages)


def _pack_w_in(w_in):
    pad = jnp.zeros((D_MODEL, W_PAD), jnp.bfloat16)
    return jnp.concatenate([_bf16(w_in[:, :W_PAD_AT]), pad, _bf16(w_in[:, W_PAD_AT:])], axis=-1)


def _lane_pad(v):
    return jnp.pad(v.reshape(1, -1), ((0, 0), (0, LANES - v.shape[-1])))


def kernel(x_prompt, x_sample, cache_k, cache_v, cache_idx_k, state_conv, state_rglru, page_table, c_prompt, c_sample, w_ada, b_ada, g_norm, w_in, w_conv, b_conv, w_ra, b_ra, w_rx, b_rx, lru_lambda, idx_k_norm_g, idx_k_norm_b, w_pa, w_pb, w_o, g_final):
    assert w_in.shape[0] == 1, "one layer"
    wts = {
        "g_norm": g_norm[0].reshape(1, -1), "w_in": _pack_w_in(w_in[0]), "w_conv": w_conv[0],
        "b_conv": b_conv[0].reshape(1, -1), "w_ra": _bf16(w_ra[0]), "b_ra": b_ra[0].reshape(1, -1),
        "w_rx": _bf16(w_rx[0]), "b_rx": b_rx[0].reshape(1, -1), "lam": lru_lambda[0].reshape(1, -1),
        "idx_g": _lane_pad(idx_k_norm_g[0]), "idx_b": _lane_pad(idx_k_norm_b[0]),
        "w_pa": _bf16(w_pa[0]), "w_pb": _bf16(w_pb[0]), "w_o": _bf16(w_o[0]), "g_final": g_final.reshape(1, -1),
    }
    half_h, half_i = HEAD_DIM // 2, IDX_DIM // 2
    invf_h = ROPE_THETA ** (-jnp.arange(half_h, dtype=jnp.float32) / half_h)
    invf_i = ROPE_THETA ** (-jnp.arange(half_i, dtype=jnp.float32) / half_i)
    invf = jnp.zeros((SUBLANES, LANES), jnp.float32)
    invf = invf.at[0].set(jnp.tile(invf_h, LANES // half_h)).at[1].set(jnp.tile(invf_i, LANES // half_i))
    tabs_prompt = _rope_tables(invf, SEQ, 0, 1)
    tabs_sample = _rope_tables(invf, SUBLANES, PAST_LEN, 0)

    mod = _ada_modulation(jnp.concatenate([c_prompt, c_sample], axis=0), _bf16(w_ada[0]), b_ada[0].reshape(1, -1))
    shift, scale, gate = mod[:, :D_MODEL], mod[:, D_MODEL:2 * D_MODEL], mod[:, 2 * D_MODEL:]

    (k_p, v_p, ki_p, kb, vt, kib, q, qi, kiwi, mpa, sgb, smb, conv_p, lru_p) = _prompt_projection(
        x_prompt, shift[:BATCH, None, :], scale[:BATCH, None, :], wts, tabs_prompt)
    o = _prompt_attention(q, qi, kiwi, kb, vt, kib)
    y_prompt = _output_projection(o, sgb, mpa, smb, x_prompt, gate[:BATCH, None, :], wts, OUT_ROWS)

    xs = x_sample[:, 0, :]
    (k_s, v_s, q_s, qi_s, kiwi_s, mpa_s, sgb_s, smb_s, conv_s, lru_s) = _sample_projection(
        xs, shift[BATCH:], scale[BATCH:], wts, tabs_sample, jnp.swapaxes(state_conv[0], 0, 1), state_rglru[0])
    pt_flat = page_table.reshape(-1)
    w_col = kiwi_s[:, IDX_DIM:IDX_DIM + N_IDX_HEADS, None]
    idx_pages = jnp.swapaxes(cache_idx_k[0], 1, 2)
    kv_pages = lambda t: t[0].reshape(-1, PAGE_SIZE * N_KV_HEADS, HEAD_DIM)
    scores = _sample_scores(pt_flat, idx_pages, qi_s.reshape(DEC_BATCH, N_IDX_HEADS, IDX_DIM), w_col,
                            kiwi_s[:, None, :])
    bias = _sample_select(scores.reshape(DEC_BATCH, (N_PAGES + 1) * PAGE_SIZE))
    bias = bias.reshape(DEC_BATCH, N_PAGES + 1, PAGE_SIZE)
    o_s = _sample_attention(
        pt_flat, kv_pages(cache_k), kv_pages(cache_v), bias, bias[:, N_PAGES:, :],
        q_s.reshape(DEC_BATCH, N_HEADS, HEAD_DIM), k_s.reshape(DEC_BATCH, N_KV_HEADS, HEAD_DIM),
        v_s.reshape(DEC_BATCH, N_KV_HEADS, HEAD_DIM))
    y_sample = _output_projection(
        o_s.reshape(1, DEC_BATCH, D_ATT), sgb_s[None], mpa_s[None], smb_s[None], xs[None], gate[None, BATCH:],
        wts, DEC_BATCH)

    kv_s = lambda t: t.reshape(1, DEC_BATCH, 1, N_KV_HEADS, HEAD_DIM)
    return (
        y_prompt, y_sample.reshape(DEC_BATCH, 1, D_MODEL),
        k_p[None], v_p[None], jnp.swapaxes(ki_p, 1, 2)[None], conv_p[None], lru_p.reshape(1, BATCH, D_RNN),
        kv_s(k_s), kv_s(v_s), kiwi_s[:, :IDX_DIM].reshape(1, DEC_BATCH, 1, IDX_DIM),
        jnp.swapaxes(conv_s, 0, 1)[None], lru_s[None],
    )
```

```python
import functools

import jax
import jax.numpy as jnp
import numpy as np
from jax import lax
from jax.experimental import pallas as pl
from jax.experimental.pallas import tpu as pltpu

D_MODEL = 1024
BATCH = 8
SEQ = 4096
DEC_BATCH = 32
PAST_LEN = 16384
PAGE_SIZE = 128
N_PAGES = PAST_LEN // PAGE_SIZE
D_RNN = D_MODEL
RNN_BLOCKS = 4
RNN_BLOCK_W = D_RNN // RNN_BLOCKS
CONV_W = 4
LRU_C = 8.0
N_HEADS = 8
HEAD_DIM = 128
N_KV_HEADS = 2
HEADS_PER_KV = N_HEADS // N_KV_HEADS
D_ATT = N_HEADS * HEAD_DIM
D_KV = N_KV_HEADS * HEAD_DIM
N_IDX_HEADS = 8
IDX_DIM = 64
D_IDX = N_IDX_HEADS * IDX_DIM
IDX_W_SCALE = (N_IDX_HEADS * IDX_DIM) ** -0.5
TOPK = 256
ROPE_THETA = 10000.0
EPS = 1e-6
SPLITS = (D_RNN, D_RNN, D_ATT, D_KV, D_KV, D_ATT, D_IDX, IDX_DIM, N_IDX_HEADS, D_MODEL, D_MODEL)

LANES = 128
SUBLANES = 8

C_XA, C_GA, C_Q, C_K, C_V, C_GB, C_QI, C_KW = (int(c) for c in np.cumsum((0,) + SPLITS[:7]))
W_PAD_AT = C_KW + IDX_DIM + N_IDX_HEADS
W_PAD = LANES - IDX_DIM - N_IDX_HEADS
C_MA = C_KW + LANES
C_MB = C_MA + D_MODEL
D_IN_PACKED = C_MB + D_MODEL
assert all(c % LANES == 0 for c in (C_XA, C_GA, C_Q, C_K, C_V, C_GB, C_QI, C_KW, C_MA, C_MB))

PROJ_ROWS = 256
Q_BLOCK = 256
KEY_CHUNK = 256
ATT_CHUNK = 512
PACKED_ROWS = 16
assert KEY_CHUNK >= TOPK and ATT_CHUNK % KEY_CHUNK == 0 and SEQ % ATT_CHUNK == 0 and ATT_CHUNK % Q_BLOCK == 0
assert PAGE_SIZE == HEAD_DIM == LANES
OUT_ROWS = 1024
SCORE_PAGES = 16
ATTN_PAGES = 32
assert N_PAGES % SCORE_PAGES == 0 and N_PAGES % ATTN_PAGES == 0
SOFTMAX_SCALE_LOG2E = (HEAD_DIM ** -0.5) * float(np.log2(np.e))
NEG_INF = float("-inf")
INT_MIN = -2 ** 31
KEY_NEG_INF = INT_MIN + 0x7FFFFF
V7X_VMEM_BYTES = 64 * 1024 * 1024
VMEM_LIMIT = V7X_VMEM_BYTES * 7 // 8


def _sigmoid(x):
    return 1.0 / (1.0 + jnp.exp(-x))


def _silu(x):
    return x * _sigmoid(x)


def _dot(a, b):
    return jnp.dot(a, b, preferred_element_type=jnp.float32)


def _dot_nt(a, b):
    return lax.dot_general(a, b, (((1,), (1,)), ((), ())), preferred_element_type=jnp.float32)


def _bf16(x):
    return x.astype(jnp.bfloat16)


def _const_spec(shape, single=True):
    nd = len(shape)
    kwargs = {"pipeline_mode": pl.Buffered(1)} if single else {}
    return pl.BlockSpec(shape, lambda *_: (0,) * nd, **kwargs)


def _rope_kernel(invf_ref, cos_h_ref, sin_h_ref, cos_i_ref, sin_ia_ref, sin_ib_ref, *, pos0, pos_step, rows):
    r0 = pl.program_id(0) * rows
    row = lax.broadcasted_iota(jnp.int32, (rows, LANES), 0) + r0
    lane = lax.broadcasted_iota(jnp.int32, (rows, LANES), 1)
    pos = (pos0 + pos_step * row).astype(jnp.float32)
    ang_h = pos * invf_ref[0:1, :]
    ang_i = pos * invf_ref[1:2, :]
    cos_h_ref[...] = jnp.cos(ang_h)
    sh = jnp.sin(ang_h)
    sin_h_ref[...] = jnp.where(lane < HEAD_DIM // 2, -sh, sh)
    cos_i_ref[...] = jnp.cos(ang_i)
    si = jnp.sin(ang_i)
    first_half = (lane % IDX_DIM) < IDX_DIM // 2
    sin_ia_ref[...] = jnp.where(first_half, -si, 0.0)
    sin_ib_ref[...] = jnp.where(first_half, 0.0, si)


def _rope_tables(invf, n, pos0, pos_step):
    rows = min(n, 512)
    out = jax.ShapeDtypeStruct((n, LANES), jnp.float32)
    spec = pl.BlockSpec((rows, LANES), lambda i: (i, 0))
    return pl.pallas_call(
        functools.partial(_rope_kernel, pos0=pos0, pos_step=pos_step, rows=rows),
        grid=(n // rows,),
        in_specs=[pl.BlockSpec((SUBLANES, LANES), lambda i: (0, 0))],
        out_specs=[spec] * 5,
        out_shape=[out] * 5,
        name="rope_tables",
    )(invf)


def _rot_head(z, cos, sin_signed):
    return z * cos + pltpu.roll(z, HEAD_DIM // 2, 1) * sin_signed


def _rot_idx(z, cos, sin_a, sin_b):
    return z * cos + pltpu.roll(z, LANES - IDX_DIM // 2, 1) * sin_a + pltpu.roll(z, IDX_DIM // 2, 1) * sin_b


def _ada_kernel(c_ref, w_ref, b_ref, o_ref):
    o_ref[...] = _dot(_bf16(_silu(c_ref[...])), w_ref[...]) + b_ref[...]


def _ada_modulation(c_all, w_ada, b_ada):
    n = c_all.shape[0]
    return pl.pallas_call(
        _ada_kernel,
        grid=(1,),
        in_specs=[_const_spec((n, D_MODEL)), _const_spec((D_MODEL, 3 * D_MODEL)), _const_spec((1, 3 * D_MODEL))],
        out_specs=_const_spec((n, 3 * D_MODEL), single=False),
        out_shape=jax.ShapeDtypeStruct((n, 3 * D_MODEL), jnp.float32),
        compiler_params=pltpu.CompilerParams(vmem_limit_bytes=VMEM_LIMIT),
        name="ada_modulation",
    )(c_all, w_ada, b_ada)


def _modulated_norm(x, g, scale, shift):
    y = x * lax.rsqrt(jnp.mean(x * x, axis=-1, keepdims=True) + EPS) * g
    return _bf16(y * (1.0 + scale) + shift)


def _lru_gates(xc, w_ra_ref, b_ra, w_rx_ref, b_rx, lam):
    xcb = _bf16(xc)
    r_parts, i_parts = [], []
    for n in range(RNN_BLOCKS):
        sl = slice(n * RNN_BLOCK_W, (n + 1) * RNN_BLOCK_W)
        r_parts.append(_dot(xcb[:, sl], w_ra_ref[n]))
        i_parts.append(_dot(xcb[:, sl], w_rx_ref[n]))
    r = _sigmoid(jnp.concatenate(r_parts, axis=1) + b_ra)
    i = _sigmoid(jnp.concatenate(i_parts, axis=1) + b_rx)
    neg_lam = -lam
    softplus = jnp.maximum(neg_lam, 0.0) + jnp.log1p(jnp.exp(-jnp.abs(neg_lam)))
    log_a = (-LRU_C) * r * softplus
    a = jnp.exp(log_a)
    u = jnp.sqrt(-jnp.tanh(log_a) * (a * a + 1.0)) * (i * xc)
    return a, u


def _idx_key_slab(z_kw, g, b, cos_i, sin_ia, sin_ib):
    lane = lax.broadcasted_iota(jnp.int32, z_kw.shape, 1)
    is_key = lane < IDX_DIM
    mu = jnp.sum(jnp.where(is_key, z_kw, 0.0), axis=-1, keepdims=True) * (1.0 / IDX_DIM)
    d = jnp.where(is_key, z_kw - mu, 0.0)
    var = jnp.sum(d * d, axis=-1, keepdims=True) * (1.0 / IDX_DIM)
    y = d * lax.rsqrt(var + EPS) * g + b
    key = _rot_idx(y, cos_i, sin_ia, sin_ib)
    is_w = jnp.logical_and(lane >= IDX_DIM, lane < IDX_DIM + N_IDX_HEADS)
    return key + jnp.where(is_w, z_kw * IDX_W_SCALE, 0.0)


def _proj_kernel(x_ref, shift_ref, scale_ref, gn_ref, w_in_ref, wconv_ref, bconv_ref, w_ra_ref, b_ra_ref,
                 w_rx_ref, b_rx_ref, lam_ref, ig_ref, ib_ref, w_pa_ref,
                 cos_h_ref, sin_h_ref, cos_i_ref, sin_ia_ref, sin_ib_ref,
                 k_ref, v_ref, ki_ref, kb_ref, vt_ref, kib_ref, q_ref, qi_ref, kiwi_ref,
                 mpa_ref, sgb_ref, smb_ref, conv_ref, lru_ref,
                 xa_ext, a_s, u_s, h_carry, tail_s, ga_s, ma_s):
    ts = PROJ_ROWS

    @pl.when(pl.program_id(1) == 0)
    def _():
        tail_s[...] = jnp.zeros(tail_s.shape, jnp.float32)
        h_carry[...] = jnp.zeros(h_carry.shape, jnp.float32)

    xn = _modulated_norm(x_ref[...], gn_ref[...], scale_ref[...], shift_ref[...])

    xa_ext[0:SUBLANES, :] = tail_s[...]
    xa_ext[SUBLANES:SUBLANES + ts, :] = _dot(xn, w_in_ref[:, C_XA:C_XA + D_RNN])
    sgb_ref[...] = _bf16(_silu(_dot(xn, w_in_ref[:, C_GB:C_GB + D_ATT])))
    smb_ref[...] = _bf16(_sigmoid(_dot(xn, w_in_ref[:, C_MB:C_MB + D_MODEL])))
    xc = bconv_ref[...]
    for t in range(CONV_W):
        off = SUBLANES - (CONV_W - 1) + t
        xc = xc + xa_ext[off:off + ts, :] * wconv_ref[t:t + 1, :]
    conv_ref[...] = xa_ext[ts + SUBLANES - (CONV_W - 1):ts + SUBLANES, :]
    tail_s[...] = xa_ext[ts:ts + SUBLANES, :]

    a, u = _lru_gates(xc, w_ra_ref, b_ra_ref[...], w_rx_ref, b_rx_ref[...], lam_ref[...])
    a_s[...] = a
    u_s[...] = u
    row = lax.broadcasted_iota(jnp.int32, (SUBLANES, D_RNN), 0)
    cos_h, sin_h = cos_h_ref[...], sin_h_ref[...]
    cos_i, sin_ia, sin_ib = cos_i_ref[...], sin_ia_ref[...], sin_ib_ref[...]

    def proj_gate_a():
        ga_s[...] = _silu(_dot(xn, w_in_ref[:, C_GA:C_GA + D_RNN]))

    def proj_merge_a():
        ma_s[...] = _sigmoid(_dot(xn, w_in_ref[:, C_MA:C_MA + D_MODEL]))

    def proj_q(lo, hi):
        def run():
            zq = _dot(xn, w_in_ref[:, C_Q + lo * HEAD_DIM:C_Q + hi * HEAD_DIM])
            for h in range(hi - lo):
                rot = _rot_head(zq[:, h * HEAD_DIM:(h + 1) * HEAD_DIM], cos_h, sin_h)
                q_ref[:, (lo + h) * HEAD_DIM:(lo + h + 1) * HEAD_DIM] = _bf16(rot * SOFTMAX_SCALE_LOG2E)
        return run

    def proj_kv():
        zk = _dot(xn, w_in_ref[:, C_K:C_K + D_KV])
        for g in range(N_KV_HEADS):
            sl = slice(g * HEAD_DIM, (g + 1) * HEAD_DIM)
            kr = _rot_head(zk[:, sl], cos_h, sin_h)
            k_ref[:, g, :] = kr
            kb_ref[:, sl] = _bf16(kr)
        zv = _dot(xn, w_in_ref[:, C_V:C_V + D_KV])
        for g in range(N_KV_HEADS):
            v_ref[:, g, :] = zv[:, g * HEAD_DIM:(g + 1) * HEAD_DIM]
        vt = _bf16(zv.T)
        for c in range(ts // KEY_CHUNK):
            vt_ref[c] = vt[:, c * KEY_CHUNK:(c + 1) * KEY_CHUNK]

    def proj_idx():
        zqi = _dot(xn, w_in_ref[:, C_QI:C_QI + D_IDX])
        for p in range(D_IDX // LANES):
            sl = slice(p * LANES, (p + 1) * LANES)
            qi_ref[:, sl] = _bf16(_rot_idx(zqi[:, sl], cos_i, sin_ia, sin_ib))
        slab = _idx_key_slab(_dot(xn, w_in_ref[:, C_KW:C_KW + LANES]), ig_ref[...], ib_ref[...],
                             cos_i, sin_ia, sin_ib)
        kiwi_ref[...] = slab
        ki_ref[...] = slab.T[:IDX_DIM, :]
        key_even = jnp.where(lax.broadcasted_iota(jnp.int32, slab.shape, 1) < IDX_DIM, slab, 0.0)
        kib_ref[:, 0:LANES] = _bf16(key_even)
        kib_ref[:, LANES:2 * LANES] = _bf16(pltpu.roll(key_even, IDX_DIM, 1))

    scan_work = [proj_gate_a, proj_merge_a, proj_q(0, N_HEADS // 2), proj_q(N_HEADS // 2, N_HEADS), proj_kv, proj_idx]

    hc = h_carry[...]
    n_groups = ts // SUBLANES
    per_chunk = n_groups // (len(scan_work) + 2)
    for g in range(n_groups):
        rows = slice(g * SUBLANES, (g + 1) * SUBLANES)
        a8 = a_s[rows, :]
        u8 = u_s[rows, :]
        for d in (1, 2, 4):
            keep = row >= d
            u8 = jnp.where(keep, a8 * pltpu.roll(u8, d, 0) + u8, u8)
            a8 = jnp.where(keep, a8 * pltpu.roll(a8, d, 0), a8)
        h8 = a8 * hc + u8
        u_s[rows, :] = h8
        hc = h8[SUBLANES - 1:SUBLANES, :]
        if g % per_chunk == per_chunk - 1 and g // per_chunk < len(scan_work):
            scan_work[g // per_chunk]()
    h_carry[...] = hc
    lru_ref[...] = hc

    ya = _dot(_bf16(u_s[...] * ga_s[...]), w_pa_ref[...])
    mpa_ref[...] = _bf16(ma_s[...] * ya)


def _prompt_projection(x, shift, scale, wts, tabs):
    ts = PROJ_ROWS
    nt = SEQ // ts
    f32, bf16 = jnp.float32, jnp.bfloat16
    row_spec = lambda w: pl.BlockSpec((None, ts, w), lambda b, j: (b, j, 0))
    bvec_spec = pl.BlockSpec((None, 1, D_MODEL), lambda b, j: (b, 0, 0))
    kv_spec = pl.BlockSpec((None, ts, N_KV_HEADS, HEAD_DIM), lambda b, j: (b, j, 0, 0))
    tab_spec = pl.BlockSpec((ts, LANES), lambda b, j: (j, 0))
    in_specs = [
        row_spec(D_MODEL), bvec_spec, bvec_spec, _const_spec((1, D_MODEL)),
        _const_spec((D_MODEL, D_IN_PACKED)), _const_spec((CONV_W, D_RNN)), _const_spec((1, D_RNN)),
        _const_spec((RNN_BLOCKS, RNN_BLOCK_W, RNN_BLOCK_W)), _const_spec((1, D_RNN)),
        _const_spec((RNN_BLOCKS, RNN_BLOCK_W, RNN_BLOCK_W)), _const_spec((1, D_RNN)), _const_spec((1, D_RNN)),
        _const_spec((1, LANES)), _const_spec((1, LANES)), _const_spec((D_RNN, D_MODEL)),
    ] + [tab_spec] * 5
    out_shape = [
        jax.ShapeDtypeStruct((BATCH, SEQ, N_KV_HEADS, HEAD_DIM), f32),
        jax.ShapeDtypeStruct((BATCH, SEQ, N_KV_HEADS, HEAD_DIM), f32),
        jax.ShapeDtypeStruct((BATCH, IDX_DIM, SEQ), f32),
        jax.ShapeDtypeStruct((BATCH, SEQ, D_KV), bf16),
        jax.ShapeDtypeStruct((BATCH, SEQ // KEY_CHUNK, D_KV, KEY_CHUNK), bf16),
        jax.ShapeDtypeStruct((BATCH, SEQ, 2 * LANES), bf16),
        jax.ShapeDtypeStruct((BATCH, SEQ, D_ATT), bf16),
        jax.ShapeDtypeStruct((BATCH, SEQ, D_IDX), bf16),
        jax.ShapeDtypeStruct((BATCH, SEQ, LANES), f32),
        jax.ShapeDtypeStruct((BATCH, SEQ, D_MODEL), bf16),
        jax.ShapeDtypeStruct((BATCH, SEQ, D_ATT), bf16),
        jax.ShapeDtypeStruct((BATCH, SEQ, D_MODEL), bf16),
        jax.ShapeDtypeStruct((BATCH, CONV_W - 1, D_RNN), f32),
        jax.ShapeDtypeStruct((BATCH, 1, D_RNN), f32),
    ]
    out_specs = [
        kv_spec, kv_spec, pl.BlockSpec((None, IDX_DIM, ts), lambda b, j: (b, 0, j)), row_spec(D_KV),
        pl.BlockSpec((None, ts // KEY_CHUNK, D_KV, KEY_CHUNK), lambda b, j: (b, j, 0, 0)),
        row_spec(2 * LANES), row_spec(D_ATT), row_spec(D_IDX), row_spec(LANES),
        row_spec(D_MODEL), row_spec(D_ATT), row_spec(D_MODEL),
        pl.BlockSpec((None, CONV_W - 1, D_RNN), lambda b, j: (b, 0, 0)),
        pl.BlockSpec((None, 1, D_RNN), lambda b, j: (b, 0, 0)),
    ]
    scratch = [
        pltpu.VMEM((ts + SUBLANES, D_RNN), f32), pltpu.VMEM((ts, D_RNN), f32),
        pltpu.VMEM((ts, D_RNN), f32), pltpu.VMEM((1, D_RNN), f32), pltpu.VMEM((SUBLANES, D_RNN), f32),
        pltpu.VMEM((ts, D_RNN), f32), pltpu.VMEM((ts, D_MODEL), f32),
    ]
    return pl.pallas_call(
        _proj_kernel,
        grid=(BATCH, nt),
        in_specs=in_specs, out_specs=out_specs, out_shape=out_shape, scratch_shapes=scratch,
        compiler_params=pltpu.CompilerParams(
            dimension_semantics=("arbitrary", "arbitrary"), vmem_limit_bytes=VMEM_LIMIT),
        name="prompt_projection",
    )(x, shift, scale, wts["g_norm"], wts["w_in"], wts["w_conv"], wts["b_conv"], wts["w_ra"], wts["b_ra"],
      wts["w_rx"], wts["b_rx"], wts["lam"], wts["idx_g"], wts["idx_b"], wts["w_pa"], *tabs)


def _threshold_value(t_unsigned):
    key = jnp.maximum(jnp.bitwise_xor(t_unsigned, jnp.int32(INT_MIN)), jnp.int32(KEY_NEG_INF))
    bits = key ^ jnp.bitwise_and(jnp.right_shift(key, 31), jnp.int32(0x7FFFFFFF))
    return pltpu.bitcast(bits, jnp.float32)


def _greedy_bits(count_ge, n_bits, shape, count_all):
    def bit_step(b, carry):
        t, cnt = carry
        cand = jnp.bitwise_or(t, jnp.left_shift(jnp.int32(1), n_bits - 1 - b))
        c = count_ge(cand)
        ok = c >= TOPK
        return jnp.where(ok, cand, t), jnp.where(ok, c, cnt)
    return lax.fori_loop(0, n_bits, bit_step, (jnp.zeros(shape, jnp.int32), count_all))


def _fold_rows(x, rows, op=jnp.add, chains=4):
    parts = [x[r:r + rows] for r in range(0, x.shape[0], rows)]
    acc = parts[:chains]
    for k, part in enumerate(parts[chains:]):
        acc[k % len(acc)] = op(acc[k % len(acc)], part)
    while len(acc) > 1:
        acc = [op(acc[k], acc[k + 1]) for k in range(0, len(acc) - 1, 2)] + ([acc[-1]] if len(acc) % 2 else [])
    return acc[0]


def _tie_cutoff(count_tie_below, need, n_bits, shape):
    def bit_step(b, x):
        cand = jnp.bitwise_or(x, jnp.left_shift(jnp.int32(1), n_bits - 1 - b))
        return jnp.where(count_tie_below(cand) < need, cand, x)
    return lax.fori_loop(0, n_bits, bit_step, jnp.zeros(shape, jnp.int32))


def _attn_kernel(q_ref, qi_ref, kiwi_ref, kb_ref, vt_ref, kib_ref, o_ref,
                 score_s, hi_s, bias_s, s_scr, acc_s):
    i = pl.program_id(1)
    t0 = i * Q_BLOCK
    n_steps = (t0 + Q_BLOCK + ATT_CHUNK - 1) // ATT_CHUNK
    step_iota = lax.broadcasted_iota(jnp.int32, (ATT_CHUNK, Q_BLOCK), 0)
    sub_iota = lax.broadcasted_iota(jnp.int32, (KEY_CHUNK, Q_BLOCK), 0)
    q_pos = t0 + lax.broadcasted_iota(jnp.int32, (1, Q_BLOCK), 1)
    lane_shape = (1, Q_BLOCK)

    def step_rows(c):
        return pl.ds(pl.multiple_of(c * ATT_CHUNK, ATT_CHUNK), ATT_CHUNK)

    w_t = kiwi_ref[...].T[IDX_DIM:IDX_DIM + N_IDX_HEADS, :]
    qi = qi_ref[...]
    n_pairs = D_IDX // LANES
    qi_rows = jnp.concatenate([qi[:, p * LANES:(p + 1) * LANES] for p in range(n_pairs)], axis=0)

    def score_step(c, causal):
        for sub in range(ATT_CHUNK // KEY_CHUNK):
            r0 = pl.multiple_of(c * ATT_CHUNK + sub * KEY_CHUNK, KEY_CHUNK)
            rows = pl.ds(r0, KEY_CHUNK)
            s_par = [_dot_nt(kib_ref[rows, par * LANES:(par + 1) * LANES], qi_rows) for par in range(2)]
            score = jnp.zeros((KEY_CHUNK, Q_BLOCK), jnp.float32)
            for h in range(N_IDX_HEADS):
                s_h = s_par[h % 2][:, (h // 2) * Q_BLOCK:(h // 2 + 1) * Q_BLOCK]
                score = score + jnp.maximum(s_h, 0.0) * w_t[h:h + 1, :]
            if causal:
                score = jnp.where((r0 + sub_iota) <= q_pos, score, NEG_INF)
            score_s[rows, :] = score
            hi_s[rows, :] = _bf16(score)

    def early_score_step(c, carry):
        score_step(c, False)
        return carry

    lax.fori_loop(0, n_steps - 1, early_score_step, 0)
    score_step(n_steps - 1, True)

    def sum_steps(one_step, zero):
        acc = lax.fori_loop(0, n_steps // 2, lambda p, a: a + one_step(2 * p) + one_step(2 * p + 1), zero)
        return lax.cond(n_steps % 2 == 1, lambda: acc + one_step(n_steps - 1), lambda: acc)

    def count_where(pred):
        def one_step(c):
            sel = pred(score_s[step_rows(c), :], c * ATT_CHUNK + step_iota)
            parts = [_fold_rows(jnp.where(sel[:, t * LANES:(t + 1) * LANES], 1.0, 0.0), SUBLANES, chains=2)
                     for t in range(Q_BLOCK // LANES)]
            return jnp.concatenate(parts, axis=1)
        acc = sum_steps(one_step, jnp.zeros((SUBLANES, Q_BLOCK), jnp.float32))
        return jnp.sum(acc, axis=0, keepdims=True)

    def count_rounded_ge(t16):
        cand = _threshold_value(jnp.left_shift(t16, 16)).astype(jnp.bfloat16)
        def one_step(c):
            one = jnp.where(hi_s[step_rows(c), :] >= cand, jnp.bfloat16(1), jnp.bfloat16(0))
            return _fold_rows(one, PACKED_ROWS, chains=2)
        acc = sum_steps(one_step, jnp.zeros((PACKED_ROWS, Q_BLOCK), jnp.bfloat16))
        return jnp.sum(acc.astype(jnp.float32), axis=0, keepdims=True)

    count_all = jnp.full(lane_shape, 1.0, jnp.float32) * (n_steps * ATT_CHUNK).astype(jnp.float32)
    t1, cnt_t1 = _greedy_bits(count_rounded_ge, 16, lane_shape, count_all)

    base = jnp.left_shift(jnp.maximum(t1 - 1, 0), 16)

    def count_ge(offset):
        cand = _threshold_value(base + offset)
        return count_where(lambda blk, idx: blk >= cand)

    off, cnt_ge = _greedy_bits(count_ge, 17, lane_shape, cnt_t1)
    thr = _threshold_value(base + off)

    def write_bias(select):
        def step(c, causal):
            blk = score_s[step_rows(c), :]
            idx = c * ATT_CHUNK + step_iota
            sel = select(blk, idx)
            if causal:
                sel = jnp.logical_and(sel, idx <= q_pos)
            bias_s[step_rows(c), :] = jnp.where(sel, 0.0, NEG_INF)

        def early_step(c, carry):
            step(c, False)
            return carry

        lax.fori_loop(0, n_steps - 1, early_step, 0)
        step(n_steps - 1, True)

    has_tie = jnp.max(jnp.where(cnt_ge > TOPK, 1.0, 0.0)) > 0.5

    @pl.when(jnp.logical_not(has_tie))
    def _():
        write_bias(lambda blk, idx: blk >= thr)

    @pl.when(has_tie)
    def _():
        need = TOPK - count_where(lambda blk, idx: blk > thr)
        cutoff = _tie_cutoff(
            lambda x: count_where(lambda blk, idx: jnp.logical_and(blk == thr, idx < x)),
            need, int(np.log2(SEQ)), lane_shape)
        write_bias(lambda blk, idx: jnp.logical_or(blk > thr, jnp.logical_and(blk == thr, idx <= cutoff)))

    q = q_ref[...]
    n_lanes = HEADS_PER_KV * Q_BLOCK
    q_rows = [
        jnp.concatenate([q[:, (g * HEADS_PER_KV + h) * HEAD_DIM:(g * HEADS_PER_KV + h + 1) * HEAD_DIM]
                         for h in range(HEADS_PER_KV)], axis=0)
        for g in range(N_KV_HEADS)]
    acc_s[...] = jnp.zeros(acc_s.shape, jnp.float32)
    subs = ATT_CHUNK // KEY_CHUNK

    ones_rows = jnp.ones((PACKED_ROWS, KEY_CHUNK), jnp.bfloat16)

    d_sl = [slice(g * HEAD_DIM, (g + 1) * HEAD_DIM) for g in range(N_KV_HEADS)]

    def logits(c, slot):
        maxima = []
        for g in range(N_KV_HEADS):
            b = bias_s[step_rows(c), :]
            x = _dot_nt(kb_ref[step_rows(c), d_sl[g]], q_rows[g]) + jnp.concatenate([b] * HEADS_PER_KV, axis=1)
            s_scr[slot, g] = x
            maxima.append(jnp.max(_fold_rows(x, SUBLANES, jnp.maximum), axis=0, keepdims=True))
        return tuple(maxima)

    def accumulate(c, slot, m_old, m_step):
        m_out = []
        for g in range(N_KV_HEADS):
            m_new = jnp.maximum(m_old[g], m_step[g])
            m_safe = jnp.where(m_new == NEG_INF, 0.0, m_new)
            alpha = jnp.exp2(m_old[g] - m_safe)
            pv = None
            for sub in range(subs):
                r = sub * KEY_CHUNK
                pb = _bf16(jnp.exp2(s_scr[slot, g, r:r + KEY_CHUNK, :] - m_safe))
                lhs = jnp.concatenate([vt_ref[c * subs + sub, d_sl[g], :], ones_rows], axis=0)
                part = _dot(lhs, pb)
                pv = part if pv is None else pv + part
            acc_s[g] = acc_s[g] * alpha + pv
            m_out.append(m_new)
        return tuple(m_out)

    def attend_pair(p, carry):
        m_run, m_even = carry
        c = 2 * p
        m_odd = logits(c + 1, 1)
        m_run = accumulate(c, 0, m_run, m_even)
        m_even = logits(jnp.minimum(c + 2, n_steps - 1), 0)
        m_run = accumulate(c + 1, 1, m_run, m_odd)
        return m_run, m_even

    m_init = tuple(jnp.full((1, n_lanes), NEG_INF, jnp.float32) for _ in range(N_KV_HEADS))
    m_run, m_even = lax.fori_loop(0, n_steps // 2, attend_pair, (m_init, logits(0, 0)))

    @pl.when(n_steps % 2 == 1)
    def _():
        accumulate(n_steps - 1, 0, m_run, m_even)

    for g in range(N_KV_HEADS):
        o_t = acc_s[g, 0:HEAD_DIM, :] / acc_s[g, HEAD_DIM:HEAD_DIM + 1, :]
        for h in range(HEADS_PER_KV):
            col = (g * HEADS_PER_KV + h) * HEAD_DIM
            o_ref[:, col:col + HEAD_DIM] = _bf16(o_t[:, h * Q_BLOCK:(h + 1) * Q_BLOCK].T)


def _prompt_attention(q, qi, kiwi, kb, vt, kib):
    nq = SEQ // Q_BLOCK
    blk = lambda w: pl.BlockSpec((None, Q_BLOCK, w), lambda b, i: (b, i, 0))
    return pl.pallas_call(
        _attn_kernel,
        grid=(BATCH, nq),
        in_specs=[
            blk(D_ATT), blk(D_IDX), blk(LANES),
            pl.BlockSpec((None, SEQ, D_KV), lambda b, i: (b, 0, 0)),
            pl.BlockSpec((None, SEQ // KEY_CHUNK, D_KV, KEY_CHUNK), lambda b, i: (b, 0, 0, 0)),
            pl.BlockSpec((None, SEQ, 2 * LANES), lambda b, i: (b, 0, 0)),
        ],
        out_specs=blk(D_ATT),
        out_shape=jax.ShapeDtypeStruct((BATCH, SEQ, D_ATT), jnp.bfloat16),
        scratch_shapes=[
            pltpu.VMEM((SEQ, Q_BLOCK), jnp.float32),
            pltpu.VMEM((SEQ, Q_BLOCK), jnp.bfloat16),
            pltpu.VMEM((SEQ, Q_BLOCK), jnp.float32),
            pltpu.VMEM((2, N_KV_HEADS, ATT_CHUNK, HEADS_PER_KV * Q_BLOCK), jnp.float32),
            pltpu.VMEM((N_KV_HEADS, HEAD_DIM + PACKED_ROWS, HEADS_PER_KV * Q_BLOCK), jnp.float32),
        ],
        compiler_params=pltpu.CompilerParams(
            dimension_semantics=("arbitrary", "arbitrary"), vmem_limit_bytes=VMEM_LIMIT),
        name="prompt_attention",
    )(q, qi, kiwi, kb, vt, kib)


def _out_kernel(o_ref, sgb_ref, mpa_ref, smb_ref, x_ref, gate_ref, w_pb_ref, w_o_ref, gf_ref, y_ref):
    f32 = jnp.float32
    yb = _dot(_bf16(o_ref[...].astype(f32) * sgb_ref[...].astype(f32)), w_pb_ref[...])
    m = mpa_ref[...].astype(f32) + smb_ref[...].astype(f32) * yb
    r = x_ref[...] + gate_ref[...] * _dot(_bf16(m), w_o_ref[...])
    y_ref[...] = r * lax.rsqrt(jnp.mean(r * r, axis=-1, keepdims=True) + EPS) * gf_ref[...]


def _output_projection(o, sgb, mpa, smb, x, gate, wts, rows):
    ng, nr, _ = x.shape
    row_spec = pl.BlockSpec((None, rows, D_MODEL), lambda b, j: (b, j, 0))
    if gate.shape[1] == 1:
        gate_spec = pl.BlockSpec((None, 1, D_MODEL), lambda b, j: (b, 0, 0))
    else:
        gate_spec = row_spec
    return pl.pallas_call(
        _out_kernel,
        grid=(ng, nr // rows),
        in_specs=[row_spec] * 5 + [gate_spec, _const_spec((D_ATT, D_MODEL)), _const_spec((D_MODEL, D_MODEL)),
                                   _const_spec((1, D_MODEL))],
        out_specs=row_spec,
        out_shape=jax.ShapeDtypeStruct(x.shape, jnp.float32),
        compiler_params=pltpu.CompilerParams(
            dimension_semantics=("arbitrary", "arbitrary"), vmem_limit_bytes=VMEM_LIMIT),
        name="output_projection",
    )(o, sgb, mpa, smb, x, gate, wts["w_pb"], wts["w_o"], wts["g_final"])


def _sample_proj_kernel(x_ref, shift_ref, scale_ref, gn_ref, w_in_ref, wconv_ref, bconv_ref, w_ra_ref, b_ra_ref,
                        w_rx_ref, b_rx_ref, lam_ref, ig_ref, ib_ref, w_pa_ref,
                        cos_h_ref, sin_h_ref, cos_i_ref, sin_ia_ref, sin_ib_ref, buf_ref, h0_ref,
                        k_ref, v_ref, q_ref, qi_ref, kiwi_ref, mpa_ref, sgb_ref, smb_ref, conv_ref, lru_ref):
    xn = _modulated_norm(x_ref[...], gn_ref[...], scale_ref[...], shift_ref[...])
    xa = _dot(xn, w_in_ref[:, C_XA:C_XA + D_RNN])
    xc = bconv_ref[...]
    for t in range(CONV_W - 1):
        xc = xc + buf_ref[t] * wconv_ref[t:t + 1, :]
        if t > 0:
            conv_ref[t - 1] = buf_ref[t]
    xc = xc + xa * wconv_ref[CONV_W - 1:CONV_W, :]
    conv_ref[CONV_W - 2] = xa
    a, u = _lru_gates(xc, w_ra_ref, b_ra_ref[...], w_rx_ref, b_rx_ref[...], lam_ref[...])
    h = a * h0_ref[...] + u
    lru_ref[...] = h
    ga = _dot(xn, w_in_ref[:, C_GA:C_GA + D_RNN])
    ya = _dot(_bf16(h * _silu(ga)), w_pa_ref[...])
    mpa_ref[...] = _sigmoid(_dot(xn, w_in_ref[:, C_MA:C_MA + D_MODEL])) * ya
    sgb_ref[...] = _silu(_dot(xn, w_in_ref[:, C_GB:C_GB + D_ATT]))
    smb_ref[...] = _sigmoid(_dot(xn, w_in_ref[:, C_MB:C_MB + D_MODEL]))

    cos_h, sin_h = cos_h_ref[0:1, :], sin_h_ref[0:1, :]
    cos_i, sin_ia, sin_ib = cos_i_ref[0:1, :], sin_ia_ref[0:1, :], sin_ib_ref[0:1, :]
    zq = _dot(xn, w_in_ref[:, C_Q:C_Q + D_ATT])
    for hd in range(N_HEADS):
        sl = slice(hd * HEAD_DIM, (hd + 1) * HEAD_DIM)
        q_ref[:, sl] = _bf16(_rot_head(zq[:, sl], cos_h, sin_h))
    zk = _dot(xn, w_in_ref[:, C_K:C_K + D_KV])
    for g in range(N_KV_HEADS):
        sl = slice(g * HEAD_DIM, (g + 1) * HEAD_DIM)
        k_ref[:, sl] = _rot_head(zk[:, sl], cos_h, sin_h)
    v_ref[...] = _dot(xn, w_in_ref[:, C_V:C_V + D_KV])
    zqi = _dot(xn, w_in_ref[:, C_QI:C_QI + D_IDX])
    for p in range(D_IDX // LANES):
        sl = slice(p * LANES, (p + 1) * LANES)
        qi_ref[:, sl] = _bf16(_rot_idx(zqi[:, sl], cos_i, sin_ia, sin_ib))
    kiwi_ref[...] = _idx_key_slab(_dot(xn, w_in_ref[:, C_KW:C_KW + LANES]), ig_ref[...], ib_ref[...],
                                  cos_i, sin_ia, sin_ib)


def _sample_projection(x, shift, scale, wts, tabs, buf_t, h0):
    n = DEC_BATCH
    f32, bf16 = jnp.float32, jnp.bfloat16
    in_specs = [
        _const_spec((n, D_MODEL)), _const_spec((n, D_MODEL)), _const_spec((n, D_MODEL)), _const_spec((1, D_MODEL)),
        _const_spec((D_MODEL, D_IN_PACKED)), _const_spec((CONV_W, D_RNN)), _const_spec((1, D_RNN)),
        _const_spec((RNN_BLOCKS, RNN_BLOCK_W, RNN_BLOCK_W)), _const_spec((1, D_RNN)),
        _const_spec((RNN_BLOCKS, RNN_BLOCK_W, RNN_BLOCK_W)), _const_spec((1, D_RNN)), _const_spec((1, D_RNN)),
        _const_spec((1, LANES)), _const_spec((1, LANES)), _const_spec((D_RNN, D_MODEL)),
    ] + [_const_spec((SUBLANES, LANES))] * 5 + [_const_spec((CONV_W - 1, n, D_RNN)), _const_spec((n, D_RNN))]
    shapes = [
        ((n, D_KV), f32), ((n, D_KV), f32), ((n, D_ATT), bf16), ((n, D_IDX), bf16), ((n, LANES), f32),
        ((n, D_MODEL), f32), ((n, D_ATT), f32), ((n, D_MODEL), f32), ((CONV_W - 1, n, D_RNN), f32), ((n, D_RNN), f32),
    ]
    return pl.pallas_call(
        _sample_proj_kernel,
        grid=(1,),
        in_specs=in_specs,
        out_specs=[_const_spec(s, single=False) for s, _ in shapes],
        out_shape=[jax.ShapeDtypeStruct(s, d) for s, d in shapes],
        compiler_params=pltpu.CompilerParams(vmem_limit_bytes=VMEM_LIMIT),
        name="sample_projection",
    )(x, shift, scale, wts["g_norm"], wts["w_in"], wts["w_conv"], wts["b_conv"], wts["w_ra"], wts["b_ra"],
      wts["w_rx"], wts["b_rx"], wts["lam"], wts["idx_g"], wts["idx_b"], wts["w_pa"], *tabs, buf_t, h0)


def _sample_score_kernel(pt_ref, qi_ref, w_ref, kinew_ref, idx_hbm, o_ref, buf, sem):
    b = pl.program_id(0)
    slot = b % 2

    def page_copy(sample, p, sl):
        return pltpu.make_async_copy(idx_hbm.at[pt_ref[sample * N_PAGES + p]], buf.at[sl, p], sem.at[sl])

    def start_sample(sample, sl):
        def body(p, carry):
            page_copy(sample, p, sl).start()
            return carry
        lax.fori_loop(0, N_PAGES, body, 0)

    @pl.when(b == 0)
    def _():
        start_sample(0, 0)

    @pl.when(b + 1 < pl.num_programs(0))
    def _():
        start_sample(b + 1, 1 - slot)

    def wait_page(p, carry):
        page_copy(b, p, slot).wait()
        return carry

    lax.fori_loop(0, N_PAGES, wait_page, 0)

    qi = qi_ref[...]
    w = w_ref[...]

    def score_pages(i, carry):
        p0 = i * SCORE_PAGES
        kt = _bf16(jnp.concatenate([buf[slot, p0 + t] for t in range(SCORE_PAGES)], axis=1))
        s = _dot(qi, kt)
        score = jnp.sum(jnp.maximum(s, 0.0) * w, axis=0, keepdims=True)
        for t in range(SCORE_PAGES):
            o_ref[pl.ds(p0 + t, 1), :] = score[:, t * PAGE_SIZE:(t + 1) * PAGE_SIZE]
        return carry

    for i in range(N_PAGES // SCORE_PAGES):
        score_pages(i, 0)

    k_self = _bf16(kinew_ref[...][:, :IDX_DIM]).astype(jnp.float32)
    s_self = jnp.sum(qi.astype(jnp.float32) * k_self, axis=1, keepdims=True)
    score_self = jnp.sum(jnp.maximum(s_self, 0.0) * w, axis=0, keepdims=True)
    lane = lax.broadcasted_iota(jnp.int32, (1, PAGE_SIZE), 1)
    o_ref[N_PAGES:N_PAGES + 1, :] = jnp.where(lane == 0, score_self, NEG_INF)


def _sample_scores(page_table_flat, idx_pages, qi3, w_col, kiwi3):
    per_sample = lambda r, w: pl.BlockSpec((None, r, w), lambda b, pt: (b, 0, 0))
    return pl.pallas_call(
        _sample_score_kernel,
        grid_spec=pltpu.PrefetchScalarGridSpec(
            num_scalar_prefetch=1,
            grid=(DEC_BATCH,),
            in_specs=[per_sample(N_IDX_HEADS, IDX_DIM), per_sample(N_IDX_HEADS, 1), per_sample(1, LANES),
                      pl.BlockSpec(memory_space=pl.ANY)],
            out_specs=per_sample(N_PAGES + 1, PAGE_SIZE),
            scratch_shapes=[pltpu.VMEM((2, N_PAGES, IDX_DIM, PAGE_SIZE), jnp.float32),
                            pltpu.SemaphoreType.DMA((2,))],
        ),
        out_shape=jax.ShapeDtypeStruct((DEC_BATCH, N_PAGES + 1, PAGE_SIZE), jnp.float32),
        compiler_params=pltpu.CompilerParams(dimension_semantics=("arbitrary",), vmem_limit_bytes=VMEM_LIMIT),
        name="sample_scores",
    )(page_table_flat, qi3, w_col, kiwi3, idx_pages)


def _sample_select_kernel(score_ref, bias_ref):
    keys = score_ref[...]
    idx = lax.broadcasted_iota(jnp.int32, keys.shape, 1)
    col_shape = (keys.shape[0], 1)
    count = lambda pred: jnp.sum(jnp.where(pred, 1.0, 0.0), axis=1, keepdims=True)
    total = jnp.full(col_shape, float(keys.shape[1]), jnp.float32)
    thr_u, _ = _greedy_bits(lambda t: count(keys >= _threshold_value(t)), 32, col_shape, total)
    thr = _threshold_value(thr_u)
    need = TOPK - count(keys > thr)
    tie = keys == thr
    n_bits = int(np.ceil(np.log2(keys.shape[1])))
    cutoff = _tie_cutoff(lambda x: count(jnp.logical_and(tie, idx < x)), need, n_bits, col_shape)
    sel = jnp.logical_or(keys > thr, jnp.logical_and(tie, idx <= cutoff))
    bias_ref[...] = jnp.where(sel, 0.0, NEG_INF)


def _sample_select(scores):
    return pl.pallas_call(
        _sample_select_kernel,
        grid=(1,),
        in_specs=[_const_spec(scores.shape)],
        out_specs=_const_spec(scores.shape, single=False),
        out_shape=jax.ShapeDtypeStruct(scores.shape, jnp.float32),
        compiler_params=pltpu.CompilerParams(vmem_limit_bytes=VMEM_LIMIT),
        name="sample_select",
    )(scores)


def _sample_attn_kernel(pt_ref, bias_ref, bias_self_ref, q_ref, knew_ref, vnew_ref, k_hbm, v_hbm, o_ref,
                        kbuf, vbuf, sem, m_s, l_s, acc_s):
    b, j = pl.program_id(0), pl.program_id(1)
    n_j = pl.num_programs(1)
    step = b * n_j + j
    slot = step % 2

    def page_copies(st, sl):
        out = []
        for t in range(ATTN_PAGES):
            page = pt_ref[st * ATTN_PAGES + t]
            out.append(pltpu.make_async_copy(k_hbm.at[page], kbuf.at[sl, t], sem.at[0, sl]))
            out.append(pltpu.make_async_copy(v_hbm.at[page], vbuf.at[sl, t], sem.at[1, sl]))
        return out

    @pl.when(step == 0)
    def _():
        for c in page_copies(0, 0):
            c.start()

    @pl.when(step + 1 < pl.num_programs(0) * n_j)
    def _():
        for c in page_copies(step + 1, 1 - slot):
            c.start()

    for c in page_copies(step, slot):
        c.wait()

    @pl.when(j == 0)
    def _():
        m_s[...] = jnp.full(m_s.shape, NEG_INF, jnp.float32)
        l_s[...] = jnp.zeros(l_s.shape, jnp.float32)
        acc_s[...] = jnp.zeros(acc_s.shape, jnp.float32)

    def online_update(s, pv_of):
        m_old = m_s[...]
        m_new = jnp.maximum(m_old, jnp.max(s, axis=1, keepdims=True))
        m_safe = jnp.where(m_new == NEG_INF, 0.0, m_new)
        pr = jnp.exp2((s - m_safe) * SOFTMAX_SCALE_LOG2E)
        alpha = jnp.exp2((m_old - m_safe) * SOFTMAX_SCALE_LOG2E)
        l_s[...] = alpha * l_s[...] + jnp.sum(pr, axis=1, keepdims=True)
        acc_s[...] = acc_s[...] * alpha + pv_of(_bf16(pr))
        m_s[...] = m_new

    q = q_ref[...]
    rows_per_page = PAGE_SIZE * N_KV_HEADS
    n_cols = ATTN_PAGES * rows_per_page
    k_all = _bf16(kbuf[slot].reshape(n_cols, HEAD_DIM))
    v_all = _bf16(vbuf[slot].reshape(n_cols, HEAD_DIM))
    dup = jnp.where(lax.broadcasted_iota(jnp.int32, (PAGE_SIZE, rows_per_page), 1) // N_KV_HEADS
                    == lax.broadcasted_iota(jnp.int32, (PAGE_SIZE, rows_per_page), 0), 1.0, 0.0)
    sel_pages = _dot(_bf16(jnp.where(bias_ref[...] == 0.0, 1.0, 0.0)), _bf16(dup))
    sel_row = jnp.concatenate([sel_pages[t:t + 1, :] for t in range(ATTN_PAGES)], axis=1)
    head = lax.broadcasted_iota(jnp.int32, (N_HEADS, n_cols), 0)
    col = lax.broadcasted_iota(jnp.int32, (N_HEADS, n_cols), 1)
    own = (col % N_KV_HEADS) == (head // HEADS_PER_KV)
    s = jnp.where(jnp.logical_and(own, sel_row > 0.5), _dot_nt(q, k_all), NEG_INF)
    online_update(s, lambda pb: _dot(pb, v_all))

    @pl.when(j == n_j - 1)
    def _():
        head_d = lax.broadcasted_iota(jnp.int32, (N_HEADS, HEAD_DIM), 0) // HEADS_PER_KV

        def own_row(ref):
            rows = _bf16(ref[...]).astype(jnp.float32)
            out = jnp.broadcast_to(rows[N_KV_HEADS - 1:N_KV_HEADS, :], (N_HEADS, HEAD_DIM))
            for g in range(N_KV_HEADS - 2, -1, -1):
                out = jnp.where(head_d == g, rows[g:g + 1, :], out)
            return out

        s_self = jnp.sum(q.astype(jnp.float32) * own_row(knew_ref), axis=1, keepdims=True)
        v_own = own_row(vnew_ref)
        online_update(s_self + bias_self_ref[...][:, 0:1], lambda pb: pb.astype(jnp.float32) * v_own)
        o_ref[...] = acc_s[...] / l_s[...]


def _sample_attention(page_table_flat, k_pages, v_pages, bias3, bias_self, q3, knew3, vnew3):
    per_sample = lambda r, w: pl.BlockSpec((None, r, w), lambda b, j, pt: (b, 0, 0))
    rows_per_page = PAGE_SIZE * N_KV_HEADS
    return pl.pallas_call(
        _sample_attn_kernel,
        grid_spec=pltpu.PrefetchScalarGridSpec(
            num_scalar_prefetch=1,
            grid=(DEC_BATCH, N_PAGES // ATTN_PAGES),
            in_specs=[
                pl.BlockSpec((None, ATTN_PAGES, PAGE_SIZE), lambda b, j, pt: (b, j, 0)),
                per_sample(1, PAGE_SIZE), per_sample(N_HEADS, HEAD_DIM),
                per_sample(N_KV_HEADS, HEAD_DIM), per_sample(N_KV_HEADS, HEAD_DIM),
                pl.BlockSpec(memory_space=pl.ANY), pl.BlockSpec(memory_space=pl.ANY),
            ],
            out_specs=per_sample(N_HEADS, HEAD_DIM),
            scratch_shapes=[
                pltpu.VMEM((2, ATTN_PAGES, rows_per_page, HEAD_DIM), jnp.float32),
                pltpu.VMEM((2, ATTN_PAGES, rows_per_page, HEAD_DIM), jnp.float32),
                pltpu.SemaphoreType.DMA((2, 2)),
                pltpu.VMEM((N_HEADS, 1), jnp.float32), pltpu.VMEM((N_HEADS, 1), jnp.float32),
                pltpu.VMEM((N_HEADS, HEAD_DIM), jnp.float32),
            ],
        ),
        out_shape=jax.ShapeDtypeStruct((DEC_BATCH, N_HEADS, HEAD_DIM), jnp.float32),
        compiler_params=pltpu.CompilerParams(
            dimension_semantics=("arbitrary", "arbitrary"), vmem_limit_bytes=VMEM_LIMIT),
        name="sample_attention",
    )(page_table_flat, bias3, bias_self, q3, knew3, vnew3, k_pages, v_pages)


def _pack_w_in(w_in):
    pad = jnp.zeros((D_MODEL, W_PAD), jnp.bfloat16)
    return jnp.concatenate([_bf16(w_in[:, :W_PAD_AT]), pad, _bf16(w_in[:, W_PAD_AT:])], axis=-1)


def _lane_pad(v):
    return jnp.pad(v.reshape(1, -1), ((0, 0), (0, LANES - v.shape[-1])))


def kernel(x_prompt, x_sample, cache_k, cache_v, cache_idx_k, state_conv, state_rglru, page_table, c_prompt, c_sample, w_ada, b_ada, g_norm, w_in, w_conv, b_conv, w_ra, b_ra, w_rx, b_rx, lru_lambda, idx_k_norm_g, idx_k_norm_b, w_pa, w_pb, w_o, g_final):
    assert w_in.shape[0] == 1, "one layer"
    wts = {
        "g_norm": g_norm[0].reshape(1, -1), "w_in": _pack_w_in(w_in[0]), "w_conv": w_conv[0],
        "b_conv": b_conv[0].reshape(1, -1), "w_ra": _bf16(w_ra[0]), "b_ra": b_ra[0].reshape(1, -1),
        "w_rx": _bf16(w_rx[0]), "b_rx": b_rx[0].reshape(1, -1), "lam": lru_lambda[0].reshape(1, -1),
        "idx_g": _lane_pad(idx_k_norm_g[0]), "idx_b": _lane_pad(idx_k_norm_b[0]),
        "w_pa": _bf16(w_pa[0]), "w_pb": _bf16(w_pb[0]), "w_o": _bf16(w_o[0]), "g_final": g_final.reshape(1, -1),
    }
    half_h, half_i = HEAD_DIM // 2, IDX_DIM // 2
    invf_h = ROPE_THETA ** (-jnp.arange(half_h, dtype=jnp.float32) / half_h)
    invf_i = ROPE_THETA ** (-jnp.arange(half_i, dtype=jnp.float32) / half_i)
    invf = jnp.zeros((SUBLANES, LANES), jnp.float32)
    invf = invf.at[0].set(jnp.tile(invf_h, LANES // half_h)).at[1].set(jnp.tile(invf_i, LANES // half_i))
    tabs_prompt = _rope_tables(invf, SEQ, 0, 1)
    tabs_sample = _rope_tables(invf, SUBLANES, PAST_LEN, 0)

    mod = _ada_modulation(jnp.concatenate([c_prompt, c_sample], axis=0), _bf16(w_ada[0]), b_ada[0].reshape(1, -1))
    shift, scale, gate = mod[:, :D_MODEL], mod[:, D_MODEL:2 * D_MODEL], mod[:, 2 * D_MODEL:]

    (k_p, v_p, ki_p, kb, vt, kib, q, qi, kiwi, mpa, sgb, smb, conv_p, lru_p) = _prompt_projection(
        x_prompt, shift[:BATCH, None, :], scale[:BATCH, None, :], wts, tabs_prompt)
    o = _prompt_attention(q, qi, kiwi, kb, vt, kib)
    y_prompt = _output_projection(o, sgb, mpa, smb, x_prompt, gate[:BATCH, None, :], wts, OUT_ROWS)

    xs = x_sample[:, 0, :]
    (k_s, v_s, q_s, qi_s, kiwi_s, mpa_s, sgb_s, smb_s, conv_s, lru_s) = _sample_projection(
        xs, shift[BATCH:], scale[BATCH:], wts, tabs_sample, jnp.swapaxes(state_conv[0], 0, 1), state_rglru[0])
    pt_flat = page_table.reshape(-1)
    w_col = kiwi_s[:, IDX_DIM:IDX_DIM + N_IDX_HEADS, None]
    idx_pages = jnp.swapaxes(cache_idx_k[0], 1, 2)
    kv_pages = lambda t: t[0].reshape(-1, PAGE_SIZE * N_KV_HEADS, HEAD_DIM)
    scores = _sample_scores(pt_flat, idx_pages, qi_s.reshape(DEC_BATCH, N_IDX_HEADS, IDX_DIM), w_col,
                            kiwi_s[:, None, :])
    bias = _sample_select(scores.reshape(DEC_BATCH, (N_PAGES + 1) * PAGE_SIZE))
    bias = bias.reshape(DEC_BATCH, N_PAGES + 1, PAGE_SIZE)
    o_s = _sample_attention(
        pt_flat, kv_pages(cache_k), kv_pages(cache_v), bias, bias[:, N_PAGES:, :],
        q_s.reshape(DEC_BATCH, N_HEADS, HEAD_DIM), k_s.reshape(DEC_BATCH, N_KV_HEADS, HEAD_DIM),
        v_s.reshape(DEC_BATCH, N_KV_HEADS, HEAD_DIM))
    y_sample = _output_projection(
        o_s.reshape(1, DEC_BATCH, D_ATT), sgb_s[None], mpa_s[None], smb_s[None], xs[None], gate[None, BATCH:],
        wts, DEC_BATCH)

    kv_s = lambda t: t.reshape(1, DEC_BATCH, 1, N_KV_HEADS, HEAD_DIM)
    return (
        y_prompt, y_sample.reshape(DEC_BATCH, 1, D_MODEL),
        k_p[None], v_p[None], jnp.swapaxes(ki_p, 1, 2)[None], conv_p[None], lru_p.reshape(1, BATCH, D_RNN),
        kv_s(k_s), kv_s(v_s), kiwi_s[:, :IDX_DIM].reshape(1, DEC_BATCH, 1, IDX_DIM),
        jnp.swapaxes(conv_s, 0, 1)[None], lru_s[None],
    )
```

```python
import functools

import jax
import jax.numpy as jnp
import numpy as np
from jax import lax
from jax.experimental import pallas as pl
from jax.experimental.pallas import tpu as pltpu

D_MODEL = 1024
BATCH = 8
SEQ = 4096
DEC_BATCH = 32
PAST_LEN = 16384
PAGE_SIZE = 128
N_PAGES = PAST_LEN // PAGE_SIZE
D_RNN = D_MODEL
RNN_BLOCKS = 4
RNN_BLOCK_W = D_RNN // RNN_BLOCKS
CONV_W = 4
LRU_C = 8.0
N_HEADS = 8
HEAD_DIM = 128
N_KV_HEADS = 2
HEADS_PER_KV = N_HEADS // N_KV_HEADS
D_ATT = N_HEADS * HEAD_DIM
D_KV = N_KV_HEADS * HEAD_DIM
N_IDX_HEADS = 8
IDX_DIM = 64
D_IDX = N_IDX_HEADS * IDX_DIM
IDX_W_SCALE = (N_IDX_HEADS * IDX_DIM) ** -0.5
TOPK = 256
ROPE_THETA = 10000.0
EPS = 1e-6
SPLITS = (D_RNN, D_RNN, D_ATT, D_KV, D_KV, D_ATT, D_IDX, IDX_DIM, N_IDX_HEADS, D_MODEL, D_MODEL)

LANES = 128
SUBLANES = 8

C_XA, C_GA, C_Q, C_K, C_V, C_GB, C_QI, C_KW = (int(c) for c in np.cumsum((0,) + SPLITS[:7]))
W_PAD_AT = C_KW + IDX_DIM + N_IDX_HEADS
W_PAD = LANES - IDX_DIM - N_IDX_HEADS
C_MA = C_KW + LANES
C_MB = C_MA + D_MODEL
D_IN_PACKED = C_MB + D_MODEL
assert all(c % LANES == 0 for c in (C_XA, C_GA, C_Q, C_K, C_V, C_GB, C_QI, C_KW, C_MA, C_MB))

PROJ_ROWS = 256
Q_BLOCK = 512
KEY_CHUNK = 256
ATT_CHUNK = 512
PACKED_ROWS = 16
assert KEY_CHUNK >= TOPK and ATT_CHUNK % KEY_CHUNK == 0 and SEQ % ATT_CHUNK == 0 and ATT_CHUNK % Q_BLOCK == 0
assert PAGE_SIZE == HEAD_DIM == LANES
OUT_ROWS = 1024
SCORE_PAGES = 16
ATTN_PAGES = 32
assert N_PAGES % SCORE_PAGES == 0 and N_PAGES % ATTN_PAGES == 0
SOFTMAX_SCALE_LOG2E = (HEAD_DIM ** -0.5) * float(np.log2(np.e))
NEG_INF = float("-inf")
INT_MIN = -2 ** 31
KEY_NEG_INF = INT_MIN + 0x7FFFFF
V7X_VMEM_BYTES = 64 * 1024 * 1024
VMEM_LIMIT = V7X_VMEM_BYTES * 7 // 8
ATTN_VMEM_LIMIT = V7X_VMEM_BYTES * 15 // 16


def _sigmoid(x):
    return 1.0 / (1.0 + jnp.exp(-x))


def _silu(x):
    return x * _sigmoid(x)


def _dot(a, b):
    return jnp.dot(a, b, preferred_element_type=jnp.float32)


def _dot_nt(a, b):
    return lax.dot_general(a, b, (((1,), (1,)), ((), ())), preferred_element_type=jnp.float32)


def _bf16(x):
    return x.astype(jnp.bfloat16)


def _const_spec(shape, single=True):
    nd = len(shape)
    kwargs = {"pipeline_mode": pl.Buffered(1)} if single else {}
    return pl.BlockSpec(shape, lambda *_: (0,) * nd, **kwargs)


def _rope_kernel(invf_ref, cos_h_ref, sin_h_ref, cos_i_ref, sin_ia_ref, sin_ib_ref, *, pos0, pos_step, rows):
    r0 = pl.program_id(0) * rows
    row = lax.broadcasted_iota(jnp.int32, (rows, LANES), 0) + r0
    lane = lax.broadcasted_iota(jnp.int32, (rows, LANES), 1)
    pos = (pos0 + pos_step * row).astype(jnp.float32)
    ang_h = pos * invf_ref[0:1, :]
    ang_i = pos * invf_ref[1:2, :]
    cos_h_ref[...] = jnp.cos(ang_h)
    sh = jnp.sin(ang_h)
    sin_h_ref[...] = jnp.where(lane < HEAD_DIM // 2, -sh, sh)
    cos_i_ref[...] = jnp.cos(ang_i)
    si = jnp.sin(ang_i)
    first_half = (lane % IDX_DIM) < IDX_DIM // 2
    sin_ia_ref[...] = jnp.where(first_half, -si, 0.0)
    sin_ib_ref[...] = jnp.where(first_half, 0.0, si)


def _rope_tables(invf, n, pos0, pos_step):
    rows = min(n, 512)
    out = jax.ShapeDtypeStruct((n, LANES), jnp.float32)
    spec = pl.BlockSpec((rows, LANES), lambda i: (i, 0))
    return pl.pallas_call(
        functools.partial(_rope_kernel, pos0=pos0, pos_step=pos_step, rows=rows),
        grid=(n // rows,),
        in_specs=[pl.BlockSpec((SUBLANES, LANES), lambda i: (0, 0))],
        out_specs=[spec] * 5,
        out_shape=[out] * 5,
        name="rope_tables",
    )(invf)


def _rot_head(z, cos, sin_signed):
    return z * cos + pltpu.roll(z, HEAD_DIM // 2, 1) * sin_signed


def _rot_idx(z, cos, sin_a, sin_b):
    return z * cos + pltpu.roll(z, LANES - IDX_DIM // 2, 1) * sin_a + pltpu.roll(z, IDX_DIM // 2, 1) * sin_b


def _ada_kernel(c_ref, w_ref, b_ref, o_ref):
    o_ref[...] = _dot(_bf16(_silu(c_ref[...])), w_ref[...]) + b_ref[...]


def _ada_modulation(c_all, w_ada, b_ada):
    n = c_all.shape[0]
    return pl.pallas_call(
        _ada_kernel,
        grid=(1,),
        in_specs=[_const_spec((n, D_MODEL)), _const_spec((D_MODEL, 3 * D_MODEL)), _const_spec((1, 3 * D_MODEL))],
        out_specs=_const_spec((n, 3 * D_MODEL), single=False),
        out_shape=jax.ShapeDtypeStruct((n, 3 * D_MODEL), jnp.float32),
        compiler_params=pltpu.CompilerParams(vmem_limit_bytes=VMEM_LIMIT),
        name="ada_modulation",
    )(c_all, w_ada, b_ada)


def _modulated_norm(x, g, scale, shift):
    y = x * lax.rsqrt(jnp.mean(x * x, axis=-1, keepdims=True) + EPS) * g
    return _bf16(y * (1.0 + scale) + shift)


def _lru_gates(xc, w_ra_ref, b_ra, w_rx_ref, b_rx, lam):
    xcb = _bf16(xc)
    r_parts, i_parts = [], []
    for n in range(RNN_BLOCKS):
        sl = slice(n * RNN_BLOCK_W, (n + 1) * RNN_BLOCK_W)
        r_parts.append(_dot(xcb[:, sl], w_ra_ref[n]))
        i_parts.append(_dot(xcb[:, sl], w_rx_ref[n]))
    r = _sigmoid(jnp.concatenate(r_parts, axis=1) + b_ra)
    i = _sigmoid(jnp.concatenate(i_parts, axis=1) + b_rx)
    neg_lam = -lam
    softplus = jnp.maximum(neg_lam, 0.0) + jnp.log1p(jnp.exp(-jnp.abs(neg_lam)))
    log_a = (-LRU_C) * r * softplus
    a = jnp.exp(log_a)
    u = jnp.sqrt(-jnp.tanh(log_a) * (a * a + 1.0)) * (i * xc)
    return a, u


def _idx_key_slab(z_kw, g, b, cos_i, sin_ia, sin_ib):
    lane = lax.broadcasted_iota(jnp.int32, z_kw.shape, 1)
    is_key = lane < IDX_DIM
    mu = jnp.sum(jnp.where(is_key, z_kw, 0.0), axis=-1, keepdims=True) * (1.0 / IDX_DIM)
    d = jnp.where(is_key, z_kw - mu, 0.0)
    var = jnp.sum(d * d, axis=-1, keepdims=True) * (1.0 / IDX_DIM)
    y = d * lax.rsqrt(var + EPS) * g + b
    key = _rot_idx(y, cos_i, sin_ia, sin_ib)
    is_w = jnp.logical_and(lane >= IDX_DIM, lane < IDX_DIM + N_IDX_HEADS)
    return key + jnp.where(is_w, z_kw * IDX_W_SCALE, 0.0)


def _proj_kernel(x_ref, shift_ref, scale_ref, gn_ref, w_in_ref, wconv_ref, bconv_ref, w_ra_ref, b_ra_ref,
                 w_rx_ref, b_rx_ref, lam_ref, ig_ref, ib_ref, w_pa_ref,
                 cos_h_ref, sin_h_ref, cos_i_ref, sin_ia_ref, sin_ib_ref,
                 k_ref, v_ref, ki_ref, kb_ref, vt_ref, kib_ref, q_ref, qi_ref, kiwi_ref,
                 mpa_ref, sgb_ref, smb_ref, conv_ref, lru_ref,
                 xa_ext, a_s, u_s, h_carry, tail_s, ga_s, ma_s):
    ts = PROJ_ROWS

    @pl.when(pl.program_id(1) == 0)
    def _():
        tail_s[...] = jnp.zeros(tail_s.shape, jnp.float32)
        h_carry[...] = jnp.zeros(h_carry.shape, jnp.float32)

    xn = _modulated_norm(x_ref[...], gn_ref[...], scale_ref[...], shift_ref[...])

    xa_ext[0:SUBLANES, :] = tail_s[...]
    xa_ext[SUBLANES:SUBLANES + ts, :] = _dot(xn, w_in_ref[:, C_XA:C_XA + D_RNN])
    sgb_ref[...] = _bf16(_silu(_dot(xn, w_in_ref[:, C_GB:C_GB + D_ATT])))
    smb_ref[...] = _bf16(_sigmoid(_dot(xn, w_in_ref[:, C_MB:C_MB + D_MODEL])))
    xc = bconv_ref[...]
    for t in range(CONV_W):
        off = SUBLANES - (CONV_W - 1) + t
        xc = xc + xa_ext[off:off + ts, :] * wconv_ref[t:t + 1, :]
    conv_ref[...] = xa_ext[ts + SUBLANES - (CONV_W - 1):ts + SUBLANES, :]
    tail_s[...] = xa_ext[ts:ts + SUBLANES, :]

    a, u = _lru_gates(xc, w_ra_ref, b_ra_ref[...], w_rx_ref, b_rx_ref[...], lam_ref[...])
    a_s[...] = a
    u_s[...] = u
    row = lax.broadcasted_iota(jnp.int32, (SUBLANES, D_RNN), 0)
    cos_h, sin_h = cos_h_ref[...], sin_h_ref[...]
    cos_i, sin_ia, sin_ib = cos_i_ref[...], sin_ia_ref[...], sin_ib_ref[...]

    def proj_gate_a():
        ga_s[...] = _silu(_dot(xn, w_in_ref[:, C_GA:C_GA + D_RNN]))

    def proj_merge_a():
        ma_s[...] = _sigmoid(_dot(xn, w_in_ref[:, C_MA:C_MA + D_MODEL]))

    def proj_q(lo, hi):
        def run():
            zq = _dot(xn, w_in_ref[:, C_Q + lo * HEAD_DIM:C_Q + hi * HEAD_DIM])
            for h in range(hi - lo):
                rot = _rot_head(zq[:, h * HEAD_DIM:(h + 1) * HEAD_DIM], cos_h, sin_h)
                q_ref[:, (lo + h) * HEAD_DIM:(lo + h + 1) * HEAD_DIM] = _bf16(rot * SOFTMAX_SCALE_LOG2E)
        return run

    def proj_kv():
        zk = _dot(xn, w_in_ref[:, C_K:C_K + D_KV])
        for g in range(N_KV_HEADS):
            sl = slice(g * HEAD_DIM, (g + 1) * HEAD_DIM)
            kr = _rot_head(zk[:, sl], cos_h, sin_h)
            k_ref[:, g, :] = kr
            kb_ref[:, sl] = _bf16(kr)
        zv = _dot(xn, w_in_ref[:, C_V:C_V + D_KV])
        for g in range(N_KV_HEADS):
            v_ref[:, g, :] = zv[:, g * HEAD_DIM:(g + 1) * HEAD_DIM]
        vt = _bf16(zv.T)
        for c in range(ts // KEY_CHUNK):
            vt_ref[c] = vt[:, c * KEY_CHUNK:(c + 1) * KEY_CHUNK]

    def proj_idx():
        zqi = _dot(xn, w_in_ref[:, C_QI:C_QI + D_IDX])
        for p in range(D_IDX // LANES):
            sl = slice(p * LANES, (p + 1) * LANES)
            qi_ref[:, sl] = _bf16(_rot_idx(zqi[:, sl], cos_i, sin_ia, sin_ib))
        slab = _idx_key_slab(_dot(xn, w_in_ref[:, C_KW:C_KW + LANES]), ig_ref[...], ib_ref[...],
                             cos_i, sin_ia, sin_ib)
        kiwi_ref[...] = slab
        ki_ref[...] = slab.T[:IDX_DIM, :]
        key_even = jnp.where(lax.broadcasted_iota(jnp.int32, slab.shape, 1) < IDX_DIM, slab, 0.0)
        kib_ref[:, 0:LANES] = _bf16(key_even)
        kib_ref[:, LANES:2 * LANES] = _bf16(pltpu.roll(key_even, IDX_DIM, 1))

    scan_work = [proj_gate_a, proj_merge_a, proj_q(0, N_HEADS // 2), proj_q(N_HEADS // 2, N_HEADS), proj_kv, proj_idx]

    hc = h_carry[...]
    n_groups = ts // SUBLANES
    per_chunk = n_groups // (len(scan_work) + 2)
    for g in range(n_groups):
        rows = slice(g * SUBLANES, (g + 1) * SUBLANES)
        a8 = a_s[rows, :]
        u8 = u_s[rows, :]
        for d in (1, 2, 4):
            keep = row >= d
            u8 = jnp.where(keep, a8 * pltpu.roll(u8, d, 0) + u8, u8)
            a8 = jnp.where(keep, a8 * pltpu.roll(a8, d, 0), a8)
        h8 = a8 * hc + u8
        u_s[rows, :] = h8
        hc = h8[SUBLANES - 1:SUBLANES, :]
        if g % per_chunk == per_chunk - 1 and g // per_chunk < len(scan_work):
            scan_work[g // per_chunk]()
    h_carry[...] = hc
    lru_ref[...] = hc

    ya = _dot(_bf16(u_s[...] * ga_s[...]), w_pa_ref[...])
    mpa_ref[...] = _bf16(ma_s[...] * ya)


def _prompt_projection(x, shift, scale, wts, tabs):
    ts = PROJ_ROWS
    nt = SEQ // ts
    f32, bf16 = jnp.float32, jnp.bfloat16
    row_spec = lambda w: pl.BlockSpec((None, ts, w), lambda b, j: (b, j, 0))
    bvec_spec = pl.BlockSpec((None, 1, D_MODEL), lambda b, j: (b, 0, 0))
    kv_spec = pl.BlockSpec((None, ts, N_KV_HEADS, HEAD_DIM), lambda b, j: (b, j, 0, 0))
    tab_spec = pl.BlockSpec((ts, LANES), lambda b, j: (j, 0))
    in_specs = [
        row_spec(D_MODEL), bvec_spec, bvec_spec, _const_spec((1, D_MODEL)),
        _const_spec((D_MODEL, D_IN_PACKED)), _const_spec((CONV_W, D_RNN)), _const_spec((1, D_RNN)),
        _const_spec((RNN_BLOCKS, RNN_BLOCK_W, RNN_BLOCK_W)), _const_spec((1, D_RNN)),
        _const_spec((RNN_BLOCKS, RNN_BLOCK_W, RNN_BLOCK_W)), _const_spec((1, D_RNN)), _const_spec((1, D_RNN)),
        _const_spec((1, LANES)), _const_spec((1, LANES)), _const_spec((D_RNN, D_MODEL)),
    ] + [tab_spec] * 5
    out_shape = [
        jax.ShapeDtypeStruct((BATCH, SEQ, N_KV_HEADS, HEAD_DIM), f32),
        jax.ShapeDtypeStruct((BATCH, SEQ, N_KV_HEADS, HEAD_DIM), f32),
        jax.ShapeDtypeStruct((BATCH, IDX_DIM, SEQ), f32),
        jax.ShapeDtypeStruct((BATCH, SEQ, D_KV), bf16),
        jax.ShapeDtypeStruct((BATCH, SEQ // KEY_CHUNK, D_KV, KEY_CHUNK), bf16),
        jax.ShapeDtypeStruct((BATCH, SEQ, 2 * LANES), bf16),
        jax.ShapeDtypeStruct((BATCH, SEQ, D_ATT), bf16),
        jax.ShapeDtypeStruct((BATCH, SEQ, D_IDX), bf16),
        jax.ShapeDtypeStruct((BATCH, SEQ, LANES), f32),
        jax.ShapeDtypeStruct((BATCH, SEQ, D_MODEL), bf16),
        jax.ShapeDtypeStruct((BATCH, SEQ, D_ATT), bf16),
        jax.ShapeDtypeStruct((BATCH, SEQ, D_MODEL), bf16),
        jax.ShapeDtypeStruct((BATCH, CONV_W - 1, D_RNN), f32),
        jax.ShapeDtypeStruct((BATCH, 1, D_RNN), f32),
    ]
    out_specs = [
        kv_spec, kv_spec, pl.BlockSpec((None, IDX_DIM, ts), lambda b, j: (b, 0, j)), row_spec(D_KV),
        pl.BlockSpec((None, ts // KEY_CHUNK, D_KV, KEY_CHUNK), lambda b, j: (b, j, 0, 0)),
        row_spec(2 * LANES), row_spec(D_ATT), row_spec(D_IDX), row_spec(LANES),
        row_spec(D_MODEL), row_spec(D_ATT), row_spec(D_MODEL),
        pl.BlockSpec((None, CONV_W - 1, D_RNN), lambda b, j: (b, 0, 0)),
        pl.BlockSpec((None, 1, D_RNN), lambda b, j: (b, 0, 0)),
    ]
    scratch = [
        pltpu.VMEM((ts + SUBLANES, D_RNN), f32), pltpu.VMEM((ts, D_RNN), f32),
        pltpu.VMEM((ts, D_RNN), f32), pltpu.VMEM((1, D_RNN), f32), pltpu.VMEM((SUBLANES, D_RNN), f32),
        pltpu.VMEM((ts, D_RNN), f32), pltpu.VMEM((ts, D_MODEL), f32),
    ]
    return pl.pallas_call(
        _proj_kernel,
        grid=(BATCH, nt),
        in_specs=in_specs, out_specs=out_specs, out_shape=out_shape, scratch_shapes=scratch,
        compiler_params=pltpu.CompilerParams(
            dimension_semantics=("arbitrary", "arbitrary"), vmem_limit_bytes=VMEM_LIMIT),
        name="prompt_projection",
    )(x, shift, scale, wts["g_norm"], wts["w_in"], wts["w_conv"], wts["b_conv"], wts["w_ra"], wts["b_ra"],
      wts["w_rx"], wts["b_rx"], wts["lam"], wts["idx_g"], wts["idx_b"], wts["w_pa"], *tabs)


def _threshold_value(t_unsigned):
    key = jnp.maximum(jnp.bitwise_xor(t_unsigned, jnp.int32(INT_MIN)), jnp.int32(KEY_NEG_INF))
    bits = key ^ jnp.bitwise_and(jnp.right_shift(key, 31), jnp.int32(0x7FFFFFFF))
    return pltpu.bitcast(bits, jnp.float32)


def _greedy_bits(count_ge, n_bits, shape, count_all):
    def bit_step(b, carry):
        t, cnt = carry
        cand = jnp.bitwise_or(t, jnp.left_shift(jnp.int32(1), n_bits - 1 - b))
        c = count_ge(cand)
        ok = c >= TOPK
        return jnp.where(ok, cand, t), jnp.where(ok, c, cnt)
    return lax.fori_loop(0, n_bits, bit_step, (jnp.zeros(shape, jnp.int32), count_all))


def _fold_rows(x, rows, op=jnp.add, chains=4):
    parts = [x[r:r + rows] for r in range(0, x.shape[0], rows)]
    acc = parts[:chains]
    for k, part in enumerate(parts[chains:]):
        acc[k % len(acc)] = op(acc[k % len(acc)], part)
    while len(acc) > 1:
        acc = [op(acc[k], acc[k + 1]) for k in range(0, len(acc) - 1, 2)] + ([acc[-1]] if len(acc) % 2 else [])
    return acc[0]


def _tie_cutoff(count_tie_below, need, n_bits, shape):
    def bit_step(b, x):
        cand = jnp.bitwise_or(x, jnp.left_shift(jnp.int32(1), n_bits - 1 - b))
        return jnp.where(count_tie_below(cand) < need, cand, x)
    return lax.fori_loop(0, n_bits, bit_step, jnp.zeros(shape, jnp.int32))


def _attn_kernel(q_ref, qi_ref, kiwi_ref, kb_ref, vt_ref, kib_ref, o_ref,
                 score_s, hi_s, bias_s, s_scr, acc_s):
    i = pl.program_id(1)
    t0 = i * Q_BLOCK
    n_steps = (t0 + Q_BLOCK + ATT_CHUNK - 1) // ATT_CHUNK
    step_iota = lax.broadcasted_iota(jnp.int32, (ATT_CHUNK, Q_BLOCK), 0)
    sub_iota = lax.broadcasted_iota(jnp.int32, (KEY_CHUNK, Q_BLOCK), 0)
    q_pos = t0 + lax.broadcasted_iota(jnp.int32, (1, Q_BLOCK), 1)
    lane_shape = (1, Q_BLOCK)

    def step_rows(c):
        return pl.ds(pl.multiple_of(c * ATT_CHUNK, ATT_CHUNK), ATT_CHUNK)

    w_t = kiwi_ref[...].T[IDX_DIM:IDX_DIM + N_IDX_HEADS, :]
    qi = qi_ref[...]
    n_pairs = D_IDX // LANES
    qi_rows = jnp.concatenate([qi[:, p * LANES:(p + 1) * LANES] for p in range(n_pairs)], axis=0)

    def score_step(c, causal):
        for sub in range(ATT_CHUNK // KEY_CHUNK):
            r0 = pl.multiple_of(c * ATT_CHUNK + sub * KEY_CHUNK, KEY_CHUNK)
            rows = pl.ds(r0, KEY_CHUNK)
            s_par = [_dot_nt(kib_ref[rows, par * LANES:(par + 1) * LANES], qi_rows) for par in range(2)]
            score = jnp.zeros((KEY_CHUNK, Q_BLOCK), jnp.float32)
            for h in range(N_IDX_HEADS):
                s_h = s_par[h % 2][:, (h // 2) * Q_BLOCK:(h // 2 + 1) * Q_BLOCK]
                score = score + jnp.maximum(s_h, 0.0) * w_t[h:h + 1, :]
            if causal:
                score = jnp.where((r0 + sub_iota) <= q_pos, score, NEG_INF)
            score_s[rows, :] = score
            hi_s[rows, :] = _bf16(score)

    def early_score_step(c, carry):
        score_step(c, False)
        return carry

    lax.fori_loop(0, n_steps - 1, early_score_step, 0)
    score_step(n_steps - 1, True)

    def sum_steps(one_step, zero):
        acc = lax.fori_loop(0, n_steps // 2, lambda p, a: a + one_step(2 * p) + one_step(2 * p + 1), zero)
        return lax.cond(n_steps % 2 == 1, lambda: acc + one_step(n_steps - 1), lambda: acc)

    def count_where(pred):
        def one_step(c):
            sel = pred(score_s[step_rows(c), :], c * ATT_CHUNK + step_iota)
            parts = [_fold_rows(jnp.where(sel[:, t * LANES:(t + 1) * LANES], 1.0, 0.0), SUBLANES, chains=2)
                     for t in range(Q_BLOCK // LANES)]
            return jnp.concatenate(parts, axis=1)
        acc = sum_steps(one_step, jnp.zeros((SUBLANES, Q_BLOCK), jnp.float32))
        return jnp.sum(acc, axis=0, keepdims=True)

    def count_rounded_ge(t16):
        cand = _threshold_value(jnp.left_shift(t16, 16)).astype(jnp.bfloat16)
        def one_step(c):
            one = jnp.where(hi_s[step_rows(c), :] >= cand, jnp.bfloat16(1), jnp.bfloat16(0))
            return _fold_rows(one, PACKED_ROWS, chains=2)
        acc = sum_steps(one_step, jnp.zeros((PACKED_ROWS, Q_BLOCK), jnp.bfloat16))
        return jnp.sum(acc.astype(jnp.float32), axis=0, keepdims=True)

    count_all = jnp.full(lane_shape, 1.0, jnp.float32) * (n_steps * ATT_CHUNK).astype(jnp.float32)
    t1, cnt_t1 = _greedy_bits(count_rounded_ge, 16, lane_shape, count_all)

    base = jnp.left_shift(jnp.maximum(t1 - 1, 0), 16)

    def count_ge(offset):
        cand = _threshold_value(base + offset)
        return count_where(lambda blk, idx: blk >= cand)

    off, cnt_ge = _greedy_bits(count_ge, 17, lane_shape, cnt_t1)
    thr = _threshold_value(base + off)

    def write_bias(select):
        def step(c, causal):
            blk = score_s[step_rows(c), :]
            idx = c * ATT_CHUNK + step_iota
            sel = select(blk, idx)
            if causal:
                sel = jnp.logical_and(sel, idx <= q_pos)
            bias_s[step_rows(c), :] = jnp.where(sel, 0.0, NEG_INF)

        def early_step(c, carry):
            step(c, False)
            return carry

        lax.fori_loop(0, n_steps - 1, early_step, 0)
        step(n_steps - 1, True)

    has_tie = jnp.max(jnp.where(cnt_ge > TOPK, 1.0, 0.0)) > 0.5

    @pl.when(jnp.logical_not(has_tie))
    def _():
        write_bias(lambda blk, idx: blk >= thr)

    @pl.when(has_tie)
    def _():
        need = TOPK - count_where(lambda blk, idx: blk > thr)
        cutoff = _tie_cutoff(
            lambda x: count_where(lambda blk, idx: jnp.logical_and(blk == thr, idx < x)),
            need, int(np.log2(SEQ)), lane_shape)
        write_bias(lambda blk, idx: jnp.logical_or(blk > thr, jnp.logical_and(blk == thr, idx <= cutoff)))

    q = q_ref[...]
    n_lanes = HEADS_PER_KV * Q_BLOCK
    q_rows = [
        jnp.concatenate([q[:, (g * HEADS_PER_KV + h) * HEAD_DIM:(g * HEADS_PER_KV + h + 1) * HEAD_DIM]
                         for h in range(HEADS_PER_KV)], axis=0)
        for g in range(N_KV_HEADS)]
    acc_s[...] = jnp.zeros(acc_s.shape, jnp.float32)
    subs = ATT_CHUNK // KEY_CHUNK

    ones_rows = jnp.ones((PACKED_ROWS, KEY_CHUNK), jnp.bfloat16)

    d_sl = [slice(g * HEAD_DIM, (g + 1) * HEAD_DIM) for g in range(N_KV_HEADS)]

    def logits(c, slot):
        maxima = []
        for g in range(N_KV_HEADS):
            b = bias_s[step_rows(c), :]
            x = _dot_nt(kb_ref[step_rows(c), d_sl[g]], q_rows[g]) + jnp.concatenate([b] * HEADS_PER_KV, axis=1)
            s_scr[slot, g] = x
            maxima.append(jnp.max(_fold_rows(x, SUBLANES, jnp.maximum), axis=0, keepdims=True))
        return tuple(maxima)

    def accumulate(c, slot, m_old, m_step):
        m_out = []
        for g in range(N_KV_HEADS):
            m_new = jnp.maximum(m_old[g], m_step[g])
            m_safe = jnp.where(m_new == NEG_INF, 0.0, m_new)
            alpha = jnp.exp2(m_old[g] - m_safe)
            pv = None
            for sub in range(subs):
                r = sub * KEY_CHUNK
                pb = _bf16(jnp.exp2(s_scr[slot, g, r:r + KEY_CHUNK, :] - m_safe))
                lhs = jnp.concatenate([vt_ref[c * subs + sub, d_sl[g], :], ones_rows], axis=0)
                part = _dot(lhs, pb)
                pv = part if pv is None else pv + part
            acc_s[g] = acc_s[g] * alpha + pv
            m_out.append(m_new)
        return tuple(m_out)

    def attend_pair(p, carry):
        m_run, m_even = carry
        c = 2 * p
        m_odd = logits(c + 1, 1)
        m_run = accumulate(c, 0, m_run, m_even)
        m_even = logits(jnp.minimum(c + 2, n_steps - 1), 0)
        m_run = accumulate(c + 1, 1, m_run, m_odd)
        return m_run, m_even

    m_init = tuple(jnp.full((1, n_lanes), NEG_INF, jnp.float32) for _ in range(N_KV_HEADS))
    m_run, m_even = lax.fori_loop(0, n_steps // 2, attend_pair, (m_init, logits(0, 0)))

    @pl.when(n_steps % 2 == 1)
    def _():
        accumulate(n_steps - 1, 0, m_run, m_even)

    for g in range(N_KV_HEADS):
        o_t = acc_s[g, 0:HEAD_DIM, :] / acc_s[g, HEAD_DIM:HEAD_DIM + 1, :]
        for h in range(HEADS_PER_KV):
            col = (g * HEADS_PER_KV + h) * HEAD_DIM
            o_ref[:, col:col + HEAD_DIM] = _bf16(o_t[:, h * Q_BLOCK:(h + 1) * Q_BLOCK].T)


def _prompt_attention(q, qi, kiwi, kb, vt, kib):
    nq = SEQ // Q_BLOCK
    blk = lambda w: pl.BlockSpec((None, Q_BLOCK, w), lambda b, i: (b, i, 0))
    return pl.pallas_call(
        _attn_kernel,
        grid=(BATCH, nq),
        in_specs=[
            blk(D_ATT), blk(D_IDX), blk(LANES),
            pl.BlockSpec((None, SEQ, D_KV), lambda b, i: (b, 0, 0), pipeline_mode=pl.Buffered(1)),
            pl.BlockSpec((None, SEQ // KEY_CHUNK, D_KV, KEY_CHUNK), lambda b, i: (b, 0, 0, 0),
                         pipeline_mode=pl.Buffered(1)),
            pl.BlockSpec((None, SEQ, 2 * LANES), lambda b, i: (b, 0, 0), pipeline_mode=pl.Buffered(1)),
        ],
        out_specs=blk(D_ATT),
        out_shape=jax.ShapeDtypeStruct((BATCH, SEQ, D_ATT), jnp.bfloat16),
        scratch_shapes=[
            pltpu.VMEM((SEQ, Q_BLOCK), jnp.float32),
            pltpu.VMEM((SEQ, Q_BLOCK), jnp.bfloat16),
            pltpu.VMEM((SEQ, Q_BLOCK), jnp.float32),
            pltpu.VMEM((2, N_KV_HEADS, ATT_CHUNK, HEADS_PER_KV * Q_BLOCK), jnp.float32),
            pltpu.VMEM((N_KV_HEADS, HEAD_DIM + PACKED_ROWS, HEADS_PER_KV * Q_BLOCK), jnp.float32),
        ],
        compiler_params=pltpu.CompilerParams(
            dimension_semantics=("arbitrary", "arbitrary"), vmem_limit_bytes=ATTN_VMEM_LIMIT),
        name="prompt_attention",
    )(q, qi, kiwi, kb, vt, kib)


def _out_kernel(o_ref, sgb_ref, mpa_ref, smb_ref, x_ref, gate_ref, w_pb_ref, w_o_ref, gf_ref, y_ref):
    f32 = jnp.float32
    yb = _dot(_bf16(o_ref[...].astype(f32) * sgb_ref[...].astype(f32)), w_pb_ref[...])
    m = mpa_ref[...].astype(f32) + smb_ref[...].astype(f32) * yb
    r = x_ref[...] + gate_ref[...] * _dot(_bf16(m), w_o_ref[...])
    y_ref[...] = r * lax.rsqrt(jnp.mean(r * r, axis=-1, keepdims=True) + EPS) * gf_ref[...]


def _output_projection(o, sgb, mpa, smb, x, gate, wts, rows):
    ng, nr, _ = x.shape
    row_spec = pl.BlockSpec((None, rows, D_MODEL), lambda b, j: (b, j, 0))
    if gate.shape[1] == 1:
        gate_spec = pl.BlockSpec((None, 1, D_MODEL), lambda b, j: (b, 0, 0))
    else:
        gate_spec = row_spec
    return pl.pallas_call(
        _out_kernel,
        grid=(ng, nr // rows),
        in_specs=[row_spec] * 5 + [gate_spec, _const_spec((D_ATT, D_MODEL)), _const_spec((D_MODEL, D_MODEL)),
                                   _const_spec((1, D_MODEL))],
        out_specs=row_spec,
        out_shape=jax.ShapeDtypeStruct(x.shape, jnp.float32),
        compiler_params=pltpu.CompilerParams(
            dimension_semantics=("arbitrary", "arbitrary"), vmem_limit_bytes=VMEM_LIMIT),
        name="output_projection",
    )(o, sgb, mpa, smb, x, gate, wts["w_pb"], wts["w_o"], wts["g_final"])


def _sample_proj_kernel(x_ref, shift_ref, scale_ref, gn_ref, w_in_ref, wconv_ref, bconv_ref, w_ra_ref, b_ra_ref,
                        w_rx_ref, b_rx_ref, lam_ref, ig_ref, ib_ref, w_pa_ref,
                        cos_h_ref, sin_h_ref, cos_i_ref, sin_ia_ref, sin_ib_ref, buf_ref, h0_ref,
                        k_ref, v_ref, q_ref, qi_ref, kiwi_ref, mpa_ref, sgb_ref, smb_ref, conv_ref, lru_ref):
    xn = _modulated_norm(x_ref[...], gn_ref[...], scale_ref[...], shift_ref[...])
    xa = _dot(xn, w_in_ref[:, C_XA:C_XA + D_RNN])
    xc = bconv_ref[...]
    for t in range(CONV_W - 1):
        xc = xc + buf_ref[t] * wconv_ref[t:t + 1, :]
        if t > 0:
            conv_ref[t - 1] = buf_ref[t]
    xc = xc + xa * wconv_ref[CONV_W - 1:CONV_W, :]
    conv_ref[CONV_W - 2] = xa
    a, u = _lru_gates(xc, w_ra_ref, b_ra_ref[...], w_rx_ref, b_rx_ref[...], lam_ref[...])
    h = a * h0_ref[...] + u
    lru_ref[...] = h
    ga = _dot(xn, w_in_ref[:, C_GA:C_GA + D_RNN])
    ya = _dot(_bf16(h * _silu(ga)), w_pa_ref[...])
    mpa_ref[...] = _sigmoid(_dot(xn, w_in_ref[:, C_MA:C_MA + D_MODEL])) * ya
    sgb_ref[...] = _silu(_dot(xn, w_in_ref[:, C_GB:C_GB + D_ATT]))
    smb_ref[...] = _sigmoid(_dot(xn, w_in_ref[:, C_MB:C_MB + D_MODEL]))

    cos_h, sin_h = cos_h_ref[0:1, :], sin_h_ref[0:1, :]
    cos_i, sin_ia, sin_ib = cos_i_ref[0:1, :], sin_ia_ref[0:1, :], sin_ib_ref[0:1, :]
    zq = _dot(xn, w_in_ref[:, C_Q:C_Q + D_ATT])
    for hd in range(N_HEADS):
        sl = slice(hd * HEAD_DIM, (hd + 1) * HEAD_DIM)
        q_ref[:, sl] = _bf16(_rot_head(zq[:, sl], cos_h, sin_h))
    zk = _dot(xn, w_in_ref[:, C_K:C_K + D_KV])
    for g in range(N_KV_HEADS):
        sl = slice(g * HEAD_DIM, (g + 1) * HEAD_DIM)
        k_ref[:, sl] = _rot_head(zk[:, sl], cos_h, sin_h)
    v_ref[...] = _dot(xn, w_in_ref[:, C_V:C_V + D_KV])
    zqi = _dot(xn, w_in_ref[:, C_QI:C_QI + D_IDX])
    for p in range(D_IDX // LANES):
        sl = slice(p * LANES, (p + 1) * LANES)
        qi_ref[:, sl] = _bf16(_rot_idx(zqi[:, sl], cos_i, sin_ia, sin_ib))
    kiwi_ref[...] = _idx_key_slab(_dot(xn, w_in_ref[:, C_KW:C_KW + LANES]), ig_ref[...], ib_ref[...],
                                  cos_i, sin_ia, sin_ib)


def _sample_projection(x, shift, scale, wts, tabs, buf_t, h0):
    n = DEC_BATCH
    f32, bf16 = jnp.float32, jnp.bfloat16
    in_specs = [
        _const_spec((n, D_MODEL)), _const_spec((n, D_MODEL)), _const_spec((n, D_MODEL)), _const_spec((1, D_MODEL)),
        _const_spec((D_MODEL, D_IN_PACKED)), _const_spec((CONV_W, D_RNN)), _const_spec((1, D_RNN)),
        _const_spec((RNN_BLOCKS, RNN_BLOCK_W, RNN_BLOCK_W)), _const_spec((1, D_RNN)),
        _const_spec((RNN_BLOCKS, RNN_BLOCK_W, RNN_BLOCK_W)), _const_spec((1, D_RNN)), _const_spec((1, D_RNN)),
        _const_spec((1, LANES)), _const_spec((1, LANES)), _const_spec((D_RNN, D_MODEL)),
    ] + [_const_spec((SUBLANES, LANES))] * 5 + [_const_spec((CONV_W - 1, n, D_RNN)), _const_spec((n, D_RNN))]
    shapes = [
        ((n, D_KV), f32), ((n, D_KV), f32), ((n, D_ATT), bf16), ((n, D_IDX), bf16), ((n, LANES), f32),
        ((n, D_MODEL), f32), ((n, D_ATT), f32), ((n, D_MODEL), f32), ((CONV_W - 1, n, D_RNN), f32), ((n, D_RNN), f32),
    ]
    return pl.pallas_call(
        _sample_proj_kernel,
        grid=(1,),
        in_specs=in_specs,
        out_specs=[_const_spec(s, single=False) for s, _ in shapes],
        out_shape=[jax.ShapeDtypeStruct(s, d) for s, d in shapes],
        compiler_params=pltpu.CompilerParams(vmem_limit_bytes=VMEM_LIMIT),
        name="sample_projection",
    )(x, shift, scale, wts["g_norm"], wts["w_in"], wts["w_conv"], wts["b_conv"], wts["w_ra"], wts["b_ra"],
      wts["w_rx"], wts["b_rx"], wts["lam"], wts["idx_g"], wts["idx_b"], wts["w_pa"], *tabs, buf_t, h0)


def _sample_score_kernel(pt_ref, qi_ref, w_ref, kinew_ref, idx_hbm, o_ref, buf, sem):
    b = pl.program_id(0)
    slot = b % 2

    def page_copy(sample, p, sl):
        return pltpu.make_async_copy(idx_hbm.at[pt_ref[sample * N_PAGES + p]], buf.at[sl, p], sem.at[sl])

    def start_sample(sample, sl):
        def body(p, carry):
            page_copy(sample, p, sl).start()
            return carry
        lax.fori_loop(0, N_PAGES, body, 0)

    @pl.when(b == 0)
    def _():
        start_sample(0, 0)

    @pl.when(b + 1 < pl.num_programs(0))
    def _():
        start_sample(b + 1, 1 - slot)

    def wait_page(p, carry):
        page_copy(b, p, slot).wait()
        return carry

    lax.fori_loop(0, N_PAGES, wait_page, 0)

    qi = qi_ref[...]
    w = w_ref[...]

    def score_pages(i, carry):
        p0 = i * SCORE_PAGES
        kt = _bf16(jnp.concatenate([buf[slot, p0 + t] for t in range(SCORE_PAGES)], axis=1))
        s = _dot(qi, kt)
        score = jnp.sum(jnp.maximum(s, 0.0) * w, axis=0, keepdims=True)
        for t in range(SCORE_PAGES):
            o_ref[pl.ds(p0 + t, 1), :] = score[:, t * PAGE_SIZE:(t + 1) * PAGE_SIZE]
        return carry

    for i in range(N_PAGES // SCORE_PAGES):
        score_pages(i, 0)

    k_self = _bf16(kinew_ref[...][:, :IDX_DIM]).astype(jnp.float32)
    s_self = jnp.sum(qi.astype(jnp.float32) * k_self, axis=1, keepdims=True)
    score_self = jnp.sum(jnp.maximum(s_self, 0.0) * w, axis=0, keepdims=True)
    lane = lax.broadcasted_iota(jnp.int32, (1, PAGE_SIZE), 1)
    o_ref[N_PAGES:N_PAGES + 1, :] = jnp.where(lane == 0, score_self, NEG_INF)


def _sample_scores(page_table_flat, idx_pages, qi3, w_col, kiwi3):
    per_sample = lambda r, w: pl.BlockSpec((None, r, w), lambda b, pt: (b, 0, 0))
    return pl.pallas_call(
        _sample_score_kernel,
        grid_spec=pltpu.PrefetchScalarGridSpec(
            num_scalar_prefetch=1,
            grid=(DEC_BATCH,),
            in_specs=[per_sample(N_IDX_HEADS, IDX_DIM), per_sample(N_IDX_HEADS, 1), per_sample(1, LANES),
                      pl.BlockSpec(memory_space=pl.ANY)],
            out_specs=per_sample(N_PAGES + 1, PAGE_SIZE),
            scratch_shapes=[pltpu.VMEM((2, N_PAGES, IDX_DIM, PAGE_SIZE), jnp.float32),
                            pltpu.SemaphoreType.DMA((2,))],
        ),
        out_shape=jax.ShapeDtypeStruct((DEC_BATCH, N_PAGES + 1, PAGE_SIZE), jnp.float32),
        compiler_params=pltpu.CompilerParams(dimension_semantics=("arbitrary",), vmem_limit_bytes=VMEM_LIMIT),
        name="sample_scores",
    )(page_table_flat, qi3, w_col, kiwi3, idx_pages)


def _sample_select_kernel(score_ref, bias_ref):
    keys = score_ref[...]
    idx = lax.broadcasted_iota(jnp.int32, keys.shape, 1)
    col_shape = (keys.shape[0], 1)
    count = lambda pred: jnp.sum(jnp.where(pred, 1.0, 0.0), axis=1, keepdims=True)
    total = jnp.full(col_shape, float(keys.shape[1]), jnp.float32)
    thr_u, _ = _greedy_bits(lambda t: count(keys >= _threshold_value(t)), 32, col_shape, total)
    thr = _threshold_value(thr_u)
    need = TOPK - count(keys > thr)
    tie = keys == thr
    n_bits = int(np.ceil(np.log2(keys.shape[1])))
    cutoff = _tie_cutoff(lambda x: count(jnp.logical_and(tie, idx < x)), need, n_bits, col_shape)
    sel = jnp.logical_or(keys > thr, jnp.logical_and(tie, idx <= cutoff))
    bias_ref[...] = jnp.where(sel, 0.0, NEG_INF)


def _sample_select(scores):
    return pl.pallas_call(
        _sample_select_kernel,
        grid=(1,),
        in_specs=[_const_spec(scores.shape)],
        out_specs=_const_spec(scores.shape, single=False),
        out_shape=jax.ShapeDtypeStruct(scores.shape, jnp.float32),
        compiler_params=pltpu.CompilerParams(vmem_limit_bytes=VMEM_LIMIT),
        name="sample_select",
    )(scores)


def _sample_attn_kernel(pt_ref, bias_ref, bias_self_ref, q_ref, knew_ref, vnew_ref, k_hbm, v_hbm, o_ref,
                        kbuf, vbuf, sem, m_s, l_s, acc_s):
    b, j = pl.program_id(0), pl.program_id(1)
    n_j = pl.num_programs(1)
    step = b * n_j + j
    slot = step % 2

    def page_copies(st, sl):
        out = []
        for t in range(ATTN_PAGES):
            page = pt_ref[st * ATTN_PAGES + t]
            out.append(pltpu.make_async_copy(k_hbm.at[page], kbuf.at[sl, t], sem.at[0, sl]))
            out.append(pltpu.make_async_copy(v_hbm.at[page], vbuf.at[sl, t], sem.at[1, sl]))
        return out

    @pl.when(step == 0)
    def _():
        for c in page_copies(0, 0):
            c.start()

    @pl.when(step + 1 < pl.num_programs(0) * n_j)
    def _():
        for c in page_copies(step + 1, 1 - slot):
            c.start()

    for c in page_copies(step, slot):
        c.wait()

    @pl.when(j == 0)
    def _():
        m_s[...] = jnp.full(m_s.shape, NEG_INF, jnp.float32)
        l_s[...] = jnp.zeros(l_s.shape, jnp.float32)
        acc_s[...] = jnp.zeros(acc_s.shape, jnp.float32)

    def online_update(s, pv_of):
        m_old = m_s[...]
        m_new = jnp.maximum(m_old, jnp.max(s, axis=1, keepdims=True))
        m_safe = jnp.where(m_new == NEG_INF, 0.0, m_new)
        pr = jnp.exp2((s - m_safe) * SOFTMAX_SCALE_LOG2E)
        alpha = jnp.exp2((m_old - m_safe) * SOFTMAX_SCALE_LOG2E)
        l_s[...] = alpha * l_s[...] + jnp.sum(pr, axis=1, keepdims=True)
        acc_s[...] = acc_s[...] * alpha + pv_of(_bf16(pr))
        m_s[...] = m_new

    q = q_ref[...]
    rows_per_page = PAGE_SIZE * N_KV_HEADS
    n_cols = ATTN_PAGES * rows_per_page
    k_all = _bf16(kbuf[slot].reshape(n_cols, HEAD_DIM))
    v_all = _bf16(vbuf[slot].reshape(n_cols, HEAD_DIM))
    dup = jnp.where(lax.broadcasted_iota(jnp.int32, (PAGE_SIZE, rows_per_page), 1) // N_KV_HEADS
                    == lax.broadcasted_iota(jnp.int32, (PAGE_SIZE, rows_per_page), 0), 1.0, 0.0)
    sel_pages = _dot(_bf16(jnp.where(bias_ref[...] == 0.0, 1.0, 0.0)), _bf16(dup))
    sel_row = jnp.concatenate([sel_pages[t:t + 1, :] for t in range(ATTN_PAGES)], axis=1)
    head = lax.broadcasted_iota(jnp.int32, (N_HEADS, n_cols), 0)
    col = lax.broadcasted_iota(jnp.int32, (N_HEADS, n_cols), 1)
    own = (col % N_KV_HEADS) == (head // HEADS_PER_KV)
    s = jnp.where(jnp.logical_and(own, sel_row > 0.5), _dot_nt(q, k_all), NEG_INF)
    online_update(s, lambda pb: _dot(pb, v_all))

    @pl.when(j == n_j - 1)
    def _():
        head_d = lax.broadcasted_iota(jnp.int32, (N_HEADS, HEAD_DIM), 0) // HEADS_PER_KV

        def own_row(ref):
            rows = _bf16(ref[...]).astype(jnp.float32)
            out = jnp.broadcast_to(rows[N_KV_HEADS - 1:N_KV_HEADS, :], (N_HEADS, HEAD_DIM))
            for g in range(N_KV_HEADS - 2, -1, -1):
                out = jnp.where(head_d == g, rows[g:g + 1, :], out)
            return out

        s_self = jnp.sum(q.astype(jnp.float32) * own_row(knew_ref), axis=1, keepdims=True)
        v_own = own_row(vnew_ref)
        online_update(s_self + bias_self_ref[...][:, 0:1], lambda pb: pb.astype(jnp.float32) * v_own)
        o_ref[...] = acc_s[...] / l_s[...]


def _sample_attention(page_table_flat, k_pages, v_pages, bias3, bias_self, q3, knew3, vnew3):
    per_sample = lambda r, w: pl.BlockSpec((None, r, w), lambda b, j, pt: (b, 0, 0))
    rows_per_page = PAGE_SIZE * N_KV_HEADS
    return pl.pallas_call(
        _sample_attn_kernel,
        grid_spec=pltpu.PrefetchScalarGridSpec(
            num_scalar_prefetch=1,
            grid=(DEC_BATCH, N_PAGES // ATTN_PAGES),
            in_specs=[
                pl.BlockSpec((None, ATTN_PAGES, PAGE_SIZE), lambda b, j, pt: (b, j, 0)),
                per_sample(1, PAGE_SIZE), per_sample(N_HEADS, HEAD_DIM),
                per_sample(N_KV_HEADS, HEAD_DIM), per_sample(N_KV_HEADS, HEAD_DIM),
                pl.BlockSpec(memory_space=pl.ANY), pl.BlockSpec(memory_space=pl.ANY),
            ],
            out_specs=per_sample(N_HEADS, HEAD_DIM),
            scratch_shapes=[
                pltpu.VMEM((2, ATTN_PAGES, rows_per_page, HEAD_DIM), jnp.float32),
                pltpu.VMEM((2, ATTN_PAGES, rows_per_page, HEAD_DIM), jnp.float32),
                pltpu.SemaphoreType.DMA((2, 2)),
                pltpu.VMEM((N_HEADS, 1), jnp.float32), pltpu.VMEM((N_HEADS, 1), jnp.float32),
                pltpu.VMEM((N_HEADS, HEAD_DIM), jnp.float32),
            ],
        ),
        out_shape=jax.ShapeDtypeStruct((DEC_BATCH, N_HEADS, HEAD_DIM), jnp.float32),
        compiler_params=pltpu.CompilerParams(
            dimension_semantics=("arbitrary", "arbitrary"), vmem_limit_bytes=VMEM_LIMIT),
        name="sample_attention",
    )(page_table_flat, bias3, bias_self, q3, knew3, vnew3, k_pages, v_pages)


def _pack_w_in(w_in):
    pad = jnp.zeros((D_MODEL, W_PAD), jnp.bfloat16)
    return jnp.concatenate([_bf16(w_in[:, :W_PAD_AT]), pad, _bf16(w_in[:, W_PAD_AT:])], axis=-1)


def _lane_pad(v):
    return jnp.pad(v.reshape(1, -1), ((0, 0), (0, LANES - v.shape[-1])))


def kernel(x_prompt, x_sample, cache_k, cache_v, cache_idx_k, state_conv, state_rglru, page_table, c_prompt, c_sample, w_ada, b_ada, g_norm, w_in, w_conv, b_conv, w_ra, b_ra, w_rx, b_rx, lru_lambda, idx_k_norm_g, idx_k_norm_b, w_pa, w_pb, w_o, g_final):
    assert w_in.shape[0] == 1, "one layer"
    wts = {
        "g_norm": g_norm[0].reshape(1, -1), "w_in": _pack_w_in(w_in[0]), "w_conv": w_conv[0],
        "b_conv": b_conv[0].reshape(1, -1), "w_ra": _bf16(w_ra[0]), "b_ra": b_ra[0].reshape(1, -1),
        "w_rx": _bf16(w_rx[0]), "b_rx": b_rx[0].reshape(1, -1), "lam": lru_lambda[0].reshape(1, -1),
        "idx_g": _lane_pad(idx_k_norm_g[0]), "idx_b": _lane_pad(idx_k_norm_b[0]),
        "w_pa": _bf16(w_pa[0]), "w_pb": _bf16(w_pb[0]), "w_o": _bf16(w_o[0]), "g_final": g_final.reshape(1, -1),
    }
    half_h, half_i = HEAD_DIM // 2, IDX_DIM // 2
    invf_h = ROPE_THETA ** (-jnp.arange(half_h, dtype=jnp.float32) / half_h)
    invf_i = ROPE_THETA ** (-jnp.arange(half_i, dtype=jnp.float32) / half_i)
    invf = jnp.zeros((SUBLANES, LANES), jnp.float32)
    invf = invf.at[0].set(jnp.tile(invf_h, LANES // half_h)).at[1].set(jnp.tile(invf_i, LANES // half_i))
    tabs_prompt = _rope_tables(invf, SEQ, 0, 1)
    tabs_sample = _rope_tables(invf, SUBLANES, PAST_LEN, 0)

    mod = _ada_modulation(jnp.concatenate([c_prompt, c_sample], axis=0), _bf16(w_ada[0]), b_ada[0].reshape(1, -1))
    shift, scale, gate = mod[:, :D_MODEL], mod[:, D_MODEL:2 * D_MODEL], mod[:, 2 * D_MODEL:]

    (k_p, v_p, ki_p, kb, vt, kib, q, qi, kiwi, mpa, sgb, smb, conv_p, lru_p) = _prompt_projection(
        x_prompt, shift[:BATCH, None, :], scale[:BATCH, None, :], wts, tabs_prompt)
    o = _prompt_attention(q, qi, kiwi, kb, vt, kib)
    y_prompt = _output_projection(o, sgb, mpa, smb, x_prompt, gate[:BATCH, None, :], wts, OUT_ROWS)

    xs = x_sample[:, 0, :]
    (k_s, v_s, q_s, qi_s, kiwi_s, mpa_s, sgb_s, smb_s, conv_s, lru_s) = _sample_projection(
        xs, shift[BATCH:], scale[BATCH:], wts, tabs_sample, jnp.swapaxes(state_conv[0], 0, 1), state_rglru[0])
    pt_flat = page_table.reshape(-1)
    w_col = kiwi_s[:, IDX_DIM:IDX_DIM + N_IDX_HEADS, None]
    idx_pages = jnp.swapaxes(cache_idx_k[0], 1, 2)
    kv_pages = lambda t: t[0].reshape(-1, PAGE_SIZE * N_KV_HEADS, HEAD_DIM)
    scores = _sample_scores(pt_flat, idx_pages, qi_s.reshape(DEC_BATCH, N_IDX_HEADS, IDX_DIM), w_col,
                            kiwi_s[:, None, :])
    bias = _sample_select(scores.reshape(DEC_BATCH, (N_PAGES + 1) * PAGE_SIZE))
    bias = bias.reshape(DEC_BATCH, N_PAGES + 1, PAGE_SIZE)
    o_s = _sample_attention(
        pt_flat, kv_pages(cache_k), kv_pages(cache_v), bias, bias[:, N_PAGES:, :],
        q_s.reshape(DEC_BATCH, N_HEADS, HEAD_DIM), k_s.reshape(DEC_BATCH, N_KV_HEADS, HEAD_DIM),
        v_s.reshape(DEC_BATCH, N_KV_HEADS, HEAD_DIM))
    y_sample = _output_projection(
        o_s.reshape(1, DEC_BATCH, D_ATT), sgb_s[None], mpa_s[None], smb_s[None], xs[None], gate[None, BATCH:],
        wts, DEC_BATCH)

    kv_s = lambda t: t.reshape(1, DEC_BATCH, 1, N_KV_HEADS, HEAD_DIM)
    return (
        y_prompt, y_sample.reshape(DEC_BATCH, 1, D_MODEL),
        k_p[None], v_p[None], jnp.swapaxes(ki_p, 1, 2)[None], conv_p[None], lru_p.reshape(1, BATCH, D_RNN),
        kv_s(k_s), kv_s(v_s), kiwi_s[:, :IDX_DIM].reshape(1, DEC_BATCH, 1, IDX_DIM),
        jnp.swapaxes(conv_s, 0, 1)[None], lru_s[None],
    )
```

```python
import functools

import jax
import jax.numpy as jnp
import numpy as np
from jax import lax
from jax.experimental import pallas as pl
from jax.experimental.pallas import tpu as pltpu

D_MODEL = 1024
BATCH = 8
SEQ = 4096
DEC_BATCH = 32
PAST_LEN = 16384
PAGE_SIZE = 128
N_PAGES = PAST_LEN // PAGE_SIZE
D_RNN = D_MODEL
RNN_BLOCKS = 4
RNN_BLOCK_W = D_RNN // RNN_BLOCKS
CONV_W = 4
LRU_C = 8.0
N_HEADS = 8
HEAD_DIM = 128
N_KV_HEADS = 2
HEADS_PER_KV = N_HEADS // N_KV_HEADS
D_ATT = N_HEADS * HEAD_DIM
D_KV = N_KV_HEADS * HEAD_DIM
N_IDX_HEADS = 8
IDX_DIM = 64
D_IDX = N_IDX_HEADS * IDX_DIM
IDX_W_SCALE = (N_IDX_HEADS * IDX_DIM) ** -0.5
TOPK = 256
ROPE_THETA = 10000.0
EPS = 1e-6
SPLITS = (D_RNN, D_RNN, D_ATT, D_KV, D_KV, D_ATT, D_IDX, IDX_DIM, N_IDX_HEADS, D_MODEL, D_MODEL)

LANES = 128
SUBLANES = 8

C_XA, C_GA, C_Q, C_K, C_V, C_GB, C_QI, C_KW = (int(c) for c in np.cumsum((0,) + SPLITS[:7]))
W_PAD_AT = C_KW + IDX_DIM + N_IDX_HEADS
W_PAD = LANES - IDX_DIM - N_IDX_HEADS
C_MA = C_KW + LANES
C_MB = C_MA + D_MODEL
D_IN_PACKED = C_MB + D_MODEL
assert all(c % LANES == 0 for c in (C_XA, C_GA, C_Q, C_K, C_V, C_GB, C_QI, C_KW, C_MA, C_MB))

PROJ_ROWS = 256
Q_BLOCK = 256
KEY_CHUNK = 256
ATT_CHUNK = 512
PACKED_ROWS = 16
assert KEY_CHUNK >= TOPK and ATT_CHUNK % KEY_CHUNK == 0 and SEQ % ATT_CHUNK == 0 and ATT_CHUNK % Q_BLOCK == 0
assert PAGE_SIZE == HEAD_DIM == LANES
OUT_ROWS = 1024
SCORE_PAGES = 16
ATTN_PAGES = 32
assert N_PAGES % SCORE_PAGES == 0 and N_PAGES % ATTN_PAGES == 0
SOFTMAX_SCALE_LOG2E = (HEAD_DIM ** -0.5) * float(np.log2(np.e))
NEG_INF = float("-inf")
INT_MIN = -2 ** 31
KEY_NEG_INF = INT_MIN + 0x7FFFFF
V7X_VMEM_BYTES = 64 * 1024 * 1024
VMEM_LIMIT = V7X_VMEM_BYTES * 7 // 8


def _sigmoid(x):
    return 1.0 / (1.0 + jnp.exp(-x))


def _silu(x):
    return x * _sigmoid(x)


def _dot(a, b):
    return jnp.dot(a, b, preferred_element_type=jnp.float32)


def _dot_nt(a, b):
    return lax.dot_general(a, b, (((1,), (1,)), ((), ())), preferred_element_type=jnp.float32)


def _bf16(x):
    return x.astype(jnp.bfloat16)


def _const_spec(shape, single=True):
    nd = len(shape)
    kwargs = {"pipeline_mode": pl.Buffered(1)} if single else {}
    return pl.BlockSpec(shape, lambda *_: (0,) * nd, **kwargs)


def _rope_kernel(invf_ref, cos_h_ref, sin_h_ref, cos_i_ref, sin_ia_ref, sin_ib_ref, *, pos0, pos_step, rows):
    r0 = pl.program_id(0) * rows
    row = lax.broadcasted_iota(jnp.int32, (rows, LANES), 0) + r0
    lane = lax.broadcasted_iota(jnp.int32, (rows, LANES), 1)
    pos = (pos0 + pos_step * row).astype(jnp.float32)
    ang_h = pos * invf_ref[0:1, :]
    ang_i = pos * invf_ref[1:2, :]
    cos_h_ref[...] = jnp.cos(ang_h)
    sh = jnp.sin(ang_h)
    sin_h_ref[...] = jnp.where(lane < HEAD_DIM // 2, -sh, sh)
    cos_i_ref[...] = jnp.cos(ang_i)
    si = jnp.sin(ang_i)
    first_half = (lane % IDX_DIM) < IDX_DIM // 2
    sin_ia_ref[...] = jnp.where(first_half, -si, 0.0)
    sin_ib_ref[...] = jnp.where(first_half, 0.0, si)


def _rope_tables(invf, n, pos0, pos_step):
    rows = min(n, 512)
    out = jax.ShapeDtypeStruct((n, LANES), jnp.float32)
    spec = pl.BlockSpec((rows, LANES), lambda i: (i, 0))
    return pl.pallas_call(
        functools.partial(_rope_kernel, pos0=pos0, pos_step=pos_step, rows=rows),
        grid=(n // rows,),
        in_specs=[pl.BlockSpec((SUBLANES, LANES), lambda i: (0, 0))],
        out_specs=[spec] * 5,
        out_shape=[out] * 5,
        name="rope_tables",
    )(invf)


def _rot_head(z, cos, sin_signed):
    return z * cos + pltpu.roll(z, HEAD_DIM // 2, 1) * sin_signed


def _rot_idx(z, cos, sin_a, sin_b):
    return z * cos + pltpu.roll(z, LANES - IDX_DIM // 2, 1) * sin_a + pltpu.roll(z, IDX_DIM // 2, 1) * sin_b


def _ada_kernel(c_ref, w_ref, b_ref, o_ref):
    o_ref[...] = _dot(_bf16(_silu(c_ref[...])), w_ref[...]) + b_ref[...]


def _ada_modulation(c_all, w_ada, b_ada):
    n = c_all.shape[0]
    return pl.pallas_call(
        _ada_kernel,
        grid=(1,),
        in_specs=[_const_spec((n, D_MODEL)), _const_spec((D_MODEL, 3 * D_MODEL)), _const_spec((1, 3 * D_MODEL))],
        out_specs=_const_spec((n, 3 * D_MODEL), single=False),
        out_shape=jax.ShapeDtypeStruct((n, 3 * D_MODEL), jnp.float32),
        compiler_params=pltpu.CompilerParams(vmem_limit_bytes=VMEM_LIMIT),
        name="ada_modulation",
    )(c_all, w_ada, b_ada)


def _modulated_norm(x, g, scale, shift):
    y = x * lax.rsqrt(jnp.mean(x * x, axis=-1, keepdims=True) + EPS) * g
    return _bf16(y * (1.0 + scale) + shift)


def _lru_gates(xc, w_ra_ref, b_ra, w_rx_ref, b_rx, lam):
    xcb = _bf16(xc)
    r_parts, i_parts = [], []
    for n in range(RNN_BLOCKS):
        sl = slice(n * RNN_BLOCK_W, (n + 1) * RNN_BLOCK_W)
        r_parts.append(_dot(xcb[:, sl], w_ra_ref[n]))
        i_parts.append(_dot(xcb[:, sl], w_rx_ref[n]))
    r = _sigmoid(jnp.concatenate(r_parts, axis=1) + b_ra)
    i = _sigmoid(jnp.concatenate(i_parts, axis=1) + b_rx)
    neg_lam = -lam
    softplus = jnp.maximum(neg_lam, 0.0) + jnp.log1p(jnp.exp(-jnp.abs(neg_lam)))
    log_a = (-LRU_C) * r * softplus
    a = jnp.exp(log_a)
    u = jnp.sqrt(-jnp.tanh(log_a) * (a * a + 1.0)) * (i * xc)
    return a, u


def _idx_key_slab(z_kw, g, b, cos_i, sin_ia, sin_ib):
    lane = lax.broadcasted_iota(jnp.int32, z_kw.shape, 1)
    is_key = lane < IDX_DIM
    mu = jnp.sum(jnp.where(is_key, z_kw, 0.0), axis=-1, keepdims=True) * (1.0 / IDX_DIM)
    d = jnp.where(is_key, z_kw - mu, 0.0)
    var = jnp.sum(d * d, axis=-1, keepdims=True) * (1.0 / IDX_DIM)
    y = d * lax.rsqrt(var + EPS) * g + b
    key = _rot_idx(y, cos_i, sin_ia, sin_ib)
    is_w = jnp.logical_and(lane >= IDX_DIM, lane < IDX_DIM + N_IDX_HEADS)
    return key + jnp.where(is_w, z_kw * IDX_W_SCALE, 0.0)


def _proj_kernel(x_ref, shift_ref, scale_ref, gn_ref, w_in_ref, wconv_ref, bconv_ref, w_ra_ref, b_ra_ref,
                 w_rx_ref, b_rx_ref, lam_ref, ig_ref, ib_ref, w_pa_ref,
                 cos_h_ref, sin_h_ref, cos_i_ref, sin_ia_ref, sin_ib_ref,
                 k_ref, v_ref, ki_ref, kb_ref, vt_ref, kib_ref, q_ref, qi_ref, kiwi_ref,
                 mpa_ref, sgb_ref, smb_ref, conv_ref, lru_ref,
                 xa_ext, a_s, u_s, h_carry, tail_s, ga_s, ma_s):
    ts = PROJ_ROWS

    @pl.when(pl.program_id(1) == 0)
    def _():
        tail_s[...] = jnp.zeros(tail_s.shape, jnp.float32)
        h_carry[...] = jnp.zeros(h_carry.shape, jnp.float32)

    xn = _modulated_norm(x_ref[...], gn_ref[...], scale_ref[...], shift_ref[...])

    xa_ext[0:SUBLANES, :] = tail_s[...]
    xa_ext[SUBLANES:SUBLANES + ts, :] = _dot(xn, w_in_ref[:, C_XA:C_XA + D_RNN])
    sgb_ref[...] = _bf16(_silu(_dot(xn, w_in_ref[:, C_GB:C_GB + D_ATT])))
    smb_ref[...] = _bf16(_sigmoid(_dot(xn, w_in_ref[:, C_MB:C_MB + D_MODEL])))
    xc = bconv_ref[...]
    for t in range(CONV_W):
        off = SUBLANES - (CONV_W - 1) + t
        xc = xc + xa_ext[off:off + ts, :] * wconv_ref[t:t + 1, :]
    conv_ref[...] = xa_ext[ts + SUBLANES - (CONV_W - 1):ts + SUBLANES, :]
    tail_s[...] = xa_ext[ts:ts + SUBLANES, :]

    a, u = _lru_gates(xc, w_ra_ref, b_ra_ref[...], w_rx_ref, b_rx_ref[...], lam_ref[...])
    a_s[...] = a
    u_s[...] = u
    row = lax.broadcasted_iota(jnp.int32, (SUBLANES, D_RNN), 0)
    cos_h, sin_h = cos_h_ref[...], sin_h_ref[...]
    cos_i, sin_ia, sin_ib = cos_i_ref[...], sin_ia_ref[...], sin_ib_ref[...]

    def proj_gate_a():
        ga_s[...] = _silu(_dot(xn, w_in_ref[:, C_GA:C_GA + D_RNN]))

    def proj_merge_a():
        ma_s[...] = _sigmoid(_dot(xn, w_in_ref[:, C_MA:C_MA + D_MODEL]))

    def proj_q(lo, hi):
        def run():
            zq = _dot(xn, w_in_ref[:, C_Q + lo * HEAD_DIM:C_Q + hi * HEAD_DIM])
            for h in range(hi - lo):
                rot = _rot_head(zq[:, h * HEAD_DIM:(h + 1) * HEAD_DIM], cos_h, sin_h)
                q_ref[:, (lo + h) * HEAD_DIM:(lo + h + 1) * HEAD_DIM] = _bf16(rot * SOFTMAX_SCALE_LOG2E)
        return run

    def proj_kv():
        zk = _dot(xn, w_in_ref[:, C_K:C_K + D_KV])
        for g in range(N_KV_HEADS):
            sl = slice(g * HEAD_DIM, (g + 1) * HEAD_DIM)
            kr = _rot_head(zk[:, sl], cos_h, sin_h)
            k_ref[:, g, :] = kr
            kb_ref[:, sl] = _bf16(kr)
        zv = _dot(xn, w_in_ref[:, C_V:C_V + D_KV])
        for g in range(N_KV_HEADS):
            v_ref[:, g, :] = zv[:, g * HEAD_DIM:(g + 1) * HEAD_DIM]
        vt = _bf16(zv.T)
        for c in range(ts // KEY_CHUNK):
            vt_ref[c] = vt[:, c * KEY_CHUNK:(c + 1) * KEY_CHUNK]

    def proj_idx():
        zqi = _dot(xn, w_in_ref[:, C_QI:C_QI + D_IDX])
        for p in range(D_IDX // LANES):
            sl = slice(p * LANES, (p + 1) * LANES)
            qi_ref[:, sl] = _bf16(_rot_idx(zqi[:, sl], cos_i, sin_ia, sin_ib))
        slab = _idx_key_slab(_dot(xn, w_in_ref[:, C_KW:C_KW + LANES]), ig_ref[...], ib_ref[...],
                             cos_i, sin_ia, sin_ib)
        kiwi_ref[...] = slab
        ki_ref[...] = slab.T[:IDX_DIM, :]
        key_even = jnp.where(lax.broadcasted_iota(jnp.int32, slab.shape, 1) < IDX_DIM, slab, 0.0)
        kib_ref[:, 0:LANES] = _bf16(key_even)
        kib_ref[:, LANES:2 * LANES] = _bf16(pltpu.roll(key_even, IDX_DIM, 1))

    scan_work = [proj_gate_a, proj_merge_a, proj_q(0, N_HEADS // 2), proj_q(N_HEADS // 2, N_HEADS), proj_kv, proj_idx]

    hc = h_carry[...]
    n_groups = ts // SUBLANES
    per_chunk = n_groups // (len(scan_work) + 2)
    for g in range(n_groups):
        rows = slice(g * SUBLANES, (g + 1) * SUBLANES)
        a8 = a_s[rows, :]
        u8 = u_s[rows, :]
        for d in (1, 2, 4):
            keep = row >= d
            u8 = jnp.where(keep, a8 * pltpu.roll(u8, d, 0) + u8, u8)
            a8 = jnp.where(keep, a8 * pltpu.roll(a8, d, 0), a8)
        h8 = a8 * hc + u8
        u_s[rows, :] = h8
        hc = h8[SUBLANES - 1:SUBLANES, :]
        if g % per_chunk == per_chunk - 1 and g // per_chunk < len(scan_work):
            scan_work[g // per_chunk]()
    h_carry[...] = hc
    lru_ref[...] = hc

    ya = _dot(_bf16(u_s[...] * ga_s[...]), w_pa_ref[...])
    mpa_ref[...] = _bf16(ma_s[...] * ya)


def _prompt_projection(x, shift, scale, wts, tabs):
    ts = PROJ_ROWS
    nt = SEQ // ts
    f32, bf16 = jnp.float32, jnp.bfloat16
    row_spec = lambda w: pl.BlockSpec((None, ts, w), lambda b, j: (b, j, 0))
    bvec_spec = pl.BlockSpec((None, 1, D_MODEL), lambda b, j: (b, 0, 0))
    kv_spec = pl.BlockSpec((None, ts, N_KV_HEADS, HEAD_DIM), lambda b, j: (b, j, 0, 0))
    tab_spec = pl.BlockSpec((ts, LANES), lambda b, j: (j, 0))
    in_specs = [
        row_spec(D_MODEL), bvec_spec, bvec_spec, _const_spec((1, D_MODEL)),
        _const_spec((D_MODEL, D_IN_PACKED)), _const_spec((CONV_W, D_RNN)), _const_spec((1, D_RNN)),
        _const_spec((RNN_BLOCKS, RNN_BLOCK_W, RNN_BLOCK_W)), _const_spec((1, D_RNN)),
        _const_spec((RNN_BLOCKS, RNN_BLOCK_W, RNN_BLOCK_W)), _const_spec((1, D_RNN)), _const_spec((1, D_RNN)),
        _const_spec((1, LANES)), _const_spec((1, LANES)), _const_spec((D_RNN, D_MODEL)),
    ] + [tab_spec] * 5
    out_shape = [
        jax.ShapeDtypeStruct((BATCH, SEQ, N_KV_HEADS, HEAD_DIM), f32),
        jax.ShapeDtypeStruct((BATCH, SEQ, N_KV_HEADS, HEAD_DIM), f32),
        jax.ShapeDtypeStruct((BATCH, IDX_DIM, SEQ), f32),
        jax.ShapeDtypeStruct((BATCH, SEQ, D_KV), bf16),
        jax.ShapeDtypeStruct((BATCH, SEQ // KEY_CHUNK, D_KV, KEY_CHUNK), bf16),
        jax.ShapeDtypeStruct((BATCH, SEQ, 2 * LANES), bf16),
        jax.ShapeDtypeStruct((BATCH, SEQ, D_ATT), bf16),
        jax.ShapeDtypeStruct((BATCH, SEQ, D_IDX), bf16),
        jax.ShapeDtypeStruct((BATCH, SEQ, LANES), f32),
        jax.ShapeDtypeStruct((BATCH, SEQ, D_MODEL), bf16),
        jax.ShapeDtypeStruct((BATCH, SEQ, D_ATT), bf16),
        jax.ShapeDtypeStruct((BATCH, SEQ, D_MODEL), bf16),
        jax.ShapeDtypeStruct((BATCH, CONV_W - 1, D_RNN), f32),
        jax.ShapeDtypeStruct((BATCH, 1, D_RNN), f32),
    ]
    out_specs = [
        kv_spec, kv_spec, pl.BlockSpec((None, IDX_DIM, ts), lambda b, j: (b, 0, j)), row_spec(D_KV),
        pl.BlockSpec((None, ts // KEY_CHUNK, D_KV, KEY_CHUNK), lambda b, j: (b, j, 0, 0)),
        row_spec(2 * LANES), row_spec(D_ATT), row_spec(D_IDX), row_spec(LANES),
        row_spec(D_MODEL), row_spec(D_ATT), row_spec(D_MODEL),
        pl.BlockSpec((None, CONV_W - 1, D_RNN), lambda b, j: (b, 0, 0)),
        pl.BlockSpec((None, 1, D_RNN), lambda b, j: (b, 0, 0)),
    ]
    scratch = [
        pltpu.VMEM((ts + SUBLANES, D_RNN), f32), pltpu.VMEM((ts, D_RNN), f32),
        pltpu.VMEM((ts, D_RNN), f32), pltpu.VMEM((1, D_RNN), f32), pltpu.VMEM((SUBLANES, D_RNN), f32),
        pltpu.VMEM((ts, D_RNN), f32), pltpu.VMEM((ts, D_MODEL), f32),
    ]
    return pl.pallas_call(
        _proj_kernel,
        grid=(BATCH, nt),
        in_specs=in_specs, out_specs=out_specs, out_shape=out_shape, scratch_shapes=scratch,
        compiler_params=pltpu.CompilerParams(
            dimension_semantics=("arbitrary", "arbitrary"), vmem_limit_bytes=VMEM_LIMIT),
        name="prompt_projection",
    )(x, shift, scale, wts["g_norm"], wts["w_in"], wts["w_conv"], wts["b_conv"], wts["w_ra"], wts["b_ra"],
      wts["w_rx"], wts["b_rx"], wts["lam"], wts["idx_g"], wts["idx_b"], wts["w_pa"], *tabs)


def _threshold_value(t_unsigned):
    key = jnp.maximum(jnp.bitwise_xor(t_unsigned, jnp.int32(INT_MIN)), jnp.int32(KEY_NEG_INF))
    bits = key ^ jnp.bitwise_and(jnp.right_shift(key, 31), jnp.int32(0x7FFFFFFF))
    return pltpu.bitcast(bits, jnp.float32)


def _greedy_bits(count_ge, n_bits, shape, count_all):
    def bit_step(b, carry):
        t, cnt = carry
        cand = jnp.bitwise_or(t, jnp.left_shift(jnp.int32(1), n_bits - 1 - b))
        c = count_ge(cand)
        ok = c >= TOPK
        return jnp.where(ok, cand, t), jnp.where(ok, c, cnt)
    return lax.fori_loop(0, n_bits, bit_step, (jnp.zeros(shape, jnp.int32), count_all))


def _fold_rows(x, rows, op=jnp.add, chains=4):
    parts = [x[r:r + rows] for r in range(0, x.shape[0], rows)]
    acc = parts[:chains]
    for k, part in enumerate(parts[chains:]):
        acc[k % len(acc)] = op(acc[k % len(acc)], part)
    while len(acc) > 1:
        acc = [op(acc[k], acc[k + 1]) for k in range(0, len(acc) - 1, 2)] + ([acc[-1]] if len(acc) % 2 else [])
    return acc[0]


def _tie_cutoff(count_tie_below, need, n_bits, shape):
    def bit_step(b, x):
        cand = jnp.bitwise_or(x, jnp.left_shift(jnp.int32(1), n_bits - 1 - b))
        return jnp.where(count_tie_below(cand) < need, cand, x)
    return lax.fori_loop(0, n_bits, bit_step, jnp.zeros(shape, jnp.int32))


def _attn_kernel(q_ref, qi_ref, kiwi_ref, kb_ref, vt_ref, kib_ref, o_ref,
                 score_s, hi_s, bias_s, s_scr, acc_s):
    i = pl.program_id(1)
    t0 = i * Q_BLOCK
    n_steps = (t0 + Q_BLOCK + ATT_CHUNK - 1) // ATT_CHUNK
    step_iota = lax.broadcasted_iota(jnp.int32, (ATT_CHUNK, Q_BLOCK), 0)
    sub_iota = lax.broadcasted_iota(jnp.int32, (KEY_CHUNK, Q_BLOCK), 0)
    q_pos = t0 + lax.broadcasted_iota(jnp.int32, (1, Q_BLOCK), 1)
    lane_shape = (1, Q_BLOCK)

    def step_rows(c):
        return pl.ds(pl.multiple_of(c * ATT_CHUNK, ATT_CHUNK), ATT_CHUNK)

    w_t = kiwi_ref[...].T[IDX_DIM:IDX_DIM + N_IDX_HEADS, :]
    qi = qi_ref[...]
    n_pairs = D_IDX // LANES
    qi_rows = jnp.concatenate([qi[:, p * LANES:(p + 1) * LANES] for p in range(n_pairs)], axis=0)

    def score_step(c, causal):
        for sub in range(ATT_CHUNK // KEY_CHUNK):
            r0 = pl.multiple_of(c * ATT_CHUNK + sub * KEY_CHUNK, KEY_CHUNK)
            rows = pl.ds(r0, KEY_CHUNK)
            s_par = [_dot_nt(kib_ref[rows, par * LANES:(par + 1) * LANES], qi_rows) for par in range(2)]
            score = jnp.zeros((KEY_CHUNK, Q_BLOCK), jnp.float32)
            for h in range(N_IDX_HEADS):
                s_h = s_par[h % 2][:, (h // 2) * Q_BLOCK:(h // 2 + 1) * Q_BLOCK]
                score = score + jnp.maximum(s_h, 0.0) * w_t[h:h + 1, :]
            if causal:
                score = jnp.where((r0 + sub_iota) <= q_pos, score, NEG_INF)
            score_s[rows, :] = score
            hi_s[rows, :] = _bf16(score)

    def early_score_step(c, carry):
        score_step(c, False)
        return carry

    lax.fori_loop(0, n_steps - 1, early_score_step, 0)
    score_step(n_steps - 1, True)

    def sum_steps(one_step, zero):
        acc = lax.fori_loop(0, n_steps // 2, lambda p, a: a + one_step(2 * p) + one_step(2 * p + 1), zero)
        return lax.cond(n_steps % 2 == 1, lambda: acc + one_step(n_steps - 1), lambda: acc)

    def count_where(pred):
        def one_step(c):
            sel = pred(score_s[step_rows(c), :], c * ATT_CHUNK + step_iota)
            parts = [_fold_rows(jnp.where(sel[:, t * LANES:(t + 1) * LANES], 1.0, 0.0), SUBLANES, chains=2)
                     for t in range(Q_BLOCK // LANES)]
            return jnp.concatenate(parts, axis=1)
        acc = sum_steps(one_step, jnp.zeros((SUBLANES, Q_BLOCK), jnp.float32))
        return jnp.sum(acc, axis=0, keepdims=True)

    def count_rounded_ge(t16):
        cand = _threshold_value(jnp.left_shift(t16, 16)).astype(jnp.bfloat16)
        def one_step(c):
            one = jnp.where(hi_s[step_rows(c), :] >= cand, jnp.bfloat16(1), jnp.bfloat16(0))
            return _fold_rows(one, PACKED_ROWS, chains=2)
        acc = sum_steps(one_step, jnp.zeros((PACKED_ROWS, Q_BLOCK), jnp.bfloat16))
        return jnp.sum(acc.astype(jnp.float32), axis=0, keepdims=True)

    count_all = jnp.full(lane_shape, 1.0, jnp.float32) * (n_steps * ATT_CHUNK).astype(jnp.float32)
    t1, cnt_t1 = _greedy_bits(count_rounded_ge, 16, lane_shape, count_all)

    base = jnp.left_shift(jnp.maximum(t1 - 1, 0), 16)

    def count_ge(offset):
        cand = _threshold_value(base + offset)
        return count_where(lambda blk, idx: blk >= cand)

    off, cnt_ge = _greedy_bits(count_ge, 17, lane_shape, cnt_t1)
    thr = _threshold_value(base + off)

    def write_bias(select):
        def step(c, causal):
            blk = score_s[step_rows(c), :]
            idx = c * ATT_CHUNK + step_iota
            sel = select(blk, idx)
            if causal:
                sel = jnp.logical_and(sel, idx <= q_pos)
            bias_s[step_rows(c), :] = jnp.where(sel, 0.0, NEG_INF)

        def early_step(c, carry):
            step(c, False)
            return carry

        lax.fori_loop(0, n_steps - 1, early_step, 0)
        step(n_steps - 1, True)

    has_tie = jnp.max(jnp.where(cnt_ge > TOPK, 1.0, 0.0)) > 0.5

    @pl.when(jnp.logical_not(has_tie))
    def _():
        write_bias(lambda blk, idx: blk >= thr)

    @pl.when(has_tie)
    def _():
        need = TOPK - count_where(lambda blk, idx: blk > thr)
        cutoff = _tie_cutoff(
            lambda x: count_where(lambda blk, idx: jnp.logical_and(blk == thr, idx < x)),
            need, int(np.log2(SEQ)), lane_shape)
        write_bias(lambda blk, idx: jnp.logical_or(blk > thr, jnp.logical_and(blk == thr, idx <= cutoff)))

    q = q_ref[...]
    n_lanes = HEADS_PER_KV * Q_BLOCK
    q_rows = [
        jnp.concatenate([q[:, (g * HEADS_PER_KV + h) * HEAD_DIM:(g * HEADS_PER_KV + h + 1) * HEAD_DIM]
                         for h in range(HEADS_PER_KV)], axis=0)
        for g in range(N_KV_HEADS)]
    acc_s[...] = jnp.zeros(acc_s.shape, jnp.float32)
    subs = ATT_CHUNK // KEY_CHUNK

    ones_rows = jnp.ones((PACKED_ROWS, KEY_CHUNK), jnp.bfloat16)

    d_sl = [slice(g * HEAD_DIM, (g + 1) * HEAD_DIM) for g in range(N_KV_HEADS)]

    def logits(c, slot):
        maxima = []
        for g in range(N_KV_HEADS):
            b = bias_s[step_rows(c), :]
            x = _dot_nt(kb_ref[step_rows(c), d_sl[g]], q_rows[g]) + jnp.concatenate([b] * HEADS_PER_KV, axis=1)
            s_scr[slot, g] = x
            maxima.append(jnp.max(_fold_rows(x, SUBLANES, jnp.maximum), axis=0, keepdims=True))
        return tuple(maxima)

    def accumulate(c, slot, m_old, m_step):
        m_out = []
        for g in range(N_KV_HEADS):
            m_new = jnp.maximum(m_old[g], m_step[g])
            m_safe = jnp.where(m_new == NEG_INF, 0.0, m_new)
            alpha = jnp.exp2(m_old[g] - m_safe)
            pv = None
            for sub in range(subs):
                r = sub * KEY_CHUNK
                pb = _bf16(jnp.exp2(s_scr[slot, g, r:r + KEY_CHUNK, :] - m_safe))
                lhs = jnp.concatenate([vt_ref[c * subs + sub, d_sl[g], :], ones_rows], axis=0)
                part = _dot(lhs, pb)
                pv = part if pv is None else pv + part
            acc_s[g] = acc_s[g] * alpha + pv
            m_out.append(m_new)
        return tuple(m_out)

    def attend_pair(p, carry):
        m_run, m_even = carry
        c = 2 * p
        m_odd = logits(c + 1, 1)
        m_run = accumulate(c, 0, m_run, m_even)
        m_even = logits(jnp.minimum(c + 2, n_steps - 1), 0)
        m_run = accumulate(c + 1, 1, m_run, m_odd)
        return m_run, m_even

    m_init = tuple(jnp.full((1, n_lanes), NEG_INF, jnp.float32) for _ in range(N_KV_HEADS))
    m_run, m_even = lax.fori_loop(0, n_steps // 2, attend_pair, (m_init, logits(0, 0)))

    @pl.when(n_steps % 2 == 1)
    def _():
        accumulate(n_steps - 1, 0, m_run, m_even)

    for g in range(N_KV_HEADS):
        o_t = acc_s[g, 0:HEAD_DIM, :] / acc_s[g, HEAD_DIM:HEAD_DIM + 1, :]
        for h in range(HEADS_PER_KV):
            col = (g * HEADS_PER_KV + h) * HEAD_DIM
            o_ref[:, col:col + HEAD_DIM] = _bf16(o_t[:, h * Q_BLOCK:(h + 1) * Q_BLOCK].T)


def _prompt_attention(q, qi, kiwi, kb, vt, kib):
    nq = SEQ // Q_BLOCK
    blk = lambda w: pl.BlockSpec((None, Q_BLOCK, w), lambda b, i: (b, i, 0))
    return pl.pallas_call(
        _attn_kernel,
        grid=(BATCH, nq),
        in_specs=[
            blk(D_ATT), blk(D_IDX), blk(LANES),
            pl.BlockSpec((None, SEQ, D_KV), lambda b, i: (b, 0, 0)),
            pl.BlockSpec((None, SEQ // KEY_CHUNK, D_KV, KEY_CHUNK), lambda b, i: (b, 0, 0, 0)),
            pl.BlockSpec((None, SEQ, 2 * LANES), lambda b, i: (b, 0, 0)),
        ],
        out_specs=blk(D_ATT),
        out_shape=jax.ShapeDtypeStruct((BATCH, SEQ, D_ATT), jnp.bfloat16),
        scratch_shapes=[
            pltpu.VMEM((SEQ, Q_BLOCK), jnp.float32),
            pltpu.VMEM((SEQ, Q_BLOCK), jnp.bfloat16),
            pltpu.VMEM((SEQ, Q_BLOCK), jnp.float32),
            pltpu.VMEM((2, N_KV_HEADS, ATT_CHUNK, HEADS_PER_KV * Q_BLOCK), jnp.float32),
            pltpu.VMEM((N_KV_HEADS, HEAD_DIM + PACKED_ROWS, HEADS_PER_KV * Q_BLOCK), jnp.float32),
        ],
        compiler_params=pltpu.CompilerParams(
            dimension_semantics=("arbitrary", "arbitrary"), vmem_limit_bytes=VMEM_LIMIT),
        name="prompt_attention",
    )(q, qi, kiwi, kb, vt, kib)


def _out_kernel(o_ref, sgb_ref, mpa_ref, smb_ref, x_ref, gate_ref, w_pb_ref, w_o_ref, gf_ref, y_ref):
    f32 = jnp.float32
    yb = _dot(_bf16(o_ref[...].astype(f32) * sgb_ref[...].astype(f32)), w_pb_ref[...])
    m = mpa_ref[...].astype(f32) + smb_ref[...].astype(f32) * yb
    r = x_ref[...] + gate_ref[...] * _dot(_bf16(m), w_o_ref[...])
    y_ref[...] = r * lax.rsqrt(jnp.mean(r * r, axis=-1, keepdims=True) + EPS) * gf_ref[...]


def _output_projection(o, sgb, mpa, smb, x, gate, wts, rows):
    ng, nr, _ = x.shape
    row_spec = pl.BlockSpec((None, rows, D_MODEL), lambda b, j: (b, j, 0))
    if gate.shape[1] == 1:
        gate_spec = pl.BlockSpec((None, 1, D_MODEL), lambda b, j: (b, 0, 0))
    else:
        gate_spec = row_spec
    return pl.pallas_call(
        _out_kernel,
        grid=(ng, nr // rows),
        in_specs=[row_spec] * 5 + [gate_spec, _const_spec((D_ATT, D_MODEL)), _const_spec((D_MODEL, D_MODEL)),
                                   _const_spec((1, D_MODEL))],
        out_specs=row_spec,
        out_shape=jax.ShapeDtypeStruct(x.shape, jnp.float32),
        compiler_params=pltpu.CompilerParams(
            dimension_semantics=("arbitrary", "arbitrary"), vmem_limit_bytes=VMEM_LIMIT),
        name="output_projection",
    )(o, sgb, mpa, smb, x, gate, wts["w_pb"], wts["w_o"], wts["g_final"])


def _sample_proj_kernel(x_ref, shift_ref, scale_ref, gn_ref, w_in_ref, wconv_ref, bconv_ref, w_ra_ref, b_ra_ref,
                        w_rx_ref, b_rx_ref, lam_ref, ig_ref, ib_ref, w_pa_ref,
                        cos_h_ref, sin_h_ref, cos_i_ref, sin_ia_ref, sin_ib_ref, buf_ref, h0_ref,
                        k_ref, v_ref, q_ref, qi_ref, kiwi_ref, mpa_ref, sgb_ref, smb_ref, conv_ref, lru_ref):
    xn = _modulated_norm(x_ref[...], gn_ref[...], scale_ref[...], shift_ref[...])
    xa = _dot(xn, w_in_ref[:, C_XA:C_XA + D_RNN])
    xc = bconv_ref[...]
    for t in range(CONV_W - 1):
        xc = xc + buf_ref[t] * wconv_ref[t:t + 1, :]
        if t > 0:
            conv_ref[t - 1] = buf_ref[t]
    xc = xc + xa * wconv_ref[CONV_W - 1:CONV_W, :]
    conv_ref[CONV_W - 2] = xa
    a, u = _lru_gates(xc, w_ra_ref, b_ra_ref[...], w_rx_ref, b_rx_ref[...], lam_ref[...])
    h = a * h0_ref[...] + u
    lru_ref[...] = h
    ga = _dot(xn, w_in_ref[:, C_GA:C_GA + D_RNN])
    ya = _dot(_bf16(h * _silu(ga)), w_pa_ref[...])
    mpa_ref[...] = _sigmoid(_dot(xn, w_in_ref[:, C_MA:C_MA + D_MODEL])) * ya
    sgb_ref[...] = _silu(_dot(xn, w_in_ref[:, C_GB:C_GB + D_ATT]))
    smb_ref[...] = _sigmoid(_dot(xn, w_in_ref[:, C_MB:C_MB + D_MODEL]))

    cos_h, sin_h = cos_h_ref[0:1, :], sin_h_ref[0:1, :]
    cos_i, sin_ia, sin_ib = cos_i_ref[0:1, :], sin_ia_ref[0:1, :], sin_ib_ref[0:1, :]
    zq = _dot(xn, w_in_ref[:, C_Q:C_Q + D_ATT])
    for hd in range(N_HEADS):
        sl = slice(hd * HEAD_DIM, (hd + 1) * HEAD_DIM)
        q_ref[:, sl] = _bf16(_rot_head(zq[:, sl], cos_h, sin_h))
    zk = _dot(xn, w_in_ref[:, C_K:C_K + D_KV])
    for g in range(N_KV_HEADS):
        sl = slice(g * HEAD_DIM, (g + 1) * HEAD_DIM)
        k_ref[:, sl] = _rot_head(zk[:, sl], cos_h, sin_h)
    v_ref[...] = _dot(xn, w_in_ref[:, C_V:C_V + D_KV])
    zqi = _dot(xn, w_in_ref[:, C_QI:C_QI + D_IDX])
    for p in range(D_IDX // LANES):
        sl = slice(p * LANES, (p + 1) * LANES)
        qi_ref[:, sl] = _bf16(_rot_idx(zqi[:, sl], cos_i, sin_ia, sin_ib))
    kiwi_ref[...] = _idx_key_slab(_dot(xn, w_in_ref[:, C_KW:C_KW + LANES]), ig_ref[...], ib_ref[...],
                                  cos_i, sin_ia, sin_ib)


def _sample_projection(x, shift, scale, wts, tabs, buf_t, h0):
    n = DEC_BATCH
    f32, bf16 = jnp.float32, jnp.bfloat16
    in_specs = [
        _const_spec((n, D_MODEL)), _const_spec((n, D_MODEL)), _const_spec((n, D_MODEL)), _const_spec((1, D_MODEL)),
        _const_spec((D_MODEL, D_IN_PACKED)), _const_spec((CONV_W, D_RNN)), _const_spec((1, D_RNN)),
        _const_spec((RNN_BLOCKS, RNN_BLOCK_W, RNN_BLOCK_W)), _const_spec((1, D_RNN)),
        _const_spec((RNN_BLOCKS, RNN_BLOCK_W, RNN_BLOCK_W)), _const_spec((1, D_RNN)), _const_spec((1, D_RNN)),
        _const_spec((1, LANES)), _const_spec((1, LANES)), _const_spec((D_RNN, D_MODEL)),
    ] + [_const_spec((SUBLANES, LANES))] * 5 + [_const_spec((CONV_W - 1, n, D_RNN)), _const_spec((n, D_RNN))]
    shapes = [
        ((n, D_KV), f32), ((n, D_KV), f32), ((n, D_ATT), bf16), ((n, D_IDX), bf16), ((n, LANES), f32),
        ((n, D_MODEL), f32), ((n, D_ATT), f32), ((n, D_MODEL), f32), ((CONV_W - 1, n, D_RNN), f32), ((n, D_RNN), f32),
    ]
    return pl.pallas_call(
        _sample_proj_kernel,
        grid=(1,),
        in_specs=in_specs,
        out_specs=[_const_spec(s, single=False) for s, _ in shapes],
        out_shape=[jax.ShapeDtypeStruct(s, d) for s, d in shapes],
        compiler_params=pltpu.CompilerParams(vmem_limit_bytes=VMEM_LIMIT),
        name="sample_projection",
    )(x, shift, scale, wts["g_norm"], wts["w_in"], wts["w_conv"], wts["b_conv"], wts["w_ra"], wts["b_ra"],
      wts["w_rx"], wts["b_rx"], wts["lam"], wts["idx_g"], wts["idx_b"], wts["w_pa"], *tabs, buf_t, h0)


def _sample_score_kernel(pt_ref, qi_ref, w_ref, kinew_ref, idx_hbm, o_ref, buf, sem):
    b = pl.program_id(0)
    slot = b % 2

    def page_copy(sample, p, sl):
        return pltpu.make_async_copy(idx_hbm.at[pt_ref[sample * N_PAGES + p]], buf.at[sl, p], sem.at[sl])

    def start_sample(sample, sl):
        def body(p, carry):
            page_copy(sample, 2 * p, sl).start(priority=0)
            page_copy(sample, 2 * p + 1, sl).start(priority=1)
            return carry
        lax.fori_loop(0, N_PAGES // 2, body, 0)

    @pl.when(b == 0)
    def _():
        start_sample(0, 0)

    @pl.when(b + 1 < pl.num_programs(0))
    def _():
        start_sample(b + 1, 1 - slot)

    def wait_page(p, carry):
        page_copy(b, p, slot).wait()
        return carry

    lax.fori_loop(0, N_PAGES, wait_page, 0)

    qi = qi_ref[...]
    w = w_ref[...]

    def score_pages(i, carry):
        p0 = i * SCORE_PAGES
        kt = _bf16(jnp.concatenate([buf[slot, p0 + t] for t in range(SCORE_PAGES)], axis=1))
        s = _dot(qi, kt)
        score = jnp.sum(jnp.maximum(s, 0.0) * w, axis=0, keepdims=True)
        for t in range(SCORE_PAGES):
            o_ref[pl.ds(p0 + t, 1), :] = score[:, t * PAGE_SIZE:(t + 1) * PAGE_SIZE]
        return carry

    for i in range(N_PAGES // SCORE_PAGES):
        score_pages(i, 0)

    k_self = _bf16(kinew_ref[...][:, :IDX_DIM]).astype(jnp.float32)
    s_self = jnp.sum(qi.astype(jnp.float32) * k_self, axis=1, keepdims=True)
    score_self = jnp.sum(jnp.maximum(s_self, 0.0) * w, axis=0, keepdims=True)
    lane = lax.broadcasted_iota(jnp.int32, (1, PAGE_SIZE), 1)
    o_ref[N_PAGES:N_PAGES + 1, :] = jnp.where(lane == 0, score_self, NEG_INF)


def _sample_scores(page_table_flat, idx_pages, qi3, w_col, kiwi3):
    per_sample = lambda r, w: pl.BlockSpec((None, r, w), lambda b, pt: (b, 0, 0))
    return pl.pallas_call(
        _sample_score_kernel,
        grid_spec=pltpu.PrefetchScalarGridSpec(
            num_scalar_prefetch=1,
            grid=(DEC_BATCH,),
            in_specs=[per_sample(N_IDX_HEADS, IDX_DIM), per_sample(N_IDX_HEADS, 1), per_sample(1, LANES),
                      pl.BlockSpec(memory_space=pl.ANY)],
            out_specs=per_sample(N_PAGES + 1, PAGE_SIZE),
            scratch_shapes=[pltpu.VMEM((2, N_PAGES, IDX_DIM, PAGE_SIZE), jnp.float32),
                            pltpu.SemaphoreType.DMA((2,))],
        ),
        out_shape=jax.ShapeDtypeStruct((DEC_BATCH, N_PAGES + 1, PAGE_SIZE), jnp.float32),
        compiler_params=pltpu.CompilerParams(dimension_semantics=("arbitrary",), vmem_limit_bytes=VMEM_LIMIT),
        name="sample_scores",
    )(page_table_flat, qi3, w_col, kiwi3, idx_pages)


def _sample_select_kernel(score_ref, bias_ref):
    keys = score_ref[...]
    idx = lax.broadcasted_iota(jnp.int32, keys.shape, 1)
    col_shape = (keys.shape[0], 1)
    count = lambda pred: jnp.sum(jnp.where(pred, 1.0, 0.0), axis=1, keepdims=True)
    total = jnp.full(col_shape, float(keys.shape[1]), jnp.float32)
    thr_u, _ = _greedy_bits(lambda t: count(keys >= _threshold_value(t)), 32, col_shape, total)
    thr = _threshold_value(thr_u)
    need = TOPK - count(keys > thr)
    tie = keys == thr
    n_bits = int(np.ceil(np.log2(keys.shape[1])))
    cutoff = _tie_cutoff(lambda x: count(jnp.logical_and(tie, idx < x)), need, n_bits, col_shape)
    sel = jnp.logical_or(keys > thr, jnp.logical_and(tie, idx <= cutoff))
    bias_ref[...] = jnp.where(sel, 0.0, NEG_INF)


def _sample_select(scores):
    return pl.pallas_call(
        _sample_select_kernel,
        grid=(1,),
        in_specs=[_const_spec(scores.shape)],
        out_specs=_const_spec(scores.shape, single=False),
        out_shape=jax.ShapeDtypeStruct(scores.shape, jnp.float32),
        compiler_params=pltpu.CompilerParams(vmem_limit_bytes=VMEM_LIMIT),
        name="sample_select",
    )(scores)


def _sample_attn_kernel(pt_ref, bias_ref, bias_self_ref, q_ref, knew_ref, vnew_ref, k_hbm, v_hbm, o_ref,
                        kbuf, vbuf, sem, m_s, l_s, acc_s):
    b, j = pl.program_id(0), pl.program_id(1)
    n_j = pl.num_programs(1)
    step = b * n_j + j
    slot = step % 2

    def page_copies(st, sl):
        out = []
        for t in range(ATTN_PAGES):
            page = pt_ref[st * ATTN_PAGES + t]
            out.append(pltpu.make_async_copy(k_hbm.at[page], kbuf.at[sl, t], sem.at[0, sl]))
            out.append(pltpu.make_async_copy(v_hbm.at[page], vbuf.at[sl, t], sem.at[1, sl]))
        return out

    @pl.when(step == 0)
    def _():
        for n, c in enumerate(page_copies(0, 0)):
            c.start(priority=n % 2)

    @pl.when(step + 1 < pl.num_programs(0) * n_j)
    def _():
        for n, c in enumerate(page_copies(step + 1, 1 - slot)):
            c.start(priority=n % 2)

    for c in page_copies(step, slot):
        c.wait()

    @pl.when(j == 0)
    def _():
        m_s[...] = jnp.full(m_s.shape, NEG_INF, jnp.float32)
        l_s[...] = jnp.zeros(l_s.shape, jnp.float32)
        acc_s[...] = jnp.zeros(acc_s.shape, jnp.float32)

    def online_update(s, pv_of):
        m_old = m_s[...]
        m_new = jnp.maximum(m_old, jnp.max(s, axis=1, keepdims=True))
        m_safe = jnp.where(m_new == NEG_INF, 0.0, m_new)
        pr = jnp.exp2((s - m_safe) * SOFTMAX_SCALE_LOG2E)
        alpha = jnp.exp2((m_old - m_safe) * SOFTMAX_SCALE_LOG2E)
        l_s[...] = alpha * l_s[...] + jnp.sum(pr, axis=1, keepdims=True)
        acc_s[...] = acc_s[...] * alpha + pv_of(_bf16(pr))
        m_s[...] = m_new

    q = q_ref[...]
    rows_per_page = PAGE_SIZE * N_KV_HEADS
    n_cols = ATTN_PAGES * rows_per_page
    k_all = _bf16(kbuf[slot].reshape(n_cols, HEAD_DIM))
    v_all = _bf16(vbuf[slot].reshape(n_cols, HEAD_DIM))
    dup = jnp.where(lax.broadcasted_iota(jnp.int32, (PAGE_SIZE, rows_per_page), 1) // N_KV_HEADS
                    == lax.broadcasted_iota(jnp.int32, (PAGE_SIZE, rows_per_page), 0), 1.0, 0.0)
    sel_pages = _dot(_bf16(jnp.where(bias_ref[...] == 0.0, 1.0, 0.0)), _bf16(dup))
    sel_row = jnp.concatenate([sel_pages[t:t + 1, :] for t in range(ATTN_PAGES)], axis=1)
    head = lax.broadcasted_iota(jnp.int32, (N_HEADS, n_cols), 0)
    col = lax.broadcasted_iota(jnp.int32, (N_HEADS, n_cols), 1)
    own = (col % N_KV_HEADS) == (head // HEADS_PER_KV)
    s = jnp.where(jnp.logical_and(own, sel_row > 0.5), _dot_nt(q, k_all), NEG_INF)
    online_update(s, lambda pb: _dot(pb, v_all))

    @pl.when(j == n_j - 1)
    def _():
        head_d = lax.broadcasted_iota(jnp.int32, (N_HEADS, HEAD_DIM), 0) // HEADS_PER_KV

        def own_row(ref):
            rows = _bf16(ref[...]).astype(jnp.float32)
            out = jnp.broadcast_to(rows[N_KV_HEADS - 1:N_KV_HEADS, :], (N_HEADS, HEAD_DIM))
            for g in range(N_KV_HEADS - 2, -1, -1):
                out = jnp.where(head_d == g, rows[g:g + 1, :], out)
            return out

        s_self = jnp.sum(q.astype(jnp.float32) * own_row(knew_ref), axis=1, keepdims=True)
        v_own = own_row(vnew_ref)
        online_update(s_self + bias_self_ref[...][:, 0:1], lambda pb: pb.astype(jnp.float32) * v_own)
        o_ref[...] = acc_s[...] / l_s[...]


def _sample_attention(page_table_flat, k_pages, v_pages, bias3, bias_self, q3, knew3, vnew3):
    per_sample = lambda r, w: pl.BlockSpec((None, r, w), lambda b, j, pt: (b, 0, 0))
    rows_per_page = PAGE_SIZE * N_KV_HEADS
    return pl.pallas_call(
        _sample_attn_kernel,
        grid_spec=pltpu.PrefetchScalarGridSpec(
            num_scalar_prefetch=1,
            grid=(DEC_BATCH, N_PAGES // ATTN_PAGES),
            in_specs=[
                pl.BlockSpec((None, ATTN_PAGES, PAGE_SIZE), lambda b, j, pt: (b, j, 0)),
                per_sample(1, PAGE_SIZE), per_sample(N_HEADS, HEAD_DIM),
                per_sample(N_KV_HEADS, HEAD_DIM), per_sample(N_KV_HEADS, HEAD_DIM),
                pl.BlockSpec(memory_space=pl.ANY), pl.BlockSpec(memory_space=pl.ANY),
            ],
            out_specs=per_sample(N_HEADS, HEAD_DIM),
            scratch_shapes=[
                pltpu.VMEM((2, ATTN_PAGES, rows_per_page, HEAD_DIM), jnp.float32),
                pltpu.VMEM((2, ATTN_PAGES, rows_per_page, HEAD_DIM), jnp.float32),
                pltpu.SemaphoreType.DMA((2, 2)),
                pltpu.VMEM((N_HEADS, 1), jnp.float32), pltpu.VMEM((N_HEADS, 1), jnp.float32),
                pltpu.VMEM((N_HEADS, HEAD_DIM), jnp.float32),
            ],
        ),
        out_shape=jax.ShapeDtypeStruct((DEC_BATCH, N_HEADS, HEAD_DIM), jnp.float32),
        compiler_params=pltpu.CompilerParams(
            dimension_semantics=("arbitrary", "arbitrary"), vmem_limit_bytes=VMEM_LIMIT),
        name="sample_attention",
    )(page_table_flat, bias3, bias_self, q3, knew3, vnew3, k_pages, v_pages)


def _pack_w_in(w_in):
    pad = jnp.zeros((D_MODEL, W_PAD), jnp.bfloat16)
    return jnp.concatenate([_bf16(w_in[:, :W_PAD_AT]), pad, _bf16(w_in[:, W_PAD_AT:])], axis=-1)


def _lane_pad(v):
    return jnp.pad(v.reshape(1, -1), ((0, 0), (0, LANES - v.shape[-1])))


def kernel(x_prompt, x_sample, cache_k, cache_v, cache_idx_k, state_conv, state_rglru, page_table, c_prompt, c_sample, w_ada, b_ada, g_norm, w_in, w_conv, b_conv, w_ra, b_ra, w_rx, b_rx, lru_lambda, idx_k_norm_g, idx_k_norm_b, w_pa, w_pb, w_o, g_final):
    assert w_in.shape[0] == 1, "one layer"
    wts = {
        "g_norm": g_norm[0].reshape(1, -1), "w_in": _pack_w_in(w_in[0]), "w_conv": w_conv[0],
        "b_conv": b_conv[0].reshape(1, -1), "w_ra": _bf16(w_ra[0]), "b_ra": b_ra[0].reshape(1, -1),
        "w_rx": _bf16(w_rx[0]), "b_rx": b_rx[0].reshape(1, -1), "lam": lru_lambda[0].reshape(1, -1),
        "idx_g": _lane_pad(idx_k_norm_g[0]), "idx_b": _lane_pad(idx_k_norm_b[0]),
        "w_pa": _bf16(w_pa[0]), "w_pb": _bf16(w_pb[0]), "w_o": _bf16(w_o[0]), "g_final": g_final.reshape(1, -1),
    }
    half_h, half_i = HEAD_DIM // 2, IDX_DIM // 2
    invf_h = ROPE_THETA ** (-jnp.arange(half_h, dtype=jnp.float32) / half_h)
    invf_i = ROPE_THETA ** (-jnp.arange(half_i, dtype=jnp.float32) / half_i)
    invf = jnp.zeros((SUBLANES, LANES), jnp.float32)
    invf = invf.at[0].set(jnp.tile(invf_h, LANES // half_h)).at[1].set(jnp.tile(invf_i, LANES // half_i))
    tabs_prompt = _rope_tables(invf, SEQ, 0, 1)
    tabs_sample = _rope_tables(invf, SUBLANES, PAST_LEN, 0)

    mod = _ada_modulation(jnp.concatenate([c_prompt, c_sample], axis=0), _bf16(w_ada[0]), b_ada[0].reshape(1, -1))
    shift, scale, gate = mod[:, :D_MODEL], mod[:, D_MODEL:2 * D_MODEL], mod[:, 2 * D_MODEL:]

    (k_p, v_p, ki_p, kb, vt, kib, q, qi, kiwi, mpa, sgb, smb, conv_p, lru_p) = _prompt_projection(
        x_prompt, shift[:BATCH, None, :], scale[:BATCH, None, :], wts, tabs_prompt)
    o = _prompt_attention(q, qi, kiwi, kb, vt, kib)
    y_prompt = _output_projection(o, sgb, mpa, smb, x_prompt, gate[:BATCH, None, :], wts, OUT_ROWS)

    xs = x_sample[:, 0, :]
    (k_s, v_s, q_s, qi_s, kiwi_s, mpa_s, sgb_s, smb_s, conv_s, lru_s) = _sample_projection(
        xs, shift[BATCH:], scale[BATCH:], wts, tabs_sample, jnp.swapaxes(state_conv[0], 0, 1), state_rglru[0])
    pt_flat = page_table.reshape(-1)
    w_col = kiwi_s[:, IDX_DIM:IDX_DIM + N_IDX_HEADS, None]
    idx_pages = jnp.swapaxes(cache_idx_k[0], 1, 2)
    kv_pages = lambda t: t[0].reshape(-1, PAGE_SIZE * N_KV_HEADS, HEAD_DIM)
    scores = _sample_scores(pt_flat, idx_pages, qi_s.reshape(DEC_BATCH, N_IDX_HEADS, IDX_DIM), w_col,
                            kiwi_s[:, None, :])
    bias = _sample_select(scores.reshape(DEC_BATCH, (N_PAGES + 1) * PAGE_SIZE))
    bias = bias.reshape(DEC_BATCH, N_PAGES + 1, PAGE_SIZE)
    o_s = _sample_attention(
        pt_flat, kv_pages(cache_k), kv_pages(cache_v), bias, bias[:, N_PAGES:, :],
        q_s.reshape(DEC_BATCH, N_HEADS, HEAD_DIM), k_s.reshape(DEC_BATCH, N_KV_HEADS, HEAD_DIM),
        v_s.reshape(DEC_BATCH, N_KV_HEADS, HEAD_DIM))
    y_sample = _output_projection(
        o_s.reshape(1, DEC_BATCH, D_ATT), sgb_s[None], mpa_s[None], smb_s[None], xs[None], gate[None, BATCH:],
        wts, DEC_BATCH)

    kv_s = lambda t: t.reshape(1, DEC_BATCH, 1, N_KV_HEADS, HEAD_DIM)
    return (
        y_prompt, y_sample.reshape(DEC_BATCH, 1, D_MODEL),
        k_p[None], v_p[None], jnp.swapaxes(ki_p, 1, 2)[None], conv_p[None], lru_p.reshape(1, BATCH, D_RNN),
        kv_s(k_s), kv_s(v_s), kiwi_s[:, :IDX_DIM].reshape(1, DEC_BATCH, 1, IDX_DIM),
        jnp.swapaxes(conv_s, 0, 1)[None], lru_s[None],
    )
```
